```python
import jax, jax.numpy as jnp
from jax import lax
import numpy as np

D_MODEL = 2048
BATCH = 32
SEQ = 256
DEPTH = 2
DEC_BATCH = 2
DEC_SEQ = 4096
PAST_LEN = 256

GRID_W = 64
N_BRANCH = 4
BRANCH_W = 512
HEAD_DIM = 64
CONV_K = 3
WIN_HEADS = 8
WIN_KV_HEADS = 2
KV_W = WIN_KV_HEADS * HEAD_DIM
WIN = 128
WIN_BLOCK = 128
FN_GROUPS = 4
FN_GROUP_W = BRANCH_W // FN_GROUPS
NA_HEADS = 8
NA_ROWS = 8
NA_COLS = 16
NA_QC = 16
NA_SLAB = 32
CTX_QBLOCK = 128
ROPE_BASE = 10000.0
EPS = 1e-6
ATTN_SCALE = HEAD_DIM ** -0.5
NEG_INF = -1e30
N_IN = 4 * BRANCH_W + (2 * BRANCH_W + 2 * KV_W) + 2 * BRANCH_W + 4 * BRANCH_W + N_BRANCH * D_MODEL

kernel_name = 'hybrid_prefix_diffusion_step'


def rms_norm(x, g):
    xf = x.astype(jnp.float32)
    y = xf * lax.rsqrt(jnp.mean(xf * xf, axis=-1, keepdims=True) + EPS)
    return (y * g.astype(jnp.float32)).astype(x.dtype)


def split_proj(p):
    sizes = [BRANCH_W] * 4 + [BRANCH_W, KV_W, KV_W, BRANCH_W] + [BRANCH_W] * 2 + [BRANCH_W] * 4 + [D_MODEL] * N_BRANCH
    return jnp.split(p, np.cumsum(sizes)[:-1].tolist(), axis=-1)


def short_conv(xin, bgate, cgate, w):
    z = cgate * xin
    zp = jnp.pad(z, ((0, 0), (1, 1), (0, 0)))
    return bgate * (zp[:, :-2] * w[0] + zp[:, 1:-1] * w[1] + zp[:, 2:] * w[2])


def fourier_mix(u):
    B, T, C = u.shape
    z = jnp.fft.fft2(u.astype(jnp.float32).reshape(B, T, FN_GROUPS, FN_GROUP_W), axes=(1, 3), norm='ortho')
    return z.real.reshape(B, T, C).astype(u.dtype)


def axial_rope(x):
    S = x.shape[1]
    t = jnp.arange(S)
    half = HEAD_DIM // 2
    quarter = half // 2
    inv = ROPE_BASE ** (-jnp.arange(quarter, dtype=jnp.float32) / quarter)
    xf = x.astype(jnp.float32)

    def rot(xa, pos):
        ang = pos.astype(jnp.float32)[:, None] * inv[None, :]
        cos = jnp.cos(ang)[None, :, None, :]
        sin = jnp.sin(ang)[None, :, None, :]
        x1, x2 = xa[..., :quarter], xa[..., quarter:]
        return jnp.concatenate([x1 * cos - x2 * sin, x1 * sin + x2 * cos], axis=-1)

    out = jnp.concatenate([rot(xf[..., :half], t // GRID_W), rot(xf[..., half:], t % GRID_W)], axis=-1)
    return out.astype(x.dtype)


def ctx_attention(q, k, v, sink):
    B, T, H, dh = q.shape
    hkv = k.shape[2]
    g = H // hkv
    nq = T // CTX_QBLOCK
    qb = q.reshape(B, nq, CTX_QBLOCK, hkv, g, dh).transpose(1, 0, 2, 3, 4, 5)

    def block(qi):
        s = jnp.einsum('bqhgd,bthd->bhgqt', qi, k).astype(jnp.float32) * ATTN_SCALE
        if sink is not None:
            sk = jnp.broadcast_to(sink.astype(jnp.float32).reshape(hkv, g)[None, :, :, None, None], s.shape[:-1] + (1,))
            p = jax.nn.softmax(jnp.concatenate([s, sk], axis=-1), axis=-1)[..., :-1]
        else:
            p = jax.nn.softmax(s, axis=-1)
        return jnp.einsum('bhgqt,bthd->bqhgd', p.astype(v.dtype), v)

    o = lax.map(block, qb)
    return o.transpose(1, 0, 2, 3, 4, 5).reshape(B, T, H * dh)


def window_attention(q, k, v, kc, vc, sink):
    B, S, H, dh = q.shape
    hkv = k.shape[2]
    g = H // hkv
    nb = S // WIN_BLOCK
    L = 3 * WIN_BLOCK
    qb = q.reshape(B, nb, WIN_BLOCK, hkv, g, dh)

    def band(a):
        ap = jnp.pad(a, ((0, 0), (WIN_BLOCK, WIN_BLOCK), (0, 0), (0, 0))).reshape(B, nb + 2, WIN_BLOCK, hkv, dh)
        return jnp.concatenate([ap[:, :nb], ap[:, 1:nb + 1], ap[:, 2:]], axis=2)

    kw, vw = band(k), band(v)
    n = np.arange(nb)[:, None, None]
    qpos = n * WIN_BLOCK + np.arange(WIN_BLOCK)[None, :, None]
    kpos = (n - 1) * WIN_BLOCK + np.arange(L)[None, None, :]
    mask = (np.abs(kpos - qpos) <= WIN) & (kpos >= 0) & (kpos < S)
    s_loc = jnp.einsum('bnqhgd,bnkhd->bnhgqk', qb, kw).astype(jnp.float32) * ATTN_SCALE
    s_loc = jnp.where(mask[None, :, None, None], s_loc, NEG_INF)
    s_ctx = jnp.einsum('bnqhgd,bthd->bnhgqt', qb, kc).astype(jnp.float32) * ATTN_SCALE
    sk = jnp.broadcast_to(sink.astype(jnp.float32).reshape(hkv, g)[None, None, :, :, None, None], s_loc.shape[:-1] + (1,))
    p = jax.nn.softmax(jnp.concatenate([s_loc, s_ctx, sk], axis=-1), axis=-1)
    T = kc.shape[1]
    o = (jnp.einsum('bnhgqk,bnkhd->bnqhgd', p[..., :L].astype(v.dtype), vw)
         + jnp.einsum('bnhgqt,bthd->bnqhgd', p[..., L:L + T].astype(vc.dtype), vc))
    return o.reshape(B, S, H * dh)


def neighbourhood_attention(q, k, v, kc, vc, rpb):
    B, S, H, dh = q.shape
    rows = S // GRID_W
    wr = min(NA_ROWS, rows)
    ncb = GRID_W // NA_QC
    K = wr * NA_SLAB
    r = np.arange(rows)
    row_idx = np.clip(r - wr // 2, 0, rows - wr)[:, None] + np.arange(wr)[None, :]
    slab0 = np.clip(np.arange(ncb) * NA_QC - NA_COLS // 2, 0, GRID_W - NA_SLAB)
    col_idx = slab0[:, None] + np.arange(NA_SLAB)[None, :]
    qcol = np.arange(ncb)[:, None] * NA_QC + np.arange(NA_QC)[None, :]
    win0 = np.clip(qcol - NA_COLS // 2, 0, GRID_W - NA_COLS)
    kcol = col_idx[:, None, :]
    cmask = (kcol >= win0[..., None]) & (kcol < win0[..., None] + NA_COLS)
    kmask = np.broadcast_to(cmask[:, :, None, :], (ncb, NA_QC, wr, NA_SLAB)).reshape(ncb, NA_QC, K)
    dr = row_idx - r[:, None]
    dc = kcol - qcol[:, :, None]
    bias = rpb[:, (dr + NA_ROWS - 1)[:, None, None, :, None], np.clip(dc + NA_COLS - 1, 0, 2 * NA_COLS - 2)[None, :, :, None, :]]
    bias = bias.reshape(H, rows, ncb, NA_QC, K).transpose(1, 2, 0, 3, 4).astype(jnp.float32)

    def gather(a):
        a = a.reshape(B, rows, GRID_W, H, dh)[:, row_idx][:, :, :, col_idx]
        return a.transpose(0, 1, 3, 2, 4, 5, 6).reshape(B, rows, ncb, K, H, dh)

    kn, vn = gather(k), gather(v)
    qg = q.reshape(B, rows, ncb, NA_QC, H, dh)
    s_nb = jnp.einsum('brcqhd,brckhd->brchqk', qg, kn).astype(jnp.float32) * ATTN_SCALE + bias[None]
    s_nb = jnp.where(kmask[None, None, :, None], s_nb, NEG_INF)
    s_ctx = jnp.einsum('brcqhd,bthd->brchqt', qg, kc).astype(jnp.float32) * ATTN_SCALE
    p = jax.nn.softmax(jnp.concatenate([s_nb, s_ctx], axis=-1), axis=-1)
    o = (jnp.einsum('brchqk,brckhd->brcqhd', p[..., :K].astype(v.dtype), vn)
         + jnp.einsum('brchqt,bthd->brcqhd', p[..., K:].astype(vc.dtype), vc))
    return o.reshape(B, S, H * dh)


def trunk_layer(x, cvec, g, w_ada, b_ada, w_in, conv_w, sink, rpb, w_branch, w_out, ctx_kv):
    B, T, _ = x.shape
    mod = jax.nn.silu(cvec) @ w_ada + b_ada
    shift, scale, gate = jnp.split(mod[:, None, :], 3, axis=-1)
    h = rms_norm(x, g) * (1 + scale) + shift
    (a_x, a_b, a_c, a_g, b_q, b_k, b_v, b_g, f_u, f_g,
     d_q, d_k, d_v, d_g, m0, m1, m2, m3) = split_proj(h @ w_in)
    q_w = b_q.reshape(B, T, WIN_HEADS, HEAD_DIM)
    k_w = b_k.reshape(B, T, WIN_KV_HEADS, HEAD_DIM)
    v_w = b_v.reshape(B, T, WIN_KV_HEADS, HEAD_DIM)
    q_n = d_q.reshape(B, T, NA_HEADS, HEAD_DIM)
    k_n = d_k.reshape(B, T, NA_HEADS, HEAD_DIM)
    v_n = d_v.reshape(B, T, NA_HEADS, HEAD_DIM)
    if ctx_kv is None:
        o_w = ctx_attention(q_w, k_w, v_w, sink)
        o_n = ctx_attention(q_n, k_n, v_n, None)
        new_kv = (k_w, v_w, k_n, v_n)
    else:
        ck_w, cv_w, ck_n, cv_n = ctx_kv
        o_w = window_attention(axial_rope(q_w), axial_rope(k_w), v_w, ck_w, cv_w, sink)
        o_n = neighbourhood_attention(q_n, k_n, v_n, ck_n, cv_n, rpb)
        new_kv = None
    o_a = short_conv(a_x, a_b, a_c, conv_w) * jax.nn.silu(a_g)
    o_w = o_w * jax.nn.silu(b_g)
    o_f = fourier_mix(f_u) * jax.nn.silu(f_g)
    o_n = o_n * jax.nn.silu(d_g)
    merged = (jax.nn.sigmoid(m0) * (o_a @ w_branch[0]) + jax.nn.sigmoid(m1) * (o_w @ w_branch[1])
              + jax.nn.sigmoid(m2) * (o_f @ w_branch[2]) + jax.nn.sigmoid(m3) * (o_n @ w_branch[3]))
    return x + gate * (merged @ w_out), new_kv


def setup_inputs(seed: int = 0) -> dict:
    key = jax.random.key(seed)
    ks = jax.random.split(key, 18)

    def nrm(k, shape, s):
        return jax.random.normal(k, shape, jnp.float32) * s

    return {
        'x_prompt': nrm(ks[0], (BATCH, SEQ, D_MODEL), 1.0),
        'x_sample': nrm(ks[1], (DEC_BATCH, DEC_SEQ, D_MODEL), 1.0),
        'cache_win_k': nrm(ks[2], (DEC_BATCH, DEPTH, PAST_LEN, WIN_KV_HEADS, HEAD_DIM), 1.0),
        'cache_win_v': nrm(ks[3], (DEC_BATCH, DEPTH, PAST_LEN, WIN_KV_HEADS, HEAD_DIM), 1.0),
        'cache_na_k': nrm(ks[4], (DEC_BATCH, DEPTH, PAST_LEN, NA_HEADS, HEAD_DIM), 1.0),
        'cache_na_v': nrm(ks[5], (DEC_BATCH, DEPTH, PAST_LEN, NA_HEADS, HEAD_DIM), 1.0),
        'c': nrm(ks[6], (DEC_BATCH, D_MODEL), 1.0),
        'c_ctx': nrm(ks[7], (D_MODEL,), 1.0),
        'norm_g': 1.0 + nrm(ks[8], (DEPTH, D_MODEL), 0.02),
        'w_ada': nrm(ks[9], (DEPTH, D_MODEL, 3 * D_MODEL), 0.5 * D_MODEL ** -0.5),
        'b_ada': nrm(ks[10], (DEPTH, 3 * D_MODEL), 0.02),
        'w_in': nrm(ks[11], (DEPTH, D_MODEL, N_IN), D_MODEL ** -0.5),
        'conv_w': nrm(ks[12], (DEPTH, CONV_K, BRANCH_W), CONV_K ** -0.5),
        'win_sink': nrm(ks[13], (DEPTH, WIN_HEADS), 0.5),
        'na_rpb': nrm(ks[14], (DEPTH, NA_HEADS, 2 * NA_ROWS - 1, 2 * NA_COLS - 1), 0.1),
        'w_branch': nrm(ks[15], (DEPTH, N_BRANCH, BRANCH_W, D_MODEL), BRANCH_W ** -0.5),
        'w_out': nrm(ks[16], (DEPTH, D_MODEL, D_MODEL), D_MODEL ** -0.5),
        'final_g': 1.0 + nrm(ks[17], (D_MODEL,), 0.02),
    }


def reference(x_prompt, x_sample, cache_win_k, cache_win_v, cache_na_k, cache_na_v, c, c_ctx,
              norm_g, w_ada, b_ada, w_in, conv_w, win_sink, na_rpb, w_branch, w_out, final_g):
    xp = x_prompt
    c_prompt = jnp.broadcast_to(c_ctx, (x_prompt.shape[0], D_MODEL))
    wk, wv, nk, nv = [], [], [], []
    for l in range(DEPTH):
        xp, kv = trunk_layer(xp, c_prompt, norm_g[l], w_ada[l], b_ada[l], w_in[l], conv_w[l], win_sink[l],
                             na_rpb[l], w_branch[l], w_out[l], None)
        wk.append(kv[0])
        wv.append(kv[1])
        nk.append(kv[2])
        nv.append(kv[3])
    xs = x_sample
    for l in range(DEPTH):
        ctx = (cache_win_k[:, l], cache_win_v[:, l], cache_na_k[:, l], cache_na_v[:, l])
        xs, _ = trunk_layer(xs, c, norm_g[l], w_ada[l], b_ada[l], w_in[l], conv_w[l], win_sink[l],
                            na_rpb[l], w_branch[l], w_out[l], ctx)
    y_prompt = rms_norm(xp, final_g)
    y_sample = rms_norm(xs, final_g)
    new_win_k = jnp.stack(wk, axis=1)
    new_win_v = jnp.stack(wv, axis=1)
    new_na_k = jnp.stack(nk, axis=1)
    new_na_v = jnp.stack(nv, axis=1)
    return (y_prompt, y_sample, new_win_k, new_win_v, new_na_k, new_na_v)
```

```python
import functools

import numpy as np
import jax
import jax.numpy as jnp
from jax import lax
from jax.experimental import pallas as pl
from jax.experimental.pallas import tpu as pltpu

D_MODEL = 2048
DEPTH = 2
GRID_W = 64
N_BRANCH = 4
BRANCH_W = 512
HEAD_DIM = 64
WIN_HEADS = 8
WIN_KV_HEADS = 2
KV_W = WIN_KV_HEADS * HEAD_DIM
WIN_BLOCK = 128
FN_GROUPS = 4
FN_GROUP_W = BRANCH_W // FN_GROUPS
NA_HEADS = 8
NA_ROWS = 8
NA_COLS = 16
ROPE_BASE = 10000.0
EPS = 1e-6
ATTN_SCALE = HEAD_DIM ** -0.5
NEG_INF = -1e30
N_MAIN = 6400

F32 = jnp.float32
BF16 = jnp.bfloat16

OFF_A = 0
OFF_BQ, OFF_BK, OFF_BV, OFF_BG = 2048, 2560, 2688, 2816
OFF_FU, OFF_FG = 3328, 3840
OFF_D = 4352

FFT_R = 64


def _silu(x):
    return x * jax.nn.sigmoid(x)


def _dot(a, b):
    return jnp.dot(a, b, preferred_element_type=F32)


def _dot_nt(a, b):
    return lax.dot_general(a, b, (((1,), (1,)), ((), ())), preferred_element_type=F32)


def _rms_mod(x, g, scale, shift):
    ms = jnp.mean(x * x, axis=-1, keepdims=True)
    return (x * lax.rsqrt(ms + EPS) * g) * (1.0 + scale) + shift


def _mod_kernel(cv_ref, w_ref, b_ref, o_ref):
    s = _silu(cv_ref[...]).astype(BF16)
    o_ref[0] = _dot(s, w_ref[0].astype(BF16)) + b_ref[0]


def _modulation(cv8, w_ada, b_ada):
    tn = 512
    n3 = 3 * D_MODEL
    return pl.pallas_call(
        _mod_kernel,
        grid=(DEPTH, n3 // tn),
        in_specs=[
            pl.BlockSpec((8, D_MODEL), lambda l, j: (0, 0)),
            pl.BlockSpec((1, D_MODEL, tn), lambda l, j: (l, 0, j)),
            pl.BlockSpec((1, 1, tn), lambda l, j: (l, 0, j)),
        ],
        out_specs=pl.BlockSpec((1, 8, tn), lambda l, j: (l, 0, j)),
        out_shape=jax.ShapeDtypeStruct((DEPTH, 8, n3), F32),
        name="modulation",
    )(cv8, w_ada, b_ada.reshape(DEPTH, 1, n3))


def _inproj_kernel(x_ref, sh_ref, sc_ref, g_ref, w_ref, o_ref, h_ref):
    @pl.when(pl.program_id(1) == 0)
    def _():
        h_ref[...] = _rms_mod(x_ref[...], g_ref[...], sc_ref[0], sh_ref[0]).astype(BF16)

    o_ref[...] = _dot(h_ref[...], w_ref[...]).astype(o_ref.dtype)


def _inproj(x2d, shift, scale, g, w, *, rows_per_cond, cond0, out_dtype, tm=1024, tn=512):
    m = x2d.shape[0]
    width = w.shape[1]
    tn = next(t for t in (tn, 640, 256, 128) if width % t == 0)
    assert m % tm == 0 and width % tn == 0 and rows_per_cond % tm == 0
    tiles_per_cond = rows_per_cond // tm
    cond = lambda i, j: (cond0 + i // tiles_per_cond, 0, 0)
    return pl.pallas_call(
        _inproj_kernel,
        grid=(m // tm, width // tn),
        in_specs=[
            pl.BlockSpec((tm, D_MODEL), lambda i, j: (i, 0)),
            pl.BlockSpec((1, 1, D_MODEL), cond),
            pl.BlockSpec((1, 1, D_MODEL), cond),
            pl.BlockSpec((1, D_MODEL), lambda i, j: (0, 0)),
            pl.BlockSpec((D_MODEL, tn), lambda i, j: (0, j)),
        ],
        out_specs=pl.BlockSpec((tm, tn), lambda i, j: (i, j)),
        out_shape=jax.ShapeDtypeStruct((m, width), out_dtype),
        scratch_shapes=[pltpu.VMEM((tm, D_MODEL), BF16)],
        compiler_params=pltpu.CompilerParams(dimension_semantics=("parallel", "arbitrary")),
        name="inproj",
    )(x2d, shift, scale, g, w)


def _out_kernel(x_ref, sh_ref, sc_ref, gt_ref, g_ref, fg_ref, oa_ref, ow_ref, of_ref, on_ref,
                wg0_ref, wg1_ref, wg2_ref, wg3_ref, wb_ref, wo_ref, y_ref, h_ref, acc_ref, *, final):
    j = pl.program_id(1)

    @pl.when(j == 0)
    def _():
        h_ref[...] = _rms_mod(x_ref[...], g_ref[...], sc_ref[0], sh_ref[0]).astype(BF16)
        acc_ref[...] = jnp.zeros_like(acc_ref)

    h = h_ref[...]
    merged = None
    for i, (o_ref, wg_ref) in enumerate(((oa_ref, wg0_ref), (ow_ref, wg1_ref), (of_ref, wg2_ref), (on_ref, wg3_ref))):
        term = jax.nn.sigmoid(_dot(h, wg_ref[...])) * _dot(o_ref[...], wb_ref[i])
        merged = term if merged is None else merged + term
    acc_ref[...] += _dot(merged.astype(BF16), wo_ref[...])

    @pl.when(j == pl.num_programs(1) - 1)
    def _():
        xn = x_ref[...] + gt_ref[0] * acc_ref[...]
        if final:
            ms = jnp.mean(xn * xn, axis=-1, keepdims=True)
            xn = xn * lax.rsqrt(ms + EPS) * fg_ref[...]
        y_ref[...] = xn


def _out_stage(x2d, shift, scale, gate, g, final_g, o_arrays, o_cols, w_gate, w_branch, w_out,
               *, rows_per_cond, cond0, final, tm=512, tc=512):
    m = x2d.shape[0]
    assert m % tm == 0 and rows_per_cond % tm == 0
    tiles_per_cond = rows_per_cond // tm
    ncol = D_MODEL // tc
    cond = lambda i, j: (cond0 + i // tiles_per_cond, 0, 0)
    o_specs = [pl.BlockSpec((tm, BRANCH_W), functools.partial(lambda i, j, c: (i, c), c=c)) for c in o_cols]
    wg_specs = [pl.BlockSpec((D_MODEL, tc), functools.partial(lambda i, j, b: (0, b * ncol + j), b=b))
                for b in range(N_BRANCH)]
    return pl.pallas_call(
        functools.partial(_out_kernel, final=final),
        grid=(m // tm, ncol),
        in_specs=[
            pl.BlockSpec((tm, D_MODEL), lambda i, j: (i, 0)),
            pl.BlockSpec((1, 1, D_MODEL), cond),
            pl.BlockSpec((1, 1, D_MODEL), cond),
            pl.BlockSpec((1, 1, D_MODEL), cond),
            pl.BlockSpec((1, D_MODEL), lambda i, j: (0, 0)),
            pl.BlockSpec((1, D_MODEL), lambda i, j: (0, 0)),
            *o_specs,
            *wg_specs,
            pl.BlockSpec((N_BRANCH, BRANCH_W, tc), lambda i, j: (0, 0, j)),
            pl.BlockSpec((tc, D_MODEL), lambda i, j: (j, 0)),
        ],
        out_specs=pl.BlockSpec((tm, D_MODEL), lambda i, j: (i, 0)),
        out_shape=jax.ShapeDtypeStruct((m, D_MODEL), F32),
        scratch_shapes=[pltpu.VMEM((tm, D_MODEL), BF16), pltpu.VMEM((tm, D_MODEL), F32)],
        compiler_params=pltpu.CompilerParams(dimension_semantics=("parallel", "arbitrary")),
        name="out_stage",
    )(x2d, shift, scale, gate, g, final_g, *o_arrays, w_gate, w_gate, w_gate, w_gate, w_branch, w_out)


def _softmax_av(s_list, v_list, extra_logit=None):
    m = s_list[0].max(axis=-1, keepdims=True)
    for s in s_list[1:]:
        m = jnp.maximum(m, s.max(axis=-1, keepdims=True))
    if extra_logit is not None:
        m = jnp.maximum(m, extra_logit)
    den = None if extra_logit is None else jnp.exp(extra_logit - m)
    acc = None
    for s, v in zip(s_list, v_list):
        e = jnp.exp(s - m)
        es = e.sum(axis=-1, keepdims=True)
        den = es if den is None else den + es
        pv = _dot(e.astype(BF16), v)
        acc = pv if acc is None else acc + pv
    return acc / den


def _shift_rows(z, first_row, last_row):
    n = z.shape[0]
    row = lax.broadcasted_iota(jnp.int32, z.shape, 0)
    z_dn = jnp.where(row == 0, first_row, pltpu.roll(z, 1, axis=0))
    z_up = jnp.where(row == n - 1, last_row, pltpu.roll(z, n - 1, axis=0))
    return z_dn, z_up


def _ctx_mixer_kernel(pa_ref, pb_ref, pc_ref, pd_ref, cw_ref, sink_ref, ct_ref, st_ref, cb_ref, sb_ref, o_ref):
    ax = pa_ref[:, 0:512]
    ab = pa_ref[:, 512:1024]
    ac = pa_ref[:, 1024:1536]
    ag = pa_ref[:, 1536:2048]
    z = ac * ax
    zero_row = jnp.zeros((1, BRANCH_W), F32)
    z_dn, z_up = _shift_rows(z, zero_row, zero_row)
    o_a = ab * (z_dn * cw_ref[0:1, :] + z * cw_ref[1:2, :] + z_up * cw_ref[2:3, :]) * _silu(ag)
    o_ref[:, 0:512] = o_a.astype(o_ref.dtype)

    heads = []
    for j in range(WIN_KV_HEADS):
        k = pb_ref[:, 1024 + j * 64:1024 + (j + 1) * 64].astype(BF16)
        v = pb_ref[:, 1152 + j * 64:1152 + (j + 1) * 64].astype(BF16)
        for gq in range(WIN_HEADS // WIN_KV_HEADS):
            h = j * (WIN_HEADS // WIN_KV_HEADS) + gq
            q = pb_ref[:, h * 64:(h + 1) * 64].astype(BF16)
            s = _dot_nt(q, k) * ATTN_SCALE
            heads.append(_softmax_av([s], [v], extra_logit=sink_ref[h:h + 1, 0:1]))
    o_w = jnp.concatenate(heads, axis=-1) * _silu(pb_ref[:, 512:1024])
    o_ref[:, 512:1024] = o_w.astype(o_ref.dtype)

    u = pc_ref[:, 0:512].astype(BF16)
    uc = _dot(u, cb_ref[...]).astype(BF16)
    us = _dot(u, sb_ref[...]).astype(BF16)
    o_f = (_dot(ct_ref[...], uc) - _dot(st_ref[...], us)) * _silu(pc_ref[:, 512:1024])
    o_ref[:, 1024:1536] = o_f.astype(o_ref.dtype)

    heads = []
    for h in range(NA_HEADS):
        q = pd_ref[:, h * 64:(h + 1) * 64].astype(BF16)
        k = pd_ref[:, 512 + h * 64:512 + (h + 1) * 64].astype(BF16)
        v = pd_ref[:, 1024 + h * 64:1024 + (h + 1) * 64].astype(BF16)
        heads.append(_softmax_av([_dot_nt(q, k) * ATTN_SCALE], [v]))
    o_n = jnp.concatenate(heads, axis=-1) * _silu(pd_ref[:, 1536:2048])
    o_ref[:, 1536:2048] = o_n.astype(o_ref.dtype)


def _ctx_mixers(pa, pb, pc, pd, conv_w, sink_b, ct, st, cb, sb, *, seq):
    m = pa.shape[0]
    whole = lambda a: pl.BlockSpec(a.shape, lambda b: (0,) * a.ndim)
    row = lambda a: pl.BlockSpec((seq, a.shape[1]), lambda b: (b, 0))
    return pl.pallas_call(
        _ctx_mixer_kernel,
        grid=(m // seq,),
        in_specs=[row(pa), row(pb), row(pc), row(pd), whole(conv_w), whole(sink_b),
                  whole(ct), whole(st), whole(cb), whole(sb)],
        out_specs=pl.BlockSpec((seq, N_BRANCH * BRANCH_W), lambda b: (b, 0)),
        out_shape=jax.ShapeDtypeStruct((m, N_BRANCH * BRANCH_W), BF16),
        compiler_params=pltpu.CompilerParams(dimension_semantics=("parallel",)),
        name="ctx_mixers",
    )(pa, pb, pc, pd, conv_w, sink_b, ct, st, cb, sb)


HALO = 16


def _lat_conv_kernel(ax_ref, ab_ref, ac_ref, ag_ref, axp_ref, acp_ref, axn_ref, acn_ref, cw_ref, o_ref):
    i = pl.program_id(1)
    z = ac_ref[...].astype(F32) * ax_ref[...].astype(F32)
    zp = acp_ref[HALO - 1:HALO, :].astype(F32) * axp_ref[HALO - 1:HALO, :].astype(F32)
    zn = acn_ref[0:1, :].astype(F32) * axn_ref[0:1, :].astype(F32)
    zp = jnp.where(i == 0, 0.0, zp)
    zn = jnp.where(i == pl.num_programs(1) - 1, 0.0, zn)
    z_dn, z_up = _shift_rows(z, zp, zn)
    y = ab_ref[...].astype(F32) * (z_dn * cw_ref[0:1, :] + z * cw_ref[1:2, :] + z_up * cw_ref[2:3, :])
    o_ref[...] = (y * _silu(ag_ref[...].astype(F32))).astype(o_ref.dtype)


def _lat_conv(pa, conv_w, *, batch, seq, tr=512):
    nt = seq // tr
    hb = tr // HALO
    last_halo = batch * seq // HALO - 1
    main = lambda c: pl.BlockSpec((tr, BRANCH_W), functools.partial(lambda b, i, c: (b * nt + i, c), c=c))
    prev = lambda c: pl.BlockSpec(
        (HALO, BRANCH_W), functools.partial(lambda b, i, c: (jnp.maximum((b * nt + i) * hb - 1, 0), c), c=c))
    nxt = lambda c: pl.BlockSpec(
        (HALO, BRANCH_W), functools.partial(lambda b, i, c: (jnp.minimum((b * nt + i + 1) * hb, last_halo), c), c=c))
    return pl.pallas_call(
        _lat_conv_kernel,
        grid=(batch, nt),
        in_specs=[main(0), main(1), main(2), main(3), prev(0), prev(2), nxt(0), nxt(2),
                  pl.BlockSpec(conv_w.shape, lambda b, i: (0, 0))],
        out_specs=pl.BlockSpec((tr, BRANCH_W), lambda b, i: (b * nt + i, 0)),
        out_shape=jax.ShapeDtypeStruct((batch * seq, BRANCH_W), BF16),
        compiler_params=pltpu.CompilerParams(dimension_semantics=("parallel", "arbitrary")),
        name="lat_conv",
    )(pa, pa, pa, pa, pa, pa, pa, pa, conv_w)


def _rope(x, cos, sin_signed):
    lane = lax.broadcasted_iota(jnp.int32, x.shape, 1)
    partner = jnp.where((lane % 32) < 16, pltpu.roll(x, 128 - 16, axis=1), pltpu.roll(x, 16, axis=1))
    return x * cos + partner * sin_signed


def _lat_win_kernel(q_ref, g_ref, kp_ref, kc_ref, kn_ref, vp_ref, vc_ref, vn_ref,
                    cq_ref, sq_ref, cp_ref, sp_ref, cn_ref, sn_ref, ck_ref, cv_ref, sink_ref, o_ref):
    n = pl.program_id(1)
    nb = pl.num_programs(1)
    wb = WIN_BLOCK
    cq, sq = cq_ref[...], sq_ref[...]
    k_rot = jnp.concatenate([
        _rope(kp_ref[...].astype(F32), cp_ref[...], sp_ref[...]),
        _rope(kc_ref[...].astype(F32), cq, sq),
        _rope(kn_ref[...].astype(F32), cn_ref[...], sn_ref[...]),
    ], axis=0).astype(BF16)
    v_all = jnp.concatenate([vp_ref[...], vc_ref[...], vn_ref[...]], axis=0).astype(BF16)
    qi = lax.broadcasted_iota(jnp.int32, (wb, 3 * wb), 0)
    kj = lax.broadcasted_iota(jnp.int32, (wb, 3 * wb), 1)
    mask = (((kj < wb) & (kj >= qi) & (n > 0)) | ((kj >= wb) & (kj < 2 * wb))
            | ((kj >= 2 * wb) & (kj - 2 * wb <= qi) & (n < nb - 1)))
    ck = ck_ref[0].astype(BF16)
    cv = cv_ref[0].astype(BF16)
    gsz = WIN_HEADS // WIN_KV_HEADS
    heads = []
    for pair in range(WIN_HEADS // 2):
        q2 = _rope(q_ref[:, pair * 128:(pair + 1) * 128].astype(F32), cq, sq).astype(BF16)
        for half in range(2):
            h = 2 * pair + half
            j = h // gsz
            q = q2[:, half * 64:(half + 1) * 64]
            s_loc = jnp.where(mask, _dot_nt(q, k_rot[:, j * 64:(j + 1) * 64]) * ATTN_SCALE, NEG_INF)
            s_ctx = _dot_nt(q, ck[:, j * 64:(j + 1) * 64]) * ATTN_SCALE
            heads.append(_softmax_av([s_loc, s_ctx],
                                     [v_all[:, j * 64:(j + 1) * 64], cv[:, j * 64:(j + 1) * 64]],
                                     extra_logit=sink_ref[h:h + 1, 0:1]))
    o = jnp.concatenate(heads, axis=-1) * _silu(g_ref[...].astype(F32))
    o_ref[...] = o.astype(o_ref.dtype)


def _lat_window(pb, cos_t, sin_t, ctx_k, ctx_v, sink_b, *, batch, seq):
    wb = WIN_BLOCK
    nb = seq // wb
    blk = lambda b, n: b * nb + n
    kv = lambda col, d: pl.BlockSpec(
        (wb, KV_W), functools.partial(lambda b, n, col, d: (blk(b, jnp.clip(n + d, 0, nb - 1)), col), col=col, d=d))
    tab = lambda d: pl.BlockSpec((wb, 128), functools.partial(lambda b, n, d: (jnp.clip(n + d, 0, nb - 1), 0), d=d))
    ctx = pl.BlockSpec((1,) + ctx_k.shape[1:], lambda b, n: (b, 0, 0))
    return pl.pallas_call(
        _lat_win_kernel,
        grid=(batch, nb),
        in_specs=[
            pl.BlockSpec((wb, BRANCH_W), lambda b, n: (blk(b, n), 0)),
            pl.BlockSpec((wb, BRANCH_W), lambda b, n: (blk(b, n), 1)),
            kv(8, -1), kv(8, 0), kv(8, 1), kv(9, -1), kv(9, 0), kv(9, 1),
            tab(0), tab(0), tab(-1), tab(-1), tab(1), tab(1),
            ctx, ctx,
            pl.BlockSpec(sink_b.shape, lambda b, n: (0, 0)),
        ],
        out_specs=pl.BlockSpec((wb, BRANCH_W), lambda b, n: (blk(b, n), 0)),
        out_shape=jax.ShapeDtypeStruct((batch * seq, BRANCH_W), BF16),
        compiler_params=pltpu.CompilerParams(dimension_semantics=("parallel", "arbitrary")),
        name="lat_window",
    )(pb, pb, pb, pb, pb, pb, pb, pb, cos_t, sin_t, cos_t, sin_t, cos_t, sin_t, ctx_k, ctx_v, sink_b)


def _fft1_kernel(u_ref, c_ref, s_ref, twr_ref, twi_ref, zr_ref, zi_ref, *, chunks):
    u = u_ref[0].astype(BF16)
    yr = _dot(c_ref[...], u)
    yi = -_dot(s_ref[...], u)
    for t in range(chunks):
        wr = twr_ref[0, :, t:t + 1]
        wi = twi_ref[0, :, t:t + 1]
        a = yr[:, t * 512:(t + 1) * 512]
        b = yi[:, t * 512:(t + 1) * 512]
        zr_ref[0, :, t * 512:(t + 1) * 512] = (a * wr - b * wi).astype(zr_ref.dtype)
        zi_ref[0, :, t * 512:(t + 1) * 512] = (a * wi + b * wr).astype(zi_ref.dtype)


def _fft2_kernel(zr_ref, zi_ref, g_ref, l_ref, cs_ref, o_ref, *, kblock):
    r = FFT_R
    for kk in range(kblock):
        zz = jnp.concatenate([zr_ref[0, kk], zi_ref[0, kk]], axis=0)
        xx = _dot(l_ref[...], zz)
        xcat = jnp.concatenate([xx[0:r], xx[r:2 * r]], axis=1).astype(BF16)
        out = _dot(xcat, cs_ref[...])
        sl = slice(kk * 512, (kk + 1) * 512)
        o_ref[0, :, sl] = (out * _silu(g_ref[0, :, sl].astype(F32))).astype(o_ref.dtype)


def _lat_fourier(fu, fg, tabs, *, batch, seq):
    r = FFT_R
    assert seq == r * r
    wide = r * BRANCH_W
    chunks = 8
    c64, s64, twr, twi, lmat, csmat = tabs
    u3 = fu.reshape(batch, r, wide)
    nct = r // chunks
    zr, zi = pl.pallas_call(
        functools.partial(_fft1_kernel, chunks=chunks),
        grid=(batch, nct),
        in_specs=[
            pl.BlockSpec((1, r, chunks * BRANCH_W), lambda b, t: (b, 0, t)),
            pl.BlockSpec((r, r), lambda b, t: (0, 0)),
            pl.BlockSpec((r, r), lambda b, t: (0, 0)),
            pl.BlockSpec((1, r, chunks), lambda b, t: (t, 0, 0)),
            pl.BlockSpec((1, r, chunks), lambda b, t: (t, 0, 0)),
        ],
        out_specs=[pl.BlockSpec((1, r, chunks * BRANCH_W), lambda b, t: (b, 0, t))] * 2,
        out_shape=[jax.ShapeDtypeStruct((batch, r, wide), BF16)] * 2,
        compiler_params=pltpu.CompilerParams(dimension_semantics=("parallel", "arbitrary")),
        name="lat_fft1",
    )(u3, c64, s64, twr, twi)
    kblock = 8
    z4 = lambda z: z.reshape(batch, r, r, BRANCH_W)
    out = pl.pallas_call(
        functools.partial(_fft2_kernel, kblock=kblock),
        grid=(batch, r // kblock),
        in_specs=[
            pl.BlockSpec((1, kblock, r, BRANCH_W), lambda b, k: (b, k, 0, 0)),
            pl.BlockSpec((1, kblock, r, BRANCH_W), lambda b, k: (b, k, 0, 0)),
            pl.BlockSpec((1, r, kblock * BRANCH_W), lambda b, k: (b, 0, k)),
            pl.BlockSpec((2 * r, 2 * r), lambda b, k: (0, 0)),
            pl.BlockSpec((2 * BRANCH_W, BRANCH_W), lambda b, k: (0, 0)),
        ],
        out_specs=pl.BlockSpec((1, r, kblock * BRANCH_W), lambda b, k: (b, 0, k)),
        out_shape=jax.ShapeDtypeStruct((batch, r, wide), BF16),
        compiler_params=pltpu.CompilerParams(dimension_semantics=("parallel", "arbitrary")),
        name="lat_fft2",
    )(z4(zr), z4(zi), fg.reshape(batch, r, wide), lmat, csmat)
    return out.reshape(batch * seq, BRANCH_W)


def _lat_na_kernel(q_ref, g_ref, k_ref, v_ref, bias_ref, ck_ref, cv_ref, o_ref, *, rows):
    r = pl.program_id(1)
    r0 = jnp.clip(r - NA_ROWS // 2, 0, rows - NA_ROWS)
    start = pl.multiple_of(r0 * GRID_W, GRID_W)
    kw = k_ref[pl.ds(start, NA_ROWS * GRID_W), :].astype(BF16)
    vw = v_ref[pl.ds(start, NA_ROWS * GRID_W), :].astype(BF16)
    ck = ck_ref[0].astype(BF16)
    cv = cv_ref[0].astype(BF16)
    qa = q_ref[...].astype(BF16)
    heads = []
    for h in range(NA_HEADS):
        sl = slice(h * 64, (h + 1) * 64)
        q = qa[:, sl]
        s_nb = _dot_nt(q, kw[:, sl]) * ATTN_SCALE + bias_ref[0, h]
        s_ctx = _dot_nt(q, ck[:, sl]) * ATTN_SCALE
        heads.append(_softmax_av([s_nb, s_ctx], [vw[:, sl], cv[:, sl]]))
    o = jnp.concatenate(heads, axis=-1) * _silu(g_ref[...].astype(F32))
    o_ref[...] = o.astype(o_ref.dtype)


def _lat_na(pd, bias, ctx_k, ctx_v, *, batch, seq):
    rows = seq // GRID_W
    var = lambda b, r: (r - jnp.clip(r - NA_ROWS // 2, 0, rows - NA_ROWS), 0, 0, 0)
    ctx = pl.BlockSpec((1,) + ctx_k.shape[1:], lambda b, r: (b, 0, 0))
    return pl.pallas_call(
        functools.partial(_lat_na_kernel, rows=rows),
        grid=(batch, rows),
        in_specs=[
            pl.BlockSpec((GRID_W, BRANCH_W), lambda b, r: (b * rows + r, 0)),
            pl.BlockSpec((GRID_W, BRANCH_W), lambda b, r: (b * rows + r, 3)),
            pl.BlockSpec((seq, BRANCH_W), lambda b, r: (b, 1)),
            pl.BlockSpec((seq, BRANCH_W), lambda b, r: (b, 2)),
            pl.BlockSpec((1, NA_HEADS, GRID_W, NA_ROWS * GRID_W), var),
            ctx, ctx,
        ],
        out_specs=pl.BlockSpec((GRID_W, BRANCH_W), lambda b, r: (b * rows + r, 0)),
        out_shape=jax.ShapeDtypeStruct((batch * seq, BRANCH_W), BF16),
        compiler_params=pltpu.CompilerParams(dimension_semantics=("parallel", "arbitrary")),
        name="lat_na",
    )(pd, pd, pd, pd, bias, ctx_k, ctx_v)


def _bf16_table(t):
    return jnp.asarray(t, F32).astype(BF16)


def _dft_cos_sin(n):
    idx = np.arange(n)
    ang = 2.0 * np.pi * ((idx[:, None] * idx[None, :]) % n) / n
    return np.cos(ang), np.sin(ang)


def _channel_dft_blocks(scale):
    c, s = _dft_cos_sin(FN_GROUP_W)
    eye = np.eye(FN_GROUPS)
    return np.kron(eye, c) * scale, np.kron(eye, s) * scale


def _ctx_fourier_tables(seq):
    ct, st = _dft_cos_sin(seq)
    cb, sb = _channel_dft_blocks(1.0 / np.sqrt(seq * FN_GROUP_W))
    return tuple(_bf16_table(t) for t in (ct, st, cb, sb))


def _lat_fourier_tables(seq, chunks=8):
    r = FFT_R
    c64, s64 = _dft_cos_sin(r)
    k1 = np.arange(r)[:, None]
    t2 = np.arange(r)[None, :]
    ang = 2.0 * np.pi * (k1 * t2) / (r * r)
    twr = np.cos(ang).reshape(r, r // chunks, chunks).transpose(1, 0, 2)
    twi = (-np.sin(ang)).reshape(r, r // chunks, chunks).transpose(1, 0, 2)
    lmat = np.block([[c64, s64], [-s64, c64]])
    cb, sb = _channel_dft_blocks(1.0 / np.sqrt(seq * FN_GROUP_W))
    csmat = np.concatenate([cb, sb], axis=0)
    return (_bf16_table(c64), _bf16_table(s64), jnp.asarray(twr, F32), jnp.asarray(twi, F32),
            _bf16_table(lmat), _bf16_table(csmat))


def _rope_tables(seq):
    half = HEAD_DIM // 2
    quarter = half // 2
    t = jnp.arange(seq)
    inv = ROPE_BASE ** (-jnp.arange(quarter, dtype=F32) / quarter)

    def cs(pos):
        ang = pos.astype(F32)[:, None] * inv[None, :]
        c, s = jnp.cos(ang), jnp.sin(ang)
        return jnp.concatenate([c, c], axis=-1), jnp.concatenate([-s, s], axis=-1)

    cr, sr = cs(t // GRID_W)
    cc, sc = cs(t % GRID_W)
    cos = jnp.concatenate([cr, cc], axis=-1)
    sin = jnp.concatenate([sr, sc], axis=-1)
    return jnp.tile(cos, (1, 2)), jnp.tile(sin, (1, 2))


def _na_bias(rpb):
    v = np.arange(NA_ROWS)[:, None, None, None]
    i = np.arange(NA_ROWS)[None, None, :, None]
    qc = np.arange(GRID_W)[None, :, None, None]
    kc = np.arange(GRID_W)[None, None, None, :]
    ridx = np.broadcast_to(i - v + NA_ROWS - 1, (NA_ROWS, GRID_W, NA_ROWS, GRID_W))
    cidx = np.broadcast_to(np.clip(kc - qc + NA_COLS - 1, 0, 2 * NA_COLS - 2), (NA_ROWS, GRID_W, NA_ROWS, GRID_W))
    win0 = np.clip(qc - NA_COLS // 2, 0, GRID_W - NA_COLS)
    vis = np.broadcast_to((kc >= win0) & (kc < win0 + NA_COLS), (NA_ROWS, GRID_W, NA_ROWS, GRID_W))
    b = rpb[:, ridx, cidx]
    b = jnp.where(vis[None], b, NEG_INF)
    return b.transpose(1, 0, 2, 3, 4).reshape(NA_ROWS, NA_HEADS, GRID_W, NA_ROWS * GRID_W)


def kernel(x_prompt, x_sample, cache_win_k, cache_win_v, cache_na_k, cache_na_v, c, c_ctx, norm_g, w_ada, b_ada,
           w_in, conv_w, win_sink, na_rpb, w_branch, w_out, final_g):
    batch, seq, _ = x_prompt.shape
    dbatch, dseq, _ = x_sample.shape
    past = cache_win_k.shape[2]

    cv8 = jnp.zeros((8, D_MODEL), F32).at[0].set(c_ctx).at[1:1 + dbatch].set(c)
    mod = _modulation(cv8, w_ada, b_ada)

    ctx_tabs = _ctx_fourier_tables(seq)
    lat_tabs = _lat_fourier_tables(dseq)
    cos_t, sin_t = _rope_tables(dseq)
    final_g2 = final_g.reshape(1, D_MODEL)

    xp = x_prompt.reshape(batch * seq, D_MODEL)
    xs = x_sample.reshape(dbatch * dseq, D_MODEL)
    new_kv = [[], [], [], []]

    for l in range(DEPTH):
        shift = mod[l, :, 0:D_MODEL].reshape(8, 1, D_MODEL)
        scale = mod[l, :, D_MODEL:2 * D_MODEL].reshape(8, 1, D_MODEL)
        gate = mod[l, :, 2 * D_MODEL:].reshape(8, 1, D_MODEL)
        g = norm_g[l].reshape(1, D_MODEL)
        wl = w_in[l]
        w_a = wl[:, OFF_A:OFF_BQ].astype(BF16)
        w_b = jnp.concatenate([wl[:, OFF_BQ:OFF_BK], wl[:, OFF_BG:OFF_FU], wl[:, OFF_BK:OFF_BG]], axis=1).astype(BF16)
        w_fu = wl[:, OFF_FU:OFF_FG].astype(BF16)
        w_fg = wl[:, OFF_FG:OFF_D].astype(BF16)
        w_c = wl[:, OFF_FU:OFF_D].astype(BF16)
        w_d = wl[:, OFF_D:N_MAIN].astype(BF16)
        w_gate = wl[:, N_MAIN:].astype(BF16)
        w_br = w_branch[l].astype(BF16)
        w_o = w_out[l].astype(BF16)
        cw = conv_w[l]
        sink_b = jnp.broadcast_to(win_sink[l][:, None], (WIN_HEADS, 128))
        final = l == DEPTH - 1

        ip = functools.partial(_inproj, xp, shift, scale, g, rows_per_cond=batch * seq, cond0=0)
        pa = ip(w_a, out_dtype=F32)
        pb = ip(w_b, out_dtype=F32)
        pc = ip(w_c, out_dtype=F32)
        pd = ip(w_d, out_dtype=F32)
        new_kv[0].append(pb[:, 1024:1152].reshape(batch, seq, WIN_KV_HEADS, HEAD_DIM))
        new_kv[1].append(pb[:, 1152:1280].reshape(batch, seq, WIN_KV_HEADS, HEAD_DIM))
        new_kv[2].append(pd[:, 512:1024].reshape(batch, seq, NA_HEADS, HEAD_DIM))
        new_kv[3].append(pd[:, 1024:1536].reshape(batch, seq, NA_HEADS, HEAD_DIM))
        o_ctx = _ctx_mixers(pa, pb, pc, pd, cw, sink_b, *ctx_tabs, seq=seq)
        xp = _out_stage(xp, shift, scale, gate, g, final_g2, [o_ctx] * 4, [0, 1, 2, 3], w_gate, w_br, w_o,
                        rows_per_cond=batch * seq, cond0=0, final=final)

        ip = functools.partial(_inproj, xs, shift, scale, g, rows_per_cond=dseq, cond0=1, out_dtype=BF16)
        qa, qb, qfu, qfg, qd = ip(w_a), ip(w_b), ip(w_fu), ip(w_fg), ip(w_d)
        o_a = _lat_conv(qa, cw, batch=dbatch, seq=dseq)
        o_w = _lat_window(qb, cos_t, sin_t,
                          cache_win_k[:, l].reshape(dbatch, past, KV_W), cache_win_v[:, l].reshape(dbatch, past, KV_W),
                          sink_b, batch=dbatch, seq=dseq)
        o_f = _lat_fourier(qfu, qfg, lat_tabs, batch=dbatch, seq=dseq)
        o_n = _lat_na(qd, _na_bias(na_rpb[l]),
                      cache_na_k[:, l].reshape(dbatch, past, BRANCH_W), cache_na_v[:, l].reshape(dbatch, past, BRANCH_W),
                      batch=dbatch, seq=dseq)
        xs = _out_stage(xs, shift, scale, gate, g, final_g2, [o_a, o_w, o_f, o_n], [0, 0, 0, 0], w_gate, w_br, w_o,
                        rows_per_cond=dseq, cond0=1, final=final)

    y_prompt = xp.reshape(batch, seq, D_MODEL)
    y_sample = xs.reshape(dbatch, dseq, D_MODEL)
    return (y_prompt, y_sample) + tuple(jnp.stack(t, axis=1) for t in new_kv)
```

```python
import functools

import numpy as np
import jax
import jax.numpy as jnp
from jax import lax
from jax.experimental import pallas as pl
from jax.experimental.pallas import tpu as pltpu

D_MODEL = 2048
DEPTH = 2
GRID_W = 64
N_BRANCH = 4
BRANCH_W = 512
HEAD_DIM = 64
WIN_HEADS = 8
WIN_KV_HEADS = 2
KV_W = WIN_KV_HEADS * HEAD_DIM
WIN_BLOCK = 128
FN_GROUPS = 4
FN_GROUP_W = BRANCH_W // FN_GROUPS
NA_HEADS = 8
NA_ROWS = 8
NA_COLS = 16
ROPE_BASE = 10000.0
EPS = 1e-6
ATTN_SCALE = HEAD_DIM ** -0.5
NEG_INF = -1e30
N_MAIN = 6400

F32 = jnp.float32
BF16 = jnp.bfloat16

OFF_A = 0
OFF_BQ, OFF_BK, OFF_BV, OFF_BG = 2048, 2560, 2688, 2816
OFF_FU, OFF_FG = 3328, 3840
OFF_DQ, OFF_DK, OFF_DV, OFF_DG = 4352, 4864, 5376, 5888

TILE = 512
T_AX, T_AB, T_AC, T_AG, T_BQ, T_BG, T_BKV, T_FU, T_FG, T_DQ, T_DK, T_DV, T_DG = range(13)
N_TILES = 13
P_WIDTH = N_TILES * TILE

FFT_R = 64


def _silu(x):
    return x * jax.nn.sigmoid(x)


def _dot(a, b):
    return jnp.dot(a, b, preferred_element_type=F32)


def _dot_nt(a, b):
    return lax.dot_general(a, b, (((1,), (1,)), ((), ())), preferred_element_type=F32)


def _rms_mod(x, g, scale, shift):
    ms = jnp.mean(x * x, axis=-1, keepdims=True)
    return (x * lax.rsqrt(ms + EPS) * g) * (1.0 + scale) + shift


def _mod_kernel(cv_ref, w_ref, b_ref, o_ref):
    s = _silu(cv_ref[...]).astype(BF16)
    o_ref[0] = _dot(s, w_ref[0].astype(BF16)) + b_ref[0]


def _modulation(cv8, w_ada, b_ada):
    tn = 512
    n3 = 3 * D_MODEL
    return pl.pallas_call(
        _mod_kernel,
        grid=(DEPTH, n3 // tn),
        in_specs=[
            pl.BlockSpec((8, D_MODEL), lambda l, j: (0, 0)),
            pl.BlockSpec((1, D_MODEL, tn), lambda l, j: (l, 0, j)),
            pl.BlockSpec((1, 1, tn), lambda l, j: (l, 0, j)),
        ],
        out_specs=pl.BlockSpec((1, 8, tn), lambda l, j: (l, 0, j)),
        out_shape=jax.ShapeDtypeStruct((DEPTH, 8, n3), F32),
        name="modulation",
    )(cv8, w_ada, b_ada.reshape(DEPTH, 1, n3))


def _inproj_kernel(x_ref, sh_ref, sc_ref, g_ref, w_ref, p_ref, *rest, extra_tiles):
    extra_refs, h_ref = rest[:-1], rest[-1]
    j = pl.program_id(1)

    @pl.when(j == 0)
    def _():
        h_ref[...] = _rms_mod(x_ref[...], g_ref[...], sc_ref[0], sh_ref[0]).astype(BF16)

    acc = _dot(h_ref[...], w_ref[...])
    p_ref[...] = acc.astype(p_ref.dtype)
    for ref, tile in zip(extra_refs, extra_tiles):
        @pl.when(j == tile)
        def _(ref=ref):
            ref[...] = acc.astype(ref.dtype)


def _inproj(x2d, shift, scale, g, w, *, rows_per_cond, cond0, extra_tiles, extra_dtype, tm=1024):
    m = x2d.shape[0]
    assert w.shape == (D_MODEL, P_WIDTH) and m % tm == 0 and rows_per_cond % tm == 0
    tiles_per_cond = rows_per_cond // tm
    cond = lambda i, j: (cond0 + i // tiles_per_cond, 0, 0)
    return pl.pallas_call(
        functools.partial(_inproj_kernel, extra_tiles=tuple(extra_tiles)),
        grid=(m // tm, N_TILES),
        in_specs=[
            pl.BlockSpec((tm, D_MODEL), lambda i, j: (i, 0)),
            pl.BlockSpec((1, 1, D_MODEL), cond),
            pl.BlockSpec((1, 1, D_MODEL), cond),
            pl.BlockSpec((1, D_MODEL), lambda i, j: (0, 0)),
            pl.BlockSpec((D_MODEL, TILE), lambda i, j: (0, j)),
        ],
        out_specs=[pl.BlockSpec((tm, TILE), lambda i, j: (i, j))]
        + [pl.BlockSpec((tm, TILE), lambda i, j: (i, 0)) for _ in extra_tiles],
        out_shape=[jax.ShapeDtypeStruct((m, P_WIDTH), BF16)]
        + [jax.ShapeDtypeStruct((m, TILE), extra_dtype) for _ in extra_tiles],
        scratch_shapes=[pltpu.VMEM((tm, D_MODEL), BF16)],
        compiler_params=pltpu.CompilerParams(dimension_semantics=("parallel", "arbitrary")),
        name="inproj",
    )(x2d, shift, scale, g, w)


def _out_kernel(x_ref, sh_ref, sc_ref, gt_ref, g_ref, fg_ref, oa_ref, ow_ref, of_ref, on_ref,
                wg0_ref, wg1_ref, wg2_ref, wg3_ref, wb_ref, wo_ref, y_ref, h_ref, acc_ref, *, final):
    j = pl.program_id(1)

    @pl.when(j == 0)
    def _():
        h_ref[...] = _rms_mod(x_ref[...], g_ref[...], sc_ref[0], sh_ref[0]).astype(BF16)
        acc_ref[...] = jnp.zeros_like(acc_ref)

    h = h_ref[...]
    merged = None
    for i, (o_ref, wg_ref) in enumerate(((oa_ref, wg0_ref), (ow_ref, wg1_ref), (of_ref, wg2_ref), (on_ref, wg3_ref))):
        term = jax.nn.sigmoid(_dot(h, wg_ref[...])) * _dot(o_ref[...], wb_ref[i])
        merged = term if merged is None else merged + term
    acc_ref[...] += _dot(merged.astype(BF16), wo_ref[...])

    @pl.when(j == pl.num_programs(1) - 1)
    def _():
        xn = x_ref[...] + gt_ref[0] * acc_ref[...]
        if final:
            ms = jnp.mean(xn * xn, axis=-1, keepdims=True)
            xn = xn * lax.rsqrt(ms + EPS) * fg_ref[...]
        y_ref[...] = xn


def _out_stage(x2d, shift, scale, gate, g, final_g, o_arrays, o_cols, w_gate, w_branch, w_out,
               *, rows_per_cond, cond0, final, tm=512, tc=512):
    m = x2d.shape[0]
    assert m % tm == 0 and rows_per_cond % tm == 0
    tiles_per_cond = rows_per_cond // tm
    ncol = D_MODEL // tc
    cond = lambda i, j: (cond0 + i // tiles_per_cond, 0, 0)
    o_specs = [pl.BlockSpec((tm, BRANCH_W), functools.partial(lambda i, j, c: (i, c), c=c)) for c in o_cols]
    wg_specs = [pl.BlockSpec((D_MODEL, tc), functools.partial(lambda i, j, b: (0, b * ncol + j), b=b))
                for b in range(N_BRANCH)]
    return pl.pallas_call(
        functools.partial(_out_kernel, final=final),
        grid=(m // tm, ncol),
        in_specs=[
            pl.BlockSpec((tm, D_MODEL), lambda i, j: (i, 0)),
            pl.BlockSpec((1, 1, D_MODEL), cond),
            pl.BlockSpec((1, 1, D_MODEL), cond),
            pl.BlockSpec((1, 1, D_MODEL), cond),
            pl.BlockSpec((1, D_MODEL), lambda i, j: (0, 0)),
            pl.BlockSpec((1, D_MODEL), lambda i, j: (0, 0)),
            *o_specs,
            *wg_specs,
            pl.BlockSpec((N_BRANCH, BRANCH_W, tc), lambda i, j: (0, 0, j)),
            pl.BlockSpec((tc, D_MODEL), lambda i, j: (j, 0)),
        ],
        out_specs=pl.BlockSpec((tm, D_MODEL), lambda i, j: (i, 0)),
        out_shape=jax.ShapeDtypeStruct((m, D_MODEL), F32),
        scratch_shapes=[pltpu.VMEM((tm, D_MODEL), BF16), pltpu.VMEM((tm, D_MODEL), F32)],
        compiler_params=pltpu.CompilerParams(dimension_semantics=("parallel", "arbitrary")),
        name="out_stage",
    )(x2d, shift, scale, gate, g, final_g, *o_arrays, w_gate, w_gate, w_gate, w_gate, w_branch, w_out)


def _softmax_av(s_list, v_list, extra_logit=None):
    m = s_list[0].max(axis=-1, keepdims=True)
    for s in s_list[1:]:
        m = jnp.maximum(m, s.max(axis=-1, keepdims=True))
    if extra_logit is not None:
        m = jnp.maximum(m, extra_logit)
    den = None if extra_logit is None else jnp.exp(extra_logit - m)
    acc = None
    for s, v in zip(s_list, v_list):
        e = jnp.exp(s - m)
        es = e.sum(axis=-1, keepdims=True)
        den = es if den is None else den + es
        pv = _dot(e.astype(BF16), v)
        acc = pv if acc is None else acc + pv
    return acc / den


def _shift_rows(z, first_row, last_row):
    n = z.shape[0]
    row = lax.broadcasted_iota(jnp.int32, z.shape, 0)
    z_dn = jnp.where(row == 0, first_row, pltpu.roll(z, 1, axis=0))
    z_up = jnp.where(row == n - 1, last_row, pltpu.roll(z, n - 1, axis=0))
    return z_dn, z_up


def _tile(ref, t, dtype=None):
    v = ref[:, t * TILE:(t + 1) * TILE]
    return v if dtype is None else v.astype(dtype)


def _head(ref, t, h, base=0):
    lo = t * TILE + base + h * HEAD_DIM
    return ref[:, lo:lo + HEAD_DIM]


def _ctx_mixer_kernel(p_ref, cw_ref, sink_ref, ct_ref, st_ref, cb_ref, sb_ref, o_ref):
    z = _tile(p_ref, T_AC, F32) * _tile(p_ref, T_AX, F32)
    zero_row = jnp.zeros((1, BRANCH_W), F32)
    z_dn, z_up = _shift_rows(z, zero_row, zero_row)
    y = _tile(p_ref, T_AB, F32) * (z_dn * cw_ref[0:1, :] + z * cw_ref[1:2, :] + z_up * cw_ref[2:3, :])
    o_ref[:, 0:512] = (y * _silu(_tile(p_ref, T_AG, F32))).astype(o_ref.dtype)

    gsz = WIN_HEADS // WIN_KV_HEADS
    heads = []
    for h in range(WIN_HEADS):
        k = _head(p_ref, T_BKV, h // gsz)
        v = _head(p_ref, T_BKV, h // gsz, base=KV_W)
        s = _dot_nt(_head(p_ref, T_BQ, h), k) * ATTN_SCALE
        heads.append(_softmax_av([s], [v], extra_logit=sink_ref[h:h + 1, 0:1]))
    o_w = jnp.concatenate(heads, axis=-1) * _silu(_tile(p_ref, T_BG, F32))
    o_ref[:, 512:1024] = o_w.astype(o_ref.dtype)

    u = _tile(p_ref, T_FU)
    uc = _dot(u, cb_ref[...]).astype(BF16)
    us = _dot(u, sb_ref[...]).astype(BF16)
    o_f = (_dot(ct_ref[...], uc) - _dot(st_ref[...], us)) * _silu(_tile(p_ref, T_FG, F32))
    o_ref[:, 1024:1536] = o_f.astype(o_ref.dtype)

    heads = []
    for h in range(NA_HEADS):
        s = _dot_nt(_head(p_ref, T_DQ, h), _head(p_ref, T_DK, h)) * ATTN_SCALE
        heads.append(_softmax_av([s], [_head(p_ref, T_DV, h)]))
    o_n = jnp.concatenate(heads, axis=-1) * _silu(_tile(p_ref, T_DG, F32))
    o_ref[:, 1536:2048] = o_n.astype(o_ref.dtype)


def _ctx_mixers(p, conv_w, sink_b, ct, st, cb, sb, *, seq):
    m = p.shape[0]
    whole = lambda a: pl.BlockSpec(a.shape, lambda b: (0,) * a.ndim)
    return pl.pallas_call(
        _ctx_mixer_kernel,
        grid=(m // seq,),
        in_specs=[pl.BlockSpec((seq, P_WIDTH), lambda b: (b, 0)), whole(conv_w), whole(sink_b),
                  whole(ct), whole(st), whole(cb), whole(sb)],
        out_specs=pl.BlockSpec((seq, N_BRANCH * BRANCH_W), lambda b: (b, 0)),
        out_shape=jax.ShapeDtypeStruct((m, N_BRANCH * BRANCH_W), BF16),
        compiler_params=pltpu.CompilerParams(dimension_semantics=("parallel",)),
        name="ctx_mixers",
    )(p, conv_w, sink_b, ct, st, cb, sb)


HALO = 16


def _lat_conv_kernel(ax_ref, ab_ref, ac_ref, ag_ref, axp_ref, acp_ref, axn_ref, acn_ref, cw_ref, o_ref):
    i = pl.program_id(1)
    z = ac_ref[...].astype(F32) * ax_ref[...].astype(F32)
    zp = acp_ref[HALO - 1:HALO, :].astype(F32) * axp_ref[HALO - 1:HALO, :].astype(F32)
    zn = acn_ref[0:1, :].astype(F32) * axn_ref[0:1, :].astype(F32)
    zp = jnp.where(i == 0, 0.0, zp)
    zn = jnp.where(i == pl.num_programs(1) - 1, 0.0, zn)
    z_dn, z_up = _shift_rows(z, zp, zn)
    y = ab_ref[...].astype(F32) * (z_dn * cw_ref[0:1, :] + z * cw_ref[1:2, :] + z_up * cw_ref[2:3, :])
    o_ref[...] = (y * _silu(ag_ref[...].astype(F32))).astype(o_ref.dtype)


def _lat_conv(pa, conv_w, *, batch, seq, tr=512):
    nt = seq // tr
    hb = tr // HALO
    last_halo = batch * seq // HALO - 1
    main = lambda c: pl.BlockSpec((tr, BRANCH_W), functools.partial(lambda b, i, c: (b * nt + i, c), c=c))
    prev = lambda c: pl.BlockSpec(
        (HALO, BRANCH_W), functools.partial(lambda b, i, c: (jnp.maximum((b * nt + i) * hb - 1, 0), c), c=c))
    nxt = lambda c: pl.BlockSpec(
        (HALO, BRANCH_W), functools.partial(lambda b, i, c: (jnp.minimum((b * nt + i + 1) * hb, last_halo), c), c=c))
    return pl.pallas_call(
        _lat_conv_kernel,
        grid=(batch, nt),
        in_specs=[main(T_AX), main(T_AB), main(T_AC), main(T_AG), prev(T_AX), prev(T_AC), nxt(T_AX), nxt(T_AC),
                  pl.BlockSpec(conv_w.shape, lambda b, i: (0, 0))],
        out_specs=pl.BlockSpec((tr, BRANCH_W), lambda b, i: (b * nt + i, 0)),
        out_shape=jax.ShapeDtypeStruct((batch * seq, BRANCH_W), BF16),
        compiler_params=pltpu.CompilerParams(dimension_semantics=("parallel", "arbitrary")),
        name="lat_conv",
    )(pa, pa, pa, pa, pa, pa, pa, pa, conv_w)


def _rope(x, cos, sin_signed):
    lane = lax.broadcasted_iota(jnp.int32, x.shape, 1)
    partner = jnp.where((lane % 32) < 16, pltpu.roll(x, 128 - 16, axis=1), pltpu.roll(x, 16, axis=1))
    return x * cos + partner * sin_signed


def _lat_win_kernel(q_ref, g_ref, kp_ref, kc_ref, kn_ref, vp_ref, vc_ref, vn_ref,
                    cq_ref, sq_ref, cp_ref, sp_ref, cn_ref, sn_ref, ck_ref, cv_ref, sink_ref, o_ref):
    n = pl.program_id(1)
    nb = pl.num_programs(1)
    wb = WIN_BLOCK
    cq, sq = cq_ref[...], sq_ref[...]
    k_rot = jnp.concatenate([
        _rope(kp_ref[...].astype(F32), cp_ref[...], sp_ref[...]),
        _rope(kc_ref[...].astype(F32), cq, sq),
        _rope(kn_ref[...].astype(F32), cn_ref[...], sn_ref[...]),
    ], axis=0).astype(BF16)
    v_all = jnp.concatenate([vp_ref[...], vc_ref[...], vn_ref[...]], axis=0).astype(BF16)
    qi = lax.broadcasted_iota(jnp.int32, (wb, 3 * wb), 0)
    kj = lax.broadcasted_iota(jnp.int32, (wb, 3 * wb), 1)
    mask = (((kj < wb) & (kj >= qi) & (n > 0)) | ((kj >= wb) & (kj < 2 * wb))
            | ((kj >= 2 * wb) & (kj - 2 * wb <= qi) & (n < nb - 1)))
    ck = ck_ref[0].astype(BF16)
    cv = cv_ref[0].astype(BF16)
    gsz = WIN_HEADS // WIN_KV_HEADS
    heads = []
    for pair in range(WIN_HEADS // 2):
        q2 = _rope(q_ref[:, pair * 128:(pair + 1) * 128].astype(F32), cq, sq).astype(BF16)
        for half in range(2):
            h = 2 * pair + half
            j = h // gsz
            q = q2[:, half * 64:(half + 1) * 64]
            s_loc = jnp.where(mask, _dot_nt(q, k_rot[:, j * 64:(j + 1) * 64]) * ATTN_SCALE, NEG_INF)
            s_ctx = _dot_nt(q, ck[:, j * 64:(j + 1) * 64]) * ATTN_SCALE
            heads.append(_softmax_av([s_loc, s_ctx],
                                     [v_all[:, j * 64:(j + 1) * 64], cv[:, j * 64:(j + 1) * 64]],
                                     extra_logit=sink_ref[h:h + 1, 0:1]))
    o = jnp.concatenate(heads, axis=-1) * _silu(g_ref[...].astype(F32))
    o_ref[...] = o.astype(o_ref.dtype)


def _lat_window(pb, cos_t, sin_t, ctx_k, ctx_v, sink_b, *, batch, seq):
    wb = WIN_BLOCK
    k_col = T_BKV * (TILE // KV_W)
    v_col = k_col + 1
    nb = seq // wb
    blk = lambda b, n: b * nb + n
    kv = lambda col, d: pl.BlockSpec(
        (wb, KV_W), functools.partial(lambda b, n, col, d: (blk(b, jnp.clip(n + d, 0, nb - 1)), col), col=col, d=d))
    tab = lambda d: pl.BlockSpec((wb, 128), functools.partial(lambda b, n, d: (jnp.clip(n + d, 0, nb - 1), 0), d=d))
    ctx = pl.BlockSpec((1,) + ctx_k.shape[1:], lambda b, n: (b, 0, 0))
    return pl.pallas_call(
        _lat_win_kernel,
        grid=(batch, nb),
        in_specs=[
            pl.BlockSpec((wb, BRANCH_W), lambda b, n: (blk(b, n), T_BQ)),
            pl.BlockSpec((wb, BRANCH_W), lambda b, n: (blk(b, n), T_BG)),
            kv(k_col, -1), kv(k_col, 0), kv(k_col, 1), kv(v_col, -1), kv(v_col, 0), kv(v_col, 1),
            tab(0), tab(0), tab(-1), tab(-1), tab(1), tab(1),
            ctx, ctx,
            pl.BlockSpec(sink_b.shape, lambda b, n: (0, 0)),
        ],
        out_specs=pl.BlockSpec((wb, BRANCH_W), lambda b, n: (blk(b, n), 0)),
        out_shape=jax.ShapeDtypeStruct((batch * seq, BRANCH_W), BF16),
        compiler_params=pltpu.CompilerParams(dimension_semantics=("parallel", "arbitrary")),
        name="lat_window",
    )(pb, pb, pb, pb, pb, pb, pb, pb, cos_t, sin_t, cos_t, sin_t, cos_t, sin_t, ctx_k, ctx_v, sink_b)


def _fft1_kernel(u_ref, c_ref, s_ref, twr_ref, twi_ref, zr_ref, zi_ref, *, chunks):
    u = u_ref[0].astype(BF16)
    yr = _dot(c_ref[...], u)
    yi = -_dot(s_ref[...], u)
    for t in range(chunks):
        wr = twr_ref[0, :, t:t + 1]
        wi = twi_ref[0, :, t:t + 1]
        a = yr[:, t * 512:(t + 1) * 512]
        b = yi[:, t * 512:(t + 1) * 512]
        zr_ref[0, :, t * 512:(t + 1) * 512] = (a * wr - b * wi).astype(zr_ref.dtype)
        zi_ref[0, :, t * 512:(t + 1) * 512] = (a * wi + b * wr).astype(zi_ref.dtype)


def _fft2_kernel(zr_ref, zi_ref, g_ref, l_ref, cs_ref, o_ref, *, kblock):
    r = FFT_R
    for kk in range(kblock):
        zz = jnp.concatenate([zr_ref[0, kk], zi_ref[0, kk]], axis=0)
        xx = _dot(l_ref[...], zz)
        xcat = jnp.concatenate([xx[0:r], xx[r:2 * r]], axis=1).astype(BF16)
        out = _dot(xcat, cs_ref[...])
        sl = slice(kk * 512, (kk + 1) * 512)
        o_ref[0, :, sl] = (out * _silu(g_ref[0, :, sl].astype(F32))).astype(o_ref.dtype)


def _lat_fourier(fu, fg, tabs, *, batch, seq):
    r = FFT_R
    assert seq == r * r
    wide = r * BRANCH_W
    chunks = 8
    c64, s64, twr, twi, lmat, csmat = tabs
    u3 = fu.reshape(batch, r, wide)
    nct = r // chunks
    zr, zi = pl.pallas_call(
        functools.partial(_fft1_kernel, chunks=chunks),
        grid=(batch, nct),
        in_specs=[
            pl.BlockSpec((1, r, chunks * BRANCH_W), lambda b, t: (b, 0, t)),
            pl.BlockSpec((r, r), lambda b, t: (0, 0)),
            pl.BlockSpec((r, r), lambda b, t: (0, 0)),
            pl.BlockSpec((1, r, chunks), lambda b, t: (t, 0, 0)),
            pl.BlockSpec((1, r, chunks), lambda b, t: (t, 0, 0)),
        ],
        out_specs=[pl.BlockSpec((1, r, chunks * BRANCH_W), lambda b, t: (b, 0, t))] * 2,
        out_shape=[jax.ShapeDtypeStruct((batch, r, wide), BF16)] * 2,
        compiler_params=pltpu.CompilerParams(dimension_semantics=("parallel", "arbitrary")),
        name="lat_fft1",
    )(u3, c64, s64, twr, twi)
    kblock = 8
    z4 = lambda z: z.reshape(batch, r, r, BRANCH_W)
    out = pl.pallas_call(
        functools.partial(_fft2_kernel, kblock=kblock),
        grid=(batch, r // kblock),
        in_specs=[
            pl.BlockSpec((1, kblock, r, BRANCH_W), lambda b, k: (b, k, 0, 0)),
            pl.BlockSpec((1, kblock, r, BRANCH_W), lambda b, k: (b, k, 0, 0)),
            pl.BlockSpec((1, r, kblock * BRANCH_W), lambda b, k: (b, 0, k)),
            pl.BlockSpec((2 * r, 2 * r), lambda b, k: (0, 0)),
            pl.BlockSpec((2 * BRANCH_W, BRANCH_W), lambda b, k: (0, 0)),
        ],
        out_specs=pl.BlockSpec((1, r, kblock * BRANCH_W), lambda b, k: (b, 0, k)),
        out_shape=jax.ShapeDtypeStruct((batch, r, wide), BF16),
        compiler_params=pltpu.CompilerParams(dimension_semantics=("parallel", "arbitrary")),
        name="lat_fft2",
    )(z4(zr), z4(zi), fg.reshape(batch, r, wide), lmat, csmat)
    return out.reshape(batch * seq, BRANCH_W)


N_RPB_ROWS = 2 * NA_ROWS - 1
N_RPB_COLS = 2 * NA_COLS - 1


def _lat_na_kernel(q_ref, g_ref, k_ref, v_ref, bias_ref, ck_ref, cv_ref, o_ref, *, rows):
    r = pl.program_id(1)
    r0 = jnp.clip(r - NA_ROWS // 2, 0, rows - NA_ROWS)
    start = pl.multiple_of(r0 * GRID_W, GRID_W)
    kw = k_ref[pl.ds(start, NA_ROWS * GRID_W), :]
    vw = v_ref[pl.ds(start, NA_ROWS * GRID_W), :]
    ck = ck_ref[0].astype(BF16)
    cv = cv_ref[0].astype(BF16)
    qa = q_ref[...]
    a0 = r0 - r + NA_ROWS - 1
    heads = []
    for h in range(NA_HEADS):
        sl = slice(h * 64, (h + 1) * 64)
        q = qa[:, sl]
        bias = jnp.concatenate([bias_ref[h, a0 + 2 * i2] for i2 in range(NA_ROWS // 2)], axis=1)
        s_nb = _dot_nt(q, kw[:, sl]) * ATTN_SCALE + bias
        s_ctx = _dot_nt(q, ck[:, sl]) * ATTN_SCALE
        heads.append(_softmax_av([s_nb, s_ctx], [vw[:, sl], cv[:, sl]]))
    o = jnp.concatenate(heads, axis=-1) * _silu(g_ref[...].astype(F32))
    o_ref[...] = o.astype(o_ref.dtype)


def _lat_na(p, bias, ctx_k, ctx_v, *, batch, seq):
    rows = seq // GRID_W
    ctx = pl.BlockSpec((1,) + ctx_k.shape[1:], lambda b, r: (b, 0, 0))
    return pl.pallas_call(
        functools.partial(_lat_na_kernel, rows=rows),
        grid=(batch, rows),
        in_specs=[
            pl.BlockSpec((GRID_W, BRANCH_W), lambda b, r: (b * rows + r, T_DQ)),
            pl.BlockSpec((GRID_W, BRANCH_W), lambda b, r: (b * rows + r, T_DG)),
            pl.BlockSpec((seq, BRANCH_W), lambda b, r: (b, T_DK)),
            pl.BlockSpec((seq, BRANCH_W), lambda b, r: (b, T_DV)),
            pl.BlockSpec(bias.shape, lambda b, r: (0, 0, 0, 0)),
            ctx, ctx,
        ],
        out_specs=pl.BlockSpec((GRID_W, BRANCH_W), lambda b, r: (b * rows + r, 0)),
        out_shape=jax.ShapeDtypeStruct((batch * seq, BRANCH_W), BF16),
        compiler_params=pltpu.CompilerParams(dimension_semantics=("parallel", "arbitrary")),
        name="lat_na",
    )(p, p, p, p, bias, ctx_k, ctx_v)


def _toeplitz_kernel(r_ref, o_ref):
    x = r_ref[...]
    hi = x.astype(BF16)
    r1 = x - hi.astype(F32)
    mid = r1.astype(BF16)
    lo = (r1 - mid.astype(F32)).astype(BF16)
    shape = (x.shape[1], GRID_W * GRID_W)
    b = lax.broadcasted_iota(jnp.int32, shape, 0)
    col = lax.broadcasted_iota(jnp.int32, shape, 1)
    kc = jnp.bitwise_and(col, GRID_W - 1)
    qc = lax.shift_right_logical(col, GRID_W.bit_length() - 1)
    onehot = jnp.where(kc - qc + (NA_COLS - 1) == b, 1.0, 0.0).astype(BF16)
    o_ref[...] = _dot(hi, onehot) + _dot(mid, onehot) + _dot(lo, onehot)


def _na_bias_tables(na_rpb):
    depth = na_rpb.shape[0]
    n = depth * NA_HEADS * N_RPB_ROWS
    n_pad = -(-n // 8) * 8
    r2 = jnp.zeros((n_pad, 128), F32).at[:n, :N_RPB_COLS].set(na_rpb.reshape(n, N_RPB_COLS))
    flat = pl.pallas_call(
        _toeplitz_kernel,
        out_shape=jax.ShapeDtypeStruct((n_pad, GRID_W * GRID_W), F32),
        name="na_bias_toeplitz",
    )(r2)
    t = flat[:n].reshape(depth, NA_HEADS, N_RPB_ROWS, GRID_W, GRID_W)
    qc = np.arange(GRID_W)[:, None]
    kc = np.arange(GRID_W)[None, :]
    win0 = np.clip(qc - NA_COLS // 2, 0, GRID_W - NA_COLS)
    vis = (kc >= win0) & (kc < win0 + NA_COLS)
    t = jnp.where(vis, t, NEG_INF)
    return jnp.concatenate([t[:, :, :-1], t[:, :, 1:]], axis=-1)


def _bf16_table(t):
    return jnp.asarray(t, F32).astype(BF16)


def _dft_cos_sin(n):
    idx = np.arange(n)
    ang = 2.0 * np.pi * ((idx[:, None] * idx[None, :]) % n) / n
    return np.cos(ang), np.sin(ang)


def _channel_dft_blocks(scale):
    c, s = _dft_cos_sin(FN_GROUP_W)
    eye = np.eye(FN_GROUPS)
    return np.kron(eye, c) * scale, np.kron(eye, s) * scale


def _ctx_fourier_tables(seq):
    ct, st = _dft_cos_sin(seq)
    cb, sb = _channel_dft_blocks(1.0 / np.sqrt(seq * FN_GROUP_W))
    return tuple(_bf16_table(t) for t in (ct, st, cb, sb))


def _lat_fourier_tables(seq, chunks=8):
    r = FFT_R
    c64, s64 = _dft_cos_sin(r)
    k1 = np.arange(r)[:, None]
    t2 = np.arange(r)[None, :]
    ang = 2.0 * np.pi * (k1 * t2) / (r * r)
    twr = np.cos(ang).reshape(r, r // chunks, chunks).transpose(1, 0, 2)
    twi = (-np.sin(ang)).reshape(r, r // chunks, chunks).transpose(1, 0, 2)
    lmat = np.block([[c64, s64], [-s64, c64]])
    cb, sb = _channel_dft_blocks(1.0 / np.sqrt(seq * FN_GROUP_W))
    csmat = np.concatenate([cb, sb], axis=0)
    return (_bf16_table(c64), _bf16_table(s64), jnp.asarray(twr, F32), jnp.asarray(twi, F32),
            _bf16_table(lmat), _bf16_table(csmat))


def _rope_tables(seq):
    half = HEAD_DIM // 2
    quarter = half // 2
    t = jnp.arange(seq)
    inv = ROPE_BASE ** (-jnp.arange(quarter, dtype=F32) / quarter)

    def cs(pos):
        ang = pos.astype(F32)[:, None] * inv[None, :]
        c, s = jnp.cos(ang), jnp.sin(ang)
        return jnp.concatenate([c, c], axis=-1), jnp.concatenate([-s, s], axis=-1)

    cr, sr = cs(t // GRID_W)
    cc, sc = cs(t % GRID_W)
    cos = jnp.concatenate([cr, cc], axis=-1)
    sin = jnp.concatenate([sr, sc], axis=-1)
    return jnp.tile(cos, (1, 2)), jnp.tile(sin, (1, 2))


def _inproj_weights(wl):
    pad = jnp.zeros((D_MODEL, TILE - 2 * KV_W), F32)
    cols = [wl[:, OFF_A:OFF_BQ], wl[:, OFF_BQ:OFF_BK], wl[:, OFF_BG:OFF_FU], wl[:, OFF_BK:OFF_BG], pad,
            wl[:, OFF_FU:N_MAIN]]
    return jnp.concatenate(cols, axis=1).astype(BF16)


def kernel(x_prompt, x_sample, cache_win_k, cache_win_v, cache_na_k, cache_na_v, c, c_ctx, norm_g, w_ada, b_ada,
           w_in, conv_w, win_sink, na_rpb, w_branch, w_out, final_g):
    batch, seq, _ = x_prompt.shape
    dbatch, dseq, _ = x_sample.shape
    past = cache_win_k.shape[2]

    cv8 = jnp.zeros((8, D_MODEL), F32).at[0].set(c_ctx).at[1:1 + dbatch].set(c)
    mod = _modulation(cv8, w_ada, b_ada)

    ctx_tabs = _ctx_fourier_tables(seq)
    lat_tabs = _lat_fourier_tables(dseq)
    cos_t, sin_t = _rope_tables(dseq)
    na_bias = _na_bias_tables(na_rpb)
    final_g2 = final_g.reshape(1, D_MODEL)

    xp = x_prompt.reshape(batch * seq, D_MODEL)
    xs = x_sample.reshape(dbatch * dseq, D_MODEL)
    new_kv = [[], [], [], []]

    for l in range(DEPTH):
        shift = mod[l, :, 0:D_MODEL].reshape(8, 1, D_MODEL)
        scale = mod[l, :, D_MODEL:2 * D_MODEL].reshape(8, 1, D_MODEL)
        gate = mod[l, :, 2 * D_MODEL:].reshape(8, 1, D_MODEL)
        g = norm_g[l].reshape(1, D_MODEL)
        wl = w_in[l]
        w_p = _inproj_weights(wl)
        w_gate = wl[:, N_MAIN:].astype(BF16)
        w_br = w_branch[l].astype(BF16)
        w_o = w_out[l].astype(BF16)
        cw = conv_w[l]
        sink_b = jnp.broadcast_to(win_sink[l][:, None], (WIN_HEADS, 128))
        final = l == DEPTH - 1

        p, kv_w, k_n, v_n = _inproj(xp, shift, scale, g, w_p, rows_per_cond=batch * seq, cond0=0,
                                    extra_tiles=(T_BKV, T_DK, T_DV), extra_dtype=F32)
        new_kv[0].append(kv_w[:, 0:KV_W].reshape(batch, seq, WIN_KV_HEADS, HEAD_DIM))
        new_kv[1].append(kv_w[:, KV_W:2 * KV_W].reshape(batch, seq, WIN_KV_HEADS, HEAD_DIM))
        new_kv[2].append(k_n.reshape(batch, seq, NA_HEADS, HEAD_DIM))
        new_kv[3].append(v_n.reshape(batch, seq, NA_HEADS, HEAD_DIM))
        o_ctx = _ctx_mixers(p, cw, sink_b, *ctx_tabs, seq=seq)
        xp = _out_stage(xp, shift, scale, gate, g, final_g2, [o_ctx] * 4, [0, 1, 2, 3], w_gate, w_br, w_o,
                        rows_per_cond=batch * seq, cond0=0, final=final)

        q, qfu, qfg = _inproj(xs, shift, scale, g, w_p, rows_per_cond=dseq, cond0=1,
                              extra_tiles=(T_FU, T_FG), extra_dtype=BF16)
        o_a = _lat_conv(q, cw, batch=dbatch, seq=dseq)
        o_w = _lat_window(q, cos_t, sin_t,
                          cache_win_k[:, l].reshape(dbatch, past, KV_W), cache_win_v[:, l].reshape(dbatch, past, KV_W),
                          sink_b, batch=dbatch, seq=dseq)
        o_f = _lat_fourier(qfu, qfg, lat_tabs, batch=dbatch, seq=dseq)
        o_n = _lat_na(q, na_bias[l],
                      cache_na_k[:, l].reshape(dbatch, past, BRANCH_W), cache_na_v[:, l].reshape(dbatch, past, BRANCH_W),
                      batch=dbatch, seq=dseq)
        xs = _out_stage(xs, shift, scale, gate, g, final_g2, [o_a, o_w, o_f, o_n], [0, 0, 0, 0], w_gate, w_br, w_o,
                        rows_per_cond=dseq, cond0=1, final=final)

    y_prompt = xp.reshape(batch, seq, D_MODEL)
    y_sample = xs.reshape(dbatch, dseq, D_MODEL)
    return (y_prompt, y_sample) + tuple(jnp.stack(t, axis=1) for t in new_kv)
```

```python
import functools

import numpy as np
import jax
import jax.numpy as jnp
from jax import lax
from jax.experimental import pallas as pl
from jax.experimental.pallas import tpu as pltpu

D_MODEL = 2048
DEPTH = 2
GRID_W = 64
N_BRANCH = 4
BRANCH_W = 512
HEAD_DIM = 64
WIN_HEADS = 8
WIN_KV_HEADS = 2
KV_W = WIN_KV_HEADS * HEAD_DIM
WIN_BLOCK = 128
FN_GROUPS = 4
FN_GROUP_W = BRANCH_W // FN_GROUPS
NA_HEADS = 8
NA_ROWS = 8
NA_COLS = 16
ROPE_BASE = 10000.0
EPS = 1e-6
ATTN_SCALE = HEAD_DIM ** -0.5
NEG_INF = -1e30
N_MAIN = 6400

F32 = jnp.float32
BF16 = jnp.bfloat16

OFF_A = 0
OFF_BQ, OFF_BK, OFF_BV, OFF_BG = 2048, 2560, 2688, 2816
OFF_FU, OFF_FG = 3328, 3840
OFF_DQ, OFF_DK, OFF_DV, OFF_DG = 4352, 4864, 5376, 5888

TILE = 512
T_AX, T_AB, T_AC, T_AG, T_BQ, T_BG, T_BKV, T_FU, T_FG, T_DQ, T_DK, T_DV, T_DG = range(13)
N_TILES = 13
P_WIDTH = N_TILES * TILE

FFT_R = 64


def _silu(x):
    return x * jax.nn.sigmoid(x)


def _dot(a, b):
    return jnp.dot(a, b, preferred_element_type=F32)


def _dot_nt(a, b):
    return lax.dot_general(a, b, (((1,), (1,)), ((), ())), preferred_element_type=F32)


def _rms_mod(x, g, scale, shift):
    ms = jnp.mean(x * x, axis=-1, keepdims=True)
    return (x * lax.rsqrt(ms + EPS) * g) * (1.0 + scale) + shift


def _store_modulated(h_ref, x_ref, g_ref, sc_ref, sh_ref):
    h_ref[...] = _rms_mod(x_ref[...], g_ref[...], sc_ref[0], sh_ref[0]).astype(h_ref.dtype)


def _mod_kernel(cv_ref, w_ref, b_ref, o_ref):
    s = _silu(cv_ref[...]).astype(BF16)
    o_ref[0] = _dot(s, w_ref[0].astype(BF16)) + b_ref[0]


def _modulation(cv8, w_ada, b_ada):
    tn = 512
    n3 = 3 * D_MODEL
    return pl.pallas_call(
        _mod_kernel,
        grid=(DEPTH, n3 // tn),
        in_specs=[
            pl.BlockSpec((8, D_MODEL), lambda l, j: (0, 0)),
            pl.BlockSpec((1, D_MODEL, tn), lambda l, j: (l, 0, j)),
            pl.BlockSpec((1, 1, tn), lambda l, j: (l, 0, j)),
        ],
        out_specs=pl.BlockSpec((1, 8, tn), lambda l, j: (l, 0, j)),
        out_shape=jax.ShapeDtypeStruct((DEPTH, 8, n3), F32),
        name="modulation",
    )(cv8, w_ada, b_ada.reshape(DEPTH, 1, n3))


def _inproj_kernel(x_ref, sh_ref, sc_ref, g_ref, w_ref, p_ref, *rest, extra_tiles):
    extra_refs, h_ref = rest[:-1], rest[-1]
    j = pl.program_id(1)

    @pl.when(j == 0)
    def _():
        _store_modulated(h_ref, x_ref, g_ref, sc_ref, sh_ref)

    acc = _dot(h_ref[...], w_ref[...])
    p_ref[...] = acc.astype(p_ref.dtype)
    for ref, tile in zip(extra_refs, extra_tiles):
        @pl.when(j == tile)
        def _(ref=ref):
            ref[...] = acc.astype(ref.dtype)


def _inproj(x2d, shift, scale, g, w, *, rows_per_cond, cond0, extra_tiles, extra_dtype, tm=1024):
    m = x2d.shape[0]
    assert w.shape == (D_MODEL, P_WIDTH) and m % tm == 0 and rows_per_cond % tm == 0
    tiles_per_cond = rows_per_cond // tm
    cond = lambda i, j: (cond0 + i // tiles_per_cond, 0, 0)
    return pl.pallas_call(
        functools.partial(_inproj_kernel, extra_tiles=tuple(extra_tiles)),
        grid=(m // tm, N_TILES),
        in_specs=[
            pl.BlockSpec((tm, D_MODEL), lambda i, j: (i, 0)),
            pl.BlockSpec((1, 1, D_MODEL), cond),
            pl.BlockSpec((1, 1, D_MODEL), cond),
            pl.BlockSpec((1, D_MODEL), lambda i, j: (0, 0)),
            pl.BlockSpec((D_MODEL, TILE), lambda i, j: (0, j)),
        ],
        out_specs=[pl.BlockSpec((tm, TILE), lambda i, j: (i, j))]
        + [pl.BlockSpec((tm, TILE), lambda i, j: (i, 0)) for _ in extra_tiles],
        out_shape=[jax.ShapeDtypeStruct((m, P_WIDTH), BF16)]
        + [jax.ShapeDtypeStruct((m, TILE), extra_dtype) for _ in extra_tiles],
        scratch_shapes=[pltpu.VMEM((tm, D_MODEL), BF16)],
        compiler_params=pltpu.CompilerParams(dimension_semantics=("parallel", "arbitrary")),
        name="inproj",
    )(x2d, shift, scale, g, w)


def _out_kernel(x_ref, sh_ref, sc_ref, gt_ref, g_ref, fg_ref, oa_ref, ow_ref, of_ref, on_ref,
                wg0_ref, wg1_ref, wg2_ref, wg3_ref, wb_ref, wo_ref, y_ref, h_ref, acc_ref, *, final):
    j = pl.program_id(1)

    @pl.when(j == 0)
    def _():
        _store_modulated(h_ref, x_ref, g_ref, sc_ref, sh_ref)
        acc_ref[...] = jnp.zeros_like(acc_ref)

    h = h_ref[...]
    merged = None
    for i, (o_ref, wg_ref) in enumerate(((oa_ref, wg0_ref), (ow_ref, wg1_ref), (of_ref, wg2_ref), (on_ref, wg3_ref))):
        term = jax.nn.sigmoid(_dot(h, wg_ref[...])) * _dot(o_ref[...], wb_ref[i])
        merged = term if merged is None else merged + term
    acc_ref[...] += _dot(merged.astype(BF16), wo_ref[...])

    @pl.when(j == pl.num_programs(1) - 1)
    def _():
        xn = x_ref[...] + gt_ref[0] * acc_ref[...]
        if final:
            ms = jnp.mean(xn * xn, axis=-1, keepdims=True)
            xn = xn * lax.rsqrt(ms + EPS) * fg_ref[...]
        y_ref[...] = xn


def _out_stage(x2d, shift, scale, gate, g, final_g, o_arrays, o_cols, w_gate, w_branch, w_out,
               *, rows_per_cond, cond0, final, tm=512, tc=512):
    m = x2d.shape[0]
    assert m % tm == 0 and rows_per_cond % tm == 0
    tiles_per_cond = rows_per_cond // tm
    ncol = D_MODEL // tc
    cond = lambda i, j: (cond0 + i // tiles_per_cond, 0, 0)
    o_specs = [pl.BlockSpec((tm, BRANCH_W), functools.partial(lambda i, j, c: (i, c), c=c)) for c in o_cols]
    wg_specs = [pl.BlockSpec((D_MODEL, tc), functools.partial(lambda i, j, b: (0, b * ncol + j), b=b))
                for b in range(N_BRANCH)]
    return pl.pallas_call(
        functools.partial(_out_kernel, final=final),
        grid=(m // tm, ncol),
        in_specs=[
            pl.BlockSpec((tm, D_MODEL), lambda i, j: (i, 0)),
            pl.BlockSpec((1, 1, D_MODEL), cond),
            pl.BlockSpec((1, 1, D_MODEL), cond),
            pl.BlockSpec((1, 1, D_MODEL), cond),
            pl.BlockSpec((1, D_MODEL), lambda i, j: (0, 0)),
            pl.BlockSpec((1, D_MODEL), lambda i, j: (0, 0)),
            *o_specs,
            *wg_specs,
            pl.BlockSpec((N_BRANCH, BRANCH_W, tc), lambda i, j: (0, 0, j)),
            pl.BlockSpec((tc, D_MODEL), lambda i, j: (j, 0)),
        ],
        out_specs=pl.BlockSpec((tm, D_MODEL), lambda i, j: (i, 0)),
        out_shape=jax.ShapeDtypeStruct((m, D_MODEL), F32),
        scratch_shapes=[pltpu.VMEM((tm, D_MODEL), BF16), pltpu.VMEM((tm, D_MODEL), F32)],
        compiler_params=pltpu.CompilerParams(dimension_semantics=("parallel", "arbitrary")),
        name="out_stage",
    )(x2d, shift, scale, gate, g, final_g, *o_arrays, w_gate, w_gate, w_gate, w_gate, w_branch, w_out)


def _attend(scores, values, sinks=None):
    heads = range(len(scores))
    m = []
    for h in heads:
        mh = scores[h][0].max(axis=-1, keepdims=True)
        for s in scores[h][1:]:
            mh = jnp.maximum(mh, s.max(axis=-1, keepdims=True))
        m.append(mh if sinks is None else jnp.maximum(mh, sinks[h]))
    e = [[jnp.exp(s - m[h]) for s in scores[h]] for h in heads]
    den = []
    for h in heads:
        d = e[h][0].sum(axis=-1, keepdims=True)
        for x in e[h][1:]:
            d = d + x.sum(axis=-1, keepdims=True)
        den.append(d if sinks is None else d + jnp.exp(sinks[h] - m[h]))
    out = []
    for h in heads:
        acc = _dot(e[h][0].astype(BF16), values[h][0])
        for x, v in zip(e[h][1:], values[h][1:]):
            acc = acc + _dot(x.astype(BF16), v)
        out.append(acc)
    return [a / d for a, d in zip(out, den)]


def _shift_rows(z, first_row, last_row):
    n = z.shape[0]
    row = lax.broadcasted_iota(jnp.int32, z.shape, 0)
    z_dn = jnp.where(row == 0, first_row, pltpu.roll(z, 1, axis=0))
    z_up = jnp.where(row == n - 1, last_row, pltpu.roll(z, n - 1, axis=0))
    return z_dn, z_up


def _tile(ref, t, dtype=None):
    v = ref[:, t * TILE:(t + 1) * TILE]
    return v if dtype is None else v.astype(dtype)


def _head(ref, t, h, base=0):
    lo = t * TILE + base + h * HEAD_DIM
    return ref[:, lo:lo + HEAD_DIM]


def _ctx_mixer_kernel(p_ref, cw_ref, sink_ref, ct_ref, st_ref, cb_ref, sb_ref, o_ref):
    z = _tile(p_ref, T_AC, F32) * _tile(p_ref, T_AX, F32)
    zero_row = jnp.zeros((1, BRANCH_W), F32)
    z_dn, z_up = _shift_rows(z, zero_row, zero_row)
    y = _tile(p_ref, T_AB, F32) * (z_dn * cw_ref[0:1, :] + z * cw_ref[1:2, :] + z_up * cw_ref[2:3, :])
    o_ref[:, 0:512] = (y * _silu(_tile(p_ref, T_AG, F32))).astype(o_ref.dtype)

    gsz = WIN_HEADS // WIN_KV_HEADS
    scores = [[_dot_nt(_head(p_ref, T_BQ, h), _head(p_ref, T_BKV, h // gsz)) * ATTN_SCALE] for h in range(WIN_HEADS)]
    values = [[_head(p_ref, T_BKV, h // gsz, base=KV_W)] for h in range(WIN_HEADS)]
    heads = _attend(scores, values, sinks=[sink_ref[h:h + 1, 0:1] for h in range(WIN_HEADS)])
    o_w = jnp.concatenate(heads, axis=-1) * _silu(_tile(p_ref, T_BG, F32))
    o_ref[:, 512:1024] = o_w.astype(o_ref.dtype)

    u = _tile(p_ref, T_FU)
    uc = _dot(u, cb_ref[...]).astype(BF16)
    us = _dot(u, sb_ref[...]).astype(BF16)
    o_f = (_dot(ct_ref[...], uc) - _dot(st_ref[...], us)) * _silu(_tile(p_ref, T_FG, F32))
    o_ref[:, 1024:1536] = o_f.astype(o_ref.dtype)

    scores = [[_dot_nt(_head(p_ref, T_DQ, h), _head(p_ref, T_DK, h)) * ATTN_SCALE] for h in range(NA_HEADS)]
    heads = _attend(scores, [[_head(p_ref, T_DV, h)] for h in range(NA_HEADS)])
    o_n = jnp.concatenate(heads, axis=-1) * _silu(_tile(p_ref, T_DG, F32))
    o_ref[:, 1536:2048] = o_n.astype(o_ref.dtype)


def _ctx_mixers(p, conv_w, sink_b, ct, st, cb, sb, *, seq):
    m = p.shape[0]
    whole = lambda a: pl.BlockSpec(a.shape, lambda b: (0,) * a.ndim)
    return pl.pallas_call(
        _ctx_mixer_kernel,
        grid=(m // seq,),
        in_specs=[pl.BlockSpec((seq, P_WIDTH), lambda b: (b, 0)), whole(conv_w), whole(sink_b),
                  whole(ct), whole(st), whole(cb), whole(sb)],
        out_specs=pl.BlockSpec((seq, N_BRANCH * BRANCH_W), lambda b: (b, 0)),
        out_shape=jax.ShapeDtypeStruct((m, N_BRANCH * BRANCH_W), BF16),
        compiler_params=pltpu.CompilerParams(dimension_semantics=("parallel",)),
        name="ctx_mixers",
    )(p, conv_w, sink_b, ct, st, cb, sb)


HALO = 16


def _lat_conv_kernel(ax_ref, ab_ref, ac_ref, ag_ref, axp_ref, acp_ref, axn_ref, acn_ref, cw_ref, o_ref):
    i = pl.program_id(1)
    z = ac_ref[...].astype(F32) * ax_ref[...].astype(F32)
    zp = acp_ref[HALO - 1:HALO, :].astype(F32) * axp_ref[HALO - 1:HALO, :].astype(F32)
    zn = acn_ref[0:1, :].astype(F32) * axn_ref[0:1, :].astype(F32)
    zp = jnp.where(i == 0, 0.0, zp)
    zn = jnp.where(i == pl.num_programs(1) - 1, 0.0, zn)
    z_dn, z_up = _shift_rows(z, zp, zn)
    y = ab_ref[...].astype(F32) * (z_dn * cw_ref[0:1, :] + z * cw_ref[1:2, :] + z_up * cw_ref[2:3, :])
    o_ref[...] = (y * _silu(ag_ref[...].astype(F32))).astype(o_ref.dtype)


def _lat_conv(pa, conv_w, *, batch, seq, tr=512):
    nt = seq // tr
    hb = tr // HALO
    last_halo = batch * seq // HALO - 1
    main = lambda c: pl.BlockSpec((tr, BRANCH_W), functools.partial(lambda b, i, c: (b * nt + i, c), c=c))
    prev = lambda c: pl.BlockSpec(
        (HALO, BRANCH_W), functools.partial(lambda b, i, c: (jnp.maximum((b * nt + i) * hb - 1, 0), c), c=c))
    nxt = lambda c: pl.BlockSpec(
        (HALO, BRANCH_W), functools.partial(lambda b, i, c: (jnp.minimum((b * nt + i + 1) * hb, last_halo), c), c=c))
    return pl.pallas_call(
        _lat_conv_kernel,
        grid=(batch, nt),
        in_specs=[main(T_AX), main(T_AB), main(T_AC), main(T_AG), prev(T_AX), prev(T_AC), nxt(T_AX), nxt(T_AC),
                  pl.BlockSpec(conv_w.shape, lambda b, i: (0, 0))],
        out_specs=pl.BlockSpec((tr, BRANCH_W), lambda b, i: (b * nt + i, 0)),
        out_shape=jax.ShapeDtypeStruct((batch * seq, BRANCH_W), BF16),
        compiler_params=pltpu.CompilerParams(dimension_semantics=("parallel", "arbitrary")),
        name="lat_conv",
    )(pa, pa, pa, pa, pa, pa, pa, pa, conv_w)


def _rope(x, cos, sin_signed):
    lane = lax.broadcasted_iota(jnp.int32, x.shape, 1)
    partner = jnp.where((lane % 32) < 16, pltpu.roll(x, 128 - 16, axis=1), pltpu.roll(x, 16, axis=1))
    return x * cos + partner * sin_signed


def _lat_win_kernel(q_ref, g_ref, kp_ref, kc_ref, kn_ref, vp_ref, vc_ref, vn_ref,
                    cq_ref, sq_ref, cp_ref, sp_ref, cn_ref, sn_ref, ck_ref, cv_ref, sink_ref, o_ref):
    n = pl.program_id(1)
    nb = pl.num_programs(1)
    wb = WIN_BLOCK
    cq, sq = cq_ref[...], sq_ref[...]
    k_rot = jnp.concatenate([
        _rope(kp_ref[...].astype(F32), cp_ref[...], sp_ref[...]),
        _rope(kc_ref[...].astype(F32), cq, sq),
        _rope(kn_ref[...].astype(F32), cn_ref[...], sn_ref[...]),
    ], axis=0).astype(BF16)
    v_all = jnp.concatenate([vp_ref[...], vc_ref[...], vn_ref[...]], axis=0).astype(BF16)
    qi = lax.broadcasted_iota(jnp.int32, (wb, 3 * wb), 0)
    kj = lax.broadcasted_iota(jnp.int32, (wb, 3 * wb), 1)
    mask = (((kj < wb) & (kj >= qi) & (n > 0)) | ((kj >= wb) & (kj < 2 * wb))
            | ((kj >= 2 * wb) & (kj - 2 * wb <= qi) & (n < nb - 1)))
    ck = ck_ref[0].astype(BF16)
    cv = cv_ref[0].astype(BF16)
    gsz = WIN_HEADS // WIN_KV_HEADS
    q_rot = [_rope(q_ref[:, pair * 128:(pair + 1) * 128].astype(F32), cq, sq).astype(BF16)
             for pair in range(WIN_HEADS // 2)]
    scores, values = [], []
    for h in range(WIN_HEADS):
        kv = slice((h // gsz) * 64, (h // gsz + 1) * 64)
        q = q_rot[h // 2][:, (h % 2) * 64:(h % 2 + 1) * 64]
        s_loc = jnp.where(mask, _dot_nt(q, k_rot[:, kv]) * ATTN_SCALE, NEG_INF)
        scores.append([s_loc, _dot_nt(q, ck[:, kv]) * ATTN_SCALE])
        values.append([v_all[:, kv], cv[:, kv]])
    heads = _attend(scores, values, sinks=[sink_ref[h:h + 1, 0:1] for h in range(WIN_HEADS)])
    o = jnp.concatenate(heads, axis=-1) * _silu(g_ref[...].astype(F32))
    o_ref[...] = o.astype(o_ref.dtype)


def _lat_window(pb, cos_t, sin_t, ctx_k, ctx_v, sink_b, *, batch, seq):
    wb = WIN_BLOCK
    k_col = T_BKV * (TILE // KV_W)
    v_col = k_col + 1
    nb = seq // wb
    blk = lambda b, n: b * nb + n
    kv = lambda col, d: pl.BlockSpec(
        (wb, KV_W), functools.partial(lambda b, n, col, d: (blk(b, jnp.clip(n + d, 0, nb - 1)), col), col=col, d=d))
    tab = lambda d: pl.BlockSpec((wb, 128), functools.partial(lambda b, n, d: (jnp.clip(n + d, 0, nb - 1), 0), d=d))
    ctx = pl.BlockSpec((1,) + ctx_k.shape[1:], lambda b, n: (b, 0, 0))
    return pl.pallas_call(
        _lat_win_kernel,
        grid=(batch, nb),
        in_specs=[
            pl.BlockSpec((wb, BRANCH_W), lambda b, n: (blk(b, n), T_BQ)),
            pl.BlockSpec((wb, BRANCH_W), lambda b, n: (blk(b, n), T_BG)),
            kv(k_col, -1), kv(k_col, 0), kv(k_col, 1), kv(v_col, -1), kv(v_col, 0), kv(v_col, 1),
            tab(0), tab(0), tab(-1), tab(-1), tab(1), tab(1),
            ctx, ctx,
            pl.BlockSpec(sink_b.shape, lambda b, n: (0, 0)),
        ],
        out_specs=pl.BlockSpec((wb, BRANCH_W), lambda b, n: (blk(b, n), 0)),
        out_shape=jax.ShapeDtypeStruct((batch * seq, BRANCH_W), BF16),
        compiler_params=pltpu.CompilerParams(dimension_semantics=("parallel", "arbitrary")),
        name="lat_window",
    )(pb, pb, pb, pb, pb, pb, pb, pb, cos_t, sin_t, cos_t, sin_t, cos_t, sin_t, ctx_k, ctx_v, sink_b)


def _fft1_kernel(u_ref, c_ref, s_ref, twr_ref, twi_ref, zr_ref, zi_ref, *, chunks):
    u = u_ref[0].astype(BF16)
    yr = _dot(c_ref[...], u)
    yi = -_dot(s_ref[...], u)
    for t in range(chunks):
        wr = twr_ref[0, :, t:t + 1]
        wi = twi_ref[0, :, t:t + 1]
        a = yr[:, t * 512:(t + 1) * 512]
        b = yi[:, t * 512:(t + 1) * 512]
        zr_ref[0, :, t * 512:(t + 1) * 512] = (a * wr - b * wi).astype(zr_ref.dtype)
        zi_ref[0, :, t * 512:(t + 1) * 512] = (a * wi + b * wr).astype(zi_ref.dtype)


def _fft2_kernel(zr_ref, zi_ref, g_ref, l_ref, cs_ref, o_ref, *, kblock):
    r = FFT_R
    for kk in range(kblock):
        zz = jnp.concatenate([zr_ref[0, kk], zi_ref[0, kk]], axis=0)
        xx = _dot(l_ref[...], zz)
        xcat = jnp.concatenate([xx[0:r], xx[r:2 * r]], axis=1).astype(BF16)
        out = _dot(xcat, cs_ref[...])
        sl = slice(kk * 512, (kk + 1) * 512)
        o_ref[0, :, sl] = (out * _silu(g_ref[0, :, sl].astype(F32))).astype(o_ref.dtype)


def _lat_fourier(fu, fg, tabs, *, batch, seq):
    r = FFT_R
    assert seq == r * r
    wide = r * BRANCH_W
    chunks = 8
    c64, s64, twr, twi, lmat, csmat = tabs
    u3 = fu.reshape(batch, r, wide)
    nct = r // chunks
    zr, zi = pl.pallas_call(
        functools.partial(_fft1_kernel, chunks=chunks),
        grid=(batch, nct),
        in_specs=[
            pl.BlockSpec((1, r, chunks * BRANCH_W), lambda b, t: (b, 0, t)),
            pl.BlockSpec((r, r), lambda b, t: (0, 0)),
            pl.BlockSpec((r, r), lambda b, t: (0, 0)),
            pl.BlockSpec((1, r, chunks), lambda b, t: (t, 0, 0)),
            pl.BlockSpec((1, r, chunks), lambda b, t: (t, 0, 0)),
        ],
        out_specs=[pl.BlockSpec((1, r, chunks * BRANCH_W), lambda b, t: (b, 0, t))] * 2,
        out_shape=[jax.ShapeDtypeStruct((batch, r, wide), BF16)] * 2,
        compiler_params=pltpu.CompilerParams(dimension_semantics=("parallel", "arbitrary")),
        name="lat_fft1",
    )(u3, c64, s64, twr, twi)
    kblock = 8
    z4 = lambda z: z.reshape(batch, r, r, BRANCH_W)
    out = pl.pallas_call(
        functools.partial(_fft2_kernel, kblock=kblock),
        grid=(batch, r // kblock),
        in_specs=[
            pl.BlockSpec((1, kblock, r, BRANCH_W), lambda b, k: (b, k, 0, 0)),
            pl.BlockSpec((1, kblock, r, BRANCH_W), lambda b, k: (b, k, 0, 0)),
            pl.BlockSpec((1, r, kblock * BRANCH_W), lambda b, k: (b, 0, k)),
            pl.BlockSpec((2 * r, 2 * r), lambda b, k: (0, 0)),
            pl.BlockSpec((2 * BRANCH_W, BRANCH_W), lambda b, k: (0, 0)),
        ],
        out_specs=pl.BlockSpec((1, r, kblock * BRANCH_W), lambda b, k: (b, 0, k)),
        out_shape=jax.ShapeDtypeStruct((batch, r, wide), BF16),
        compiler_params=pltpu.CompilerParams(dimension_semantics=("parallel", "arbitrary")),
        name="lat_fft2",
    )(z4(zr), z4(zi), fg.reshape(batch, r, wide), lmat, csmat)
    return out.reshape(batch * seq, BRANCH_W)


N_RPB_ROWS = 2 * NA_ROWS - 1
N_RPB_COLS = 2 * NA_COLS - 1


def _lat_na_kernel(q_ref, g_ref, k_ref, v_ref, bias_ref, ck_ref, cv_ref, o_ref, *, rows):
    r = pl.program_id(1)
    r0 = jnp.clip(r - NA_ROWS // 2, 0, rows - NA_ROWS)
    start = pl.multiple_of(r0 * GRID_W, GRID_W)
    kw = k_ref[pl.ds(start, NA_ROWS * GRID_W), :]
    vw = v_ref[pl.ds(start, NA_ROWS * GRID_W), :]
    ck = ck_ref[0].astype(BF16)
    cv = cv_ref[0].astype(BF16)
    qa = q_ref[...]
    a0 = r0 - r + NA_ROWS - 1
    sls = [slice(h * 64, (h + 1) * 64) for h in range(NA_HEADS)]
    scores = []
    for h, sl in enumerate(sls):
        bias = jnp.concatenate([bias_ref[h, a0 + 2 * i2] for i2 in range(NA_ROWS // 2)], axis=1)
        scores.append([_dot_nt(qa[:, sl], kw[:, sl]) * ATTN_SCALE + bias, _dot_nt(qa[:, sl], ck[:, sl]) * ATTN_SCALE])
    heads = _attend(scores, [[vw[:, sl], cv[:, sl]] for sl in sls])
    o = jnp.concatenate(heads, axis=-1) * _silu(g_ref[...].astype(F32))
    o_ref[...] = o.astype(o_ref.dtype)


def _lat_na(p, bias, ctx_k, ctx_v, *, batch, seq):
    rows = seq // GRID_W
    ctx = pl.BlockSpec((1,) + ctx_k.shape[1:], lambda b, r: (b, 0, 0))
    return pl.pallas_call(
        functools.partial(_lat_na_kernel, rows=rows),
        grid=(batch, rows),
        in_specs=[
            pl.BlockSpec((GRID_W, BRANCH_W), lambda b, r: (b * rows + r, T_DQ)),
            pl.BlockSpec((GRID_W, BRANCH_W), lambda b, r: (b * rows + r, T_DG)),
            pl.BlockSpec((seq, BRANCH_W), lambda b, r: (b, T_DK)),
            pl.BlockSpec((seq, BRANCH_W), lambda b, r: (b, T_DV)),
            pl.BlockSpec(bias.shape, lambda b, r: (0, 0, 0, 0)),
            ctx, ctx,
        ],
        out_specs=pl.BlockSpec((GRID_W, BRANCH_W), lambda b, r: (b * rows + r, 0)),
        out_shape=jax.ShapeDtypeStruct((batch * seq, BRANCH_W), BF16),
        compiler_params=pltpu.CompilerParams(dimension_semantics=("parallel", "arbitrary")),
        name="lat_na",
    )(p, p, p, p, bias, ctx_k, ctx_v)


def _toeplitz_kernel(r_ref, o_ref):
    x = r_ref[...]
    hi = x.astype(BF16)
    r1 = x - hi.astype(F32)
    mid = r1.astype(BF16)
    lo = (r1 - mid.astype(F32)).astype(BF16)
    shape = (x.shape[1], GRID_W * GRID_W)
    b = lax.broadcasted_iota(jnp.int32, shape, 0)
    col = lax.broadcasted_iota(jnp.int32, shape, 1)
    kc = jnp.bitwise_and(col, GRID_W - 1)
    qc = lax.shift_right_logical(col, GRID_W.bit_length() - 1)
    onehot = jnp.where(kc - qc + (NA_COLS - 1) == b, 1.0, 0.0).astype(BF16)
    o_ref[...] = _dot(hi, onehot) + _dot(mid, onehot) + _dot(lo, onehot)


def _na_bias_tables(na_rpb):
    depth = na_rpb.shape[0]
    n = depth * NA_HEADS * N_RPB_ROWS
    n_pad = -(-n // 8) * 8
    r2 = jnp.zeros((n_pad, 128), F32).at[:n, :N_RPB_COLS].set(na_rpb.reshape(n, N_RPB_COLS))
    flat = pl.pallas_call(
        _toeplitz_kernel,
        out_shape=jax.ShapeDtypeStruct((n_pad, GRID_W * GRID_W), F32),
        name="na_bias_toeplitz",
    )(r2)
    t = flat[:n].reshape(depth, NA_HEADS, N_RPB_ROWS, GRID_W, GRID_W)
    qc = np.arange(GRID_W)[:, None]
    kc = np.arange(GRID_W)[None, :]
    win0 = np.clip(qc - NA_COLS // 2, 0, GRID_W - NA_COLS)
    vis = (kc >= win0) & (kc < win0 + NA_COLS)
    t = jnp.where(vis, t, NEG_INF)
    return jnp.concatenate([t[:, :, :-1], t[:, :, 1:]], axis=-1)


def _bf16_table(t):
    return jnp.asarray(t, F32).astype(BF16)


def _dft_cos_sin(n):
    idx = np.arange(n)
    ang = 2.0 * np.pi * ((idx[:, None] * idx[None, :]) % n) / n
    return np.cos(ang), np.sin(ang)


def _channel_dft_blocks(scale):
    c, s = _dft_cos_sin(FN_GROUP_W)
    eye = np.eye(FN_GROUPS)
    return np.kron(eye, c) * scale, np.kron(eye, s) * scale


def _ctx_fourier_tables(seq):
    ct, st = _dft_cos_sin(seq)
    cb, sb = _channel_dft_blocks(1.0 / np.sqrt(seq * FN_GROUP_W))
    return tuple(_bf16_table(t) for t in (ct, st, cb, sb))


def _lat_fourier_tables(seq, chunks=8):
    r = FFT_R
    c64, s64 = _dft_cos_sin(r)
    k1 = np.arange(r)[:, None]
    t2 = np.arange(r)[None, :]
    ang = 2.0 * np.pi * (k1 * t2) / (r * r)
    twr = np.cos(ang).reshape(r, r // chunks, chunks).transpose(1, 0, 2)
    twi = (-np.sin(ang)).reshape(r, r // chunks, chunks).transpose(1, 0, 2)
    lmat = np.block([[c64, s64], [-s64, c64]])
    cb, sb = _channel_dft_blocks(1.0 / np.sqrt(seq * FN_GROUP_W))
    csmat = np.concatenate([cb, sb], axis=0)
    return (_bf16_table(c64), _bf16_table(s64), jnp.asarray(twr, F32), jnp.asarray(twi, F32),
            _bf16_table(lmat), _bf16_table(csmat))


def _rope_tables(seq):
    half = HEAD_DIM // 2
    quarter = half // 2
    t = jnp.arange(seq)
    inv = ROPE_BASE ** (-jnp.arange(quarter, dtype=F32) / quarter)

    def cs(pos):
        ang = pos.astype(F32)[:, None] * inv[None, :]
        c, s = jnp.cos(ang), jnp.sin(ang)
        return jnp.concatenate([c, c], axis=-1), jnp.concatenate([-s, s], axis=-1)

    cr, sr = cs(t // GRID_W)
    cc, sc = cs(t % GRID_W)
    cos = jnp.concatenate([cr, cc], axis=-1)
    sin = jnp.concatenate([sr, sc], axis=-1)
    return jnp.tile(cos, (1, 2)), jnp.tile(sin, (1, 2))


def _inproj_weights(wl):
    pad = jnp.zeros((D_MODEL, TILE - 2 * KV_W), F32)
    cols = [wl[:, OFF_A:OFF_BQ], wl[:, OFF_BQ:OFF_BK], wl[:, OFF_BG:OFF_FU], wl[:, OFF_BK:OFF_BG], pad,
            wl[:, OFF_FU:N_MAIN]]
    return jnp.concatenate(cols, axis=1).astype(BF16)


def kernel(x_prompt, x_sample, cache_win_k, cache_win_v, cache_na_k, cache_na_v, c, c_ctx, norm_g, w_ada, b_ada,
           w_in, conv_w, win_sink, na_rpb, w_branch, w_out, final_g):
    batch, seq, _ = x_prompt.shape
    dbatch, dseq, _ = x_sample.shape
    past = cache_win_k.shape[2]

    cv8 = jnp.zeros((8, D_MODEL), F32).at[0].set(c_ctx).at[1:1 + dbatch].set(c)
    mod = _modulation(cv8, w_ada, b_ada)

    ctx_tabs = _ctx_fourier_tables(seq)
    lat_tabs = _lat_fourier_tables(dseq)
    cos_t, sin_t = _rope_tables(dseq)
    na_bias = _na_bias_tables(na_rpb)
    final_g2 = final_g.reshape(1, D_MODEL)

    xp = x_prompt.reshape(batch * seq, D_MODEL)
    xs = x_sample.reshape(dbatch * dseq, D_MODEL)
    new_kv = [[], [], [], []]

    for l in range(DEPTH):
        shift = mod[l, :, 0:D_MODEL].reshape(8, 1, D_MODEL)
        scale = mod[l, :, D_MODEL:2 * D_MODEL].reshape(8, 1, D_MODEL)
        gate = mod[l, :, 2 * D_MODEL:].reshape(8, 1, D_MODEL)
        g = norm_g[l].reshape(1, D_MODEL)
        wl = w_in[l]
        w_p = _inproj_weights(wl)
        w_gate = wl[:, N_MAIN:].astype(BF16)
        w_br = w_branch[l].astype(BF16)
        w_o = w_out[l].astype(BF16)
        cw = conv_w[l]
        sink_b = jnp.broadcast_to(win_sink[l][:, None], (WIN_HEADS, 128))
        final = l == DEPTH - 1

        p, kv_w, k_n, v_n = _inproj(xp, shift, scale, g, w_p, rows_per_cond=batch * seq, cond0=0,
                                    extra_tiles=(T_BKV, T_DK, T_DV), extra_dtype=F32)
        new_kv[0].append(kv_w[:, 0:KV_W].reshape(batch, seq, WIN_KV_HEADS, HEAD_DIM))
        new_kv[1].append(kv_w[:, KV_W:2 * KV_W].reshape(batch, seq, WIN_KV_HEADS, HEAD_DIM))
        new_kv[2].append(k_n.reshape(batch, seq, NA_HEADS, HEAD_DIM))
        new_kv[3].append(v_n.reshape(batch, seq, NA_HEADS, HEAD_DIM))
        o_ctx = _ctx_mixers(p, cw, sink_b, *ctx_tabs, seq=seq)
        xp = _out_stage(xp, shift, scale, gate, g, final_g2, [o_ctx] * 4, [0, 1, 2, 3], w_gate, w_br, w_o,
                        rows_per_cond=batch * seq, cond0=0, final=final)

        q, qfu, qfg = _inproj(xs, shift, scale, g, w_p, rows_per_cond=dseq, cond0=1,
                              extra_tiles=(T_FU, T_FG), extra_dtype=BF16)
        o_a = _lat_conv(q, cw, batch=dbatch, seq=dseq)
        o_w = _lat_window(q, cos_t, sin_t,
                          cache_win_k[:, l].reshape(dbatch, past, KV_W), cache_win_v[:, l].reshape(dbatch, past, KV_W),
                          sink_b, batch=dbatch, seq=dseq)
        o_f = _lat_fourier(qfu, qfg, lat_tabs, batch=dbatch, seq=dseq)
        o_n = _lat_na(q, na_bias[l],
                      cache_na_k[:, l].reshape(dbatch, past, BRANCH_W), cache_na_v[:, l].reshape(dbatch, past, BRANCH_W),
                      batch=dbatch, seq=dseq)
        xs = _out_stage(xs, shift, scale, gate, g, final_g2, [o_a, o_w, o_f, o_n], [0, 0, 0, 0], w_gate, w_br, w_o,
                        rows_per_cond=dseq, cond0=1, final=final)

    y_prompt = xp.reshape(batch, seq, D_MODEL)
    y_sample = xs.reshape(dbatch, dseq, D_MODEL)
    return (y_prompt, y_sample) + tuple(jnp.stack(t, axis=1) for t in new_kv)
```

```python
import functools

import numpy as np
import jax
import jax.numpy as jnp
from jax import lax
from jax.experimental import pallas as pl
from jax.experimental.pallas import tpu as pltpu

D_MODEL = 2048
DEPTH = 2
GRID_W = 64
N_BRANCH = 4
BRANCH_W = 512
HEAD_DIM = 64
WIN_HEADS = 8
WIN_KV_HEADS = 2
KV_W = WIN_KV_HEADS * HEAD_DIM
WIN_BLOCK = 128
FN_GROUPS = 4
FN_GROUP_W = BRANCH_W // FN_GROUPS
NA_HEADS = 8
NA_ROWS = 8
NA_COLS = 16
ROPE_BASE = 10000.0
EPS = 1e-6
ATTN_SCALE = HEAD_DIM ** -0.5
NEG_INF = -1e30
N_MAIN = 6400

F32 = jnp.float32
BF16 = jnp.bfloat16

OFF_A = 0
OFF_BQ, OFF_BK, OFF_BV, OFF_BG = 2048, 2560, 2688, 2816
OFF_FU, OFF_FG = 3328, 3840
OFF_DQ, OFF_DK, OFF_DV, OFF_DG = 4352, 4864, 5376, 5888

TILE = 512
T_AX, T_AB, T_AC, T_AG, T_BQ, T_BG, T_BKV, T_FU, T_FG, T_DQ, T_DK, T_DV, T_DG = range(13)
N_TILES = 13
P_WIDTH = N_TILES * TILE

FFT_R = 64


def _silu(x):
    return x * jax.nn.sigmoid(x)


def _dot(a, b):
    return jnp.dot(a, b, preferred_element_type=F32)


def _dot_nt(a, b):
    return lax.dot_general(a, b, (((1,), (1,)), ((), ())), preferred_element_type=F32)


def _rms_mod(x, g, scale, shift):
    ms = jnp.mean(x * x, axis=-1, keepdims=True)
    return (x * lax.rsqrt(ms + EPS) * g) * (1.0 + scale) + shift


def _store_modulated(h_ref, x_ref, g_ref, sc_ref, sh_ref):
    h_ref[...] = _rms_mod(x_ref[...], g_ref[...], sc_ref[0], sh_ref[0]).astype(h_ref.dtype)


def _mod_kernel(cv_ref, w_ref, b_ref, o_ref):
    s = _silu(cv_ref[...]).astype(BF16)
    o_ref[0] = _dot(s, w_ref[0].astype(BF16)) + b_ref[0]


def _modulation(cv8, w_ada, b_ada):
    tn = 512
    n3 = 3 * D_MODEL
    return pl.pallas_call(
        _mod_kernel,
        grid=(DEPTH, n3 // tn),
        in_specs=[
            pl.BlockSpec((8, D_MODEL), lambda l, j: (0, 0)),
            pl.BlockSpec((1, D_MODEL, tn), lambda l, j: (l, 0, j)),
            pl.BlockSpec((1, 1, tn), lambda l, j: (l, 0, j)),
        ],
        out_specs=pl.BlockSpec((1, 8, tn), lambda l, j: (l, 0, j)),
        out_shape=jax.ShapeDtypeStruct((DEPTH, 8, n3), F32),
        name="modulation",
    )(cv8, w_ada, b_ada.reshape(DEPTH, 1, n3))


def _inproj_kernel(x_ref, sh_ref, sc_ref, g_ref, w_ref, p_ref, *rest, extras):
    extra_refs, h_ref = rest[:-1], rest[-1]
    j = pl.program_id(1)

    @pl.when(j == 0)
    def _():
        _store_modulated(h_ref, x_ref, g_ref, sc_ref, sh_ref)

    acc = _dot(h_ref[...], w_ref[...])
    p_ref[...] = acc.astype(p_ref.dtype)
    for ref, (tile, lo) in zip(extra_refs, extras):
        @pl.when(j == tile)
        def _(ref=ref, lo=lo):
            ref[...] = acc[:, lo:lo + ref.shape[1]].astype(ref.dtype)


def _inproj(x2d, shift, scale, g, w, *, rows_per_cond, cond0, extras, extra_dtype, tm=1024):
    m = x2d.shape[0]
    assert w.shape == (D_MODEL, P_WIDTH) and m % tm == 0 and rows_per_cond % tm == 0
    tiles_per_cond = rows_per_cond // tm
    cond = lambda i, j: (cond0 + i // tiles_per_cond, 0, 0)
    return pl.pallas_call(
        functools.partial(_inproj_kernel, extras=tuple((t, lo) for t, lo, _ in extras)),
        grid=(m // tm, N_TILES),
        in_specs=[
            pl.BlockSpec((tm, D_MODEL), lambda i, j: (i, 0)),
            pl.BlockSpec((1, 1, D_MODEL), cond),
            pl.BlockSpec((1, 1, D_MODEL), cond),
            pl.BlockSpec((1, D_MODEL), lambda i, j: (0, 0)),
            pl.BlockSpec((D_MODEL, TILE), lambda i, j: (0, j)),
        ],
        out_specs=[pl.BlockSpec((tm, TILE), lambda i, j: (i, j))]
        + [pl.BlockSpec((tm, width), lambda i, j: (i, 0)) for _, _, width in extras],
        out_shape=[jax.ShapeDtypeStruct((m, P_WIDTH), BF16)]
        + [jax.ShapeDtypeStruct((m, width), extra_dtype) for _, _, width in extras],
        scratch_shapes=[pltpu.VMEM((tm, D_MODEL), BF16)],
        compiler_params=pltpu.CompilerParams(dimension_semantics=("parallel", "arbitrary")),
        name="inproj",
    )(x2d, shift, scale, g, w)


def _out_kernel(x_ref, sh_ref, sc_ref, gt_ref, g_ref, fg_ref, oa_ref, ow_ref, of_ref, on_ref,
                wg0_ref, wg1_ref, wg2_ref, wg3_ref, wb_ref, wo_ref, y_ref, h_ref, acc_ref, *, final):
    j = pl.program_id(1)

    @pl.when(j == 0)
    def _():
        _store_modulated(h_ref, x_ref, g_ref, sc_ref, sh_ref)
        acc_ref[...] = jnp.zeros_like(acc_ref)

    h = h_ref[...]
    merged = None
    for i, (o_ref, wg_ref) in enumerate(((oa_ref, wg0_ref), (ow_ref, wg1_ref), (of_ref, wg2_ref), (on_ref, wg3_ref))):
        term = jax.nn.sigmoid(_dot(h, wg_ref[...])) * _dot(o_ref[...], wb_ref[i])
        merged = term if merged is None else merged + term
    acc_ref[...] += _dot(merged.astype(BF16), wo_ref[...])

    @pl.when(j == pl.num_programs(1) - 1)
    def _():
        xn = x_ref[...] + gt_ref[0] * acc_ref[...]
        if final:
            ms = jnp.mean(xn * xn, axis=-1, keepdims=True)
            xn = xn * lax.rsqrt(ms + EPS) * fg_ref[...]
        y_ref[...] = xn


def _out_stage(x2d, shift, scale, gate, g, final_g, o_arrays, o_cols, w_gate, w_branch, w_out,
               *, rows_per_cond, cond0, final, tm=512, tc=512):
    m = x2d.shape[0]
    assert m % tm == 0 and rows_per_cond % tm == 0
    tiles_per_cond = rows_per_cond // tm
    ncol = D_MODEL // tc
    cond = lambda i, j: (cond0 + i // tiles_per_cond, 0, 0)
    o_specs = [pl.BlockSpec((tm, BRANCH_W), functools.partial(lambda i, j, c: (i, c), c=c)) for c in o_cols]
    wg_specs = [pl.BlockSpec((D_MODEL, tc), functools.partial(lambda i, j, b: (0, b * ncol + j), b=b))
                for b in range(N_BRANCH)]
    return pl.pallas_call(
        functools.partial(_out_kernel, final=final),
        grid=(m // tm, ncol),
        in_specs=[
            pl.BlockSpec((tm, D_MODEL), lambda i, j: (i, 0)),
            pl.BlockSpec((1, 1, D_MODEL), cond),
            pl.BlockSpec((1, 1, D_MODEL), cond),
            pl.BlockSpec((1, 1, D_MODEL), cond),
            pl.BlockSpec((1, D_MODEL), lambda i, j: (0, 0)),
            pl.BlockSpec((1, D_MODEL), lambda i, j: (0, 0)),
            *o_specs,
            *wg_specs,
            pl.BlockSpec((N_BRANCH, BRANCH_W, tc), lambda i, j: (0, 0, j)),
            pl.BlockSpec((tc, D_MODEL), lambda i, j: (j, 0)),
        ],
        out_specs=pl.BlockSpec((tm, D_MODEL), lambda i, j: (i, 0)),
        out_shape=jax.ShapeDtypeStruct((m, D_MODEL), F32),
        scratch_shapes=[pltpu.VMEM((tm, D_MODEL), BF16), pltpu.VMEM((tm, D_MODEL), F32)],
        compiler_params=pltpu.CompilerParams(dimension_semantics=("parallel", "arbitrary")),
        name="out_stage",
    )(x2d, shift, scale, gate, g, final_g, *o_arrays, w_gate, w_gate, w_gate, w_gate, w_branch, w_out)


def _attend(scores, values, sinks=None):
    heads = range(len(scores))
    m = []
    for h in heads:
        mh = scores[h][0].max(axis=-1, keepdims=True)
        for s in scores[h][1:]:
            mh = jnp.maximum(mh, s.max(axis=-1, keepdims=True))
        m.append(mh if sinks is None else jnp.maximum(mh, sinks[h]))
    e = [[jnp.exp(s - m[h]) for s in scores[h]] for h in heads]
    den = []
    for h in heads:
        d = e[h][0].sum(axis=-1, keepdims=True)
        for x in e[h][1:]:
            d = d + x.sum(axis=-1, keepdims=True)
        den.append(d if sinks is None else d + jnp.exp(sinks[h] - m[h]))
    out = []
    for h in heads:
        acc = _dot(e[h][0].astype(BF16), values[h][0])
        for x, v in zip(e[h][1:], values[h][1:]):
            acc = acc + _dot(x.astype(BF16), v)
        out.append(acc)
    return [a / d for a, d in zip(out, den)]


def _shift_rows(z, first_row, last_row):
    n = z.shape[0]
    row = lax.broadcasted_iota(jnp.int32, z.shape, 0)
    z_dn = jnp.where(row == 0, first_row, pltpu.roll(z, 1, axis=0))
    z_up = jnp.where(row == n - 1, last_row, pltpu.roll(z, n - 1, axis=0))
    return z_dn, z_up


def _tile(ref, t, dtype=None):
    v = ref[:, t * TILE:(t + 1) * TILE]
    return v if dtype is None else v.astype(dtype)


def _head(ref, t, h, base=0):
    lo = t * TILE + base + h * HEAD_DIM
    return ref[:, lo:lo + HEAD_DIM]


def _ctx_mixer_kernel(p_ref, cw_ref, sink_ref, ct_ref, st_ref, cb_ref, sb_ref, o_ref):
    z = _tile(p_ref, T_AC, F32) * _tile(p_ref, T_AX, F32)
    zero_row = jnp.zeros((1, BRANCH_W), F32)
    z_dn, z_up = _shift_rows(z, zero_row, zero_row)
    y = _tile(p_ref, T_AB, F32) * (z_dn * cw_ref[0:1, :] + z * cw_ref[1:2, :] + z_up * cw_ref[2:3, :])
    o_ref[:, 0:512] = (y * _silu(_tile(p_ref, T_AG, F32))).astype(o_ref.dtype)

    gsz = WIN_HEADS // WIN_KV_HEADS
    scores = [[_dot_nt(_head(p_ref, T_BQ, h), _head(p_ref, T_BKV, h // gsz)) * ATTN_SCALE] for h in range(WIN_HEADS)]
    values = [[_head(p_ref, T_BKV, h // gsz, base=KV_W)] for h in range(WIN_HEADS)]
    heads = _attend(scores, values, sinks=[sink_ref[h:h + 1, 0:1] for h in range(WIN_HEADS)])
    o_w = jnp.concatenate(heads, axis=-1) * _silu(_tile(p_ref, T_BG, F32))
    o_ref[:, 512:1024] = o_w.astype(o_ref.dtype)

    u = _tile(p_ref, T_FU)
    uc = _dot(u, cb_ref[...]).astype(BF16)
    us = _dot(u, sb_ref[...]).astype(BF16)
    o_f = (_dot(ct_ref[...], uc) - _dot(st_ref[...], us)) * _silu(_tile(p_ref, T_FG, F32))
    o_ref[:, 1024:1536] = o_f.astype(o_ref.dtype)

    scores = [[_dot_nt(_head(p_ref, T_DQ, h), _head(p_ref, T_DK, h)) * ATTN_SCALE] for h in range(NA_HEADS)]
    heads = _attend(scores, [[_head(p_ref, T_DV, h)] for h in range(NA_HEADS)])
    o_n = jnp.concatenate(heads, axis=-1) * _silu(_tile(p_ref, T_DG, F32))
    o_ref[:, 1536:2048] = o_n.astype(o_ref.dtype)


def _ctx_mixers(p, conv_w, sink_b, ct, st, cb, sb, *, seq):
    m = p.shape[0]
    whole = lambda a: pl.BlockSpec(a.shape, lambda b: (0,) * a.ndim)
    return pl.pallas_call(
        _ctx_mixer_kernel,
        grid=(m // seq,),
        in_specs=[pl.BlockSpec((seq, P_WIDTH), lambda b: (b, 0)), whole(conv_w), whole(sink_b),
                  whole(ct), whole(st), whole(cb), whole(sb)],
        out_specs=pl.BlockSpec((seq, N_BRANCH * BRANCH_W), lambda b: (b, 0)),
        out_shape=jax.ShapeDtypeStruct((m, N_BRANCH * BRANCH_W), BF16),
        compiler_params=pltpu.CompilerParams(dimension_semantics=("parallel",)),
        name="ctx_mixers",
    )(p, conv_w, sink_b, ct, st, cb, sb)


HALO = 16


def _lat_conv_kernel(ax_ref, ab_ref, ac_ref, ag_ref, axp_ref, acp_ref, axn_ref, acn_ref, cw_ref, o_ref):
    i = pl.program_id(1)
    z = ac_ref[...].astype(F32) * ax_ref[...].astype(F32)
    zp = acp_ref[HALO - 1:HALO, :].astype(F32) * axp_ref[HALO - 1:HALO, :].astype(F32)
    zn = acn_ref[0:1, :].astype(F32) * axn_ref[0:1, :].astype(F32)
    zp = jnp.where(i == 0, 0.0, zp)
    zn = jnp.where(i == pl.num_programs(1) - 1, 0.0, zn)
    z_dn, z_up = _shift_rows(z, zp, zn)
    y = ab_ref[...].astype(F32) * (z_dn * cw_ref[0:1, :] + z * cw_ref[1:2, :] + z_up * cw_ref[2:3, :])
    o_ref[...] = (y * _silu(ag_ref[...].astype(F32))).astype(o_ref.dtype)


def _lat_conv(pa, conv_w, *, batch, seq, tr=512):
    nt = seq // tr
    hb = tr // HALO
    last_halo = batch * seq // HALO - 1
    main = lambda c: pl.BlockSpec((tr, BRANCH_W), functools.partial(lambda b, i, c: (b * nt + i, c), c=c))
    prev = lambda c: pl.BlockSpec(
        (HALO, BRANCH_W), functools.partial(lambda b, i, c: (jnp.maximum((b * nt + i) * hb - 1, 0), c), c=c))
    nxt = lambda c: pl.BlockSpec(
        (HALO, BRANCH_W), functools.partial(lambda b, i, c: (jnp.minimum((b * nt + i + 1) * hb, last_halo), c), c=c))
    return pl.pallas_call(
        _lat_conv_kernel,
        grid=(batch, nt),
        in_specs=[main(T_AX), main(T_AB), main(T_AC), main(T_AG), prev(T_AX), prev(T_AC), nxt(T_AX), nxt(T_AC),
                  pl.BlockSpec(conv_w.shape, lambda b, i: (0, 0))],
        out_specs=pl.BlockSpec((tr, BRANCH_W), lambda b, i: (b * nt + i, 0)),
        out_shape=jax.ShapeDtypeStruct((batch * seq, BRANCH_W), BF16),
        compiler_params=pltpu.CompilerParams(dimension_semantics=("parallel", "arbitrary")),
        name="lat_conv",
    )(pa, pa, pa, pa, pa, pa, pa, pa, conv_w)


def _rope(x, cos, sin_signed):
    lane = lax.broadcasted_iota(jnp.int32, x.shape, 1)
    partner = jnp.where((lane % 32) < 16, pltpu.roll(x, 128 - 16, axis=1), pltpu.roll(x, 16, axis=1))
    return x * cos + partner * sin_signed


def _lat_win_kernel(q_ref, g_ref, kp_ref, kc_ref, kn_ref, vp_ref, vc_ref, vn_ref,
                    cq_ref, sq_ref, cp_ref, sp_ref, cn_ref, sn_ref, ck_ref, cv_ref, sink_ref, o_ref):
    n = pl.program_id(1)
    nb = pl.num_programs(1)
    wb = WIN_BLOCK
    cq, sq = cq_ref[...], sq_ref[...]
    k_rot = jnp.concatenate([
        _rope(kp_ref[...].astype(F32), cp_ref[...], sp_ref[...]),
        _rope(kc_ref[...].astype(F32), cq, sq),
        _rope(kn_ref[...].astype(F32), cn_ref[...], sn_ref[...]),
    ], axis=0).astype(BF16)
    v_all = jnp.concatenate([vp_ref[...], vc_ref[...], vn_ref[...]], axis=0).astype(BF16)
    qi = lax.broadcasted_iota(jnp.int32, (wb, 3 * wb), 0)
    kj = lax.broadcasted_iota(jnp.int32, (wb, 3 * wb), 1)
    mask = (((kj < wb) & (kj >= qi) & (n > 0)) | ((kj >= wb) & (kj < 2 * wb))
            | ((kj >= 2 * wb) & (kj - 2 * wb <= qi) & (n < nb - 1)))
    ck = ck_ref[0].astype(BF16)
    cv = cv_ref[0].astype(BF16)
    gsz = WIN_HEADS // WIN_KV_HEADS
    q_rot = [_rope(q_ref[:, pair * 128:(pair + 1) * 128].astype(F32), cq, sq).astype(BF16)
             for pair in range(WIN_HEADS // 2)]
    scores, values = [], []
    for h in range(WIN_HEADS):
        kv = slice((h // gsz) * 64, (h // gsz + 1) * 64)
        q = q_rot[h // 2][:, (h % 2) * 64:(h % 2 + 1) * 64]
        s_loc = jnp.where(mask, _dot_nt(q, k_rot[:, kv]) * ATTN_SCALE, NEG_INF)
        scores.append([s_loc, _dot_nt(q, ck[:, kv]) * ATTN_SCALE])
        values.append([v_all[:, kv], cv[:, kv]])
    heads = _attend(scores, values, sinks=[sink_ref[h:h + 1, 0:1] for h in range(WIN_HEADS)])
    o = jnp.concatenate(heads, axis=-1) * _silu(g_ref[...].astype(F32))
    o_ref[...] = o.astype(o_ref.dtype)


def _lat_window(pb, cos_t, sin_t, ctx_k, ctx_v, sink_b, *, batch, seq):
    wb = WIN_BLOCK
    k_col = T_BKV * (TILE // KV_W)
    v_col = k_col + 1
    nb = seq // wb
    blk = lambda b, n: b * nb + n
    kv = lambda col, d: pl.BlockSpec(
        (wb, KV_W), functools.partial(lambda b, n, col, d: (blk(b, jnp.clip(n + d, 0, nb - 1)), col), col=col, d=d))
    tab = lambda d: pl.BlockSpec((wb, 128), functools.partial(lambda b, n, d: (jnp.clip(n + d, 0, nb - 1), 0), d=d))
    ctx = pl.BlockSpec((1,) + ctx_k.shape[1:], lambda b, n: (b, 0, 0))
    return pl.pallas_call(
        _lat_win_kernel,
        grid=(batch, nb),
        in_specs=[
            pl.BlockSpec((wb, BRANCH_W), lambda b, n: (blk(b, n), T_BQ)),
            pl.BlockSpec((wb, BRANCH_W), lambda b, n: (blk(b, n), T_BG)),
            kv(k_col, -1), kv(k_col, 0), kv(k_col, 1), kv(v_col, -1), kv(v_col, 0), kv(v_col, 1),
            tab(0), tab(0), tab(-1), tab(-1), tab(1), tab(1),
            ctx, ctx,
            pl.BlockSpec(sink_b.shape, lambda b, n: (0, 0)),
        ],
        out_specs=pl.BlockSpec((wb, BRANCH_W), lambda b, n: (blk(b, n), 0)),
        out_shape=jax.ShapeDtypeStruct((batch * seq, BRANCH_W), BF16),
        compiler_params=pltpu.CompilerParams(dimension_semantics=("parallel", "arbitrary")),
        name="lat_window",
    )(pb, pb, pb, pb, pb, pb, pb, pb, cos_t, sin_t, cos_t, sin_t, cos_t, sin_t, ctx_k, ctx_v, sink_b)


def _fft1_kernel(u_ref, c_ref, s_ref, twr_ref, twi_ref, zr_ref, zi_ref, *, chunks):
    u = u_ref[0].astype(BF16)
    yr = _dot(c_ref[...], u)
    yi = -_dot(s_ref[...], u)
    for t in range(chunks):
        wr = twr_ref[0, :, t:t + 1]
        wi = twi_ref[0, :, t:t + 1]
        a = yr[:, t * 512:(t + 1) * 512]
        b = yi[:, t * 512:(t + 1) * 512]
        zr_ref[0, :, t * 512:(t + 1) * 512] = (a * wr - b * wi).astype(zr_ref.dtype)
        zi_ref[0, :, t * 512:(t + 1) * 512] = (a * wi + b * wr).astype(zi_ref.dtype)


def _fft2_kernel(zr_ref, zi_ref, g_ref, l_ref, cs_ref, o_ref, *, kblock):
    r = FFT_R
    for kk in range(kblock):
        zz = jnp.concatenate([zr_ref[0, kk], zi_ref[0, kk]], axis=0)
        xx = _dot(l_ref[...], zz)
        xcat = jnp.concatenate([xx[0:r], xx[r:2 * r]], axis=1).astype(BF16)
        out = _dot(xcat, cs_ref[...])
        sl = slice(kk * 512, (kk + 1) * 512)
        o_ref[0, :, sl] = (out * _silu(g_ref[0, :, sl].astype(F32))).astype(o_ref.dtype)


def _lat_fourier(fu, fg, tabs, *, batch, seq):
    r = FFT_R
    assert seq == r * r
    wide = r * BRANCH_W
    chunks = 8
    c64, s64, twr, twi, lmat, csmat = tabs
    u3 = fu.reshape(batch, r, wide)
    nct = r // chunks
    zr, zi = pl.pallas_call(
        functools.partial(_fft1_kernel, chunks=chunks),
        grid=(batch, nct),
        in_specs=[
            pl.BlockSpec((1, r, chunks * BRANCH_W), lambda b, t: (b, 0, t)),
            pl.BlockSpec((r, r), lambda b, t: (0, 0)),
            pl.BlockSpec((r, r), lambda b, t: (0, 0)),
            pl.BlockSpec((1, r, chunks), lambda b, t: (t, 0, 0)),
            pl.BlockSpec((1, r, chunks), lambda b, t: (t, 0, 0)),
        ],
        out_specs=[pl.BlockSpec((1, r, chunks * BRANCH_W), lambda b, t: (b, 0, t))] * 2,
        out_shape=[jax.ShapeDtypeStruct((batch, r, wide), BF16)] * 2,
        compiler_params=pltpu.CompilerParams(dimension_semantics=("parallel", "arbitrary")),
        name="lat_fft1",
    )(u3, c64, s64, twr, twi)
    kblock = 8
    z4 = lambda z: z.reshape(batch, r, r, BRANCH_W)
    out = pl.pallas_call(
        functools.partial(_fft2_kernel, kblock=kblock),
        grid=(batch, r // kblock),
        in_specs=[
            pl.BlockSpec((1, kblock, r, BRANCH_W), lambda b, k: (b, k, 0, 0)),
            pl.BlockSpec((1, kblock, r, BRANCH_W), lambda b, k: (b, k, 0, 0)),
            pl.BlockSpec((1, r, kblock * BRANCH_W), lambda b, k: (b, 0, k)),
            pl.BlockSpec((2 * r, 2 * r), lambda b, k: (0, 0)),
            pl.BlockSpec((2 * BRANCH_W, BRANCH_W), lambda b, k: (0, 0)),
        ],
        out_specs=pl.BlockSpec((1, r, kblock * BRANCH_W), lambda b, k: (b, 0, k)),
        out_shape=jax.ShapeDtypeStruct((batch, r, wide), BF16),
        compiler_params=pltpu.CompilerParams(dimension_semantics=("parallel", "arbitrary")),
        name="lat_fft2",
    )(z4(zr), z4(zi), fg.reshape(batch, r, wide), lmat, csmat)
    return out.reshape(batch * seq, BRANCH_W)


N_RPB_ROWS = 2 * NA_ROWS - 1
N_RPB_COLS = 2 * NA_COLS - 1


NA_ROWS_PER_STEP = 4


def _lat_na_kernel(q_ref, g_ref, k_ref, v_ref, bias_ref, ck_ref, cv_ref, o_ref, *, rows):
    ck = ck_ref[0].astype(BF16)
    cv = cv_ref[0].astype(BF16)
    sls = [slice(h * 64, (h + 1) * 64) for h in range(NA_HEADS)]
    scores, values = [], []
    for d in range(NA_ROWS_PER_STEP):
        r = pl.program_id(1) * NA_ROWS_PER_STEP + d
        r0 = jnp.clip(r - NA_ROWS // 2, 0, rows - NA_ROWS)
        start = pl.multiple_of(r0 * GRID_W, GRID_W)
        kw = k_ref[pl.ds(start, NA_ROWS * GRID_W), :]
        vw = v_ref[pl.ds(start, NA_ROWS * GRID_W), :]
        qa = q_ref[d * GRID_W:(d + 1) * GRID_W, :]
        a0 = r0 - r + NA_ROWS - 1
        for h, sl in enumerate(sls):
            bias = jnp.concatenate([bias_ref[h, a0 + 2 * i2] for i2 in range(NA_ROWS // 2)], axis=1)
            scores.append([_dot_nt(qa[:, sl], kw[:, sl]) * ATTN_SCALE + bias,
                           _dot_nt(qa[:, sl], ck[:, sl]) * ATTN_SCALE])
            values.append([vw[:, sl], cv[:, sl]])
    heads = _attend(scores, values)
    o = jnp.concatenate([jnp.concatenate(heads[d * NA_HEADS:(d + 1) * NA_HEADS], axis=-1)
                         for d in range(NA_ROWS_PER_STEP)], axis=0)
    o_ref[...] = (o * _silu(g_ref[...].astype(F32))).astype(o_ref.dtype)


def _lat_na(p, bias, ctx_k, ctx_v, *, batch, seq):
    rows = seq // GRID_W
    steps = rows // NA_ROWS_PER_STEP
    qrows = NA_ROWS_PER_STEP * GRID_W
    ctx = pl.BlockSpec((1,) + ctx_k.shape[1:], lambda b, r: (b, 0, 0))
    return pl.pallas_call(
        functools.partial(_lat_na_kernel, rows=rows),
        grid=(batch, steps),
        in_specs=[
            pl.BlockSpec((qrows, BRANCH_W), lambda b, r: (b * steps + r, T_DQ)),
            pl.BlockSpec((qrows, BRANCH_W), lambda b, r: (b * steps + r, T_DG)),
            pl.BlockSpec((seq, BRANCH_W), lambda b, r: (b, T_DK)),
            pl.BlockSpec((seq, BRANCH_W), lambda b, r: (b, T_DV)),
            pl.BlockSpec(bias.shape, lambda b, r: (0, 0, 0, 0)),
            ctx, ctx,
        ],
        out_specs=pl.BlockSpec((qrows, BRANCH_W), lambda b, r: (b * steps + r, 0)),
        out_shape=jax.ShapeDtypeStruct((batch * seq, BRANCH_W), BF16),
        compiler_params=pltpu.CompilerParams(dimension_semantics=("parallel", "arbitrary")),
        name="lat_na",
    )(p, p, p, p, bias, ctx_k, ctx_v)


def _toeplitz_kernel(r_ref, o_ref):
    x = r_ref[...]
    hi = x.astype(BF16)
    r1 = x - hi.astype(F32)
    mid = r1.astype(BF16)
    lo = (r1 - mid.astype(F32)).astype(BF16)
    shape = (x.shape[1], GRID_W * GRID_W)
    b = lax.broadcasted_iota(jnp.int32, shape, 0)
    col = lax.broadcasted_iota(jnp.int32, shape, 1)
    kc = jnp.bitwise_and(col, GRID_W - 1)
    qc = lax.shift_right_logical(col, GRID_W.bit_length() - 1)
    onehot = jnp.where(kc - qc + (NA_COLS - 1) == b, 1.0, 0.0).astype(BF16)
    o_ref[...] = _dot(hi, onehot) + _dot(mid, onehot) + _dot(lo, onehot)


def _na_bias_tables(na_rpb):
    depth = na_rpb.shape[0]
    n = depth * NA_HEADS * N_RPB_ROWS
    n_pad = -(-n // 8) * 8
    r2 = jnp.zeros((n_pad, 128), F32).at[:n, :N_RPB_COLS].set(na_rpb.reshape(n, N_RPB_COLS))
    flat = pl.pallas_call(
        _toeplitz_kernel,
        out_shape=jax.ShapeDtypeStruct((n_pad, GRID_W * GRID_W), F32),
        name="na_bias_toeplitz",
    )(r2)
    t = flat[:n].reshape(depth, NA_HEADS, N_RPB_ROWS, GRID_W, GRID_W)
    qc = np.arange(GRID_W)[:, None]
    kc = np.arange(GRID_W)[None, :]
    win0 = np.clip(qc - NA_COLS // 2, 0, GRID_W - NA_COLS)
    vis = (kc >= win0) & (kc < win0 + NA_COLS)
    t = jnp.where(vis, t, NEG_INF)
    return jnp.concatenate([t[:, :, :-1], t[:, :, 1:]], axis=-1)


def _bf16_table(t):
    return jnp.asarray(t, F32).astype(BF16)


def _dft_cos_sin(n):
    idx = np.arange(n)
    ang = 2.0 * np.pi * ((idx[:, None] * idx[None, :]) % n) / n
    return np.cos(ang), np.sin(ang)


def _channel_dft_blocks(scale):
    c, s = _dft_cos_sin(FN_GROUP_W)
    eye = np.eye(FN_GROUPS)
    return np.kron(eye, c) * scale, np.kron(eye, s) * scale


def _ctx_fourier_tables(seq):
    ct, st = _dft_cos_sin(seq)
    cb, sb = _channel_dft_blocks(1.0 / np.sqrt(seq * FN_GROUP_W))
    return tuple(_bf16_table(t) for t in (ct, st, cb, sb))


def _lat_fourier_tables(seq, chunks=8):
    r = FFT_R
    c64, s64 = _dft_cos_sin(r)
    k1 = np.arange(r)[:, None]
    t2 = np.arange(r)[None, :]
    ang = 2.0 * np.pi * (k1 * t2) / (r * r)
    twr = np.cos(ang).reshape(r, r // chunks, chunks).transpose(1, 0, 2)
    twi = (-np.sin(ang)).reshape(r, r // chunks, chunks).transpose(1, 0, 2)
    lmat = np.block([[c64, s64], [-s64, c64]])
    cb, sb = _channel_dft_blocks(1.0 / np.sqrt(seq * FN_GROUP_W))
    csmat = np.concatenate([cb, sb], axis=0)
    return (_bf16_table(c64), _bf16_table(s64), jnp.asarray(twr, F32), jnp.asarray(twi, F32),
            _bf16_table(lmat), _bf16_table(csmat))


def _rope_tables(seq):
    half = HEAD_DIM // 2
    quarter = half // 2
    t = jnp.arange(seq)
    inv = ROPE_BASE ** (-jnp.arange(quarter, dtype=F32) / quarter)

    def cs(pos):
        ang = pos.astype(F32)[:, None] * inv[None, :]
        c, s = jnp.cos(ang), jnp.sin(ang)
        return jnp.concatenate([c, c], axis=-1), jnp.concatenate([-s, s], axis=-1)

    cr, sr = cs(t // GRID_W)
    cc, sc = cs(t % GRID_W)
    cos = jnp.concatenate([cr, cc], axis=-1)
    sin = jnp.concatenate([sr, sc], axis=-1)
    return jnp.tile(cos, (1, 2)), jnp.tile(sin, (1, 2))


def _inproj_weights(wl):
    pad = jnp.zeros((D_MODEL, TILE - 2 * KV_W), F32)
    cols = [wl[:, OFF_A:OFF_BQ], wl[:, OFF_BQ:OFF_BK], wl[:, OFF_BG:OFF_FU], wl[:, OFF_BK:OFF_BG], pad,
            wl[:, OFF_FU:N_MAIN]]
    return jnp.concatenate(cols, axis=1).astype(BF16)


def kernel(x_prompt, x_sample, cache_win_k, cache_win_v, cache_na_k, cache_na_v, c, c_ctx, norm_g, w_ada, b_ada,
           w_in, conv_w, win_sink, na_rpb, w_branch, w_out, final_g):
    batch, seq, _ = x_prompt.shape
    dbatch, dseq, _ = x_sample.shape
    past = cache_win_k.shape[2]

    cv8 = jnp.zeros((8, D_MODEL), F32).at[0].set(c_ctx).at[1:1 + dbatch].set(c)
    mod = _modulation(cv8, w_ada, b_ada)

    ctx_tabs = _ctx_fourier_tables(seq)
    lat_tabs = _lat_fourier_tables(dseq)
    cos_t, sin_t = _rope_tables(dseq)
    na_bias = _na_bias_tables(na_rpb)
    final_g2 = final_g.reshape(1, D_MODEL)

    xp = x_prompt.reshape(batch * seq, D_MODEL)
    xs = x_sample.reshape(dbatch * dseq, D_MODEL)
    new_kv = [[], [], [], []]

    for l in range(DEPTH):
        shift = mod[l, :, 0:D_MODEL].reshape(8, 1, D_MODEL)
        scale = mod[l, :, D_MODEL:2 * D_MODEL].reshape(8, 1, D_MODEL)
        gate = mod[l, :, 2 * D_MODEL:].reshape(8, 1, D_MODEL)
        g = norm_g[l].reshape(1, D_MODEL)
        wl = w_in[l]
        w_p = _inproj_weights(wl)
        w_gate = wl[:, N_MAIN:].astype(BF16)
        w_br = w_branch[l].astype(BF16)
        w_o = w_out[l].astype(BF16)
        cw = conv_w[l]
        sink_b = jnp.broadcast_to(win_sink[l][:, None], (WIN_HEADS, 128))
        final = l == DEPTH - 1

        p, k_w, v_w, k_n, v_n = _inproj(
            xp, shift, scale, g, w_p, rows_per_cond=batch * seq, cond0=0, extra_dtype=F32,
            extras=((T_BKV, 0, KV_W), (T_BKV, KV_W, KV_W), (T_DK, 0, BRANCH_W), (T_DV, 0, BRANCH_W)))
        new_kv[0].append(k_w.reshape(batch, seq, WIN_KV_HEADS, HEAD_DIM))
        new_kv[1].append(v_w.reshape(batch, seq, WIN_KV_HEADS, HEAD_DIM))
        new_kv[2].append(k_n.reshape(batch, seq, NA_HEADS, HEAD_DIM))
        new_kv[3].append(v_n.reshape(batch, seq, NA_HEADS, HEAD_DIM))
        o_ctx = _ctx_mixers(p, cw, sink_b, *ctx_tabs, seq=seq)
        xp = _out_stage(xp, shift, scale, gate, g, final_g2, [o_ctx] * 4, [0, 1, 2, 3], w_gate, w_br, w_o,
                        rows_per_cond=batch * seq, cond0=0, final=final)

        q, qfu, qfg = _inproj(xs, shift, scale, g, w_p, rows_per_cond=dseq, cond0=1, extra_dtype=BF16,
                              extras=((T_FU, 0, BRANCH_W), (T_FG, 0, BRANCH_W)))
        o_a = _lat_conv(q, cw, batch=dbatch, seq=dseq)
        o_w = _lat_window(q, cos_t, sin_t,
                          cache_win_k[:, l].reshape(dbatch, past, KV_W), cache_win_v[:, l].reshape(dbatch, past, KV_W),
                          sink_b, batch=dbatch, seq=dseq)
        o_f = _lat_fourier(qfu, qfg, lat_tabs, batch=dbatch, seq=dseq)
        o_n = _lat_na(q, na_bias[l],
                      cache_na_k[:, l].reshape(dbatch, past, BRANCH_W), cache_na_v[:, l].reshape(dbatch, past, BRANCH_W),
                      batch=dbatch, seq=dseq)
        xs = _out_stage(xs, shift, scale, gate, g, final_g2, [o_a, o_w, o_f, o_n], [0, 0, 0, 0], w_gate, w_br, w_o,
                        rows_per_cond=dseq, cond0=1, final=final)

    y_prompt = xp.reshape(batch, seq, D_MODEL)
    y_sample = xs.reshape(dbatch, dseq, D_MODEL)
    return (y_prompt, y_sample) + tuple(jnp.stack(t, axis=1) for t in new_kv)
```

```python
import functools

import numpy as np
import jax
import jax.numpy as jnp
from jax import lax
from jax.experimental import pallas as pl
from jax.experimental.pallas import tpu as pltpu

D_MODEL = 2048
DEPTH = 2
GRID_W = 64
N_BRANCH = 4
BRANCH_W = 512
HEAD_DIM = 64
WIN_HEADS = 8
WIN_KV_HEADS = 2
KV_W = WIN_KV_HEADS * HEAD_DIM
WIN_BLOCK = 128
FN_GROUPS = 4
FN_GROUP_W = BRANCH_W // FN_GROUPS
NA_HEADS = 8
NA_ROWS = 8
NA_COLS = 16
ROPE_BASE = 10000.0
EPS = 1e-6
ATTN_SCALE = HEAD_DIM ** -0.5
NEG_INF = -1e30
N_MAIN = 6400

F32 = jnp.float32
BF16 = jnp.bfloat16

OFF_A = 0
OFF_BQ, OFF_BK, OFF_BV, OFF_BG = 2048, 2560, 2688, 2816
OFF_FU, OFF_FG = 3328, 3840
OFF_DQ, OFF_DK, OFF_DV, OFF_DG = 4352, 4864, 5376, 5888

TILE = 512
T_AX, T_AB, T_AC, T_AG, T_BQ, T_BG, T_BKV, T_FU, T_FG, T_DQ, T_DK, T_DV, T_DG = range(13)
N_TILES = 13
P_WIDTH = N_TILES * TILE

FFT_R = 64


def _silu(x):
    return x * jax.nn.sigmoid(x)


def _dot(a, b):
    return jnp.dot(a, b, preferred_element_type=F32)


def _dot_nt(a, b):
    return lax.dot_general(a, b, (((1,), (1,)), ((), ())), preferred_element_type=F32)


def _rms_mod(x, g, scale, shift):
    ms = jnp.mean(x * x, axis=-1, keepdims=True)
    return (x * lax.rsqrt(ms + EPS) * g) * (1.0 + scale) + shift


def _store_modulated(h_ref, x_ref, g_ref, sc_ref, sh_ref):
    h_ref[...] = _rms_mod(x_ref[...], g_ref[...], sc_ref[0], sh_ref[0]).astype(h_ref.dtype)


def _mod_kernel(cv_ref, w_ref, b_ref, o_ref):
    s = _silu(cv_ref[...]).astype(BF16)
    o_ref[0] = _dot(s, w_ref[0].astype(BF16)) + b_ref[0]


def _modulation(cv8, w_ada, b_ada):
    tn = 512
    n3 = 3 * D_MODEL
    return pl.pallas_call(
        _mod_kernel,
        grid=(DEPTH, n3 // tn),
        in_specs=[
            pl.BlockSpec((8, D_MODEL), lambda l, j: (0, 0)),
            pl.BlockSpec((1, D_MODEL, tn), lambda l, j: (l, 0, j)),
            pl.BlockSpec((1, 1, tn), lambda l, j: (l, 0, j)),
        ],
        out_specs=pl.BlockSpec((1, 8, tn), lambda l, j: (l, 0, j)),
        out_shape=jax.ShapeDtypeStruct((DEPTH, 8, n3), F32),
        name="modulation",
    )(cv8, w_ada, b_ada.reshape(DEPTH, 1, n3))


def _inproj_kernel(x_ref, sh_ref, sc_ref, g_ref, w_ref, *rest, extras, n_updated):
    p_ref, extra_refs, h_ref = rest[n_updated], rest[n_updated + 1:-1], rest[-1]
    j = pl.program_id(1)

    @pl.when(j == 0)
    def _():
        _store_modulated(h_ref, x_ref, g_ref, sc_ref, sh_ref)

    acc = _dot(h_ref[...], w_ref[...])
    p_ref[...] = acc.astype(p_ref.dtype)
    for ref, (tile, lo) in zip(extra_refs, extras):
        @pl.when(j == tile)
        def _(ref=ref, lo=lo):
            cols = acc[:, lo:lo + ref.shape[-1]].astype(ref.dtype)
            if ref.ndim == 2:
                ref[...] = cols
            else:
                ref[:, 0] = cols.reshape(ref.shape[0], ref.shape[2], ref.shape[3])


def _inproj(x2d, shift, scale, g, w, *, rows_per_cond, cond0, extras, extra_dtype=None, update=None, tm=1024):
    m = x2d.shape[0]
    assert w.shape[0] == D_MODEL and w.shape[1] >= P_WIDTH and m % tm == 0 and rows_per_cond % tm == 0
    tiles_per_cond = rows_per_cond // tm
    cond = lambda i, j: (cond0 + i // tiles_per_cond, 0, 0)
    if update is None:
        updated, aliases = [], {}
        extra_specs = [pl.BlockSpec((tm, width), lambda i, j: (i, 0)) for _, _, width in extras]
        extra_shapes = [jax.ShapeDtypeStruct((m, width), extra_dtype) for _, _, width in extras]
    else:
        updated, layer, seq = update
        assert tm % seq == 0 and all(a.shape[2:] == (seq, width) for a, (_, _, width) in zip(updated, extras))
        extra_specs = [pl.BlockSpec((tm // seq, 1, seq, width), lambda i, j: (i, layer, 0, 0)) for _, _, width in extras]
        extra_shapes = [jax.ShapeDtypeStruct(a.shape, a.dtype) for a in updated]
        aliases = {5 + k: 1 + k for k in range(len(updated))}
    return pl.pallas_call(
        functools.partial(_inproj_kernel, extras=tuple((t, lo) for t, lo, _ in extras), n_updated=len(updated)),
        grid=(m // tm, N_TILES),
        in_specs=[
            pl.BlockSpec((tm, D_MODEL), lambda i, j: (i, 0)),
            pl.BlockSpec((1, 1, D_MODEL), cond),
            pl.BlockSpec((1, 1, D_MODEL), cond),
            pl.BlockSpec((1, D_MODEL), lambda i, j: (0, 0)),
            pl.BlockSpec((D_MODEL, TILE), lambda i, j: (0, j)),
        ] + [pl.BlockSpec(memory_space=pl.ANY) for _ in updated],
        out_specs=[pl.BlockSpec((tm, TILE), lambda i, j: (i, j))] + extra_specs,
        out_shape=[jax.ShapeDtypeStruct((m, P_WIDTH), BF16)] + extra_shapes,
        scratch_shapes=[pltpu.VMEM((tm, D_MODEL), BF16)],
        input_output_aliases=aliases,
        compiler_params=pltpu.CompilerParams(dimension_semantics=("parallel", "arbitrary")),
        name="inproj",
    )(x2d, shift, scale, g, w, *updated)


def _out_kernel(x_ref, sh_ref, sc_ref, gt_ref, g_ref, fg_ref, oa_ref, ow_ref, of_ref, on_ref,
                wg0_ref, wg1_ref, wg2_ref, wg3_ref, wb_ref, wo_ref, y_ref, h_ref, acc_ref, *, final):
    j = pl.program_id(1)

    @pl.when(j == 0)
    def _():
        _store_modulated(h_ref, x_ref, g_ref, sc_ref, sh_ref)
        acc_ref[...] = jnp.zeros_like(acc_ref)

    h = h_ref[...]
    merged = None
    for i, (o_ref, wg_ref) in enumerate(((oa_ref, wg0_ref), (ow_ref, wg1_ref), (of_ref, wg2_ref), (on_ref, wg3_ref))):
        term = jax.nn.sigmoid(_dot(h, wg_ref[...])) * _dot(o_ref[...], wb_ref[i])
        merged = term if merged is None else merged + term
    acc_ref[...] += _dot(merged.astype(BF16), wo_ref[...])

    @pl.when(j == pl.num_programs(1) - 1)
    def _():
        xn = x_ref[...] + gt_ref[0] * acc_ref[...]
        if final:
            ms = jnp.mean(xn * xn, axis=-1, keepdims=True)
            xn = xn * lax.rsqrt(ms + EPS) * fg_ref[...]
        y_ref[...] = xn


def _out_stage(x2d, shift, scale, gate, g, final_g, o_arrays, o_cols, w_gate, w_branch, w_out,
               *, rows_per_cond, cond0, final, tm=512, tc=512):
    m = x2d.shape[0]
    assert m % tm == 0 and rows_per_cond % tm == 0
    tiles_per_cond = rows_per_cond // tm
    ncol = D_MODEL // tc
    cond = lambda i, j: (cond0 + i // tiles_per_cond, 0, 0)
    o_specs = [pl.BlockSpec((tm, BRANCH_W), functools.partial(lambda i, j, c: (i, c), c=c)) for c in o_cols]
    assert tc == TILE and w_gate.shape == (D_MODEL, P_WIDTH + N_BRANCH * D_MODEL)
    wg_specs = [pl.BlockSpec((D_MODEL, tc), functools.partial(lambda i, j, b: (0, N_TILES + b * ncol + j), b=b))
                for b in range(N_BRANCH)]
    return pl.pallas_call(
        functools.partial(_out_kernel, final=final),
        grid=(m // tm, ncol),
        in_specs=[
            pl.BlockSpec((tm, D_MODEL), lambda i, j: (i, 0)),
            pl.BlockSpec((1, 1, D_MODEL), cond),
            pl.BlockSpec((1, 1, D_MODEL), cond),
            pl.BlockSpec((1, 1, D_MODEL), cond),
            pl.BlockSpec((1, D_MODEL), lambda i, j: (0, 0)),
            pl.BlockSpec((1, D_MODEL), lambda i, j: (0, 0)),
            *o_specs,
            *wg_specs,
            pl.BlockSpec((N_BRANCH, BRANCH_W, tc), lambda i, j: (0, 0, j)),
            pl.BlockSpec((tc, D_MODEL), lambda i, j: (j, 0)),
        ],
        out_specs=pl.BlockSpec((tm, D_MODEL), lambda i, j: (i, 0)),
        out_shape=jax.ShapeDtypeStruct((m, D_MODEL), F32),
        scratch_shapes=[pltpu.VMEM((tm, D_MODEL), BF16), pltpu.VMEM((tm, D_MODEL), F32)],
        compiler_params=pltpu.CompilerParams(dimension_semantics=("parallel", "arbitrary")),
        name="out_stage",
    )(x2d, shift, scale, gate, g, final_g, *o_arrays, w_gate, w_gate, w_gate, w_gate, w_branch, w_out)


def _attend(scores, values, sinks=None):
    heads = range(len(scores))
    m = []
    for h in heads:
        mh = scores[h][0].max(axis=-1, keepdims=True)
        for s in scores[h][1:]:
            mh = jnp.maximum(mh, s.max(axis=-1, keepdims=True))
        m.append(mh if sinks is None else jnp.maximum(mh, sinks[h]))
    e = [[jnp.exp(s - m[h]) for s in scores[h]] for h in heads]
    den = []
    for h in heads:
        d = e[h][0].sum(axis=-1, keepdims=True)
        for x in e[h][1:]:
            d = d + x.sum(axis=-1, keepdims=True)
        den.append(d if sinks is None else d + jnp.exp(sinks[h] - m[h]))
    out = []
    for h in heads:
        acc = _dot(e[h][0].astype(BF16), values[h][0])
        for x, v in zip(e[h][1:], values[h][1:]):
            acc = acc + _dot(x.astype(BF16), v)
        out.append(acc)
    return [a / d for a, d in zip(out, den)]


def _shift_rows(z, first_row, last_row):
    n = z.shape[0]
    row = lax.broadcasted_iota(jnp.int32, z.shape, 0)
    z_dn = jnp.where(row == 0, first_row, pltpu.roll(z, 1, axis=0))
    z_up = jnp.where(row == n - 1, last_row, pltpu.roll(z, n - 1, axis=0))
    return z_dn, z_up


def _tile(ref, t, dtype=None):
    v = ref[:, t * TILE:(t + 1) * TILE]
    return v if dtype is None else v.astype(dtype)


def _head(ref, t, h, base=0):
    lo = t * TILE + base + h * HEAD_DIM
    return ref[:, lo:lo + HEAD_DIM]


def _ctx_mixer_kernel(p_ref, cw_ref, sink_ref, ct_ref, st_ref, cb_ref, sb_ref, o_ref):
    z = _tile(p_ref, T_AC, F32) * _tile(p_ref, T_AX, F32)
    zero_row = jnp.zeros((1, BRANCH_W), F32)
    z_dn, z_up = _shift_rows(z, zero_row, zero_row)
    y = _tile(p_ref, T_AB, F32) * (z_dn * cw_ref[0:1, :] + z * cw_ref[1:2, :] + z_up * cw_ref[2:3, :])
    o_ref[:, 0:512] = (y * _silu(_tile(p_ref, T_AG, F32))).astype(o_ref.dtype)

    gsz = WIN_HEADS // WIN_KV_HEADS
    scores = [[_dot_nt(_head(p_ref, T_BQ, h), _head(p_ref, T_BKV, h // gsz)) * ATTN_SCALE] for h in range(WIN_HEADS)]
    values = [[_head(p_ref, T_BKV, h // gsz, base=KV_W)] for h in range(WIN_HEADS)]
    heads = _attend(scores, values, sinks=[sink_ref[h:h + 1, 0:1] for h in range(WIN_HEADS)])
    o_w = jnp.concatenate(heads, axis=-1) * _silu(_tile(p_ref, T_BG, F32))
    o_ref[:, 512:1024] = o_w.astype(o_ref.dtype)

    u = _tile(p_ref, T_FU)
    uc = _dot(u, cb_ref[...]).astype(BF16)
    us = _dot(u, sb_ref[...]).astype(BF16)
    o_f = (_dot(ct_ref[...], uc) - _dot(st_ref[...], us)) * _silu(_tile(p_ref, T_FG, F32))
    o_ref[:, 1024:1536] = o_f.astype(o_ref.dtype)

    scores = [[_dot_nt(_head(p_ref, T_DQ, h), _head(p_ref, T_DK, h)) * ATTN_SCALE] for h in range(NA_HEADS)]
    heads = _attend(scores, [[_head(p_ref, T_DV, h)] for h in range(NA_HEADS)])
    o_n = jnp.concatenate(heads, axis=-1) * _silu(_tile(p_ref, T_DG, F32))
    o_ref[:, 1536:2048] = o_n.astype(o_ref.dtype)


def _ctx_mixers(p, conv_w, sink_b, ct, st, cb, sb, *, seq):
    m = p.shape[0]
    whole = lambda a: pl.BlockSpec(a.shape, lambda b: (0,) * a.ndim)
    return pl.pallas_call(
        _ctx_mixer_kernel,
        grid=(m // seq,),
        in_specs=[pl.BlockSpec((seq, P_WIDTH), lambda b: (b, 0)), whole(conv_w), whole(sink_b),
                  whole(ct), whole(st), whole(cb), whole(sb)],
        out_specs=pl.BlockSpec((seq, N_BRANCH * BRANCH_W), lambda b: (b, 0)),
        out_shape=jax.ShapeDtypeStruct((m, N_BRANCH * BRANCH_W), BF16),
        compiler_params=pltpu.CompilerParams(dimension_semantics=("parallel",)),
        name="ctx_mixers",
    )(p, conv_w, sink_b, ct, st, cb, sb)


HALO = 16


def _lat_conv_kernel(ax_ref, ab_ref, ac_ref, ag_ref, axp_ref, acp_ref, axn_ref, acn_ref, cw_ref, o_ref):
    i = pl.program_id(1)
    z = ac_ref[...].astype(F32) * ax_ref[...].astype(F32)
    zp = acp_ref[HALO - 1:HALO, :].astype(F32) * axp_ref[HALO - 1:HALO, :].astype(F32)
    zn = acn_ref[0:1, :].astype(F32) * axn_ref[0:1, :].astype(F32)
    zp = jnp.where(i == 0, 0.0, zp)
    zn = jnp.where(i == pl.num_programs(1) - 1, 0.0, zn)
    z_dn, z_up = _shift_rows(z, zp, zn)
    y = ab_ref[...].astype(F32) * (z_dn * cw_ref[0:1, :] + z * cw_ref[1:2, :] + z_up * cw_ref[2:3, :])
    o_ref[...] = (y * _silu(ag_ref[...].astype(F32))).astype(o_ref.dtype)


def _lat_conv(pa, conv_w, *, batch, seq, tr=512):
    nt = seq // tr
    hb = tr // HALO
    last_halo = batch * seq // HALO - 1
    main = lambda c: pl.BlockSpec((tr, BRANCH_W), functools.partial(lambda b, i, c: (b * nt + i, c), c=c))
    prev = lambda c: pl.BlockSpec(
        (HALO, BRANCH_W), functools.partial(lambda b, i, c: (jnp.maximum((b * nt + i) * hb - 1, 0), c), c=c))
    nxt = lambda c: pl.BlockSpec(
        (HALO, BRANCH_W), functools.partial(lambda b, i, c: (jnp.minimum((b * nt + i + 1) * hb, last_halo), c), c=c))
    return pl.pallas_call(
        _lat_conv_kernel,
        grid=(batch, nt),
        in_specs=[main(T_AX), main(T_AB), main(T_AC), main(T_AG), prev(T_AX), prev(T_AC), nxt(T_AX), nxt(T_AC),
                  pl.BlockSpec(conv_w.shape, lambda b, i: (0, 0))],
        out_specs=pl.BlockSpec((tr, BRANCH_W), lambda b, i: (b * nt + i, 0)),
        out_shape=jax.ShapeDtypeStruct((batch * seq, BRANCH_W), BF16),
        compiler_params=pltpu.CompilerParams(dimension_semantics=("parallel", "arbitrary")),
        name="lat_conv",
    )(pa, pa, pa, pa, pa, pa, pa, pa, conv_w)


def _rope(x, cos, sin_signed):
    lane = lax.broadcasted_iota(jnp.int32, x.shape, 1)
    partner = jnp.where((lane % 32) < 16, pltpu.roll(x, 128 - 16, axis=1), pltpu.roll(x, 16, axis=1))
    return x * cos + partner * sin_signed


def _lat_win_kernel(q_ref, g_ref, kp_ref, kc_ref, kn_ref, vp_ref, vc_ref, vn_ref,
                    cq_ref, sq_ref, cp_ref, sp_ref, cn_ref, sn_ref, ck_ref, cv_ref, sink_ref, o_ref):
    n = pl.program_id(1)
    nb = pl.num_programs(1)
    wb = WIN_BLOCK
    cq, sq = cq_ref[...], sq_ref[...]
    k_rot = jnp.concatenate([
        _rope(kp_ref[...].astype(F32), cp_ref[...], sp_ref[...]),
        _rope(kc_ref[...].astype(F32), cq, sq),
        _rope(kn_ref[...].astype(F32), cn_ref[...], sn_ref[...]),
    ], axis=0).astype(BF16)
    v_all = jnp.concatenate([vp_ref[...], vc_ref[...], vn_ref[...]], axis=0).astype(BF16)
    qi = lax.broadcasted_iota(jnp.int32, (wb, 3 * wb), 0)
    kj = lax.broadcasted_iota(jnp.int32, (wb, 3 * wb), 1)
    mask = (((kj < wb) & (kj >= qi) & (n > 0)) | ((kj >= wb) & (kj < 2 * wb))
            | ((kj >= 2 * wb) & (kj - 2 * wb <= qi) & (n < nb - 1)))
    ck = ck_ref[0].astype(BF16)
    cv = cv_ref[0].astype(BF16)
    gsz = WIN_HEADS // WIN_KV_HEADS
    q_rot = [_rope(q_ref[:, pair * 128:(pair + 1) * 128].astype(F32), cq, sq).astype(BF16)
             for pair in range(WIN_HEADS // 2)]
    scores, values = [], []
    for h in range(WIN_HEADS):
        kv = slice((h // gsz) * 64, (h // gsz + 1) * 64)
        q = q_rot[h // 2][:, (h % 2) * 64:(h % 2 + 1) * 64]
        s_loc = jnp.where(mask, _dot_nt(q, k_rot[:, kv]) * ATTN_SCALE, NEG_INF)
        scores.append([s_loc, _dot_nt(q, ck[:, kv]) * ATTN_SCALE])
        values.append([v_all[:, kv], cv[:, kv]])
    heads = _attend(scores, values, sinks=[sink_ref[h:h + 1, 0:1] for h in range(WIN_HEADS)])
    o = jnp.concatenate(heads, axis=-1) * _silu(g_ref[...].astype(F32))
    o_ref[...] = o.astype(o_ref.dtype)


def _lat_window(pb, cos_t, sin_t, ctx_k, ctx_v, sink_b, *, batch, seq):
    wb = WIN_BLOCK
    k_col = T_BKV * (TILE // KV_W)
    v_col = k_col + 1
    nb = seq // wb
    blk = lambda b, n: b * nb + n
    kv = lambda col, d: pl.BlockSpec(
        (wb, KV_W), functools.partial(lambda b, n, col, d: (blk(b, jnp.clip(n + d, 0, nb - 1)), col), col=col, d=d))
    tab = lambda d: pl.BlockSpec((wb, 128), functools.partial(lambda b, n, d: (jnp.clip(n + d, 0, nb - 1), 0), d=d))
    ctx = pl.BlockSpec((1,) + ctx_k.shape[1:], lambda b, n: (b, 0, 0))
    return pl.pallas_call(
        _lat_win_kernel,
        grid=(batch, nb),
        in_specs=[
            pl.BlockSpec((wb, BRANCH_W), lambda b, n: (blk(b, n), T_BQ)),
            pl.BlockSpec((wb, BRANCH_W), lambda b, n: (blk(b, n), T_BG)),
            kv(k_col, -1), kv(k_col, 0), kv(k_col, 1), kv(v_col, -1), kv(v_col, 0), kv(v_col, 1),
            tab(0), tab(0), tab(-1), tab(-1), tab(1), tab(1),
            ctx, ctx,
            pl.BlockSpec(sink_b.shape, lambda b, n: (0, 0)),
        ],
        out_specs=pl.BlockSpec((wb, BRANCH_W), lambda b, n: (blk(b, n), 0)),
        out_shape=jax.ShapeDtypeStruct((batch * seq, BRANCH_W), BF16),
        compiler_params=pltpu.CompilerParams(dimension_semantics=("parallel", "arbitrary")),
        name="lat_window",
    )(pb, pb, pb, pb, pb, pb, pb, pb, cos_t, sin_t, cos_t, sin_t, cos_t, sin_t, ctx_k, ctx_v, sink_b)


def _fft1_kernel(u_ref, c_ref, s_ref, twr_ref, twi_ref, zr_ref, zi_ref, *, chunks):
    u = u_ref[0].astype(BF16)
    yr = _dot(c_ref[...], u)
    yi = -_dot(s_ref[...], u)
    for t in range(chunks):
        wr = twr_ref[0, :, t:t + 1]
        wi = twi_ref[0, :, t:t + 1]
        a = yr[:, t * 512:(t + 1) * 512]
        b = yi[:, t * 512:(t + 1) * 512]
        zr_ref[0, :, t * 512:(t + 1) * 512] = (a * wr - b * wi).astype(zr_ref.dtype)
        zi_ref[0, :, t * 512:(t + 1) * 512] = (a * wi + b * wr).astype(zi_ref.dtype)


def _fft2_kernel(zr_ref, zi_ref, g_ref, l_ref, cs_ref, o_ref, *, kblock):
    r = FFT_R
    for kk in range(kblock):
        zz = jnp.concatenate([zr_ref[0, kk], zi_ref[0, kk]], axis=0)
        xx = _dot(l_ref[...], zz)
        xcat = jnp.concatenate([xx[0:r], xx[r:2 * r]], axis=1).astype(BF16)
        out = _dot(xcat, cs_ref[...])
        sl = slice(kk * 512, (kk + 1) * 512)
        o_ref[0, :, sl] = (out * _silu(g_ref[0, :, sl].astype(F32))).astype(o_ref.dtype)


def _lat_fourier(fu, fg, tabs, *, batch, seq):
    r = FFT_R
    assert seq == r * r
    wide = r * BRANCH_W
    chunks = 8
    c64, s64, twr, twi, lmat, csmat = tabs
    u3 = fu.reshape(batch, r, wide)
    nct = r // chunks
    zr, zi = pl.pallas_call(
        functools.partial(_fft1_kernel, chunks=chunks),
        grid=(batch, nct),
        in_specs=[
            pl.BlockSpec((1, r, chunks * BRANCH_W), lambda b, t: (b, 0, t)),
            pl.BlockSpec((r, r), lambda b, t: (0, 0)),
            pl.BlockSpec((r, r), lambda b, t: (0, 0)),
            pl.BlockSpec((1, r, chunks), lambda b, t: (t, 0, 0)),
            pl.BlockSpec((1, r, chunks), lambda b, t: (t, 0, 0)),
        ],
        out_specs=[pl.BlockSpec((1, r, chunks * BRANCH_W), lambda b, t: (b, 0, t))] * 2,
        out_shape=[jax.ShapeDtypeStruct((batch, r, wide), BF16)] * 2,
        compiler_params=pltpu.CompilerParams(dimension_semantics=("parallel", "arbitrary")),
        name="lat_fft1",
    )(u3, c64, s64, twr, twi)
    kblock = 8
    z4 = lambda z: z.reshape(batch, r, r, BRANCH_W)
    out = pl.pallas_call(
        functools.partial(_fft2_kernel, kblock=kblock),
        grid=(batch, r // kblock),
        in_specs=[
            pl.BlockSpec((1, kblock, r, BRANCH_W), lambda b, k: (b, k, 0, 0)),
            pl.BlockSpec((1, kblock, r, BRANCH_W), lambda b, k: (b, k, 0, 0)),
            pl.BlockSpec((1, r, kblock * BRANCH_W), lambda b, k: (b, 0, k)),
            pl.BlockSpec((2 * r, 2 * r), lambda b, k: (0, 0)),
            pl.BlockSpec((2 * BRANCH_W, BRANCH_W), lambda b, k: (0, 0)),
        ],
        out_specs=pl.BlockSpec((1, r, kblock * BRANCH_W), lambda b, k: (b, 0, k)),
        out_shape=jax.ShapeDtypeStruct((batch, r, wide), BF16),
        compiler_params=pltpu.CompilerParams(dimension_semantics=("parallel", "arbitrary")),
        name="lat_fft2",
    )(z4(zr), z4(zi), fg.reshape(batch, r, wide), lmat, csmat)
    return out.reshape(batch * seq, BRANCH_W)


N_RPB_ROWS = 2 * NA_ROWS - 1
N_RPB_COLS = 2 * NA_COLS - 1


NA_ROWS_PER_STEP = 4


def _lat_na_kernel(q_ref, g_ref, k_ref, v_ref, bias_ref, ck_ref, cv_ref, o_ref, *, rows):
    ck = ck_ref[0].astype(BF16)
    cv = cv_ref[0].astype(BF16)
    sls = [slice(h * 64, (h + 1) * 64) for h in range(NA_HEADS)]
    scores, values = [], []
    for d in range(NA_ROWS_PER_STEP):
        r = pl.program_id(1) * NA_ROWS_PER_STEP + d
        r0 = jnp.clip(r - NA_ROWS // 2, 0, rows - NA_ROWS)
        start = pl.multiple_of(r0 * GRID_W, GRID_W)
        kw = k_ref[pl.ds(start, NA_ROWS * GRID_W), :]
        vw = v_ref[pl.ds(start, NA_ROWS * GRID_W), :]
        qa = q_ref[d * GRID_W:(d + 1) * GRID_W, :]
        a0 = r0 - r + NA_ROWS - 1
        for h, sl in enumerate(sls):
            bias = jnp.concatenate([bias_ref[h, a0 + 2 * i2] for i2 in range(NA_ROWS // 2)], axis=1)
            scores.append([_dot_nt(qa[:, sl], kw[:, sl]) * ATTN_SCALE + bias,
                           _dot_nt(qa[:, sl], ck[:, sl]) * ATTN_SCALE])
            values.append([vw[:, sl], cv[:, sl]])
    heads = _attend(scores, values)
    o = jnp.concatenate([jnp.concatenate(heads[d * NA_HEADS:(d + 1) * NA_HEADS], axis=-1)
                         for d in range(NA_ROWS_PER_STEP)], axis=0)
    o_ref[...] = (o * _silu(g_ref[...].astype(F32))).astype(o_ref.dtype)


def _lat_na(p, bias, ctx_k, ctx_v, *, batch, seq):
    rows = seq // GRID_W
    steps = rows // NA_ROWS_PER_STEP
    qrows = NA_ROWS_PER_STEP * GRID_W
    ctx = pl.BlockSpec((1,) + ctx_k.shape[1:], lambda b, r: (b, 0, 0))
    return pl.pallas_call(
        functools.partial(_lat_na_kernel, rows=rows),
        grid=(batch, steps),
        in_specs=[
            pl.BlockSpec((qrows, BRANCH_W), lambda b, r: (b * steps + r, T_DQ)),
            pl.BlockSpec((qrows, BRANCH_W), lambda b, r: (b * steps + r, T_DG)),
            pl.BlockSpec((seq, BRANCH_W), lambda b, r: (b, T_DK)),
            pl.BlockSpec((seq, BRANCH_W), lambda b, r: (b, T_DV)),
            pl.BlockSpec(bias.shape, lambda b, r: (0, 0, 0, 0)),
            ctx, ctx,
        ],
        out_specs=pl.BlockSpec((qrows, BRANCH_W), lambda b, r: (b * steps + r, 0)),
        out_shape=jax.ShapeDtypeStruct((batch * seq, BRANCH_W), BF16),
        compiler_params=pltpu.CompilerParams(dimension_semantics=("parallel", "arbitrary")),
        name="lat_na",
    )(p, p, p, p, bias, ctx_k, ctx_v)


def _toeplitz_kernel(r_ref, o_ref):
    x = r_ref[...]
    hi = x.astype(BF16)
    r1 = x - hi.astype(F32)
    mid = r1.astype(BF16)
    lo = (r1 - mid.astype(F32)).astype(BF16)
    shape = (x.shape[1], GRID_W * GRID_W)
    b = lax.broadcasted_iota(jnp.int32, shape, 0)
    col = lax.broadcasted_iota(jnp.int32, shape, 1)
    kc = jnp.bitwise_and(col, GRID_W - 1)
    qc = lax.shift_right_logical(col, GRID_W.bit_length() - 1)
    onehot = jnp.where(kc - qc + (NA_COLS - 1) == b, 1.0, 0.0).astype(BF16)
    o_ref[...] = _dot(hi, onehot) + _dot(mid, onehot) + _dot(lo, onehot)


def _na_bias_tables(na_rpb):
    depth = na_rpb.shape[0]
    n = depth * NA_HEADS * N_RPB_ROWS
    n_pad = -(-n // 8) * 8
    r2 = jnp.zeros((n_pad, 128), F32).at[:n, :N_RPB_COLS].set(na_rpb.reshape(n, N_RPB_COLS))
    flat = pl.pallas_call(
        _toeplitz_kernel,
        out_shape=jax.ShapeDtypeStruct((n_pad, GRID_W * GRID_W), F32),
        name="na_bias_toeplitz",
    )(r2)
    t = flat[:n].reshape(depth, NA_HEADS, N_RPB_ROWS, GRID_W, GRID_W)
    qc = np.arange(GRID_W)[:, None]
    kc = np.arange(GRID_W)[None, :]
    win0 = np.clip(qc - NA_COLS // 2, 0, GRID_W - NA_COLS)
    vis = (kc >= win0) & (kc < win0 + NA_COLS)
    t = jnp.where(vis, t, NEG_INF)
    return jnp.concatenate([t[:, :, :-1], t[:, :, 1:]], axis=-1)


def _bf16_table(t):
    return jnp.asarray(t, F32).astype(BF16)


def _dft_cos_sin(n):
    idx = np.arange(n)
    ang = 2.0 * np.pi * ((idx[:, None] * idx[None, :]) % n) / n
    return np.cos(ang), np.sin(ang)


def _channel_dft_blocks(scale):
    c, s = _dft_cos_sin(FN_GROUP_W)
    eye = np.eye(FN_GROUPS)
    return np.kron(eye, c) * scale, np.kron(eye, s) * scale


def _ctx_fourier_tables(seq):
    ct, st = _dft_cos_sin(seq)
    cb, sb = _channel_dft_blocks(1.0 / np.sqrt(seq * FN_GROUP_W))
    return tuple(_bf16_table(t) for t in (ct, st, cb, sb))


def _lat_fourier_tables(seq, chunks=8):
    r = FFT_R
    c64, s64 = _dft_cos_sin(r)
    k1 = np.arange(r)[:, None]
    t2 = np.arange(r)[None, :]
    ang = 2.0 * np.pi * (k1 * t2) / (r * r)
    twr = np.cos(ang).reshape(r, r // chunks, chunks).transpose(1, 0, 2)
    twi = (-np.sin(ang)).reshape(r, r // chunks, chunks).transpose(1, 0, 2)
    lmat = np.block([[c64, s64], [-s64, c64]])
    cb, sb = _channel_dft_blocks(1.0 / np.sqrt(seq * FN_GROUP_W))
    csmat = np.concatenate([cb, sb], axis=0)
    return (_bf16_table(c64), _bf16_table(s64), jnp.asarray(twr, F32), jnp.asarray(twi, F32),
            _bf16_table(lmat), _bf16_table(csmat))


def _rope_tables(seq):
    half = HEAD_DIM // 2
    quarter = half // 2
    t = jnp.arange(seq)
    inv = ROPE_BASE ** (-jnp.arange(quarter, dtype=F32) / quarter)

    def cs(pos):
        ang = pos.astype(F32)[:, None] * inv[None, :]
        c, s = jnp.cos(ang), jnp.sin(ang)
        return jnp.concatenate([c, c], axis=-1), jnp.concatenate([-s, s], axis=-1)

    cr, sr = cs(t // GRID_W)
    cc, sc = cs(t % GRID_W)
    cos = jnp.concatenate([cr, cc], axis=-1)
    sin = jnp.concatenate([sr, sc], axis=-1)
    return jnp.tile(cos, (1, 2)), jnp.tile(sin, (1, 2))


def _tiled_in_weights(wl):
    pad = jnp.zeros((D_MODEL, TILE - 2 * KV_W), F32)
    cols = [wl[:, OFF_A:OFF_BQ], wl[:, OFF_BQ:OFF_BK], wl[:, OFF_BG:OFF_FU], wl[:, OFF_BK:OFF_BG], pad,
            wl[:, OFF_FU:]]
    return jnp.concatenate(cols, axis=1).astype(BF16)


def kernel(x_prompt, x_sample, cache_win_k, cache_win_v, cache_na_k, cache_na_v, c, c_ctx, norm_g, w_ada, b_ada,
           w_in, conv_w, win_sink, na_rpb, w_branch, w_out, final_g):
    batch, seq, _ = x_prompt.shape
    dbatch, dseq, _ = x_sample.shape
    past = cache_win_k.shape[2]

    cv8 = jnp.zeros((8, D_MODEL), F32).at[0].set(c_ctx).at[1:1 + dbatch].set(c)
    mod = _modulation(cv8, w_ada, b_ada)

    ctx_tabs = _ctx_fourier_tables(seq)
    lat_tabs = _lat_fourier_tables(dseq)
    cos_t, sin_t = _rope_tables(dseq)
    na_bias = _na_bias_tables(na_rpb)
    final_g2 = final_g.reshape(1, D_MODEL)

    xp = x_prompt.reshape(batch * seq, D_MODEL)
    xs = x_sample.reshape(dbatch * dseq, D_MODEL)
    new_kv = [jnp.zeros((batch, DEPTH, seq, width), F32) for width in (KV_W, KV_W, BRANCH_W, BRANCH_W)]

    for l in range(DEPTH):
        shift = mod[l, :, 0:D_MODEL].reshape(8, 1, D_MODEL)
        scale = mod[l, :, D_MODEL:2 * D_MODEL].reshape(8, 1, D_MODEL)
        gate = mod[l, :, 2 * D_MODEL:].reshape(8, 1, D_MODEL)
        g = norm_g[l].reshape(1, D_MODEL)
        w_p = _tiled_in_weights(w_in[l])
        w_br = w_branch[l].astype(BF16)
        w_o = w_out[l].astype(BF16)
        cw = conv_w[l]
        sink_b = jnp.broadcast_to(win_sink[l][:, None], (WIN_HEADS, 128))
        final = l == DEPTH - 1

        p, *new_kv = _inproj(
            xp, shift, scale, g, w_p, rows_per_cond=batch * seq, cond0=0, update=(new_kv, l, seq),
            extras=((T_BKV, 0, KV_W), (T_BKV, KV_W, KV_W), (T_DK, 0, BRANCH_W), (T_DV, 0, BRANCH_W)))
        o_ctx = _ctx_mixers(p, cw, sink_b, *ctx_tabs, seq=seq)
        xp = _out_stage(xp, shift, scale, gate, g, final_g2, [o_ctx] * 4, [0, 1, 2, 3], w_p, w_br, w_o,
                        rows_per_cond=batch * seq, cond0=0, final=final)

        q, qfu, qfg = _inproj(xs, shift, scale, g, w_p, rows_per_cond=dseq, cond0=1, extra_dtype=BF16,
                              extras=((T_FU, 0, BRANCH_W), (T_FG, 0, BRANCH_W)))
        o_a = _lat_conv(q, cw, batch=dbatch, seq=dseq)
        o_w = _lat_window(q, cos_t, sin_t,
                          cache_win_k[:, l].reshape(dbatch, past, KV_W), cache_win_v[:, l].reshape(dbatch, past, KV_W),
                          sink_b, batch=dbatch, seq=dseq)
        o_f = _lat_fourier(qfu, qfg, lat_tabs, batch=dbatch, seq=dseq)
        o_n = _lat_na(q, na_bias[l],
                      cache_na_k[:, l].reshape(dbatch, past, BRANCH_W), cache_na_v[:, l].reshape(dbatch, past, BRANCH_W),
                      batch=dbatch, seq=dseq)
        xs = _out_stage(xs, shift, scale, gate, g, final_g2, [o_a, o_w, o_f, o_n], [0, 0, 0, 0], w_p, w_br, w_o,
                        rows_per_cond=dseq, cond0=1, final=final)

    y_prompt = xp.reshape(batch, seq, D_MODEL)
    y_sample = xs.reshape(dbatch, dseq, D_MODEL)
    heads = (WIN_KV_HEADS, WIN_KV_HEADS, NA_HEADS, NA_HEADS)
    return (y_prompt, y_sample) + tuple(a.reshape(batch, DEPTH, seq, n, HEAD_DIM) for a, n in zip(new_kv, heads))
```

```python
import functools

import numpy as np
import jax
import jax.numpy as jnp
from jax import lax
from jax.experimental import pallas as pl
from jax.experimental.pallas import tpu as pltpu

D_MODEL = 2048
DEPTH = 2
GRID_W = 64
N_BRANCH = 4
BRANCH_W = 512
HEAD_DIM = 64
WIN_HEADS = 8
WIN_KV_HEADS = 2
KV_W = WIN_KV_HEADS * HEAD_DIM
WIN_BLOCK = 128
FN_GROUPS = 4
FN_GROUP_W = BRANCH_W // FN_GROUPS
NA_HEADS = 8
NA_ROWS = 8
NA_COLS = 16
ROPE_BASE = 10000.0
EPS = 1e-6
ATTN_SCALE = HEAD_DIM ** -0.5
NEG_INF = -1e30
N_MAIN = 6400

F32 = jnp.float32
BF16 = jnp.bfloat16

OFF_A = 0
OFF_BQ, OFF_BK, OFF_BV, OFF_BG = 2048, 2560, 2688, 2816
OFF_FU, OFF_FG = 3328, 3840
OFF_DQ, OFF_DK, OFF_DV, OFF_DG = 4352, 4864, 5376, 5888

TILE = 512
T_AX, T_AB, T_AC, T_AG, T_BQ, T_BG, T_BKV, T_FU, T_FG, T_DQ, T_DK, T_DV, T_DG = range(13)
N_TILES = 13
P_WIDTH = N_TILES * TILE

FFT_R = 64


def _silu(x):
    return x * jax.nn.sigmoid(x)


def _dot(a, b):
    return jnp.dot(a, b, preferred_element_type=F32)


def _dot_nt(a, b):
    return lax.dot_general(a, b, (((1,), (1,)), ((), ())), preferred_element_type=F32)


def _rms_mod(x, g, scale, shift):
    ms = jnp.mean(x * x, axis=-1, keepdims=True)
    return (x * lax.rsqrt(ms + EPS) * g) * (1.0 + scale) + shift


def _store_modulated(h_ref, x_ref, g_ref, sc_ref, sh_ref):
    h_ref[...] = _rms_mod(x_ref[...], g_ref[...], sc_ref[0], sh_ref[0]).astype(h_ref.dtype)


def _mod_kernel(cv_ref, w_ref, b_ref, o_ref):
    s = _silu(cv_ref[...]).astype(BF16)
    o_ref[0] = _dot(s, w_ref[0].astype(BF16)) + b_ref[0]


def _modulation(cv8, w_ada, b_ada):
    tn = 512
    n3 = 3 * D_MODEL
    return pl.pallas_call(
        _mod_kernel,
        grid=(DEPTH, n3 // tn),
        in_specs=[
            pl.BlockSpec((8, D_MODEL), lambda l, j: (0, 0)),
            pl.BlockSpec((1, D_MODEL, tn), lambda l, j: (l, 0, j)),
            pl.BlockSpec((1, 1, tn), lambda l, j: (l, 0, j)),
        ],
        out_specs=pl.BlockSpec((1, 8, tn), lambda l, j: (l, 0, j)),
        out_shape=jax.ShapeDtypeStruct((DEPTH, 8, n3), F32),
        name="modulation",
    )(cv8, w_ada, b_ada.reshape(DEPTH, 1, n3))


def _inproj_kernel(x_ref, sh_ref, sc_ref, g_ref, w_ref, *rest, extras, n_updated):
    p_ref, h_ref, extra_refs = rest[n_updated], rest[n_updated + 1], rest[n_updated + 2:]
    j = pl.program_id(1)

    @pl.when(j == 0)
    def _():
        _store_modulated(h_ref, x_ref, g_ref, sc_ref, sh_ref)

    acc = _dot(h_ref[...], w_ref[...])
    p_ref[...] = acc.astype(p_ref.dtype)
    for ref, (tile, lo) in zip(extra_refs, extras):
        @pl.when(j == tile)
        def _(ref=ref, lo=lo):
            cols = acc[:, lo:lo + ref.shape[-1]].astype(ref.dtype)
            if ref.ndim == 2:
                ref[...] = cols
            else:
                ref[:, 0] = cols.reshape(ref.shape[0], ref.shape[2], ref.shape[3])


def _inproj(x2d, shift, scale, g, w, *, rows_per_cond, cond0, extras, extra_dtype=None, update=None, tm=1024):
    m = x2d.shape[0]
    assert w.shape[0] == D_MODEL and w.shape[1] >= P_WIDTH and m % tm == 0 and rows_per_cond % tm == 0
    tiles_per_cond = rows_per_cond // tm
    cond = lambda i, j: (cond0 + i // tiles_per_cond, 0, 0)
    if update is None:
        updated, aliases = [], {}
        extra_specs = [pl.BlockSpec((tm, width), lambda i, j: (i, 0)) for _, _, width in extras]
        extra_shapes = [jax.ShapeDtypeStruct((m, width), extra_dtype) for _, _, width in extras]
    else:
        updated, layer, seq = update
        assert tm % seq == 0 and all(a.shape[2:] == (seq, width) for a, (_, _, width) in zip(updated, extras))
        extra_specs = [pl.BlockSpec((tm // seq, 1, seq, width), lambda i, j: (i, layer, 0, 0)) for _, _, width in extras]
        extra_shapes = [jax.ShapeDtypeStruct(a.shape, a.dtype) for a in updated]
        aliases = {5 + k: 2 + k for k in range(len(updated))}
    return pl.pallas_call(
        functools.partial(_inproj_kernel, extras=tuple((t, lo) for t, lo, _ in extras), n_updated=len(updated)),
        grid=(m // tm, N_TILES),
        in_specs=[
            pl.BlockSpec((tm, D_MODEL), lambda i, j: (i, 0)),
            pl.BlockSpec((1, 1, D_MODEL), cond),
            pl.BlockSpec((1, 1, D_MODEL), cond),
            pl.BlockSpec((1, D_MODEL), lambda i, j: (0, 0)),
            pl.BlockSpec((D_MODEL, TILE), lambda i, j: (0, j)),
        ] + [pl.BlockSpec(memory_space=pl.ANY) for _ in updated],
        out_specs=[pl.BlockSpec((tm, TILE), lambda i, j: (i, j)), pl.BlockSpec((tm, D_MODEL), lambda i, j: (i, 0))]
        + extra_specs,
        out_shape=[jax.ShapeDtypeStruct((m, P_WIDTH), BF16), jax.ShapeDtypeStruct((m, D_MODEL), BF16)] + extra_shapes,
        input_output_aliases=aliases,
        compiler_params=pltpu.CompilerParams(dimension_semantics=("parallel", "arbitrary")),
        name="inproj",
    )(x2d, shift, scale, g, w, *updated)


def _out_kernel(x_ref, h_ref, gt_ref, fg_ref, oa_ref, ow_ref, of_ref, on_ref,
                wg0_ref, wg1_ref, wg2_ref, wg3_ref, wb_ref, wo_ref, y_ref, *, final):
    j = pl.program_id(1)

    @pl.when(j == 0)
    def _():
        y_ref[...] = jnp.zeros_like(y_ref)

    h = h_ref[...]
    merged = None
    for i, (o_ref, wg_ref) in enumerate(((oa_ref, wg0_ref), (ow_ref, wg1_ref), (of_ref, wg2_ref), (on_ref, wg3_ref))):
        term = jax.nn.sigmoid(_dot(h, wg_ref[...])) * _dot(o_ref[...], wb_ref[i])
        merged = term if merged is None else merged + term
    y_ref[...] += _dot(merged.astype(BF16), wo_ref[...])

    @pl.when(j == pl.num_programs(1) - 1)
    def _():
        xn = x_ref[...] + gt_ref[0] * y_ref[...]
        if final:
            ms = jnp.mean(xn * xn, axis=-1, keepdims=True)
            xn = xn * lax.rsqrt(ms + EPS) * fg_ref[...]
        y_ref[...] = xn


def _out_stage(x2d, h2d, gate, final_g, o_arrays, o_cols, w_gate, w_branch, w_out,
               *, rows_per_cond, cond0, final, tm=512, tc=512):
    m = x2d.shape[0]
    assert m % tm == 0 and rows_per_cond % tm == 0
    tiles_per_cond = rows_per_cond // tm
    ncol = D_MODEL // tc
    cond = lambda i, j: (cond0 + i // tiles_per_cond, 0, 0)
    o_specs = [pl.BlockSpec((tm, BRANCH_W), functools.partial(lambda i, j, c: (i, c), c=c)) for c in o_cols]
    assert tc == TILE and w_gate.shape == (D_MODEL, P_WIDTH + N_BRANCH * D_MODEL)
    wg_specs = [pl.BlockSpec((D_MODEL, tc), functools.partial(lambda i, j, b: (0, N_TILES + b * ncol + j), b=b))
                for b in range(N_BRANCH)]
    return pl.pallas_call(
        functools.partial(_out_kernel, final=final),
        grid=(m // tm, ncol),
        in_specs=[
            pl.BlockSpec((tm, D_MODEL), lambda i, j: (i, 0)),
            pl.BlockSpec((tm, D_MODEL), lambda i, j: (i, 0)),
            pl.BlockSpec((1, 1, D_MODEL), cond),
            pl.BlockSpec((1, D_MODEL), lambda i, j: (0, 0)),
            *o_specs,
            *wg_specs,
            pl.BlockSpec((N_BRANCH, BRANCH_W, tc), lambda i, j: (0, 0, j)),
            pl.BlockSpec((tc, D_MODEL), lambda i, j: (j, 0)),
        ],
        out_specs=pl.BlockSpec((tm, D_MODEL), lambda i, j: (i, 0)),
        out_shape=jax.ShapeDtypeStruct((m, D_MODEL), F32),
        compiler_params=pltpu.CompilerParams(dimension_semantics=("parallel", "arbitrary")),
        name="out_stage",
    )(x2d, h2d, gate, final_g, *o_arrays, w_gate, w_gate, w_gate, w_gate, w_branch, w_out)


def _attend(scores, values, sinks=None):
    heads = range(len(scores))
    m = []
    for h in heads:
        mh = scores[h][0].max(axis=-1, keepdims=True)
        for s in scores[h][1:]:
            mh = jnp.maximum(mh, s.max(axis=-1, keepdims=True))
        m.append(mh if sinks is None else jnp.maximum(mh, sinks[h]))
    e = [[jnp.exp(s - m[h]) for s in scores[h]] for h in heads]
    den = []
    for h in heads:
        d = e[h][0].sum(axis=-1, keepdims=True)
        for x in e[h][1:]:
            d = d + x.sum(axis=-1, keepdims=True)
        den.append(d if sinks is None else d + jnp.exp(sinks[h] - m[h]))
    out = []
    for h in heads:
        acc = _dot(e[h][0].astype(BF16), values[h][0])
        for x, v in zip(e[h][1:], values[h][1:]):
            acc = acc + _dot(x.astype(BF16), v)
        out.append(acc)
    return [a / d for a, d in zip(out, den)]


def _shift_rows(z, first_row, last_row):
    n = z.shape[0]
    row = lax.broadcasted_iota(jnp.int32, z.shape, 0)
    z_dn = jnp.where(row == 0, first_row, pltpu.roll(z, 1, axis=0))
    z_up = jnp.where(row == n - 1, last_row, pltpu.roll(z, n - 1, axis=0))
    return z_dn, z_up


def _tile(ref, t, dtype=None):
    v = ref[:, t * TILE:(t + 1) * TILE]
    return v if dtype is None else v.astype(dtype)


def _head(ref, t, h, base=0):
    lo = t * TILE + base + h * HEAD_DIM
    return ref[:, lo:lo + HEAD_DIM]


def _ctx_mixer_kernel(p_ref, cw_ref, sink_ref, ct_ref, st_ref, cb_ref, sb_ref, o_ref):
    z = _tile(p_ref, T_AC, F32) * _tile(p_ref, T_AX, F32)
    zero_row = jnp.zeros((1, BRANCH_W), F32)
    z_dn, z_up = _shift_rows(z, zero_row, zero_row)
    y = _tile(p_ref, T_AB, F32) * (z_dn * cw_ref[0:1, :] + z * cw_ref[1:2, :] + z_up * cw_ref[2:3, :])
    o_ref[:, 0:512] = (y * _silu(_tile(p_ref, T_AG, F32))).astype(o_ref.dtype)

    gsz = WIN_HEADS // WIN_KV_HEADS
    scores = [[_dot_nt(_head(p_ref, T_BQ, h), _head(p_ref, T_BKV, h // gsz)) * ATTN_SCALE] for h in range(WIN_HEADS)]
    values = [[_head(p_ref, T_BKV, h // gsz, base=KV_W)] for h in range(WIN_HEADS)]
    heads = _attend(scores, values, sinks=[sink_ref[h:h + 1, 0:1] for h in range(WIN_HEADS)])
    o_w = jnp.concatenate(heads, axis=-1) * _silu(_tile(p_ref, T_BG, F32))
    o_ref[:, 512:1024] = o_w.astype(o_ref.dtype)

    u = _tile(p_ref, T_FU)
    uc = _dot(u, cb_ref[...]).astype(BF16)
    us = _dot(u, sb_ref[...]).astype(BF16)
    o_f = (_dot(ct_ref[...], uc) - _dot(st_ref[...], us)) * _silu(_tile(p_ref, T_FG, F32))
    o_ref[:, 1024:1536] = o_f.astype(o_ref.dtype)

    scores = [[_dot_nt(_head(p_ref, T_DQ, h), _head(p_ref, T_DK, h)) * ATTN_SCALE] for h in range(NA_HEADS)]
    heads = _attend(scores, [[_head(p_ref, T_DV, h)] for h in range(NA_HEADS)])
    o_n = jnp.concatenate(heads, axis=-1) * _silu(_tile(p_ref, T_DG, F32))
    o_ref[:, 1536:2048] = o_n.astype(o_ref.dtype)


def _ctx_mixers(p, conv_w, sink_b, ct, st, cb, sb, *, seq):
    m = p.shape[0]
    whole = lambda a: pl.BlockSpec(a.shape, lambda b: (0,) * a.ndim)
    return pl.pallas_call(
        _ctx_mixer_kernel,
        grid=(m // seq,),
        in_specs=[pl.BlockSpec((seq, P_WIDTH), lambda b: (b, 0)), whole(conv_w), whole(sink_b),
                  whole(ct), whole(st), whole(cb), whole(sb)],
        out_specs=pl.BlockSpec((seq, N_BRANCH * BRANCH_W), lambda b: (b, 0)),
        out_shape=jax.ShapeDtypeStruct((m, N_BRANCH * BRANCH_W), BF16),
        compiler_params=pltpu.CompilerParams(dimension_semantics=("parallel",)),
        name="ctx_mixers",
    )(p, conv_w, sink_b, ct, st, cb, sb)


HALO = 16


def _lat_conv_kernel(ax_ref, ab_ref, ac_ref, ag_ref, axp_ref, acp_ref, axn_ref, acn_ref, cw_ref, o_ref):
    i = pl.program_id(1)
    z = ac_ref[...].astype(F32) * ax_ref[...].astype(F32)
    zp = acp_ref[HALO - 1:HALO, :].astype(F32) * axp_ref[HALO - 1:HALO, :].astype(F32)
    zn = acn_ref[0:1, :].astype(F32) * axn_ref[0:1, :].astype(F32)
    zp = jnp.where(i == 0, 0.0, zp)
    zn = jnp.where(i == pl.num_programs(1) - 1, 0.0, zn)
    z_dn, z_up = _shift_rows(z, zp, zn)
    y = ab_ref[...].astype(F32) * (z_dn * cw_ref[0:1, :] + z * cw_ref[1:2, :] + z_up * cw_ref[2:3, :])
    o_ref[...] = (y * _silu(ag_ref[...].astype(F32))).astype(o_ref.dtype)


def _lat_conv(pa, conv_w, *, batch, seq, tr=512):
    nt = seq // tr
    hb = tr // HALO
    last_halo = batch * seq // HALO - 1
    main = lambda c: pl.BlockSpec((tr, BRANCH_W), functools.partial(lambda b, i, c: (b * nt + i, c), c=c))
    prev = lambda c: pl.BlockSpec(
        (HALO, BRANCH_W), functools.partial(lambda b, i, c: (jnp.maximum((b * nt + i) * hb - 1, 0), c), c=c))
    nxt = lambda c: pl.BlockSpec(
        (HALO, BRANCH_W), functools.partial(lambda b, i, c: (jnp.minimum((b * nt + i + 1) * hb, last_halo), c), c=c))
    return pl.pallas_call(
        _lat_conv_kernel,
        grid=(batch, nt),
        in_specs=[main(T_AX), main(T_AB), main(T_AC), main(T_AG), prev(T_AX), prev(T_AC), nxt(T_AX), nxt(T_AC),
                  pl.BlockSpec(conv_w.shape, lambda b, i: (0, 0))],
        out_specs=pl.BlockSpec((tr, BRANCH_W), lambda b, i: (b * nt + i, 0)),
        out_shape=jax.ShapeDtypeStruct((batch * seq, BRANCH_W), BF16),
        compiler_params=pltpu.CompilerParams(dimension_semantics=("parallel", "arbitrary")),
        name="lat_conv",
    )(pa, pa, pa, pa, pa, pa, pa, pa, conv_w)


def _rope(x, cos, sin_signed):
    lane = lax.broadcasted_iota(jnp.int32, x.shape, 1)
    partner = jnp.where((lane % 32) < 16, pltpu.roll(x, 128 - 16, axis=1), pltpu.roll(x, 16, axis=1))
    return x * cos + partner * sin_signed


WIN_BLOCKS_PER_STEP = 2


def _lat_win_kernel(q_ref, g_ref, kp_ref, kc_ref, kn_ref, vp_ref, vc_ref, vn_ref,
                    cq_ref, sq_ref, cp_ref, sp_ref, cn_ref, sn_ref, ck_ref, cv_ref, sink_ref, o_ref):
    nq = WIN_BLOCKS_PER_STEP
    nb = pl.num_programs(1) * nq
    wb = WIN_BLOCK
    cq, sq = cq_ref[...], sq_ref[...]
    k_rot = jnp.concatenate([
        _rope(kp_ref[...].astype(F32), cp_ref[...], sp_ref[...]),
        _rope(kc_ref[...].astype(F32), cq, sq),
        _rope(kn_ref[...].astype(F32), cn_ref[...], sn_ref[...]),
    ], axis=0).astype(BF16)
    v_all = jnp.concatenate([vp_ref[...], vc_ref[...], vn_ref[...]], axis=0).astype(BF16)
    qi = lax.broadcasted_iota(jnp.int32, (wb, 3 * wb), 0)
    kj = lax.broadcasted_iota(jnp.int32, (wb, 3 * wb), 1)
    ck = ck_ref[0].astype(BF16)
    cv = cv_ref[0].astype(BF16)
    gsz = WIN_HEADS // WIN_KV_HEADS
    q_rot = [_rope(q_ref[:, pair * 128:(pair + 1) * 128].astype(F32), cq, sq).astype(BF16)
             for pair in range(WIN_HEADS // 2)]
    scores, values, sinks = [], [], []
    for d in range(nq):
        n = pl.program_id(1) * nq + d
        mask = (((kj < wb) & (kj >= qi) & (n > 0)) | ((kj >= wb) & (kj < 2 * wb))
                | ((kj >= 2 * wb) & (kj - 2 * wb <= qi) & (n < nb - 1)))
        rows = slice(d * wb, (d + 1) * wb)
        win = slice(d * wb, (d + 3) * wb)
        for h in range(WIN_HEADS):
            kv = slice((h // gsz) * 64, (h // gsz + 1) * 64)
            q = q_rot[h // 2][rows, (h % 2) * 64:(h % 2 + 1) * 64]
            s_loc = jnp.where(mask, _dot_nt(q, k_rot[win, kv]) * ATTN_SCALE, NEG_INF)
            scores.append([s_loc, _dot_nt(q, ck[:, kv]) * ATTN_SCALE])
            values.append([v_all[win, kv], cv[:, kv]])
            sinks.append(sink_ref[h:h + 1, 0:1])
    heads = _attend(scores, values, sinks=sinks)
    o = jnp.concatenate([jnp.concatenate(heads[d * WIN_HEADS:(d + 1) * WIN_HEADS], axis=-1) for d in range(nq)], axis=0)
    o_ref[...] = (o * _silu(g_ref[...].astype(F32))).astype(o_ref.dtype)


def _lat_window(pb, cos_t, sin_t, ctx_k, ctx_v, sink_b, *, batch, seq):
    wb = WIN_BLOCK
    k_col = T_BKV * (TILE // KV_W)
    v_col = k_col + 1
    nq = WIN_BLOCKS_PER_STEP
    nb = seq // wb
    steps = nb // nq
    nbr = lambda s, d: jnp.clip(s * nq + (nq if d > 0 else -1), 0, nb - 1)
    own = lambda col, width: pl.BlockSpec(
        (nq * wb, width), functools.partial(lambda b, s, col: (b * steps + s, col), col=col))
    kv = lambda col, d: pl.BlockSpec(
        (wb, KV_W), functools.partial(lambda b, s, col, d: (b * nb + nbr(s, d), col), col=col, d=d))
    tab = lambda d: pl.BlockSpec((wb, 128), functools.partial(lambda b, s, d: (nbr(s, d), 0), d=d))
    tab_own = pl.BlockSpec((nq * wb, 128), lambda b, s: (s, 0))
    ctx = pl.BlockSpec((1,) + ctx_k.shape[1:], lambda b, s: (b, 0, 0))
    return pl.pallas_call(
        _lat_win_kernel,
        grid=(batch, steps),
        in_specs=[
            own(T_BQ, BRANCH_W), own(T_BG, BRANCH_W),
            kv(k_col, -1), own(k_col, KV_W), kv(k_col, 1), kv(v_col, -1), own(v_col, KV_W), kv(v_col, 1),
            tab_own, tab_own, tab(-1), tab(-1), tab(1), tab(1),
            ctx, ctx,
            pl.BlockSpec(sink_b.shape, lambda b, s: (0, 0)),
        ],
        out_specs=pl.BlockSpec((nq * wb, BRANCH_W), lambda b, s: (b * steps + s, 0)),
        out_shape=jax.ShapeDtypeStruct((batch * seq, BRANCH_W), BF16),
        compiler_params=pltpu.CompilerParams(dimension_semantics=("parallel", "arbitrary")),
        name="lat_window",
    )(pb, pb, pb, pb, pb, pb, pb, pb, cos_t, sin_t, cos_t, sin_t, cos_t, sin_t, ctx_k, ctx_v, sink_b)


def _fft1_kernel(u_ref, c_ref, s_ref, twr_ref, twi_ref, zr_ref, zi_ref, *, chunks):
    u = u_ref[0].astype(BF16)
    yr = _dot(c_ref[...], u)
    yi = -_dot(s_ref[...], u)
    for t in range(chunks):
        wr = twr_ref[0, :, t:t + 1]
        wi = twi_ref[0, :, t:t + 1]
        a = yr[:, t * 512:(t + 1) * 512]
        b = yi[:, t * 512:(t + 1) * 512]
        zr_ref[0, :, t * 512:(t + 1) * 512] = (a * wr - b * wi).astype(zr_ref.dtype)
        zi_ref[0, :, t * 512:(t + 1) * 512] = (a * wi + b * wr).astype(zi_ref.dtype)


def _fft2_kernel(zr_ref, zi_ref, g_ref, l_ref, cs_ref, o_ref, *, kblock):
    r = FFT_R
    for kk in range(kblock):
        zz = jnp.concatenate([zr_ref[0, kk], zi_ref[0, kk]], axis=0)
        xx = _dot(l_ref[...], zz)
        xcat = jnp.concatenate([xx[0:r], xx[r:2 * r]], axis=1).astype(BF16)
        out = _dot(xcat, cs_ref[...])
        sl = slice(kk * 512, (kk + 1) * 512)
        o_ref[0, :, sl] = (out * _silu(g_ref[0, :, sl].astype(F32))).astype(o_ref.dtype)


def _lat_fourier(fu, fg, tabs, *, batch, seq):
    r = FFT_R
    assert seq == r * r
    wide = r * BRANCH_W
    chunks = 8
    c64, s64, twr, twi, lmat, csmat = tabs
    u3 = fu.reshape(batch, r, wide)
    nct = r // chunks
    zr, zi = pl.pallas_call(
        functools.partial(_fft1_kernel, chunks=chunks),
        grid=(batch, nct),
        in_specs=[
            pl.BlockSpec((1, r, chunks * BRANCH_W), lambda b, t: (b, 0, t)),
            pl.BlockSpec((r, r), lambda b, t: (0, 0)),
            pl.BlockSpec((r, r), lambda b, t: (0, 0)),
            pl.BlockSpec((1, r, chunks), lambda b, t: (t, 0, 0)),
            pl.BlockSpec((1, r, chunks), lambda b, t: (t, 0, 0)),
        ],
        out_specs=[pl.BlockSpec((1, r, chunks * BRANCH_W), lambda b, t: (b, 0, t))] * 2,
        out_shape=[jax.ShapeDtypeStruct((batch, r, wide), BF16)] * 2,
        compiler_params=pltpu.CompilerParams(dimension_semantics=("parallel", "arbitrary")),
        name="lat_fft1",
    )(u3, c64, s64, twr, twi)
    kblock = 8
    z4 = lambda z: z.reshape(batch, r, r, BRANCH_W)
    out = pl.pallas_call(
        functools.partial(_fft2_kernel, kblock=kblock),
        grid=(batch, r // kblock),
        in_specs=[
            pl.BlockSpec((1, kblock, r, BRANCH_W), lambda b, k: (b, k, 0, 0)),
            pl.BlockSpec((1, kblock, r, BRANCH_W), lambda b, k: (b, k, 0, 0)),
            pl.BlockSpec((1, r, kblock * BRANCH_W), lambda b, k: (b, 0, k)),
            pl.BlockSpec((2 * r, 2 * r), lambda b, k: (0, 0)),
            pl.BlockSpec((2 * BRANCH_W, BRANCH_W), lambda b, k: (0, 0)),
        ],
        out_specs=pl.BlockSpec((1, r, kblock * BRANCH_W), lambda b, k: (b, 0, k)),
        out_shape=jax.ShapeDtypeStruct((batch, r, wide), BF16),
        compiler_params=pltpu.CompilerParams(dimension_semantics=("parallel", "arbitrary")),
        name="lat_fft2",
    )(z4(zr), z4(zi), fg.reshape(batch, r, wide), lmat, csmat)
    return out.reshape(batch * seq, BRANCH_W)


N_RPB_ROWS = 2 * NA_ROWS - 1
N_RPB_COLS = 2 * NA_COLS - 1


NA_ROWS_PER_STEP = 4


def _lat_na_kernel(q_ref, g_ref, k_ref, v_ref, bias_ref, ck_ref, cv_ref, o_ref, *, rows):
    ck = ck_ref[0].astype(BF16)
    cv = cv_ref[0].astype(BF16)
    sls = [slice(h * 64, (h + 1) * 64) for h in range(NA_HEADS)]
    scores, values = [], []
    for d in range(NA_ROWS_PER_STEP):
        r = pl.program_id(1) * NA_ROWS_PER_STEP + d
        r0 = jnp.clip(r - NA_ROWS // 2, 0, rows - NA_ROWS)
        start = pl.multiple_of(r0 * GRID_W, GRID_W)
        kw = k_ref[pl.ds(start, NA_ROWS * GRID_W), :]
        vw = v_ref[pl.ds(start, NA_ROWS * GRID_W), :]
        qa = q_ref[d * GRID_W:(d + 1) * GRID_W, :]
        a0 = r0 - r + NA_ROWS - 1
        for h, sl in enumerate(sls):
            bias = jnp.concatenate([bias_ref[h, a0 + 2 * i2] for i2 in range(NA_ROWS // 2)], axis=1)
            scores.append([_dot_nt(qa[:, sl], kw[:, sl]) * ATTN_SCALE + bias,
                           _dot_nt(qa[:, sl], ck[:, sl]) * ATTN_SCALE])
            values.append([vw[:, sl], cv[:, sl]])
    heads = _attend(scores, values)
    o = jnp.concatenate([jnp.concatenate(heads[d * NA_HEADS:(d + 1) * NA_HEADS], axis=-1)
                         for d in range(NA_ROWS_PER_STEP)], axis=0)
    o_ref[...] = (o * _silu(g_ref[...].astype(F32))).astype(o_ref.dtype)


def _lat_na(p, bias, ctx_k, ctx_v, *, batch, seq):
    rows = seq // GRID_W
    steps = rows // NA_ROWS_PER_STEP
    qrows = NA_ROWS_PER_STEP * GRID_W
    ctx = pl.BlockSpec((1,) + ctx_k.shape[1:], lambda b, r: (b, 0, 0))
    return pl.pallas_call(
        functools.partial(_lat_na_kernel, rows=rows),
        grid=(batch, steps),
        in_specs=[
            pl.BlockSpec((qrows, BRANCH_W), lambda b, r: (b * steps + r, T_DQ)),
            pl.BlockSpec((qrows, BRANCH_W), lambda b, r: (b * steps + r, T_DG)),
            pl.BlockSpec((seq, BRANCH_W), lambda b, r: (b, T_DK)),
            pl.BlockSpec((seq, BRANCH_W), lambda b, r: (b, T_DV)),
            pl.BlockSpec(bias.shape, lambda b, r: (0, 0, 0, 0)),
            ctx, ctx,
        ],
        out_specs=pl.BlockSpec((qrows, BRANCH_W), lambda b, r: (b * steps + r, 0)),
        out_shape=jax.ShapeDtypeStruct((batch * seq, BRANCH_W), BF16),
        compiler_params=pltpu.CompilerParams(dimension_semantics=("parallel", "arbitrary")),
        name="lat_na",
    )(p, p, p, p, bias, ctx_k, ctx_v)


def _toeplitz_kernel(r_ref, o_ref):
    x = r_ref[...]
    hi = x.astype(BF16)
    r1 = x - hi.astype(F32)
    mid = r1.astype(BF16)
    lo = (r1 - mid.astype(F32)).astype(BF16)
    shape = (x.shape[1], GRID_W * GRID_W)
    b = lax.broadcasted_iota(jnp.int32, shape, 0)
    col = lax.broadcasted_iota(jnp.int32, shape, 1)
    kc = jnp.bitwise_and(col, GRID_W - 1)
    qc = lax.shift_right_logical(col, GRID_W.bit_length() - 1)
    onehot = jnp.where(kc - qc + (NA_COLS - 1) == b, 1.0, 0.0).astype(BF16)
    o_ref[...] = _dot(hi, onehot) + _dot(mid, onehot) + _dot(lo, onehot)


def _na_bias_tables(na_rpb):
    depth = na_rpb.shape[0]
    n = depth * NA_HEADS * N_RPB_ROWS
    n_pad = -(-n // 8) * 8
    r2 = jnp.zeros((n_pad, 128), F32).at[:n, :N_RPB_COLS].set(na_rpb.reshape(n, N_RPB_COLS))
    flat = pl.pallas_call(
        _toeplitz_kernel,
        out_shape=jax.ShapeDtypeStruct((n_pad, GRID_W * GRID_W), F32),
        name="na_bias_toeplitz",
    )(r2)
    t = flat[:n].reshape(depth, NA_HEADS, N_RPB_ROWS, GRID_W, GRID_W)
    qc = np.arange(GRID_W)[:, None]
    kc = np.arange(GRID_W)[None, :]
    win0 = np.clip(qc - NA_COLS // 2, 0, GRID_W - NA_COLS)
    vis = (kc >= win0) & (kc < win0 + NA_COLS)
    t = jnp.where(vis, t, NEG_INF)
    return jnp.concatenate([t[:, :, :-1], t[:, :, 1:]], axis=-1)


def _bf16_table(t):
    return jnp.asarray(t, F32).astype(BF16)


def _dft_cos_sin(n):
    idx = np.arange(n)
    ang = 2.0 * np.pi * ((idx[:, None] * idx[None, :]) % n) / n
    return np.cos(ang), np.sin(ang)


def _channel_dft_blocks(scale):
    c, s = _dft_cos_sin(FN_GROUP_W)
    eye = np.eye(FN_GROUPS)
    return np.kron(eye, c) * scale, np.kron(eye, s) * scale


def _ctx_fourier_tables(seq):
    ct, st = _dft_cos_sin(seq)
    cb, sb = _channel_dft_blocks(1.0 / np.sqrt(seq * FN_GROUP_W))
    return tuple(_bf16_table(t) for t in (ct, st, cb, sb))


def _lat_fourier_tables(seq, chunks=8):
    r = FFT_R
    c64, s64 = _dft_cos_sin(r)
    k1 = np.arange(r)[:, None]
    t2 = np.arange(r)[None, :]
    ang = 2.0 * np.pi * (k1 * t2) / (r * r)
    twr = np.cos(ang).reshape(r, r // chunks, chunks).transpose(1, 0, 2)
    twi = (-np.sin(ang)).reshape(r, r // chunks, chunks).transpose(1, 0, 2)
    lmat = np.block([[c64, s64], [-s64, c64]])
    cb, sb = _channel_dft_blocks(1.0 / np.sqrt(seq * FN_GROUP_W))
    csmat = np.concatenate([cb, sb], axis=0)
    return (_bf16_table(c64), _bf16_table(s64), jnp.asarray(twr, F32), jnp.asarray(twi, F32),
            _bf16_table(lmat), _bf16_table(csmat))


def _rope_tables(seq):
    half = HEAD_DIM // 2
    quarter = half // 2
    t = jnp.arange(seq)
    inv = ROPE_BASE ** (-jnp.arange(quarter, dtype=F32) / quarter)

    def cs(pos):
        ang = pos.astype(F32)[:, None] * inv[None, :]
        c, s = jnp.cos(ang), jnp.sin(ang)
        return jnp.concatenate([c, c], axis=-1), jnp.concatenate([-s, s], axis=-1)

    cr, sr = cs(t // GRID_W)
    cc, sc = cs(t % GRID_W)
    cos = jnp.concatenate([cr, cc], axis=-1)
    sin = jnp.concatenate([sr, sc], axis=-1)
    return jnp.tile(cos, (1, 2)), jnp.tile(sin, (1, 2))


def _tiled_in_weights(wl):
    pad = jnp.zeros((D_MODEL, TILE - 2 * KV_W), F32)
    cols = [wl[:, OFF_A:OFF_BQ], wl[:, OFF_BQ:OFF_BK], wl[:, OFF_BG:OFF_FU], wl[:, OFF_BK:OFF_BG], pad,
            wl[:, OFF_FU:]]
    return jnp.concatenate(cols, axis=1).astype(BF16)


def kernel(x_prompt, x_sample, cache_win_k, cache_win_v, cache_na_k, cache_na_v, c, c_ctx, norm_g, w_ada, b_ada,
           w_in, conv_w, win_sink, na_rpb, w_branch, w_out, final_g):
    batch, seq, _ = x_prompt.shape
    dbatch, dseq, _ = x_sample.shape
    past = cache_win_k.shape[2]

    cv8 = jnp.zeros((8, D_MODEL), F32).at[0].set(c_ctx).at[1:1 + dbatch].set(c)
    mod = _modulation(cv8, w_ada, b_ada)

    ctx_tabs = _ctx_fourier_tables(seq)
    lat_tabs = _lat_fourier_tables(dseq)
    cos_t, sin_t = _rope_tables(dseq)
    na_bias = _na_bias_tables(na_rpb)
    final_g2 = final_g.reshape(1, D_MODEL)

    xp = x_prompt.reshape(batch * seq, D_MODEL)
    xs = x_sample.reshape(dbatch * dseq, D_MODEL)
    new_kv = [jnp.zeros((batch, DEPTH, seq, width), F32) for width in (KV_W, KV_W, BRANCH_W, BRANCH_W)]

    for l in range(DEPTH):
        shift = mod[l, :, 0:D_MODEL].reshape(8, 1, D_MODEL)
        scale = mod[l, :, D_MODEL:2 * D_MODEL].reshape(8, 1, D_MODEL)
        gate = mod[l, :, 2 * D_MODEL:].reshape(8, 1, D_MODEL)
        g = norm_g[l].reshape(1, D_MODEL)
        w_p = _tiled_in_weights(w_in[l])
        w_br = w_branch[l].astype(BF16)
        w_o = w_out[l].astype(BF16)
        cw = conv_w[l]
        sink_b = jnp.broadcast_to(win_sink[l][:, None], (WIN_HEADS, 128))
        final = l == DEPTH - 1

        p, hp, *new_kv = _inproj(
            xp, shift, scale, g, w_p, rows_per_cond=batch * seq, cond0=0, update=(new_kv, l, seq),
            extras=((T_BKV, 0, KV_W), (T_BKV, KV_W, KV_W), (T_DK, 0, BRANCH_W), (T_DV, 0, BRANCH_W)))
        o_ctx = _ctx_mixers(p, cw, sink_b, *ctx_tabs, seq=seq)
        xp = _out_stage(xp, hp, gate, final_g2, [o_ctx] * 4, [0, 1, 2, 3], w_p, w_br, w_o,
                        rows_per_cond=batch * seq, cond0=0, final=final)

        q, hs, qfu, qfg = _inproj(xs, shift, scale, g, w_p, rows_per_cond=dseq, cond0=1, extra_dtype=BF16,
                              extras=((T_FU, 0, BRANCH_W), (T_FG, 0, BRANCH_W)))
        o_a = _lat_conv(q, cw, batch=dbatch, seq=dseq)
        o_w = _lat_window(q, cos_t, sin_t,
                          cache_win_k[:, l].reshape(dbatch, past, KV_W), cache_win_v[:, l].reshape(dbatch, past, KV_W),
                          sink_b, batch=dbatch, seq=dseq)
        o_f = _lat_fourier(qfu, qfg, lat_tabs, batch=dbatch, seq=dseq)
        o_n = _lat_na(q, na_bias[l],
                      cache_na_k[:, l].reshape(dbatch, past, BRANCH_W), cache_na_v[:, l].reshape(dbatch, past, BRANCH_W),
                      batch=dbatch, seq=dseq)
        xs = _out_stage(xs, hs, gate, final_g2, [o_a, o_w, o_f, o_n], [0, 0, 0, 0], w_p, w_br, w_o,
                        rows_per_cond=dseq, cond0=1, final=final)

    y_prompt = xp.reshape(batch, seq, D_MODEL)
    y_sample = xs.reshape(dbatch, dseq, D_MODEL)
    heads = (WIN_KV_HEADS, WIN_KV_HEADS, NA_HEADS, NA_HEADS)
    return (y_prompt, y_sample) + tuple(a.reshape(batch, DEPTH, seq, n, HEAD_DIM) for a, n in zip(new_kv, heads))
```

```python
import functools

import numpy as np
import jax
import jax.numpy as jnp
from jax import lax
from jax.experimental import pallas as pl
from jax.experimental.pallas import tpu as pltpu

D_MODEL = 2048
DEPTH = 2
GRID_W = 64
N_BRANCH = 4
BRANCH_W = 512
HEAD_DIM = 64
WIN_HEADS = 8
WIN_KV_HEADS = 2
KV_W = WIN_KV_HEADS * HEAD_DIM
WIN_BLOCK = 128
FN_GROUPS = 4
FN_GROUP_W = BRANCH_W // FN_GROUPS
NA_HEADS = 8
NA_ROWS = 8
NA_COLS = 16
ROPE_BASE = 10000.0
EPS = 1e-6
ATTN_SCALE = HEAD_DIM ** -0.5
NEG_INF = -1e30
N_MAIN = 6400

F32 = jnp.float32
BF16 = jnp.bfloat16

OFF_A = 0
OFF_BQ, OFF_BK, OFF_BV, OFF_BG = 2048, 2560, 2688, 2816
OFF_FU, OFF_FG = 3328, 3840
OFF_DQ, OFF_DK, OFF_DV, OFF_DG = 4352, 4864, 5376, 5888

TILE = 512
T_AX, T_AB, T_AC, T_AG, T_BQ, T_BG, T_BKV, T_FU, T_FG, T_DQ, T_DK, T_DV, T_DG = range(13)
N_TILES = 13
P_WIDTH = N_TILES * TILE

FFT_R = 64


def _silu(x):
    return x * jax.nn.sigmoid(x)


def _dot(a, b):
    return jnp.dot(a, b, preferred_element_type=F32)


def _dot_nt(a, b):
    return lax.dot_general(a, b, (((1,), (1,)), ((), ())), preferred_element_type=F32)


def _rms_mod(x, g, scale, shift):
    ms = jnp.mean(x * x, axis=-1, keepdims=True)
    return (x * lax.rsqrt(ms + EPS) * g) * (1.0 + scale) + shift


def _store_modulated(h_ref, x_ref, g_ref, sc_ref, sh_ref):
    h_ref[...] = _rms_mod(x_ref[...], g_ref[...], sc_ref[0], sh_ref[0]).astype(h_ref.dtype)


def _mod_kernel(cv_ref, w_ref, b_ref, o_ref):
    s = _silu(cv_ref[...]).astype(BF16)
    o_ref[0] = _dot(s, w_ref[0].astype(BF16)) + b_ref[0]


def _modulation(cv8, w_ada, b_ada):
    tn = 512
    n3 = 3 * D_MODEL
    return pl.pallas_call(
        _mod_kernel,
        grid=(DEPTH, n3 // tn),
        in_specs=[
            pl.BlockSpec((8, D_MODEL), lambda l, j: (0, 0)),
            pl.BlockSpec((1, D_MODEL, tn), lambda l, j: (l, 0, j)),
            pl.BlockSpec((1, 1, tn), lambda l, j: (l, 0, j)),
        ],
        out_specs=pl.BlockSpec((1, 8, tn), lambda l, j: (l, 0, j)),
        out_shape=jax.ShapeDtypeStruct((DEPTH, 8, n3), F32),
        name="modulation",
    )(cv8, w_ada, b_ada.reshape(DEPTH, 1, n3))


def _inproj_kernel(x_ref, sh_ref, sc_ref, g_ref, w_ref, *rest, extras, n_updated):
    p_ref, h_ref, extra_refs = rest[n_updated], rest[n_updated + 1], rest[n_updated + 2:]
    j = pl.program_id(1)

    @pl.when(j == 0)
    def _():
        _store_modulated(h_ref, x_ref, g_ref, sc_ref, sh_ref)

    acc = _dot(h_ref[...], w_ref[...])
    p_ref[...] = acc.astype(p_ref.dtype)
    for ref, (tile, lo) in zip(extra_refs, extras):
        @pl.when(j == tile)
        def _(ref=ref, lo=lo):
            cols = acc[:, lo:lo + ref.shape[-1]].astype(ref.dtype)
            if ref.ndim == 2:
                ref[...] = cols
            else:
                ref[:, 0] = cols.reshape(ref.shape[0], ref.shape[2], ref.shape[3])


def _inproj(x2d, shift, scale, g, w, *, rows_per_cond, cond0, extras, extra_dtype=None, update=None, tm=1024):
    m = x2d.shape[0]
    assert w.shape[0] == D_MODEL and w.shape[1] >= P_WIDTH and m % tm == 0 and rows_per_cond % tm == 0
    tiles_per_cond = rows_per_cond // tm
    cond = lambda i, j: (cond0 + i // tiles_per_cond, 0, 0)
    if update is None:
        updated, aliases = [], {}
        extra_specs = [pl.BlockSpec((tm, width), lambda i, j: (i, 0)) for _, _, width in extras]
        extra_shapes = [jax.ShapeDtypeStruct((m, width), extra_dtype) for _, _, width in extras]
    else:
        updated, layer, seq = update
        assert tm % seq == 0 and all(a.shape[2:] == (seq, width) for a, (_, _, width) in zip(updated, extras))
        extra_specs = [pl.BlockSpec((tm // seq, 1, seq, width), lambda i, j: (i, layer, 0, 0)) for _, _, width in extras]
        extra_shapes = [jax.ShapeDtypeStruct(a.shape, a.dtype) for a in updated]
        aliases = {5 + k: 2 + k for k in range(len(updated))}
    return pl.pallas_call(
        functools.partial(_inproj_kernel, extras=tuple((t, lo) for t, lo, _ in extras), n_updated=len(updated)),
        grid=(m // tm, N_TILES),
        in_specs=[
            pl.BlockSpec((tm, D_MODEL), lambda i, j: (i, 0)),
            pl.BlockSpec((1, 1, D_MODEL), cond),
            pl.BlockSpec((1, 1, D_MODEL), cond),
            pl.BlockSpec((1, D_MODEL), lambda i, j: (0, 0)),
            pl.BlockSpec((D_MODEL, TILE), lambda i, j: (0, j)),
        ] + [pl.BlockSpec(memory_space=pl.ANY) for _ in updated],
        out_specs=[pl.BlockSpec((tm, TILE), lambda i, j: (i, j)), pl.BlockSpec((tm, D_MODEL), lambda i, j: (i, 0))]
        + extra_specs,
        out_shape=[jax.ShapeDtypeStruct((m, P_WIDTH), BF16), jax.ShapeDtypeStruct((m, D_MODEL), BF16)] + extra_shapes,
        input_output_aliases=aliases,
        compiler_params=pltpu.CompilerParams(dimension_semantics=("parallel", "arbitrary")),
        name="inproj",
    )(x2d, shift, scale, g, w, *updated)


V7X_VMEM_BYTES = 64 * 2 ** 20
OUT_STAGE_VMEM_BYTES = V7X_VMEM_BYTES - 2 ** 20


def _out_kernel(x_ref, h_ref, gt_ref, fg_ref, oa_ref, ow_ref, of_ref, on_ref,
                wg0_ref, wg1_ref, wg2_ref, wg3_ref, wb_ref, wo_ref, y_ref, merged_ref, *, final):
    j = pl.program_id(1)
    h = h_ref[...]
    merged = None
    for i, (o_ref, wg_ref) in enumerate(((oa_ref, wg0_ref), (ow_ref, wg1_ref), (of_ref, wg2_ref), (on_ref, wg3_ref))):
        term = jax.nn.sigmoid(_dot(h, wg_ref[...])) * _dot(o_ref[...], wb_ref[i])
        merged = term if merged is None else merged + term
    merged_ref[j] = merged.astype(BF16)

    @pl.when(j == pl.num_programs(1) - 1)
    def _():
        m_all = jnp.concatenate([merged_ref[k] for k in range(merged_ref.shape[0])], axis=1)
        xn = x_ref[...] + gt_ref[0] * _dot(m_all, wo_ref[...])
        if final:
            ms = jnp.mean(xn * xn, axis=-1, keepdims=True)
            xn = xn * lax.rsqrt(ms + EPS) * fg_ref[...]
        y_ref[...] = xn


def _out_stage(x2d, h2d, gate, final_g, o_arrays, o_cols, w_gate, w_branch, w_out,
               *, rows_per_cond, cond0, final, tm=512, tc=512):
    m = x2d.shape[0]
    assert m % tm == 0 and rows_per_cond % tm == 0
    tiles_per_cond = rows_per_cond // tm
    ncol = D_MODEL // tc
    cond = lambda i, j: (cond0 + i // tiles_per_cond, 0, 0)
    o_specs = [pl.BlockSpec((tm, BRANCH_W), functools.partial(lambda i, j, c: (i, c), c=c)) for c in o_cols]
    assert tc == TILE and w_gate.shape == (D_MODEL, P_WIDTH + N_BRANCH * D_MODEL)
    wg_specs = [pl.BlockSpec((D_MODEL, tc), functools.partial(lambda i, j, b: (0, N_TILES + b * ncol + j), b=b))
                for b in range(N_BRANCH)]
    return pl.pallas_call(
        functools.partial(_out_kernel, final=final),
        grid=(m // tm, ncol),
        in_specs=[
            pl.BlockSpec((tm, D_MODEL), lambda i, j: (i, 0)),
            pl.BlockSpec((tm, D_MODEL), lambda i, j: (i, 0)),
            pl.BlockSpec((1, 1, D_MODEL), cond),
            pl.BlockSpec((1, D_MODEL), lambda i, j: (0, 0)),
            *o_specs,
            *wg_specs,
            pl.BlockSpec((N_BRANCH, BRANCH_W, tc), lambda i, j: (0, 0, j)),
            pl.BlockSpec((D_MODEL, D_MODEL), lambda i, j: (0, 0), pipeline_mode=pl.Buffered(1)),
        ],
        out_specs=pl.BlockSpec((tm, D_MODEL), lambda i, j: (i, 0)),
        out_shape=jax.ShapeDtypeStruct((m, D_MODEL), F32),
        scratch_shapes=[pltpu.VMEM((ncol, tm, tc), BF16)],
        compiler_params=pltpu.CompilerParams(dimension_semantics=("parallel", "arbitrary"),
                                             vmem_limit_bytes=OUT_STAGE_VMEM_BYTES),
        name="out_stage",
    )(x2d, h2d, gate, final_g, *o_arrays, w_gate, w_gate, w_gate, w_gate, w_branch, w_out)


def _attend(scores, values, sinks=None):
    heads = range(len(scores))
    m = []
    for h in heads:
        mh = scores[h][0].max(axis=-1, keepdims=True)
        for s in scores[h][1:]:
            mh = jnp.maximum(mh, s.max(axis=-1, keepdims=True))
        m.append(mh if sinks is None else jnp.maximum(mh, sinks[h]))
    e = [[jnp.exp(s - m[h]) for s in scores[h]] for h in heads]
    den = []
    for h in heads:
        d = e[h][0].sum(axis=-1, keepdims=True)
        for x in e[h][1:]:
            d = d + x.sum(axis=-1, keepdims=True)
        den.append(d if sinks is None else d + jnp.exp(sinks[h] - m[h]))
    out = []
    for h in heads:
        acc = _dot(e[h][0].astype(BF16), values[h][0])
        for x, v in zip(e[h][1:], values[h][1:]):
            acc = acc + _dot(x.astype(BF16), v)
        out.append(acc)
    return [a / d for a, d in zip(out, den)]


def _shift_rows(z, first_row, last_row):
    n = z.shape[0]
    row = lax.broadcasted_iota(jnp.int32, z.shape, 0)
    z_dn = jnp.where(row == 0, first_row, pltpu.roll(z, 1, axis=0))
    z_up = jnp.where(row == n - 1, last_row, pltpu.roll(z, n - 1, axis=0))
    return z_dn, z_up


def _tile(ref, t, dtype=None):
    v = ref[:, t * TILE:(t + 1) * TILE]
    return v if dtype is None else v.astype(dtype)


def _head(ref, t, h, base=0):
    lo = t * TILE + base + h * HEAD_DIM
    return ref[:, lo:lo + HEAD_DIM]


def _ctx_mixer_kernel(p_ref, cw_ref, sink_ref, ct_ref, st_ref, cb_ref, sb_ref, o_ref):
    z = _tile(p_ref, T_AC, F32) * _tile(p_ref, T_AX, F32)
    zero_row = jnp.zeros((1, BRANCH_W), F32)
    z_dn, z_up = _shift_rows(z, zero_row, zero_row)
    y = _tile(p_ref, T_AB, F32) * (z_dn * cw_ref[0:1, :] + z * cw_ref[1:2, :] + z_up * cw_ref[2:3, :])
    o_ref[:, 0:512] = (y * _silu(_tile(p_ref, T_AG, F32))).astype(o_ref.dtype)

    gsz = WIN_HEADS // WIN_KV_HEADS
    scores = [[_dot_nt(_head(p_ref, T_BQ, h), _head(p_ref, T_BKV, h // gsz)) * ATTN_SCALE] for h in range(WIN_HEADS)]
    values = [[_head(p_ref, T_BKV, h // gsz, base=KV_W)] for h in range(WIN_HEADS)]
    heads = _attend(scores, values, sinks=[sink_ref[h:h + 1, 0:1] for h in range(WIN_HEADS)])
    o_w = jnp.concatenate(heads, axis=-1) * _silu(_tile(p_ref, T_BG, F32))
    o_ref[:, 512:1024] = o_w.astype(o_ref.dtype)

    u = _tile(p_ref, T_FU)
    uc = _dot(u, cb_ref[...]).astype(BF16)
    us = _dot(u, sb_ref[...]).astype(BF16)
    o_f = (_dot(ct_ref[...], uc) - _dot(st_ref[...], us)) * _silu(_tile(p_ref, T_FG, F32))
    o_ref[:, 1024:1536] = o_f.astype(o_ref.dtype)

    scores = [[_dot_nt(_head(p_ref, T_DQ, h), _head(p_ref, T_DK, h)) * ATTN_SCALE] for h in range(NA_HEADS)]
    heads = _attend(scores, [[_head(p_ref, T_DV, h)] for h in range(NA_HEADS)])
    o_n = jnp.concatenate(heads, axis=-1) * _silu(_tile(p_ref, T_DG, F32))
    o_ref[:, 1536:2048] = o_n.astype(o_ref.dtype)


def _ctx_mixers(p, conv_w, sink_b, ct, st, cb, sb, *, seq):
    m = p.shape[0]
    whole = lambda a: pl.BlockSpec(a.shape, lambda b: (0,) * a.ndim)
    return pl.pallas_call(
        _ctx_mixer_kernel,
        grid=(m // seq,),
        in_specs=[pl.BlockSpec((seq, P_WIDTH), lambda b: (b, 0)), whole(conv_w), whole(sink_b),
                  whole(ct), whole(st), whole(cb), whole(sb)],
        out_specs=pl.BlockSpec((seq, N_BRANCH * BRANCH_W), lambda b: (b, 0)),
        out_shape=jax.ShapeDtypeStruct((m, N_BRANCH * BRANCH_W), BF16),
        compiler_params=pltpu.CompilerParams(dimension_semantics=("parallel",)),
        name="ctx_mixers",
    )(p, conv_w, sink_b, ct, st, cb, sb)


HALO = 16


def _lat_conv_kernel(ax_ref, ab_ref, ac_ref, ag_ref, axp_ref, acp_ref, axn_ref, acn_ref, cw_ref, o_ref):
    i = pl.program_id(1)
    z = ac_ref[...].astype(F32) * ax_ref[...].astype(F32)
    zp = acp_ref[HALO - 1:HALO, :].astype(F32) * axp_ref[HALO - 1:HALO, :].astype(F32)
    zn = acn_ref[0:1, :].astype(F32) * axn_ref[0:1, :].astype(F32)
    zp = jnp.where(i == 0, 0.0, zp)
    zn = jnp.where(i == pl.num_programs(1) - 1, 0.0, zn)
    z_dn, z_up = _shift_rows(z, zp, zn)
    y = ab_ref[...].astype(F32) * (z_dn * cw_ref[0:1, :] + z * cw_ref[1:2, :] + z_up * cw_ref[2:3, :])
    o_ref[...] = (y * _silu(ag_ref[...].astype(F32))).astype(o_ref.dtype)


def _lat_conv(pa, conv_w, *, batch, seq, tr=512):
    nt = seq // tr
    hb = tr // HALO
    last_halo = batch * seq // HALO - 1
    main = lambda c: pl.BlockSpec((tr, BRANCH_W), functools.partial(lambda b, i, c: (b * nt + i, c), c=c))
    prev = lambda c: pl.BlockSpec(
        (HALO, BRANCH_W), functools.partial(lambda b, i, c: (jnp.maximum((b * nt + i) * hb - 1, 0), c), c=c))
    nxt = lambda c: pl.BlockSpec(
        (HALO, BRANCH_W), functools.partial(lambda b, i, c: (jnp.minimum((b * nt + i + 1) * hb, last_halo), c), c=c))
    return pl.pallas_call(
        _lat_conv_kernel,
        grid=(batch, nt),
        in_specs=[main(T_AX), main(T_AB), main(T_AC), main(T_AG), prev(T_AX), prev(T_AC), nxt(T_AX), nxt(T_AC),
                  pl.BlockSpec(conv_w.shape, lambda b, i: (0, 0))],
        out_specs=pl.BlockSpec((tr, BRANCH_W), lambda b, i: (b * nt + i, 0)),
        out_shape=jax.ShapeDtypeStruct((batch * seq, BRANCH_W), BF16),
        compiler_params=pltpu.CompilerParams(dimension_semantics=("parallel", "arbitrary")),
        name="lat_conv",
    )(pa, pa, pa, pa, pa, pa, pa, pa, conv_w)


def _rope(x, cos, sin_signed):
    lane = lax.broadcasted_iota(jnp.int32, x.shape, 1)
    partner = jnp.where((lane % 32) < 16, pltpu.roll(x, 128 - 16, axis=1), pltpu.roll(x, 16, axis=1))
    return x * cos + partner * sin_signed


WIN_BLOCKS_PER_STEP = 2


def _lat_win_kernel(q_ref, g_ref, kp_ref, kc_ref, kn_ref, vp_ref, vc_ref, vn_ref,
                    cq_ref, sq_ref, cp_ref, sp_ref, cn_ref, sn_ref, ck_ref, cv_ref, sink_ref, o_ref):
    nq = WIN_BLOCKS_PER_STEP
    nb = pl.num_programs(1) * nq
    wb = WIN_BLOCK
    cq, sq = cq_ref[...], sq_ref[...]
    k_rot = jnp.concatenate([
        _rope(kp_ref[...].astype(F32), cp_ref[...], sp_ref[...]),
        _rope(kc_ref[...].astype(F32), cq, sq),
        _rope(kn_ref[...].astype(F32), cn_ref[...], sn_ref[...]),
    ], axis=0).astype(BF16)
    v_all = jnp.concatenate([vp_ref[...], vc_ref[...], vn_ref[...]], axis=0).astype(BF16)
    qi = lax.broadcasted_iota(jnp.int32, (wb, 3 * wb), 0)
    kj = lax.broadcasted_iota(jnp.int32, (wb, 3 * wb), 1)
    ck = ck_ref[0].astype(BF16)
    cv = cv_ref[0].astype(BF16)
    gsz = WIN_HEADS // WIN_KV_HEADS
    q_rot = [_rope(q_ref[:, pair * 128:(pair + 1) * 128].astype(F32), cq, sq).astype(BF16)
             for pair in range(WIN_HEADS // 2)]
    scores, values, sinks = [], [], []
    for d in range(nq):
        n = pl.program_id(1) * nq + d
        mask = (((kj < wb) & (kj >= qi) & (n > 0)) | ((kj >= wb) & (kj < 2 * wb))
                | ((kj >= 2 * wb) & (kj - 2 * wb <= qi) & (n < nb - 1)))
        rows = slice(d * wb, (d + 1) * wb)
        win = slice(d * wb, (d + 3) * wb)
        for h in range(WIN_HEADS):
            kv = slice((h // gsz) * 64, (h // gsz + 1) * 64)
            q = q_rot[h // 2][rows, (h % 2) * 64:(h % 2 + 1) * 64]
            s_loc = jnp.where(mask, _dot_nt(q, k_rot[win, kv]) * ATTN_SCALE, NEG_INF)
            scores.append([s_loc, _dot_nt(q, ck[:, kv]) * ATTN_SCALE])
            values.append([v_all[win, kv], cv[:, kv]])
            sinks.append(sink_ref[h:h + 1, 0:1])
    heads = _attend(scores, values, sinks=sinks)
    o = jnp.concatenate([jnp.concatenate(heads[d * WIN_HEADS:(d + 1) * WIN_HEADS], axis=-1) for d in range(nq)], axis=0)
    o_ref[...] = (o * _silu(g_ref[...].astype(F32))).astype(o_ref.dtype)


def _lat_window(pb, cos_t, sin_t, ctx_k, ctx_v, sink_b, *, batch, seq):
    wb = WIN_BLOCK
    k_col = T_BKV * (TILE // KV_W)
    v_col = k_col + 1
    nq = WIN_BLOCKS_PER_STEP
    nb = seq // wb
    steps = nb // nq
    nbr = lambda s, d: jnp.clip(s * nq + (nq if d > 0 else -1), 0, nb - 1)
    own = lambda col, width: pl.BlockSpec(
        (nq * wb, width), functools.partial(lambda b, s, col: (b * steps + s, col), col=col))
    kv = lambda col, d: pl.BlockSpec(
        (wb, KV_W), functools.partial(lambda b, s, col, d: (b * nb + nbr(s, d), col), col=col, d=d))
    tab = lambda d: pl.BlockSpec((wb, 128), functools.partial(lambda b, s, d: (nbr(s, d), 0), d=d))
    tab_own = pl.BlockSpec((nq * wb, 128), lambda b, s: (s, 0))
    ctx = pl.BlockSpec((1,) + ctx_k.shape[1:], lambda b, s: (b, 0, 0))
    return pl.pallas_call(
        _lat_win_kernel,
        grid=(batch, steps),
        in_specs=[
            own(T_BQ, BRANCH_W), own(T_BG, BRANCH_W),
            kv(k_col, -1), own(k_col, KV_W), kv(k_col, 1), kv(v_col, -1), own(v_col, KV_W), kv(v_col, 1),
            tab_own, tab_own, tab(-1), tab(-1), tab(1), tab(1),
            ctx, ctx,
            pl.BlockSpec(sink_b.shape, lambda b, s: (0, 0)),
        ],
        out_specs=pl.BlockSpec((nq * wb, BRANCH_W), lambda b, s: (b * steps + s, 0)),
        out_shape=jax.ShapeDtypeStruct((batch * seq, BRANCH_W), BF16),
        compiler_params=pltpu.CompilerParams(dimension_semantics=("parallel", "arbitrary")),
        name="lat_window",
    )(pb, pb, pb, pb, pb, pb, pb, pb, cos_t, sin_t, cos_t, sin_t, cos_t, sin_t, ctx_k, ctx_v, sink_b)


def _fft1_kernel(u_ref, c_ref, s_ref, twr_ref, twi_ref, zr_ref, zi_ref, *, chunks):
    u = u_ref[0].astype(BF16)
    yr = _dot(c_ref[...], u)
    yi = -_dot(s_ref[...], u)
    for t in range(chunks):
        wr = twr_ref[0, :, t:t + 1]
        wi = twi_ref[0, :, t:t + 1]
        a = yr[:, t * 512:(t + 1) * 512]
        b = yi[:, t * 512:(t + 1) * 512]
        zr_ref[0, :, t * 512:(t + 1) * 512] = (a * wr - b * wi).astype(zr_ref.dtype)
        zi_ref[0, :, t * 512:(t + 1) * 512] = (a * wi + b * wr).astype(zi_ref.dtype)


def _fft2_kernel(zr_ref, zi_ref, g_ref, l_ref, cs_ref, o_ref, *, kblock):
    r = FFT_R
    zz = [jnp.concatenate([zr_ref[0, kk], zi_ref[0, kk]], axis=0) for kk in range(kblock)]
    xx = [_dot(l_ref[...], z) for z in zz]
    xcat = [jnp.concatenate([x[0:r], x[r:2 * r]], axis=1).astype(BF16) for x in xx]
    out = [_dot(x, cs_ref[...]) for x in xcat]
    for kk in range(kblock):
        sl = slice(kk * 512, (kk + 1) * 512)
        o_ref[0, :, sl] = (out[kk] * _silu(g_ref[0, :, sl].astype(F32))).astype(o_ref.dtype)


def _lat_fourier(fu, fg, tabs, *, batch, seq):
    r = FFT_R
    assert seq == r * r
    wide = r * BRANCH_W
    chunks = 8
    c64, s64, twr, twi, lmat, csmat = tabs
    u3 = fu.reshape(batch, r, wide)
    nct = r // chunks
    zr, zi = pl.pallas_call(
        functools.partial(_fft1_kernel, chunks=chunks),
        grid=(batch, nct),
        in_specs=[
            pl.BlockSpec((1, r, chunks * BRANCH_W), lambda b, t: (b, 0, t)),
            pl.BlockSpec((r, r), lambda b, t: (0, 0)),
            pl.BlockSpec((r, r), lambda b, t: (0, 0)),
            pl.BlockSpec((1, r, chunks), lambda b, t: (t, 0, 0)),
            pl.BlockSpec((1, r, chunks), lambda b, t: (t, 0, 0)),
        ],
        out_specs=[pl.BlockSpec((1, r, chunks * BRANCH_W), lambda b, t: (b, 0, t))] * 2,
        out_shape=[jax.ShapeDtypeStruct((batch, r, wide), BF16)] * 2,
        compiler_params=pltpu.CompilerParams(dimension_semantics=("parallel", "arbitrary")),
        name="lat_fft1",
    )(u3, c64, s64, twr, twi)
    kblock = 8
    z4 = lambda z: z.reshape(batch, r, r, BRANCH_W)
    out = pl.pallas_call(
        functools.partial(_fft2_kernel, kblock=kblock),
        grid=(batch, r // kblock),
        in_specs=[
            pl.BlockSpec((1, kblock, r, BRANCH_W), lambda b, k: (b, k, 0, 0)),
            pl.BlockSpec((1, kblock, r, BRANCH_W), lambda b, k: (b, k, 0, 0)),
            pl.BlockSpec((1, r, kblock * BRANCH_W), lambda b, k: (b, 0, k)),
            pl.BlockSpec((2 * r, 2 * r), lambda b, k: (0, 0)),
            pl.BlockSpec((2 * BRANCH_W, BRANCH_W), lambda b, k: (0, 0)),
        ],
        out_specs=pl.BlockSpec((1, r, kblock * BRANCH_W), lambda b, k: (b, 0, k)),
        out_shape=jax.ShapeDtypeStruct((batch, r, wide), BF16),
        compiler_params=pltpu.CompilerParams(dimension_semantics=("parallel", "arbitrary")),
        name="lat_fft2",
    )(z4(zr), z4(zi), fg.reshape(batch, r, wide), lmat, csmat)
    return out.reshape(batch * seq, BRANCH_W)


N_RPB_ROWS = 2 * NA_ROWS - 1
N_RPB_COLS = 2 * NA_COLS - 1


NA_ROWS_PER_STEP = 4


def _lat_na_kernel(q_ref, g_ref, k_ref, v_ref, bias_ref, ck_ref, cv_ref, o_ref, *, rows):
    ck = ck_ref[0].astype(BF16)
    cv = cv_ref[0].astype(BF16)
    sls = [slice(h * 64, (h + 1) * 64) for h in range(NA_HEADS)]
    scores, values = [], []
    for d in range(NA_ROWS_PER_STEP):
        r = pl.program_id(1) * NA_ROWS_PER_STEP + d
        r0 = jnp.clip(r - NA_ROWS // 2, 0, rows - NA_ROWS)
        start = pl.multiple_of(r0 * GRID_W, GRID_W)
        kw = k_ref[pl.ds(start, NA_ROWS * GRID_W), :]
        vw = v_ref[pl.ds(start, NA_ROWS * GRID_W), :]
        qa = q_ref[d * GRID_W:(d + 1) * GRID_W, :]
        a0 = r0 - r + NA_ROWS - 1
        for h, sl in enumerate(sls):
            bias = jnp.concatenate([bias_ref[h, a0 + 2 * i2] for i2 in range(NA_ROWS // 2)], axis=1)
            scores.append([_dot_nt(qa[:, sl], kw[:, sl]) * ATTN_SCALE + bias,
                           _dot_nt(qa[:, sl], ck[:, sl]) * ATTN_SCALE])
            values.append([vw[:, sl], cv[:, sl]])
    heads = _attend(scores, values)
    o = jnp.concatenate([jnp.concatenate(heads[d * NA_HEADS:(d + 1) * NA_HEADS], axis=-1)
                         for d in range(NA_ROWS_PER_STEP)], axis=0)
    o_ref[...] = (o * _silu(g_ref[...].astype(F32))).astype(o_ref.dtype)


def _lat_na(p, bias, ctx_k, ctx_v, *, batch, seq):
    rows = seq // GRID_W
    steps = rows // NA_ROWS_PER_STEP
    qrows = NA_ROWS_PER_STEP * GRID_W
    ctx = pl.BlockSpec((1,) + ctx_k.shape[1:], lambda b, r: (b, 0, 0))
    return pl.pallas_call(
        functools.partial(_lat_na_kernel, rows=rows),
        grid=(batch, steps),
        in_specs=[
            pl.BlockSpec((qrows, BRANCH_W), lambda b, r: (b * steps + r, T_DQ)),
            pl.BlockSpec((qrows, BRANCH_W), lambda b, r: (b * steps + r, T_DG)),
            pl.BlockSpec((seq, BRANCH_W), lambda b, r: (b, T_DK)),
            pl.BlockSpec((seq, BRANCH_W), lambda b, r: (b, T_DV)),
            pl.BlockSpec(bias.shape, lambda b, r: (0, 0, 0, 0)),
            ctx, ctx,
        ],
        out_specs=pl.BlockSpec((qrows, BRANCH_W), lambda b, r: (b * steps + r, 0)),
        out_shape=jax.ShapeDtypeStruct((batch * seq, BRANCH_W), BF16),
        compiler_params=pltpu.CompilerParams(dimension_semantics=("parallel", "arbitrary")),
        name="lat_na",
    )(p, p, p, p, bias, ctx_k, ctx_v)


def _toeplitz_kernel(r_ref, o_ref):
    x = r_ref[...]
    hi = x.astype(BF16)
    r1 = x - hi.astype(F32)
    mid = r1.astype(BF16)
    lo = (r1 - mid.astype(F32)).astype(BF16)
    shape = (x.shape[1], GRID_W * GRID_W)
    b = lax.broadcasted_iota(jnp.int32, shape, 0)
    col = lax.broadcasted_iota(jnp.int32, shape, 1)
    kc = jnp.bitwise_and(col, GRID_W - 1)
    qc = lax.shift_right_logical(col, GRID_W.bit_length() - 1)
    onehot = jnp.where(kc - qc + (NA_COLS - 1) == b, 1.0, 0.0).astype(BF16)
    o_ref[...] = _dot(hi, onehot) + _dot(mid, onehot) + _dot(lo, onehot)


def _na_bias_tables(na_rpb):
    depth = na_rpb.shape[0]
    n = depth * NA_HEADS * N_RPB_ROWS
    n_pad = -(-n // 8) * 8
    r2 = jnp.zeros((n_pad, 128), F32).at[:n, :N_RPB_COLS].set(na_rpb.reshape(n, N_RPB_COLS))
    flat = pl.pallas_call(
        _toeplitz_kernel,
        out_shape=jax.ShapeDtypeStruct((n_pad, GRID_W * GRID_W), F32),
        name="na_bias_toeplitz",
    )(r2)
    t = flat[:n].reshape(depth, NA_HEADS, N_RPB_ROWS, GRID_W, GRID_W)
    qc = np.arange(GRID_W)[:, None]
    kc = np.arange(GRID_W)[None, :]
    win0 = np.clip(qc - NA_COLS // 2, 0, GRID_W - NA_COLS)
    vis = (kc >= win0) & (kc < win0 + NA_COLS)
    t = jnp.where(vis, t, NEG_INF)
    return jnp.concatenate([t[:, :, :-1], t[:, :, 1:]], axis=-1)


def _bf16_table(t):
    return jnp.asarray(t, F32).astype(BF16)


def _dft_cos_sin(n):
    idx = np.arange(n)
    ang = 2.0 * np.pi * ((idx[:, None] * idx[None, :]) % n) / n
    return np.cos(ang), np.sin(ang)


def _channel_dft_blocks(scale):
    c, s = _dft_cos_sin(FN_GROUP_W)
    eye = np.eye(FN_GROUPS)
    return np.kron(eye, c) * scale, np.kron(eye, s) * scale


def _ctx_fourier_tables(seq):
    ct, st = _dft_cos_sin(seq)
    cb, sb = _channel_dft_blocks(1.0 / np.sqrt(seq * FN_GROUP_W))
    return tuple(_bf16_table(t) for t in (ct, st, cb, sb))


def _lat_fourier_tables(seq, chunks=8):
    r = FFT_R
    c64, s64 = _dft_cos_sin(r)
    k1 = np.arange(r)[:, None]
    t2 = np.arange(r)[None, :]
    ang = 2.0 * np.pi * (k1 * t2) / (r * r)
    twr = np.cos(ang).reshape(r, r // chunks, chunks).transpose(1, 0, 2)
    twi = (-np.sin(ang)).reshape(r, r // chunks, chunks).transpose(1, 0, 2)
    lmat = np.block([[c64, s64], [-s64, c64]])
    cb, sb = _channel_dft_blocks(1.0 / np.sqrt(seq * FN_GROUP_W))
    csmat = np.concatenate([cb, sb], axis=0)
    return (_bf16_table(c64), _bf16_table(s64), jnp.asarray(twr, F32), jnp.asarray(twi, F32),
            _bf16_table(lmat), _bf16_table(csmat))


def _rope_tables(seq):
    half = HEAD_DIM // 2
    quarter = half // 2
    t = jnp.arange(seq)
    inv = ROPE_BASE ** (-jnp.arange(quarter, dtype=F32) / quarter)

    def cs(pos):
        ang = pos.astype(F32)[:, None] * inv[None, :]
        c, s = jnp.cos(ang), jnp.sin(ang)
        return jnp.concatenate([c, c], axis=-1), jnp.concatenate([-s, s], axis=-1)

    cr, sr = cs(t // GRID_W)
    cc, sc = cs(t % GRID_W)
    cos = jnp.concatenate([cr, cc], axis=-1)
    sin = jnp.concatenate([sr, sc], axis=-1)
    return jnp.tile(cos, (1, 2)), jnp.tile(sin, (1, 2))


def _tiled_in_weights(wl):
    pad = jnp.zeros((D_MODEL, TILE - 2 * KV_W), F32)
    cols = [wl[:, OFF_A:OFF_BQ], wl[:, OFF_BQ:OFF_BK], wl[:, OFF_BG:OFF_FU], wl[:, OFF_BK:OFF_BG], pad,
            wl[:, OFF_FU:]]
    return jnp.concatenate(cols, axis=1).astype(BF16)


def kernel(x_prompt, x_sample, cache_win_k, cache_win_v, cache_na_k, cache_na_v, c, c_ctx, norm_g, w_ada, b_ada,
           w_in, conv_w, win_sink, na_rpb, w_branch, w_out, final_g):
    batch, seq, _ = x_prompt.shape
    dbatch, dseq, _ = x_sample.shape
    past = cache_win_k.shape[2]

    cv8 = jnp.zeros((8, D_MODEL), F32).at[0].set(c_ctx).at[1:1 + dbatch].set(c)
    mod = _modulation(cv8, w_ada, b_ada)

    ctx_tabs = _ctx_fourier_tables(seq)
    lat_tabs = _lat_fourier_tables(dseq)
    cos_t, sin_t = _rope_tables(dseq)
    na_bias = _na_bias_tables(na_rpb)
    final_g2 = final_g.reshape(1, D_MODEL)

    xp = x_prompt.reshape(batch * seq, D_MODEL)
    xs = x_sample.reshape(dbatch * dseq, D_MODEL)
    new_kv = [jnp.zeros((batch, DEPTH, seq, width), F32) for width in (KV_W, KV_W, BRANCH_W, BRANCH_W)]

    for l in range(DEPTH):
        shift = mod[l, :, 0:D_MODEL].reshape(8, 1, D_MODEL)
        scale = mod[l, :, D_MODEL:2 * D_MODEL].reshape(8, 1, D_MODEL)
        gate = mod[l, :, 2 * D_MODEL:].reshape(8, 1, D_MODEL)
        g = norm_g[l].reshape(1, D_MODEL)
        w_p = _tiled_in_weights(w_in[l])
        w_br = w_branch[l].astype(BF16)
        w_o = w_out[l].astype(BF16)
        cw = conv_w[l]
        sink_b = jnp.broadcast_to(win_sink[l][:, None], (WIN_HEADS, 128))
        final = l == DEPTH - 1

        p, hp, *new_kv = _inproj(
            xp, shift, scale, g, w_p, rows_per_cond=batch * seq, cond0=0, update=(new_kv, l, seq),
            extras=((T_BKV, 0, KV_W), (T_BKV, KV_W, KV_W), (T_DK, 0, BRANCH_W), (T_DV, 0, BRANCH_W)))
        o_ctx = _ctx_mixers(p, cw, sink_b, *ctx_tabs, seq=seq)
        xp = _out_stage(xp, hp, gate, final_g2, [o_ctx] * 4, [0, 1, 2, 3], w_p, w_br, w_o,
                        rows_per_cond=batch * seq, cond0=0, final=final)

        q, hs, qfu, qfg = _inproj(xs, shift, scale, g, w_p, rows_per_cond=dseq, cond0=1, extra_dtype=BF16,
                              extras=((T_FU, 0, BRANCH_W), (T_FG, 0, BRANCH_W)))
        o_a = _lat_conv(q, cw, batch=dbatch, seq=dseq)
        o_w = _lat_window(q, cos_t, sin_t,
                          cache_win_k[:, l].reshape(dbatch, past, KV_W), cache_win_v[:, l].reshape(dbatch, past, KV_W),
                          sink_b, batch=dbatch, seq=dseq)
        o_f = _lat_fourier(qfu, qfg, lat_tabs, batch=dbatch, seq=dseq)
        o_n = _lat_na(q, na_bias[l],
                      cache_na_k[:, l].reshape(dbatch, past, BRANCH_W), cache_na_v[:, l].reshape(dbatch, past, BRANCH_W),
                      batch=dbatch, seq=dseq)
        xs = _out_stage(xs, hs, gate, final_g2, [o_a, o_w, o_f, o_n], [0, 0, 0, 0], w_p, w_br, w_o,
                        rows_per_cond=dseq, cond0=1, final=final)

    y_prompt = xp.reshape(batch, seq, D_MODEL)
    y_sample = xs.reshape(dbatch, dseq, D_MODEL)
    heads = (WIN_KV_HEADS, WIN_KV_HEADS, NA_HEADS, NA_HEADS)
    return (y_prompt, y_sample) + tuple(a.reshape(batch, DEPTH, seq, n, HEAD_DIM) for a, n in zip(new_kv, heads))
```

```python
import functools

import numpy as np
import jax
import jax.numpy as jnp
from jax import lax
from jax.experimental import pallas as pl
from jax.experimental.pallas import tpu as pltpu

D_MODEL = 2048
DEPTH = 2
GRID_W = 64
N_BRANCH = 4
BRANCH_W = 512
HEAD_DIM = 64
WIN_HEADS = 8
WIN_KV_HEADS = 2
KV_W = WIN_KV_HEADS * HEAD_DIM
WIN_BLOCK = 128
FN_GROUPS = 4
FN_GROUP_W = BRANCH_W // FN_GROUPS
NA_HEADS = 8
NA_ROWS = 8
NA_COLS = 16
ROPE_BASE = 10000.0
EPS = 1e-6
ATTN_SCALE = HEAD_DIM ** -0.5
NEG_INF = -1e30
N_MAIN = 6400

F32 = jnp.float32
BF16 = jnp.bfloat16

OFF_A = 0
OFF_BQ, OFF_BK, OFF_BV, OFF_BG = 2048, 2560, 2688, 2816
OFF_FU, OFF_FG = 3328, 3840
OFF_DQ, OFF_DK, OFF_DV, OFF_DG = 4352, 4864, 5376, 5888

TILE = 512
T_AX, T_AB, T_AC, T_AG, T_BQ, T_BG, T_BKV, T_FU, T_FG, T_DQ, T_DK, T_DV, T_DG = range(13)
STEP_TILES = 2
N_TILES = 14
P_WIDTH = N_TILES * TILE

V7X_VMEM_BYTES = 64 * 2 ** 20
PROJ_VMEM_BYTES = V7X_VMEM_BYTES - 2 ** 20

FFT_R = 64


def _silu(x):
    return x * jax.nn.sigmoid(x)


def _dot(a, b):
    return jnp.dot(a, b, preferred_element_type=F32)


def _dot_nt(a, b):
    return lax.dot_general(a, b, (((1,), (1,)), ((), ())), preferred_element_type=F32)


def _rms_mod(x, g, scale, shift):
    ms = jnp.mean(x * x, axis=-1, keepdims=True)
    return (x * lax.rsqrt(ms + EPS) * g) * (1.0 + scale) + shift


def _store_modulated(h_ref, x_ref, g_ref, sc_ref, sh_ref):
    h_ref[...] = _rms_mod(x_ref[...], g_ref[...], sc_ref[0], sh_ref[0]).astype(h_ref.dtype)


def _mod_kernel(cv_ref, w_ref, b_ref, o_ref):
    s = _silu(cv_ref[...]).astype(BF16)
    o_ref[0] = _dot(s, w_ref[0].astype(BF16)) + b_ref[0]


def _modulation(cv8, w_ada, b_ada):
    tn = 512
    n3 = 3 * D_MODEL
    return pl.pallas_call(
        _mod_kernel,
        grid=(DEPTH, n3 // tn),
        in_specs=[
            pl.BlockSpec((8, D_MODEL), lambda l, j: (0, 0)),
            pl.BlockSpec((1, D_MODEL, tn), lambda l, j: (l, 0, j)),
            pl.BlockSpec((1, 1, tn), lambda l, j: (l, 0, j)),
        ],
        out_specs=pl.BlockSpec((1, 8, tn), lambda l, j: (l, 0, j)),
        out_shape=jax.ShapeDtypeStruct((DEPTH, 8, n3), F32),
        name="modulation",
    )(cv8, w_ada, b_ada.reshape(DEPTH, 1, n3))


def _inproj_kernel(x_ref, sh_ref, sc_ref, g_ref, w_ref, *rest, extras, n_updated):
    p_ref, h_ref, extra_refs = rest[n_updated], rest[n_updated + 1], rest[n_updated + 2:]
    j = pl.program_id(1)

    @pl.when(j == 0)
    def _():
        _store_modulated(h_ref, x_ref, g_ref, sc_ref, sh_ref)

    acc = _dot(h_ref[...], w_ref[...])
    p_ref[...] = acc.astype(p_ref.dtype)
    for ref, (tile, lo) in zip(extra_refs, extras):
        @pl.when(j == tile // STEP_TILES)
        def _(ref=ref, lo=(tile % STEP_TILES) * TILE + lo):
            cols = acc[:, lo:lo + ref.shape[-1]].astype(ref.dtype)
            if ref.ndim == 2:
                ref[...] = cols
            else:
                ref[:, 0] = cols.reshape(ref.shape[0], ref.shape[2], ref.shape[3])


def _inproj(x2d, shift, scale, g, w, *, rows_per_cond, cond0, extras, extra_dtype=None, update=None, tm=1024):
    m = x2d.shape[0]
    assert w.shape[0] == D_MODEL and w.shape[1] >= P_WIDTH and m % tm == 0 and rows_per_cond % tm == 0
    tiles_per_cond = rows_per_cond // tm
    cond = lambda i, j: (cond0 + i // tiles_per_cond, 0, 0)
    if update is None:
        updated, aliases = [], {}
        extra_specs = [pl.BlockSpec((tm, width), lambda i, j: (i, 0)) for _, _, width in extras]
        extra_shapes = [jax.ShapeDtypeStruct((m, width), extra_dtype) for _, _, width in extras]
    else:
        updated, layer, seq = update
        assert tm % seq == 0 and all(a.shape[2:] == (seq, width) for a, (_, _, width) in zip(updated, extras))
        extra_specs = [pl.BlockSpec((tm // seq, 1, seq, width), lambda i, j: (i, layer, 0, 0)) for _, _, width in extras]
        extra_shapes = [jax.ShapeDtypeStruct(a.shape, a.dtype) for a in updated]
        aliases = {5 + k: 2 + k for k in range(len(updated))}
    return pl.pallas_call(
        functools.partial(_inproj_kernel, extras=tuple((t, lo) for t, lo, _ in extras), n_updated=len(updated)),
        grid=(m // tm, N_TILES // STEP_TILES),
        in_specs=[
            pl.BlockSpec((tm, D_MODEL), lambda i, j: (i, 0)),
            pl.BlockSpec((1, 1, D_MODEL), cond),
            pl.BlockSpec((1, 1, D_MODEL), cond),
            pl.BlockSpec((1, D_MODEL), lambda i, j: (0, 0)),
            pl.BlockSpec((D_MODEL, STEP_TILES * TILE), lambda i, j: (0, j)),
        ] + [pl.BlockSpec(memory_space=pl.ANY) for _ in updated],
        out_specs=[pl.BlockSpec((tm, STEP_TILES * TILE), lambda i, j: (i, j)),
                   pl.BlockSpec((tm, D_MODEL), lambda i, j: (i, 0))]
        + extra_specs,
        out_shape=[jax.ShapeDtypeStruct((m, P_WIDTH), BF16), jax.ShapeDtypeStruct((m, D_MODEL), BF16)] + extra_shapes,
        input_output_aliases=aliases,
        compiler_params=pltpu.CompilerParams(dimension_semantics=("parallel", "arbitrary"),
                                             vmem_limit_bytes=PROJ_VMEM_BYTES),
        name="inproj",
    )(x2d, shift, scale, g, w, *updated)


def _out_kernel(x_ref, h_ref, gt_ref, fg_ref, oa_ref, ow_ref, of_ref, on_ref,
                wg0_ref, wg1_ref, wg2_ref, wg3_ref, wb_ref, wo_ref, y_ref, merged_ref, *, final):
    j = pl.program_id(1)
    h = h_ref[...]
    merged = None
    for i, (o_ref, wg_ref) in enumerate(((oa_ref, wg0_ref), (ow_ref, wg1_ref), (of_ref, wg2_ref), (on_ref, wg3_ref))):
        term = jax.nn.sigmoid(_dot(h, wg_ref[...])) * _dot(o_ref[...], wb_ref[i])
        merged = term if merged is None else merged + term
    merged_ref[j] = merged.astype(BF16)

    @pl.when(j == pl.num_programs(1) - 1)
    def _():
        m_all = jnp.concatenate([merged_ref[k] for k in range(merged_ref.shape[0])], axis=1)
        xn = x_ref[...] + gt_ref[0] * _dot(m_all, wo_ref[...])
        if final:
            ms = jnp.mean(xn * xn, axis=-1, keepdims=True)
            xn = xn * lax.rsqrt(ms + EPS) * fg_ref[...]
        y_ref[...] = xn


def _out_stage(x2d, h2d, gate, final_g, o_arrays, o_cols, w_gate, w_branch, w_out,
               *, rows_per_cond, cond0, final, tm=512, tc=512):
    m = x2d.shape[0]
    assert m % tm == 0 and rows_per_cond % tm == 0
    tiles_per_cond = rows_per_cond // tm
    ncol = D_MODEL // tc
    cond = lambda i, j: (cond0 + i // tiles_per_cond, 0, 0)
    o_specs = [pl.BlockSpec((tm, BRANCH_W), functools.partial(lambda i, j, c: (i, c), c=c)) for c in o_cols]
    assert tc == TILE and w_gate.shape == (D_MODEL, P_WIDTH + N_BRANCH * D_MODEL)
    wg_specs = [pl.BlockSpec((D_MODEL, tc), functools.partial(lambda i, j, b: (0, N_TILES + b * ncol + j), b=b))
                for b in range(N_BRANCH)]
    return pl.pallas_call(
        functools.partial(_out_kernel, final=final),
        grid=(m // tm, ncol),
        in_specs=[
            pl.BlockSpec((tm, D_MODEL), lambda i, j: (i, 0)),
            pl.BlockSpec((tm, D_MODEL), lambda i, j: (i, 0)),
            pl.BlockSpec((1, 1, D_MODEL), cond),
            pl.BlockSpec((1, D_MODEL), lambda i, j: (0, 0)),
            *o_specs,
            *wg_specs,
            pl.BlockSpec((N_BRANCH, BRANCH_W, tc), lambda i, j: (0, 0, j)),
            pl.BlockSpec((D_MODEL, D_MODEL), lambda i, j: (0, 0), pipeline_mode=pl.Buffered(1)),
        ],
        out_specs=pl.BlockSpec((tm, D_MODEL), lambda i, j: (i, 0)),
        out_shape=jax.ShapeDtypeStruct((m, D_MODEL), F32),
        scratch_shapes=[pltpu.VMEM((ncol, tm, tc), BF16)],
        compiler_params=pltpu.CompilerParams(dimension_semantics=("parallel", "arbitrary"),
                                             vmem_limit_bytes=PROJ_VMEM_BYTES),
        name="out_stage",
    )(x2d, h2d, gate, final_g, *o_arrays, w_gate, w_gate, w_gate, w_gate, w_branch, w_out)


def _attend(scores, values, sinks=None):
    heads = range(len(scores))
    m = []
    for h in heads:
        mh = scores[h][0].max(axis=-1, keepdims=True)
        for s in scores[h][1:]:
            mh = jnp.maximum(mh, s.max(axis=-1, keepdims=True))
        m.append(mh if sinks is None else jnp.maximum(mh, sinks[h]))
    e = [[jnp.exp(s - m[h]) for s in scores[h]] for h in heads]
    den = []
    for h in heads:
        d = e[h][0].sum(axis=-1, keepdims=True)
        for x in e[h][1:]:
            d = d + x.sum(axis=-1, keepdims=True)
        den.append(d if sinks is None else d + jnp.exp(sinks[h] - m[h]))
    out = []
    for h in heads:
        acc = _dot(e[h][0].astype(BF16), values[h][0])
        for x, v in zip(e[h][1:], values[h][1:]):
            acc = acc + _dot(x.astype(BF16), v)
        out.append(acc)
    return [a / d for a, d in zip(out, den)]


def _shift_rows(z, first_row, last_row):
    n = z.shape[0]
    row = lax.broadcasted_iota(jnp.int32, z.shape, 0)
    z_dn = jnp.where(row == 0, first_row, pltpu.roll(z, 1, axis=0))
    z_up = jnp.where(row == n - 1, last_row, pltpu.roll(z, n - 1, axis=0))
    return z_dn, z_up


def _tile(ref, t, dtype=None):
    v = ref[:, t * TILE:(t + 1) * TILE]
    return v if dtype is None else v.astype(dtype)


def _head(ref, t, h, base=0):
    lo = t * TILE + base + h * HEAD_DIM
    return ref[:, lo:lo + HEAD_DIM]


def _ctx_mixer_kernel(p_ref, cw_ref, sink_ref, ct_ref, st_ref, cb_ref, sb_ref, o_ref):
    z = _tile(p_ref, T_AC, F32) * _tile(p_ref, T_AX, F32)
    zero_row = jnp.zeros((1, BRANCH_W), F32)
    z_dn, z_up = _shift_rows(z, zero_row, zero_row)
    y = _tile(p_ref, T_AB, F32) * (z_dn * cw_ref[0:1, :] + z * cw_ref[1:2, :] + z_up * cw_ref[2:3, :])
    o_ref[:, 0:512] = (y * _silu(_tile(p_ref, T_AG, F32))).astype(o_ref.dtype)

    gsz = WIN_HEADS // WIN_KV_HEADS
    scores = [[_dot_nt(_head(p_ref, T_BQ, h), _head(p_ref, T_BKV, h // gsz)) * ATTN_SCALE] for h in range(WIN_HEADS)]
    values = [[_head(p_ref, T_BKV, h // gsz, base=KV_W)] for h in range(WIN_HEADS)]
    heads = _attend(scores, values, sinks=[sink_ref[h:h + 1, 0:1] for h in range(WIN_HEADS)])
    o_w = jnp.concatenate(heads, axis=-1) * _silu(_tile(p_ref, T_BG, F32))
    o_ref[:, 512:1024] = o_w.astype(o_ref.dtype)

    u = _tile(p_ref, T_FU)
    uc = _dot(u, cb_ref[...]).astype(BF16)
    us = _dot(u, sb_ref[...]).astype(BF16)
    o_f = (_dot(ct_ref[...], uc) - _dot(st_ref[...], us)) * _silu(_tile(p_ref, T_FG, F32))
    o_ref[:, 1024:1536] = o_f.astype(o_ref.dtype)

    scores = [[_dot_nt(_head(p_ref, T_DQ, h), _head(p_ref, T_DK, h)) * ATTN_SCALE] for h in range(NA_HEADS)]
    heads = _attend(scores, [[_head(p_ref, T_DV, h)] for h in range(NA_HEADS)])
    o_n = jnp.concatenate(heads, axis=-1) * _silu(_tile(p_ref, T_DG, F32))
    o_ref[:, 1536:2048] = o_n.astype(o_ref.dtype)


def _ctx_mixers(p, conv_w, sink_b, ct, st, cb, sb, *, seq):
    m = p.shape[0]
    whole = lambda a: pl.BlockSpec(a.shape, lambda b: (0,) * a.ndim)
    return pl.pallas_call(
        _ctx_mixer_kernel,
        grid=(m // seq,),
        in_specs=[pl.BlockSpec((seq, P_WIDTH), lambda b: (b, 0)), whole(conv_w), whole(sink_b),
                  whole(ct), whole(st), whole(cb), whole(sb)],
        out_specs=pl.BlockSpec((seq, N_BRANCH * BRANCH_W), lambda b: (b, 0)),
        out_shape=jax.ShapeDtypeStruct((m, N_BRANCH * BRANCH_W), BF16),
        compiler_params=pltpu.CompilerParams(dimension_semantics=("parallel",)),
        name="ctx_mixers",
    )(p, conv_w, sink_b, ct, st, cb, sb)


HALO = 16


def _lat_conv_kernel(ax_ref, ab_ref, ac_ref, ag_ref, axp_ref, acp_ref, axn_ref, acn_ref, cw_ref, o_ref):
    i = pl.program_id(1)
    z = ac_ref[...].astype(F32) * ax_ref[...].astype(F32)
    zp = acp_ref[HALO - 1:HALO, :].astype(F32) * axp_ref[HALO - 1:HALO, :].astype(F32)
    zn = acn_ref[0:1, :].astype(F32) * axn_ref[0:1, :].astype(F32)
    zp = jnp.where(i == 0, 0.0, zp)
    zn = jnp.where(i == pl.num_programs(1) - 1, 0.0, zn)
    z_dn, z_up = _shift_rows(z, zp, zn)
    y = ab_ref[...].astype(F32) * (z_dn * cw_ref[0:1, :] + z * cw_ref[1:2, :] + z_up * cw_ref[2:3, :])
    o_ref[...] = (y * _silu(ag_ref[...].astype(F32))).astype(o_ref.dtype)


def _lat_conv(pa, conv_w, *, batch, seq, tr=512):
    nt = seq // tr
    hb = tr // HALO
    last_halo = batch * seq // HALO - 1
    main = lambda c: pl.BlockSpec((tr, BRANCH_W), functools.partial(lambda b, i, c: (b * nt + i, c), c=c))
    prev = lambda c: pl.BlockSpec(
        (HALO, BRANCH_W), functools.partial(lambda b, i, c: (jnp.maximum((b * nt + i) * hb - 1, 0), c), c=c))
    nxt = lambda c: pl.BlockSpec(
        (HALO, BRANCH_W), functools.partial(lambda b, i, c: (jnp.minimum((b * nt + i + 1) * hb, last_halo), c), c=c))
    return pl.pallas_call(
        _lat_conv_kernel,
        grid=(batch, nt),
        in_specs=[main(T_AX), main(T_AB), main(T_AC), main(T_AG), prev(T_AX), prev(T_AC), nxt(T_AX), nxt(T_AC),
                  pl.BlockSpec(conv_w.shape, lambda b, i: (0, 0))],
        out_specs=pl.BlockSpec((tr, BRANCH_W), lambda b, i: (b * nt + i, 0)),
        out_shape=jax.ShapeDtypeStruct((batch * seq, BRANCH_W), BF16),
        compiler_params=pltpu.CompilerParams(dimension_semantics=("parallel", "arbitrary")),
        name="lat_conv",
    )(pa, pa, pa, pa, pa, pa, pa, pa, conv_w)


def _rope(x, cos, sin_signed):
    lane = lax.broadcasted_iota(jnp.int32, x.shape, 1)
    partner = jnp.where((lane % 32) < 16, pltpu.roll(x, 128 - 16, axis=1), pltpu.roll(x, 16, axis=1))
    return x * cos + partner * sin_signed


WIN_BLOCKS_PER_STEP = 2


def _lat_win_kernel(q_ref, g_ref, kp_ref, kc_ref, kn_ref, vp_ref, vc_ref, vn_ref,
                    cq_ref, sq_ref, cp_ref, sp_ref, cn_ref, sn_ref, ck_ref, cv_ref, sink_ref, o_ref):
    nq = WIN_BLOCKS_PER_STEP
    nb = pl.num_programs(1) * nq
    wb = WIN_BLOCK
    cq, sq = cq_ref[...], sq_ref[...]
    k_rot = jnp.concatenate([
        _rope(kp_ref[...].astype(F32), cp_ref[...], sp_ref[...]),
        _rope(kc_ref[...].astype(F32), cq, sq),
        _rope(kn_ref[...].astype(F32), cn_ref[...], sn_ref[...]),
    ], axis=0).astype(BF16)
    v_all = jnp.concatenate([vp_ref[...], vc_ref[...], vn_ref[...]], axis=0).astype(BF16)
    qi = lax.broadcasted_iota(jnp.int32, (wb, 3 * wb), 0)
    kj = lax.broadcasted_iota(jnp.int32, (wb, 3 * wb), 1)
    ck = ck_ref[0].astype(BF16)
    cv = cv_ref[0].astype(BF16)
    gsz = WIN_HEADS // WIN_KV_HEADS
    q_rot = [_rope(q_ref[:, pair * 128:(pair + 1) * 128].astype(F32), cq, sq).astype(BF16)
             for pair in range(WIN_HEADS // 2)]
    scores, values, sinks = [], [], []
    for d in range(nq):
        n = pl.program_id(1) * nq + d
        mask = (((kj < wb) & (kj >= qi) & (n > 0)) | ((kj >= wb) & (kj < 2 * wb))
                | ((kj >= 2 * wb) & (kj - 2 * wb <= qi) & (n < nb - 1)))
        rows = slice(d * wb, (d + 1) * wb)
        win = slice(d * wb, (d + 3) * wb)
        for h in range(WIN_HEADS):
            kv = slice((h // gsz) * 64, (h // gsz + 1) * 64)
            q = q_rot[h // 2][rows, (h % 2) * 64:(h % 2 + 1) * 64]
            s_loc = jnp.where(mask, _dot_nt(q, k_rot[win, kv]) * ATTN_SCALE, NEG_INF)
            scores.append([s_loc, _dot_nt(q, ck[:, kv]) * ATTN_SCALE])
            values.append([v_all[win, kv], cv[:, kv]])
            sinks.append(sink_ref[h:h + 1, 0:1])
    heads = _attend(scores, values, sinks=sinks)
    o = jnp.concatenate([jnp.concatenate(heads[d * WIN_HEADS:(d + 1) * WIN_HEADS], axis=-1) for d in range(nq)], axis=0)
    o_ref[...] = (o * _silu(g_ref[...].astype(F32))).astype(o_ref.dtype)


def _lat_window(pb, cos_t, sin_t, ctx_k, ctx_v, sink_b, *, batch, seq):
    wb = WIN_BLOCK
    k_col = T_BKV * (TILE // KV_W)
    v_col = k_col + 1
    nq = WIN_BLOCKS_PER_STEP
    nb = seq // wb
    steps = nb // nq
    nbr = lambda s, d: jnp.clip(s * nq + (nq if d > 0 else -1), 0, nb - 1)
    own = lambda col, width: pl.BlockSpec(
        (nq * wb, width), functools.partial(lambda b, s, col: (b * steps + s, col), col=col))
    kv = lambda col, d: pl.BlockSpec(
        (wb, KV_W), functools.partial(lambda b, s, col, d: (b * nb + nbr(s, d), col), col=col, d=d))
    tab = lambda d: pl.BlockSpec((wb, 128), functools.partial(lambda b, s, d: (nbr(s, d), 0), d=d))
    tab_own = pl.BlockSpec((nq * wb, 128), lambda b, s: (s, 0))
    ctx = pl.BlockSpec((1,) + ctx_k.shape[1:], lambda b, s: (b, 0, 0))
    return pl.pallas_call(
        _lat_win_kernel,
        grid=(batch, steps),
        in_specs=[
            own(T_BQ, BRANCH_W), own(T_BG, BRANCH_W),
            kv(k_col, -1), own(k_col, KV_W), kv(k_col, 1), kv(v_col, -1), own(v_col, KV_W), kv(v_col, 1),
            tab_own, tab_own, tab(-1), tab(-1), tab(1), tab(1),
            ctx, ctx,
            pl.BlockSpec(sink_b.shape, lambda b, s: (0, 0)),
        ],
        out_specs=pl.BlockSpec((nq * wb, BRANCH_W), lambda b, s: (b * steps + s, 0)),
        out_shape=jax.ShapeDtypeStruct((batch * seq, BRANCH_W), BF16),
        compiler_params=pltpu.CompilerParams(dimension_semantics=("parallel", "arbitrary")),
        name="lat_window",
    )(pb, pb, pb, pb, pb, pb, pb, pb, cos_t, sin_t, cos_t, sin_t, cos_t, sin_t, ctx_k, ctx_v, sink_b)


def _fft1_kernel(u_ref, c_ref, s_ref, twr_ref, twi_ref, zr_ref, zi_ref, *, chunks):
    u = u_ref[0].astype(BF16)
    yr = _dot(c_ref[...], u)
    yi = -_dot(s_ref[...], u)
    for t in range(chunks):
        wr = twr_ref[0, :, t:t + 1]
        wi = twi_ref[0, :, t:t + 1]
        a = yr[:, t * 512:(t + 1) * 512]
        b = yi[:, t * 512:(t + 1) * 512]
        zr_ref[0, :, t * 512:(t + 1) * 512] = (a * wr - b * wi).astype(zr_ref.dtype)
        zi_ref[0, :, t * 512:(t + 1) * 512] = (a * wi + b * wr).astype(zi_ref.dtype)


def _fft2_kernel(zr_ref, zi_ref, g_ref, l_ref, cs_ref, o_ref, *, kblock):
    r = FFT_R
    zz = [jnp.concatenate([zr_ref[0, kk], zi_ref[0, kk]], axis=0) for kk in range(kblock)]
    xx = [_dot(l_ref[...], z) for z in zz]
    xcat = [jnp.concatenate([x[0:r], x[r:2 * r]], axis=1).astype(BF16) for x in xx]
    out = [_dot(x, cs_ref[...]) for x in xcat]
    for kk in range(kblock):
        sl = slice(kk * 512, (kk + 1) * 512)
        o_ref[0, :, sl] = (out[kk] * _silu(g_ref[0, :, sl].astype(F32))).astype(o_ref.dtype)


def _lat_fourier(fu, fg, tabs, *, batch, seq):
    r = FFT_R
    assert seq == r * r
    wide = r * BRANCH_W
    chunks = 8
    c64, s64, twr, twi, lmat, csmat = tabs
    u3 = fu.reshape(batch, r, wide)
    nct = r // chunks
    zr, zi = pl.pallas_call(
        functools.partial(_fft1_kernel, chunks=chunks),
        grid=(batch, nct),
        in_specs=[
            pl.BlockSpec((1, r, chunks * BRANCH_W), lambda b, t: (b, 0, t)),
            pl.BlockSpec((r, r), lambda b, t: (0, 0)),
            pl.BlockSpec((r, r), lambda b, t: (0, 0)),
            pl.BlockSpec((1, r, chunks), lambda b, t: (t, 0, 0)),
            pl.BlockSpec((1, r, chunks), lambda b, t: (t, 0, 0)),
        ],
        out_specs=[pl.BlockSpec((1, r, chunks * BRANCH_W), lambda b, t: (b, 0, t))] * 2,
        out_shape=[jax.ShapeDtypeStruct((batch, r, wide), BF16)] * 2,
        compiler_params=pltpu.CompilerParams(dimension_semantics=("parallel", "arbitrary")),
        name="lat_fft1",
    )(u3, c64, s64, twr, twi)
    kblock = 8
    z4 = lambda z: z.reshape(batch, r, r, BRANCH_W)
    out = pl.pallas_call(
        functools.partial(_fft2_kernel, kblock=kblock),
        grid=(batch, r // kblock),
        in_specs=[
            pl.BlockSpec((1, kblock, r, BRANCH_W), lambda b, k: (b, k, 0, 0)),
            pl.BlockSpec((1, kblock, r, BRANCH_W), lambda b, k: (b, k, 0, 0)),
            pl.BlockSpec((1, r, kblock * BRANCH_W), lambda b, k: (b, 0, k)),
            pl.BlockSpec((2 * r, 2 * r), lambda b, k: (0, 0)),
            pl.BlockSpec((2 * BRANCH_W, BRANCH_W), lambda b, k: (0, 0)),
        ],
        out_specs=pl.BlockSpec((1, r, kblock * BRANCH_W), lambda b, k: (b, 0, k)),
        out_shape=jax.ShapeDtypeStruct((batch, r, wide), BF16),
        compiler_params=pltpu.CompilerParams(dimension_semantics=("parallel", "arbitrary")),
        name="lat_fft2",
    )(z4(zr), z4(zi), fg.reshape(batch, r, wide), lmat, csmat)
    return out.reshape(batch * seq, BRANCH_W)


N_RPB_ROWS = 2 * NA_ROWS - 1
N_RPB_COLS = 2 * NA_COLS - 1


NA_ROWS_PER_STEP = 4


def _lat_na_kernel(q_ref, g_ref, k_ref, v_ref, bias_ref, ck_ref, cv_ref, o_ref, *, rows):
    ck = ck_ref[0].astype(BF16)
    cv = cv_ref[0].astype(BF16)
    sls = [slice(h * 64, (h + 1) * 64) for h in range(NA_HEADS)]
    scores, values = [], []
    for d in range(NA_ROWS_PER_STEP):
        r = pl.program_id(1) * NA_ROWS_PER_STEP + d
        r0 = jnp.clip(r - NA_ROWS // 2, 0, rows - NA_ROWS)
        start = pl.multiple_of(r0 * GRID_W, GRID_W)
        kw = k_ref[pl.ds(start, NA_ROWS * GRID_W), :]
        vw = v_ref[pl.ds(start, NA_ROWS * GRID_W), :]
        qa = q_ref[d * GRID_W:(d + 1) * GRID_W, :]
        a0 = r0 - r + NA_ROWS - 1
        for h, sl in enumerate(sls):
            bias = jnp.concatenate([bias_ref[h, a0 + 2 * i2] for i2 in range(NA_ROWS // 2)], axis=1)
            scores.append([_dot_nt(qa[:, sl], kw[:, sl]) * ATTN_SCALE + bias,
                           _dot_nt(qa[:, sl], ck[:, sl]) * ATTN_SCALE])
            values.append([vw[:, sl], cv[:, sl]])
    heads = _attend(scores, values)
    o = jnp.concatenate([jnp.concatenate(heads[d * NA_HEADS:(d + 1) * NA_HEADS], axis=-1)
                         for d in range(NA_ROWS_PER_STEP)], axis=0)
    o_ref[...] = (o * _silu(g_ref[...].astype(F32))).astype(o_ref.dtype)


def _lat_na(p, bias, ctx_k, ctx_v, *, batch, seq):
    rows = seq // GRID_W
    steps = rows // NA_ROWS_PER_STEP
    qrows = NA_ROWS_PER_STEP * GRID_W
    ctx = pl.BlockSpec((1,) + ctx_k.shape[1:], lambda b, r: (b, 0, 0))
    return pl.pallas_call(
        functools.partial(_lat_na_kernel, rows=rows),
        grid=(batch, steps),
        in_specs=[
            pl.BlockSpec((qrows, BRANCH_W), lambda b, r: (b * steps + r, T_DQ)),
            pl.BlockSpec((qrows, BRANCH_W), lambda b, r: (b * steps + r, T_DG)),
            pl.BlockSpec((seq, BRANCH_W), lambda b, r: (b, T_DK)),
            pl.BlockSpec((seq, BRANCH_W), lambda b, r: (b, T_DV)),
            pl.BlockSpec(bias.shape, lambda b, r: (0, 0, 0, 0)),
            ctx, ctx,
        ],
        out_specs=pl.BlockSpec((qrows, BRANCH_W), lambda b, r: (b * steps + r, 0)),
        out_shape=jax.ShapeDtypeStruct((batch * seq, BRANCH_W), BF16),
        compiler_params=pltpu.CompilerParams(dimension_semantics=("parallel", "arbitrary")),
        name="lat_na",
    )(p, p, p, p, bias, ctx_k, ctx_v)


def _toeplitz_kernel(r_ref, o_ref):
    x = r_ref[...]
    hi = x.astype(BF16)
    r1 = x - hi.astype(F32)
    mid = r1.astype(BF16)
    lo = (r1 - mid.astype(F32)).astype(BF16)
    shape = (x.shape[1], GRID_W * GRID_W)
    b = lax.broadcasted_iota(jnp.int32, shape, 0)
    col = lax.broadcasted_iota(jnp.int32, shape, 1)
    kc = jnp.bitwise_and(col, GRID_W - 1)
    qc = lax.shift_right_logical(col, GRID_W.bit_length() - 1)
    onehot = jnp.where(kc - qc + (NA_COLS - 1) == b, 1.0, 0.0).astype(BF16)
    o_ref[...] = _dot(hi, onehot) + _dot(mid, onehot) + _dot(lo, onehot)


def _na_bias_tables(na_rpb):
    depth = na_rpb.shape[0]
    n = depth * NA_HEADS * N_RPB_ROWS
    n_pad = -(-n // 8) * 8
    r2 = jnp.zeros((n_pad, 128), F32).at[:n, :N_RPB_COLS].set(na_rpb.reshape(n, N_RPB_COLS))
    flat = pl.pallas_call(
        _toeplitz_kernel,
        out_shape=jax.ShapeDtypeStruct((n_pad, GRID_W * GRID_W), F32),
        name="na_bias_toeplitz",
    )(r2)
    t = flat[:n].reshape(depth, NA_HEADS, N_RPB_ROWS, GRID_W, GRID_W)
    qc = np.arange(GRID_W)[:, None]
    kc = np.arange(GRID_W)[None, :]
    win0 = np.clip(qc - NA_COLS // 2, 0, GRID_W - NA_COLS)
    vis = (kc >= win0) & (kc < win0 + NA_COLS)
    t = jnp.where(vis, t, NEG_INF)
    return jnp.concatenate([t[:, :, :-1], t[:, :, 1:]], axis=-1)


def _bf16_table(t):
    return jnp.asarray(t, F32).astype(BF16)


def _dft_cos_sin(n):
    idx = np.arange(n)
    ang = 2.0 * np.pi * ((idx[:, None] * idx[None, :]) % n) / n
    return np.cos(ang), np.sin(ang)


def _channel_dft_blocks(scale):
    c, s = _dft_cos_sin(FN_GROUP_W)
    eye = np.eye(FN_GROUPS)
    return np.kron(eye, c) * scale, np.kron(eye, s) * scale


def _ctx_fourier_tables(seq):
    ct, st = _dft_cos_sin(seq)
    cb, sb = _channel_dft_blocks(1.0 / np.sqrt(seq * FN_GROUP_W))
    return tuple(_bf16_table(t) for t in (ct, st, cb, sb))


def _lat_fourier_tables(seq, chunks=8):
    r = FFT_R
    c64, s64 = _dft_cos_sin(r)
    k1 = np.arange(r)[:, None]
    t2 = np.arange(r)[None, :]
    ang = 2.0 * np.pi * (k1 * t2) / (r * r)
    twr = np.cos(ang).reshape(r, r // chunks, chunks).transpose(1, 0, 2)
    twi = (-np.sin(ang)).reshape(r, r // chunks, chunks).transpose(1, 0, 2)
    lmat = np.block([[c64, s64], [-s64, c64]])
    cb, sb = _channel_dft_blocks(1.0 / np.sqrt(seq * FN_GROUP_W))
    csmat = np.concatenate([cb, sb], axis=0)
    return (_bf16_table(c64), _bf16_table(s64), jnp.asarray(twr, F32), jnp.asarray(twi, F32),
            _bf16_table(lmat), _bf16_table(csmat))


def _rope_tables(seq):
    half = HEAD_DIM // 2
    quarter = half // 2
    t = jnp.arange(seq)
    inv = ROPE_BASE ** (-jnp.arange(quarter, dtype=F32) / quarter)

    def cs(pos):
        ang = pos.astype(F32)[:, None] * inv[None, :]
        c, s = jnp.cos(ang), jnp.sin(ang)
        return jnp.concatenate([c, c], axis=-1), jnp.concatenate([-s, s], axis=-1)

    cr, sr = cs(t // GRID_W)
    cc, sc = cs(t % GRID_W)
    cos = jnp.concatenate([cr, cc], axis=-1)
    sin = jnp.concatenate([sr, sc], axis=-1)
    return jnp.tile(cos, (1, 2)), jnp.tile(sin, (1, 2))


def _tiled_in_weights(wl):
    kv_pad = jnp.zeros((D_MODEL, TILE - 2 * KV_W), F32)
    zero_tile = jnp.zeros((D_MODEL, TILE), F32)
    cols = [wl[:, OFF_A:OFF_BQ], wl[:, OFF_BQ:OFF_BK], wl[:, OFF_BG:OFF_FU], wl[:, OFF_BK:OFF_BG], kv_pad,
            wl[:, OFF_FU:N_MAIN], zero_tile, wl[:, N_MAIN:]]
    return jnp.concatenate(cols, axis=1).astype(BF16)


def kernel(x_prompt, x_sample, cache_win_k, cache_win_v, cache_na_k, cache_na_v, c, c_ctx, norm_g, w_ada, b_ada,
           w_in, conv_w, win_sink, na_rpb, w_branch, w_out, final_g):
    batch, seq, _ = x_prompt.shape
    dbatch, dseq, _ = x_sample.shape
    past = cache_win_k.shape[2]

    cv8 = jnp.zeros((8, D_MODEL), F32).at[0].set(c_ctx).at[1:1 + dbatch].set(c)
    mod = _modulation(cv8, w_ada, b_ada)

    ctx_tabs = _ctx_fourier_tables(seq)
    lat_tabs = _lat_fourier_tables(dseq)
    cos_t, sin_t = _rope_tables(dseq)
    na_bias = _na_bias_tables(na_rpb)
    final_g2 = final_g.reshape(1, D_MODEL)

    xp = x_prompt.reshape(batch * seq, D_MODEL)
    xs = x_sample.reshape(dbatch * dseq, D_MODEL)
    new_kv = [jnp.zeros((batch, DEPTH, seq, width), F32) for width in (KV_W, KV_W, BRANCH_W, BRANCH_W)]

    for l in range(DEPTH):
        shift = mod[l, :, 0:D_MODEL].reshape(8, 1, D_MODEL)
        scale = mod[l, :, D_MODEL:2 * D_MODEL].reshape(8, 1, D_MODEL)
        gate = mod[l, :, 2 * D_MODEL:].reshape(8, 1, D_MODEL)
        g = norm_g[l].reshape(1, D_MODEL)
        w_p = _tiled_in_weights(w_in[l])
        w_br = w_branch[l].astype(BF16)
        w_o = w_out[l].astype(BF16)
        cw = conv_w[l]
        sink_b = jnp.broadcast_to(win_sink[l][:, None], (WIN_HEADS, 128))
        final = l == DEPTH - 1

        p, hp, *new_kv = _inproj(
            xp, shift, scale, g, w_p, rows_per_cond=batch * seq, cond0=0, update=(new_kv, l, seq),
            extras=((T_BKV, 0, KV_W), (T_BKV, KV_W, KV_W), (T_DK, 0, BRANCH_W), (T_DV, 0, BRANCH_W)))
        o_ctx = _ctx_mixers(p, cw, sink_b, *ctx_tabs, seq=seq)
        xp = _out_stage(xp, hp, gate, final_g2, [o_ctx] * 4, [0, 1, 2, 3], w_p, w_br, w_o,
                        rows_per_cond=batch * seq, cond0=0, final=final)

        q, hs, qfu, qfg = _inproj(xs, shift, scale, g, w_p, rows_per_cond=dseq, cond0=1, extra_dtype=BF16,
                              extras=((T_FU, 0, BRANCH_W), (T_FG, 0, BRANCH_W)))
        o_a = _lat_conv(q, cw, batch=dbatch, seq=dseq)
        o_w = _lat_window(q, cos_t, sin_t,
                          cache_win_k[:, l].reshape(dbatch, past, KV_W), cache_win_v[:, l].reshape(dbatch, past, KV_W),
                          sink_b, batch=dbatch, seq=dseq)
        o_f = _lat_fourier(qfu, qfg, lat_tabs, batch=dbatch, seq=dseq)
        o_n = _lat_na(q, na_bias[l],
                      cache_na_k[:, l].reshape(dbatch, past, BRANCH_W), cache_na_v[:, l].reshape(dbatch, past, BRANCH_W),
                      batch=dbatch, seq=dseq)
        xs = _out_stage(xs, hs, gate, final_g2, [o_a, o_w, o_f, o_n], [0, 0, 0, 0], w_p, w_br, w_o,
                        rows_per_cond=dseq, cond0=1, final=final)

    y_prompt = xp.reshape(batch, seq, D_MODEL)
    y_sample = xs.reshape(dbatch, dseq, D_MODEL)
    heads = (WIN_KV_HEADS, WIN_KV_HEADS, NA_HEADS, NA_HEADS)
    return (y_prompt, y_sample) + tuple(a.reshape(batch, DEPTH, seq, n, HEAD_DIM) for a, n in zip(new_kv, heads))
```

```python
import functools
import math

import numpy as np
import jax
import jax.numpy as jnp
from jax import lax
from jax.experimental import pallas as pl
from jax.experimental.pallas import tpu as pltpu

D_MODEL = 2048
DEPTH = 2
GRID_W = 64
N_BRANCH = 4
BRANCH_W = 512
HEAD_DIM = 64
WIN_HEADS = 8
WIN_KV_HEADS = 2
KV_W = WIN_KV_HEADS * HEAD_DIM
WIN_BLOCK = 128
FN_GROUPS = 4
FN_GROUP_W = BRANCH_W // FN_GROUPS
NA_HEADS = 8
NA_ROWS = 8
NA_COLS = 16
ROPE_BASE = 10000.0
EPS = 1e-6
ATTN_SCALE = HEAD_DIM ** -0.5
NEG_INF = -1e30
N_MAIN = 6400

F32 = jnp.float32
BF16 = jnp.bfloat16
LANES = 128

OFF_A = 0
OFF_BQ = OFF_A + 4 * BRANCH_W
OFF_BK = OFF_BQ + BRANCH_W
OFF_BG = OFF_BK + 2 * KV_W
OFF_FU = OFF_BG + BRANCH_W
assert OFF_FU + 6 * BRANCH_W == N_MAIN

TILE = 512
T_AX, T_AB, T_AC, T_AG, T_BQ, T_BG, T_BKV, T_FU, T_FG, T_DQ, T_DK, T_DV, T_DG = range(13)
STEP_TILES = 2
N_TILES = 14
P_WIDTH = N_TILES * TILE

V7X_VMEM_BYTES = 64 * 2 ** 20
PROJ_VMEM_BYTES = V7X_VMEM_BYTES - 2 ** 20

FFT_R = 64


def _silu(x):
    return x * jax.nn.sigmoid(x)


def _dot(a, b):
    return jnp.dot(a, b, preferred_element_type=F32)


def _dot_nt(a, b):
    return lax.dot_general(a, b, (((1,), (1,)), ((), ())), preferred_element_type=F32)


def _rms_mod(x, g, scale, shift):
    ms = jnp.mean(x * x, axis=-1, keepdims=True)
    return (x * lax.rsqrt(ms + EPS) * g) * (1.0 + scale) + shift


def _store_modulated(h_ref, x_ref, g_ref, sc_ref, sh_ref):
    h_ref[...] = _rms_mod(x_ref[...], g_ref[...], sc_ref[0], sh_ref[0]).astype(h_ref.dtype)


def _mod_kernel(cv_ref, w_ref, b_ref, o_ref):
    s = _silu(cv_ref[...]).astype(BF16)
    o_ref[0] = _dot(s, w_ref[0].astype(BF16)) + b_ref[0]


def _modulation(cv8, w_ada, b_ada):
    tn = 512
    n3 = 3 * D_MODEL
    return pl.pallas_call(
        _mod_kernel,
        grid=(DEPTH, n3 // tn),
        in_specs=[
            pl.BlockSpec((8, D_MODEL), lambda l, j: (0, 0)),
            pl.BlockSpec((1, D_MODEL, tn), lambda l, j: (l, 0, j)),
            pl.BlockSpec((1, 1, tn), lambda l, j: (l, 0, j)),
        ],
        out_specs=pl.BlockSpec((1, 8, tn), lambda l, j: (l, 0, j)),
        out_shape=jax.ShapeDtypeStruct((DEPTH, 8, n3), F32),
        name="modulation",
    )(cv8, w_ada, b_ada.reshape(DEPTH, 1, n3))


def _inproj_kernel(x_ref, sh_ref, sc_ref, g_ref, w_ref, *rest, extras, n_updated):
    p_ref, h_ref, extra_refs = rest[n_updated], rest[n_updated + 1], rest[n_updated + 2:]
    j = pl.program_id(1)

    @pl.when(j == 0)
    def _():
        _store_modulated(h_ref, x_ref, g_ref, sc_ref, sh_ref)

    acc = _dot(h_ref[...], w_ref[...])
    p_ref[...] = acc.astype(p_ref.dtype)
    for ref, (tile, lo) in zip(extra_refs, extras):
        @pl.when(j == tile // STEP_TILES)
        def _(ref=ref, lo=(tile % STEP_TILES) * TILE + lo):
            cols = acc[:, lo:lo + ref.shape[-1]].astype(ref.dtype)
            if ref.ndim == 2:
                ref[...] = cols
            else:
                ref[:, 0] = cols.reshape(ref.shape[0], ref.shape[2], ref.shape[3])


def _inproj(x2d, shift, scale, g, w, *, rows_per_cond, cond0, extras, extra_dtype=None, update=None, tm=1024):
    m = x2d.shape[0]
    assert w.shape[0] == D_MODEL and w.shape[1] >= P_WIDTH and m % tm == 0 and rows_per_cond % tm == 0
    tiles_per_cond = rows_per_cond // tm
    cond = lambda i, j: (cond0 + i // tiles_per_cond, 0, 0)
    if update is None:
        updated, aliases = [], {}
        extra_specs = [pl.BlockSpec((tm, width), lambda i, j: (i, 0)) for _, _, width in extras]
        extra_shapes = [jax.ShapeDtypeStruct((m, width), extra_dtype) for _, _, width in extras]
    else:
        updated, layer, seq = update
        assert tm % seq == 0 and all(a.shape[2:] == (seq, width) for a, (_, _, width) in zip(updated, extras))
        extra_specs = [pl.BlockSpec((tm // seq, 1, seq, width), lambda i, j: (i, layer, 0, 0)) for _, _, width in extras]
        extra_shapes = [jax.ShapeDtypeStruct(a.shape, a.dtype) for a in updated]
        aliases = {5 + k: 2 + k for k in range(len(updated))}
    return pl.pallas_call(
        functools.partial(_inproj_kernel, extras=tuple((t, lo) for t, lo, _ in extras), n_updated=len(updated)),
        grid=(m // tm, N_TILES // STEP_TILES),
        in_specs=[
            pl.BlockSpec((tm, D_MODEL), lambda i, j: (i, 0)),
            pl.BlockSpec((1, 1, D_MODEL), cond),
            pl.BlockSpec((1, 1, D_MODEL), cond),
            pl.BlockSpec((1, D_MODEL), lambda i, j: (0, 0)),
            pl.BlockSpec((D_MODEL, STEP_TILES * TILE), lambda i, j: (0, j)),
        ] + [pl.BlockSpec(memory_space=pl.ANY) for _ in updated],
        out_specs=[pl.BlockSpec((tm, STEP_TILES * TILE), lambda i, j: (i, j)),
                   pl.BlockSpec((tm, D_MODEL), lambda i, j: (i, 0))]
        + extra_specs,
        out_shape=[jax.ShapeDtypeStruct((m, P_WIDTH), BF16), jax.ShapeDtypeStruct((m, D_MODEL), BF16)] + extra_shapes,
        input_output_aliases=aliases,
        compiler_params=pltpu.CompilerParams(dimension_semantics=("parallel", "arbitrary"),
                                             vmem_limit_bytes=PROJ_VMEM_BYTES),
        name="inproj",
    )(x2d, shift, scale, g, w, *updated)


def _out_kernel(x_ref, h_ref, gt_ref, fg_ref, oa_ref, ow_ref, of_ref, on_ref,
                wg0_ref, wg1_ref, wg2_ref, wg3_ref, wb_ref, wo_ref, y_ref, merged_ref, *, final):
    j = pl.program_id(1)
    h = h_ref[...]
    merged = None
    for i, (o_ref, wg_ref) in enumerate(((oa_ref, wg0_ref), (ow_ref, wg1_ref), (of_ref, wg2_ref), (on_ref, wg3_ref))):
        term = jax.nn.sigmoid(_dot(h, wg_ref[...])) * _dot(o_ref[...], wb_ref[i])
        merged = term if merged is None else merged + term
    merged_ref[j] = merged.astype(BF16)

    @pl.when(j == pl.num_programs(1) - 1)
    def _():
        m_all = jnp.concatenate([merged_ref[k] for k in range(merged_ref.shape[0])], axis=1)
        xn = x_ref[...] + gt_ref[0] * _dot(m_all, wo_ref[...])
        if final:
            ms = jnp.mean(xn * xn, axis=-1, keepdims=True)
            xn = xn * lax.rsqrt(ms + EPS) * fg_ref[...]
        y_ref[...] = xn


def _out_stage(x2d, h2d, gate, final_g, o_arrays, o_cols, w_gate, w_branch, w_out,
               *, rows_per_cond, cond0, final, tm=512, tc=512):
    m = x2d.shape[0]
    assert m % tm == 0 and rows_per_cond % tm == 0
    tiles_per_cond = rows_per_cond // tm
    ncol = D_MODEL // tc
    cond = lambda i, j: (cond0 + i // tiles_per_cond, 0, 0)
    o_specs = [pl.BlockSpec((tm, BRANCH_W), functools.partial(lambda i, j, c: (i, c), c=c)) for c in o_cols]
    assert tc == TILE and w_gate.shape == (D_MODEL, P_WIDTH + N_BRANCH * D_MODEL)
    wg_specs = [pl.BlockSpec((D_MODEL, tc), functools.partial(lambda i, j, b: (0, N_TILES + b * ncol + j), b=b))
                for b in range(N_BRANCH)]
    return pl.pallas_call(
        functools.partial(_out_kernel, final=final),
        grid=(m // tm, ncol),
        in_specs=[
            pl.BlockSpec((tm, D_MODEL), lambda i, j: (i, 0)),
            pl.BlockSpec((tm, D_MODEL), lambda i, j: (i, 0)),
            pl.BlockSpec((1, 1, D_MODEL), cond),
            pl.BlockSpec((1, D_MODEL), lambda i, j: (0, 0)),
            *o_specs,
            *wg_specs,
            pl.BlockSpec((N_BRANCH, BRANCH_W, tc), lambda i, j: (0, 0, j)),
            pl.BlockSpec((D_MODEL, D_MODEL), lambda i, j: (0, 0), pipeline_mode=pl.Buffered(1)),
        ],
        out_specs=pl.BlockSpec((tm, D_MODEL), lambda i, j: (i, 0)),
        out_shape=jax.ShapeDtypeStruct((m, D_MODEL), F32),
        scratch_shapes=[pltpu.VMEM((ncol, tm, tc), BF16)],
        compiler_params=pltpu.CompilerParams(dimension_semantics=("parallel", "arbitrary"),
                                             vmem_limit_bytes=PROJ_VMEM_BYTES),
        name="out_stage",
    )(x2d, h2d, gate, final_g, *o_arrays, w_gate, w_gate, w_gate, w_gate, w_branch, w_out)


def _attend(scores, values, sinks=None):
    heads = range(len(scores))
    sinks = [None] * len(scores) if sinks is None else sinks
    m = []
    for h in heads:
        mh = scores[h][0].max(axis=-1, keepdims=True)
        for s in scores[h][1:]:
            mh = jnp.maximum(mh, s.max(axis=-1, keepdims=True))
        m.append(mh if sinks[h] is None else jnp.maximum(mh, sinks[h]))
    e = [[jnp.exp(s - m[h]) for s in scores[h]] for h in heads]
    den = []
    for h in heads:
        d = e[h][0].sum(axis=-1, keepdims=True)
        for x in e[h][1:]:
            d = d + x.sum(axis=-1, keepdims=True)
        den.append(d if sinks[h] is None else d + jnp.exp(sinks[h] - m[h]))
    out = []
    for h in heads:
        acc = _dot(e[h][0].astype(BF16), values[h][0])
        for x, v in zip(e[h][1:], values[h][1:]):
            acc = acc + _dot(x.astype(BF16), v)
        out.append(acc)
    return [a / d for a, d in zip(out, den)]


def _shift_rows(z, first_row, last_row):
    n = z.shape[0]
    row = lax.broadcasted_iota(jnp.int32, z.shape, 0)
    z_dn = jnp.where(row == 0, first_row, pltpu.roll(z, 1, axis=0))
    z_up = jnp.where(row == n - 1, last_row, pltpu.roll(z, n - 1, axis=0))
    return z_dn, z_up


def _tile(ref, t, dtype=None):
    v = ref[:, t * TILE:(t + 1) * TILE]
    return v if dtype is None else v.astype(dtype)


def _head(ref, t, h, base=0):
    lo = t * TILE + base + h * HEAD_DIM
    return ref[:, lo:lo + HEAD_DIM]


def _scaled_queries(q):
    assert math.frexp(ATTN_SCALE)[0] == 0.5
    return q * ATTN_SCALE


def _ctx_mixer_kernel(p_ref, cw_ref, sink_ref, ct_ref, st_ref, cb_ref, sb_ref, o_ref):
    z = _tile(p_ref, T_AC, F32) * _tile(p_ref, T_AX, F32)
    zero_row = jnp.zeros((1, BRANCH_W), F32)
    z_dn, z_up = _shift_rows(z, zero_row, zero_row)
    y = _tile(p_ref, T_AB, F32) * (z_dn * cw_ref[0:1, :] + z * cw_ref[1:2, :] + z_up * cw_ref[2:3, :])
    out = lambda b: slice(b * BRANCH_W, (b + 1) * BRANCH_W)
    o_ref[:, out(0)] = (y * _silu(_tile(p_ref, T_AG, F32))).astype(o_ref.dtype)

    gsz = WIN_HEADS // WIN_KV_HEADS
    head = lambda a, h: a[:, h * HEAD_DIM:(h + 1) * HEAD_DIM]
    qb = _scaled_queries(_tile(p_ref, T_BQ))
    qd = _scaled_queries(_tile(p_ref, T_DQ))
    scores = ([[_dot_nt(head(qb, h), _head(p_ref, T_BKV, h // gsz))] for h in range(WIN_HEADS)]
              + [[_dot_nt(head(qd, h), _head(p_ref, T_DK, h))] for h in range(NA_HEADS)])
    values = ([[_head(p_ref, T_BKV, h // gsz, base=KV_W)] for h in range(WIN_HEADS)]
              + [[_head(p_ref, T_DV, h)] for h in range(NA_HEADS)])
    sinks = [sink_ref[h:h + 1, 0:1] for h in range(WIN_HEADS)] + [None] * NA_HEADS
    heads = _attend(scores, values, sinks=sinks)
    o_w = jnp.concatenate(heads[:WIN_HEADS], axis=-1) * _silu(_tile(p_ref, T_BG, F32))
    o_ref[:, out(1)] = o_w.astype(o_ref.dtype)
    o_n = jnp.concatenate(heads[WIN_HEADS:], axis=-1) * _silu(_tile(p_ref, T_DG, F32))
    o_ref[:, out(3)] = o_n.astype(o_ref.dtype)

    u = _tile(p_ref, T_FU)
    uc = _dot(u, cb_ref[...]).astype(BF16)
    us = _dot(u, sb_ref[...]).astype(BF16)
    o_f = (_dot(ct_ref[...], uc) - _dot(st_ref[...], us)) * _silu(_tile(p_ref, T_FG, F32))
    o_ref[:, out(2)] = o_f.astype(o_ref.dtype)


def _ctx_mixers(p, conv_w, sink_b, ct, st, cb, sb, *, seq):
    m = p.shape[0]
    whole = lambda a: pl.BlockSpec(a.shape, lambda b: (0,) * a.ndim)
    return pl.pallas_call(
        _ctx_mixer_kernel,
        grid=(m // seq,),
        in_specs=[pl.BlockSpec((seq, P_WIDTH), lambda b: (b, 0)), whole(conv_w), whole(sink_b),
                  whole(ct), whole(st), whole(cb), whole(sb)],
        out_specs=pl.BlockSpec((seq, N_BRANCH * BRANCH_W), lambda b: (b, 0)),
        out_shape=jax.ShapeDtypeStruct((m, N_BRANCH * BRANCH_W), BF16),
        compiler_params=pltpu.CompilerParams(dimension_semantics=("parallel",)),
        name="ctx_mixers",
    )(p, conv_w, sink_b, ct, st, cb, sb)


HALO = 16


def _lat_conv_kernel(ax_ref, ab_ref, ac_ref, ag_ref, axp_ref, acp_ref, axn_ref, acn_ref, cw_ref, o_ref):
    i = pl.program_id(1)
    z = ac_ref[...].astype(F32) * ax_ref[...].astype(F32)
    zp = acp_ref[HALO - 1:HALO, :].astype(F32) * axp_ref[HALO - 1:HALO, :].astype(F32)
    zn = acn_ref[0:1, :].astype(F32) * axn_ref[0:1, :].astype(F32)
    zp = jnp.where(i == 0, 0.0, zp)
    zn = jnp.where(i == pl.num_programs(1) - 1, 0.0, zn)
    z_dn, z_up = _shift_rows(z, zp, zn)
    y = ab_ref[...].astype(F32) * (z_dn * cw_ref[0:1, :] + z * cw_ref[1:2, :] + z_up * cw_ref[2:3, :])
    o_ref[...] = (y * _silu(ag_ref[...].astype(F32))).astype(o_ref.dtype)


def _lat_conv(pa, conv_w, *, batch, seq, tr=512):
    nt = seq // tr
    hb = tr // HALO
    last_halo = batch * seq // HALO - 1
    main = lambda c: pl.BlockSpec((tr, BRANCH_W), functools.partial(lambda b, i, c: (b * nt + i, c), c=c))
    prev = lambda c: pl.BlockSpec(
        (HALO, BRANCH_W), functools.partial(lambda b, i, c: (jnp.maximum((b * nt + i) * hb - 1, 0), c), c=c))
    nxt = lambda c: pl.BlockSpec(
        (HALO, BRANCH_W), functools.partial(lambda b, i, c: (jnp.minimum((b * nt + i + 1) * hb, last_halo), c), c=c))
    return pl.pallas_call(
        _lat_conv_kernel,
        grid=(batch, nt),
        in_specs=[main(T_AX), main(T_AB), main(T_AC), main(T_AG), prev(T_AX), prev(T_AC), nxt(T_AX), nxt(T_AC),
                  pl.BlockSpec(conv_w.shape, lambda b, i: (0, 0))],
        out_specs=pl.BlockSpec((tr, BRANCH_W), lambda b, i: (b * nt + i, 0)),
        out_shape=jax.ShapeDtypeStruct((batch * seq, BRANCH_W), BF16),
        compiler_params=pltpu.CompilerParams(dimension_semantics=("parallel", "arbitrary")),
        name="lat_conv",
    )(pa, pa, pa, pa, pa, pa, pa, pa, conv_w)


def _rope(x, cos, sin_signed):
    lane = lax.broadcasted_iota(jnp.int32, x.shape, 1)
    quarter = HEAD_DIM // 4
    partner = jnp.where((lane % (2 * quarter)) < quarter,
                        pltpu.roll(x, LANES - quarter, axis=1), pltpu.roll(x, quarter, axis=1))
    return x * cos + partner * sin_signed


WIN_BLOCKS_PER_STEP = 2


def _lat_win_kernel(q_ref, g_ref, kp_ref, kc_ref, kn_ref, vp_ref, vc_ref, vn_ref,
                    cq_ref, sq_ref, cp_ref, sp_ref, cn_ref, sn_ref, ck_ref, cv_ref, sink_ref, o_ref):
    nq = WIN_BLOCKS_PER_STEP
    nb = pl.num_programs(1) * nq
    wb = WIN_BLOCK
    cq, sq = cq_ref[...], sq_ref[...]
    k_rot = jnp.concatenate([
        _rope(kp_ref[...].astype(F32), cp_ref[...], sp_ref[...]),
        _rope(kc_ref[...].astype(F32), cq, sq),
        _rope(kn_ref[...].astype(F32), cn_ref[...], sn_ref[...]),
    ], axis=0).astype(BF16)
    v_all = jnp.concatenate([vp_ref[...], vc_ref[...], vn_ref[...]], axis=0).astype(BF16)
    qi = lax.broadcasted_iota(jnp.int32, (wb, 3 * wb), 0)
    kj = lax.broadcasted_iota(jnp.int32, (wb, 3 * wb), 1)
    ck = ck_ref[0].astype(BF16)
    cv = cv_ref[0].astype(BF16)
    gsz = WIN_HEADS // WIN_KV_HEADS
    q_rot = [_scaled_queries(_rope(q_ref[:, pair * LANES:(pair + 1) * LANES].astype(F32), cq, sq)).astype(BF16)
             for pair in range(WIN_HEADS // 2)]
    scores, values, sinks = [], [], []
    for d in range(nq):
        n = pl.program_id(1) * nq + d
        mask = (((kj < wb) & (kj >= qi) & (n > 0)) | ((kj >= wb) & (kj < 2 * wb))
                | ((kj >= 2 * wb) & (kj - 2 * wb <= qi) & (n < nb - 1)))
        rows = slice(d * wb, (d + 1) * wb)
        win = slice(d * wb, (d + 3) * wb)
        for h in range(WIN_HEADS):
            kv = slice((h // gsz) * HEAD_DIM, (h // gsz + 1) * HEAD_DIM)
            q = q_rot[h // 2][rows, (h % 2) * HEAD_DIM:(h % 2 + 1) * HEAD_DIM]
            s_loc = jnp.where(mask, _dot_nt(q, k_rot[win, kv]), NEG_INF)
            scores.append([s_loc, _dot_nt(q, ck[:, kv])])
            values.append([v_all[win, kv], cv[:, kv]])
            sinks.append(sink_ref[h:h + 1, 0:1])
    heads = _attend(scores, values, sinks=sinks)
    o = jnp.concatenate([jnp.concatenate(heads[d * WIN_HEADS:(d + 1) * WIN_HEADS], axis=-1) for d in range(nq)], axis=0)
    o_ref[...] = (o * _silu(g_ref[...].astype(F32))).astype(o_ref.dtype)


def _lat_window(pb, cos_t, sin_t, ctx_k, ctx_v, sink_b, *, batch, seq):
    wb = WIN_BLOCK
    k_col = T_BKV * (TILE // KV_W)
    v_col = k_col + 1
    nq = WIN_BLOCKS_PER_STEP
    nb = seq // wb
    steps = nb // nq
    nbr = lambda s, d: jnp.clip(s * nq + (nq if d > 0 else -1), 0, nb - 1)
    own = lambda col, width: pl.BlockSpec(
        (nq * wb, width), functools.partial(lambda b, s, col: (b * steps + s, col), col=col))
    kv = lambda col, d: pl.BlockSpec(
        (wb, KV_W), functools.partial(lambda b, s, col, d: (b * nb + nbr(s, d), col), col=col, d=d))
    tab = lambda d: pl.BlockSpec((wb, LANES), functools.partial(lambda b, s, d: (nbr(s, d), 0), d=d))
    tab_own = pl.BlockSpec((nq * wb, LANES), lambda b, s: (s, 0))
    ctx = pl.BlockSpec((1,) + ctx_k.shape[1:], lambda b, s: (b, 0, 0))
    return pl.pallas_call(
        _lat_win_kernel,
        grid=(batch, steps),
        in_specs=[
            own(T_BQ, BRANCH_W), own(T_BG, BRANCH_W),
            kv(k_col, -1), own(k_col, KV_W), kv(k_col, 1), kv(v_col, -1), own(v_col, KV_W), kv(v_col, 1),
            tab_own, tab_own, tab(-1), tab(-1), tab(1), tab(1),
            ctx, ctx,
            pl.BlockSpec(sink_b.shape, lambda b, s: (0, 0)),
        ],
        out_specs=pl.BlockSpec((nq * wb, BRANCH_W), lambda b, s: (b * steps + s, 0)),
        out_shape=jax.ShapeDtypeStruct((batch * seq, BRANCH_W), BF16),
        compiler_params=pltpu.CompilerParams(dimension_semantics=("parallel", "arbitrary")),
        name="lat_window",
    )(pb, pb, pb, pb, pb, pb, pb, pb, cos_t, sin_t, cos_t, sin_t, cos_t, sin_t, ctx_k, ctx_v, sink_b)


def _fft1_kernel(u_ref, c_ref, s_ref, twr_ref, twi_ref, zr_ref, zi_ref, *, chunks):
    u = u_ref[0].astype(BF16)
    yr = _dot(c_ref[...], u)
    yi = -_dot(s_ref[...], u)
    for t in range(chunks):
        wr = twr_ref[0, :, t:t + 1]
        wi = twi_ref[0, :, t:t + 1]
        a = yr[:, t * BRANCH_W:(t + 1) * BRANCH_W]
        b = yi[:, t * BRANCH_W:(t + 1) * BRANCH_W]
        zr_ref[0, :, t * BRANCH_W:(t + 1) * BRANCH_W] = (a * wr - b * wi).astype(zr_ref.dtype)
        zi_ref[0, :, t * BRANCH_W:(t + 1) * BRANCH_W] = (a * wi + b * wr).astype(zi_ref.dtype)


def _fft2_kernel(zr_ref, zi_ref, g_ref, l_ref, cs_ref, o_ref, *, kblock):
    r = FFT_R
    zz = [jnp.concatenate([zr_ref[0, kk], zi_ref[0, kk]], axis=0) for kk in range(kblock)]
    xx = [_dot(l_ref[...], z) for z in zz]
    xcat = [jnp.concatenate([x[0:r], x[r:2 * r]], axis=1).astype(BF16) for x in xx]
    out = [_dot(x, cs_ref[...]) for x in xcat]
    for kk in range(kblock):
        sl = slice(kk * BRANCH_W, (kk + 1) * BRANCH_W)
        o_ref[0, :, sl] = (out[kk] * _silu(g_ref[0, :, sl].astype(F32))).astype(o_ref.dtype)


def _lat_fourier(fu, fg, tabs, *, batch, seq):
    r = FFT_R
    assert seq == r * r
    wide = r * BRANCH_W
    chunks = 8
    c64, s64, twr, twi, lmat, csmat = tabs
    u3 = fu.reshape(batch, r, wide)
    nct = r // chunks
    zr, zi = pl.pallas_call(
        functools.partial(_fft1_kernel, chunks=chunks),
        grid=(batch, nct),
        in_specs=[
            pl.BlockSpec((1, r, chunks * BRANCH_W), lambda b, t: (b, 0, t)),
            pl.BlockSpec((r, r), lambda b, t: (0, 0)),
            pl.BlockSpec((r, r), lambda b, t: (0, 0)),
            pl.BlockSpec((1, r, chunks), lambda b, t: (t, 0, 0)),
            pl.BlockSpec((1, r, chunks), lambda b, t: (t, 0, 0)),
        ],
        out_specs=[pl.BlockSpec((1, r, chunks * BRANCH_W), lambda b, t: (b, 0, t))] * 2,
        out_shape=[jax.ShapeDtypeStruct((batch, r, wide), BF16)] * 2,
        compiler_params=pltpu.CompilerParams(dimension_semantics=("parallel", "arbitrary")),
        name="lat_fft1",
    )(u3, c64, s64, twr, twi)
    kblock = 8
    z4 = lambda z: z.reshape(batch, r, r, BRANCH_W)
    out = pl.pallas_call(
        functools.partial(_fft2_kernel, kblock=kblock),
        grid=(batch, r // kblock),
        in_specs=[
            pl.BlockSpec((1, kblock, r, BRANCH_W), lambda b, k: (b, k, 0, 0)),
            pl.BlockSpec((1, kblock, r, BRANCH_W), lambda b, k: (b, k, 0, 0)),
            pl.BlockSpec((1, r, kblock * BRANCH_W), lambda b, k: (b, 0, k)),
            pl.BlockSpec((2 * r, 2 * r), lambda b, k: (0, 0)),
            pl.BlockSpec((2 * BRANCH_W, BRANCH_W), lambda b, k: (0, 0)),
        ],
        out_specs=pl.BlockSpec((1, r, kblock * BRANCH_W), lambda b, k: (b, 0, k)),
        out_shape=jax.ShapeDtypeStruct((batch, r, wide), BF16),
        compiler_params=pltpu.CompilerParams(dimension_semantics=("parallel", "arbitrary")),
        name="lat_fft2",
    )(z4(zr), z4(zi), fg.reshape(batch, r, wide), lmat, csmat)
    return out.reshape(batch * seq, BRANCH_W)


N_RPB_ROWS = 2 * NA_ROWS - 1
N_RPB_COLS = 2 * NA_COLS - 1


NA_ROWS_PER_STEP = 4


def _lat_na_kernel(q_ref, g_ref, k_ref, v_ref, bias_ref, ck_ref, cv_ref, o_ref, *, rows):
    ck = ck_ref[0].astype(BF16)
    cv = cv_ref[0].astype(BF16)
    sls = [slice(h * HEAD_DIM, (h + 1) * HEAD_DIM) for h in range(NA_HEADS)]
    scores, values = [], []
    for d in range(NA_ROWS_PER_STEP):
        r = pl.program_id(1) * NA_ROWS_PER_STEP + d
        r0 = jnp.clip(r - NA_ROWS // 2, 0, rows - NA_ROWS)
        start = pl.multiple_of(r0 * GRID_W, GRID_W)
        kw = k_ref[pl.ds(start, NA_ROWS * GRID_W), :]
        vw = v_ref[pl.ds(start, NA_ROWS * GRID_W), :]
        qa = _scaled_queries(q_ref[d * GRID_W:(d + 1) * GRID_W, :])
        a0 = r0 - r + NA_ROWS - 1
        for h, sl in enumerate(sls):
            bias = jnp.concatenate([bias_ref[h, a0 + 2 * i2] for i2 in range(NA_ROWS // 2)], axis=1)
            scores.append([_dot_nt(qa[:, sl], kw[:, sl]) + bias, _dot_nt(qa[:, sl], ck[:, sl])])
            values.append([vw[:, sl], cv[:, sl]])
    heads = _attend(scores, values)
    o = jnp.concatenate([jnp.concatenate(heads[d * NA_HEADS:(d + 1) * NA_HEADS], axis=-1)
                         for d in range(NA_ROWS_PER_STEP)], axis=0)
    o_ref[...] = (o * _silu(g_ref[...].astype(F32))).astype(o_ref.dtype)


def _lat_na(p, bias, ctx_k, ctx_v, *, batch, seq):
    rows = seq // GRID_W
    steps = rows // NA_ROWS_PER_STEP
    qrows = NA_ROWS_PER_STEP * GRID_W
    ctx = pl.BlockSpec((1,) + ctx_k.shape[1:], lambda b, r: (b, 0, 0))
    return pl.pallas_call(
        functools.partial(_lat_na_kernel, rows=rows),
        grid=(batch, steps),
        in_specs=[
            pl.BlockSpec((qrows, BRANCH_W), lambda b, r: (b * steps + r, T_DQ)),
            pl.BlockSpec((qrows, BRANCH_W), lambda b, r: (b * steps + r, T_DG)),
            pl.BlockSpec((seq, BRANCH_W), lambda b, r: (b, T_DK)),
            pl.BlockSpec((seq, BRANCH_W), lambda b, r: (b, T_DV)),
            pl.BlockSpec(bias.shape, lambda b, r: (0, 0, 0, 0)),
            ctx, ctx,
        ],
        out_specs=pl.BlockSpec((qrows, BRANCH_W), lambda b, r: (b * steps + r, 0)),
        out_shape=jax.ShapeDtypeStruct((batch * seq, BRANCH_W), BF16),
        compiler_params=pltpu.CompilerParams(dimension_semantics=("parallel", "arbitrary")),
        name="lat_na",
    )(p, p, p, p, bias, ctx_k, ctx_v)


def _toeplitz_kernel(r_ref, o_ref):
    x = r_ref[...]
    hi = x.astype(BF16)
    r1 = x - hi.astype(F32)
    mid = r1.astype(BF16)
    lo = (r1 - mid.astype(F32)).astype(BF16)
    shape = (x.shape[1], GRID_W * GRID_W)
    b = lax.broadcasted_iota(jnp.int32, shape, 0)
    col = lax.broadcasted_iota(jnp.int32, shape, 1)
    kc = jnp.bitwise_and(col, GRID_W - 1)
    qc = lax.shift_right_logical(col, GRID_W.bit_length() - 1)
    onehot = jnp.where(kc - qc + (NA_COLS - 1) == b, 1.0, 0.0).astype(BF16)
    o_ref[...] = _dot(hi, onehot) + _dot(mid, onehot) + _dot(lo, onehot)


def _na_bias_tables(na_rpb):
    depth = na_rpb.shape[0]
    n = depth * NA_HEADS * N_RPB_ROWS
    n_pad = -(-n // 8) * 8
    r2 = jnp.zeros((n_pad, LANES), F32).at[:n, :N_RPB_COLS].set(na_rpb.reshape(n, N_RPB_COLS))
    flat = pl.pallas_call(
        _toeplitz_kernel,
        out_shape=jax.ShapeDtypeStruct((n_pad, GRID_W * GRID_W), F32),
        name="na_bias_toeplitz",
    )(r2)
    t = flat[:n].reshape(depth, NA_HEADS, N_RPB_ROWS, GRID_W, GRID_W)
    qc = np.arange(GRID_W)[:, None]
    kc = np.arange(GRID_W)[None, :]
    win0 = np.clip(qc - NA_COLS // 2, 0, GRID_W - NA_COLS)
    vis = (kc >= win0) & (kc < win0 + NA_COLS)
    t = jnp.where(vis, t, NEG_INF)
    return jnp.concatenate([t[:, :, :-1], t[:, :, 1:]], axis=-1)


def _bf16_table(t):
    return jnp.asarray(t, F32).astype(BF16)


def _dft_cos_sin(n):
    idx = np.arange(n)
    ang = 2.0 * np.pi * ((idx[:, None] * idx[None, :]) % n) / n
    return np.cos(ang), np.sin(ang)


def _channel_dft_blocks(scale):
    c, s = _dft_cos_sin(FN_GROUP_W)
    eye = np.eye(FN_GROUPS)
    return np.kron(eye, c) * scale, np.kron(eye, s) * scale


def _ctx_fourier_tables(seq):
    ct, st = _dft_cos_sin(seq)
    cb, sb = _channel_dft_blocks(1.0 / np.sqrt(seq * FN_GROUP_W))
    return tuple(_bf16_table(t) for t in (ct, st, cb, sb))


def _lat_fourier_tables(seq, chunks=8):
    r = FFT_R
    c64, s64 = _dft_cos_sin(r)
    k1 = np.arange(r)[:, None]
    t2 = np.arange(r)[None, :]
    ang = 2.0 * np.pi * (k1 * t2) / (r * r)
    twr = np.cos(ang).reshape(r, r // chunks, chunks).transpose(1, 0, 2)
    twi = (-np.sin(ang)).reshape(r, r // chunks, chunks).transpose(1, 0, 2)
    lmat = np.block([[c64, s64], [-s64, c64]])
    cb, sb = _channel_dft_blocks(1.0 / np.sqrt(seq * FN_GROUP_W))
    csmat = np.concatenate([cb, sb], axis=0)
    return (_bf16_table(c64), _bf16_table(s64), jnp.asarray(twr, F32), jnp.asarray(twi, F32),
            _bf16_table(lmat), _bf16_table(csmat))


def _rope_tables(seq):
    half = HEAD_DIM // 2
    quarter = half // 2
    t = jnp.arange(seq)
    inv = ROPE_BASE ** (-jnp.arange(quarter, dtype=F32) / quarter)

    def cs(pos):
        ang = pos.astype(F32)[:, None] * inv[None, :]
        c, s = jnp.cos(ang), jnp.sin(ang)
        return jnp.concatenate([c, c], axis=-1), jnp.concatenate([-s, s], axis=-1)

    cr, sr = cs(t // GRID_W)
    cc, sc = cs(t % GRID_W)
    cos = jnp.concatenate([cr, cc], axis=-1)
    sin = jnp.concatenate([sr, sc], axis=-1)
    return jnp.tile(cos, (1, 2)), jnp.tile(sin, (1, 2))


def _tiled_in_weights(wl):
    kv_pad = jnp.zeros((D_MODEL, TILE - 2 * KV_W), F32)
    zero_tile = jnp.zeros((D_MODEL, TILE), F32)
    cols = [wl[:, OFF_A:OFF_BQ], wl[:, OFF_BQ:OFF_BK], wl[:, OFF_BG:OFF_FU], wl[:, OFF_BK:OFF_BG], kv_pad,
            wl[:, OFF_FU:N_MAIN], zero_tile, wl[:, N_MAIN:]]
    return jnp.concatenate(cols, axis=1).astype(BF16)


def kernel(x_prompt, x_sample, cache_win_k, cache_win_v, cache_na_k, cache_na_v, c, c_ctx, norm_g, w_ada, b_ada,
           w_in, conv_w, win_sink, na_rpb, w_branch, w_out, final_g):
    batch, seq, _ = x_prompt.shape
    dbatch, dseq, _ = x_sample.shape
    past = cache_win_k.shape[2]

    cv8 = jnp.zeros((8, D_MODEL), F32).at[0].set(c_ctx).at[1:1 + dbatch].set(c)
    mod = _modulation(cv8, w_ada, b_ada)

    ctx_tabs = _ctx_fourier_tables(seq)
    lat_tabs = _lat_fourier_tables(dseq)
    cos_t, sin_t = _rope_tables(dseq)
    na_bias = _na_bias_tables(na_rpb)
    final_g2 = final_g.reshape(1, D_MODEL)

    xp = x_prompt.reshape(batch * seq, D_MODEL)
    xs = x_sample.reshape(dbatch * dseq, D_MODEL)
    new_kv = [jnp.zeros((batch, DEPTH, seq, width), F32) for width in (KV_W, KV_W, BRANCH_W, BRANCH_W)]

    for l in range(DEPTH):
        shift = mod[l, :, 0:D_MODEL].reshape(8, 1, D_MODEL)
        scale = mod[l, :, D_MODEL:2 * D_MODEL].reshape(8, 1, D_MODEL)
        gate = mod[l, :, 2 * D_MODEL:].reshape(8, 1, D_MODEL)
        g = norm_g[l].reshape(1, D_MODEL)
        w_p = _tiled_in_weights(w_in[l])
        w_br = w_branch[l].astype(BF16)
        w_o = w_out[l].astype(BF16)
        cw = conv_w[l]
        sink_b = jnp.broadcast_to(win_sink[l][:, None], (WIN_HEADS, LANES))
        final = l == DEPTH - 1

        p, hp, *new_kv = _inproj(
            xp, shift, scale, g, w_p, rows_per_cond=batch * seq, cond0=0, update=(new_kv, l, seq),
            extras=((T_BKV, 0, KV_W), (T_BKV, KV_W, KV_W), (T_DK, 0, BRANCH_W), (T_DV, 0, BRANCH_W)))
        o_ctx = _ctx_mixers(p, cw, sink_b, *ctx_tabs, seq=seq)
        xp = _out_stage(xp, hp, gate, final_g2, [o_ctx] * 4, [0, 1, 2, 3], w_p, w_br, w_o,
                        rows_per_cond=batch * seq, cond0=0, final=final)

        q, hs, qfu, qfg = _inproj(xs, shift, scale, g, w_p, rows_per_cond=dseq, cond0=1, extra_dtype=BF16,
                              extras=((T_FU, 0, BRANCH_W), (T_FG, 0, BRANCH_W)))
        o_a = _lat_conv(q, cw, batch=dbatch, seq=dseq)
        o_w = _lat_window(q, cos_t, sin_t,
                          cache_win_k[:, l].reshape(dbatch, past, KV_W), cache_win_v[:, l].reshape(dbatch, past, KV_W),
                          sink_b, batch=dbatch, seq=dseq)
        o_f = _lat_fourier(qfu, qfg, lat_tabs, batch=dbatch, seq=dseq)
        o_n = _lat_na(q, na_bias[l],
                      cache_na_k[:, l].reshape(dbatch, past, BRANCH_W), cache_na_v[:, l].reshape(dbatch, past, BRANCH_W),
                      batch=dbatch, seq=dseq)
        xs = _out_stage(xs, hs, gate, final_g2, [o_a, o_w, o_f, o_n], [0, 0, 0, 0], w_p, w_br, w_o,
                        rows_per_cond=dseq, cond0=1, final=final)

    y_prompt = xp.reshape(batch, seq, D_MODEL)
    y_sample = xs.reshape(dbatch, dseq, D_MODEL)
    heads = (WIN_KV_HEADS, WIN_KV_HEADS, NA_HEADS, NA_HEADS)
    return (y_prompt, y_sample) + tuple(a.reshape(batch, DEPTH, seq, n, HEAD_DIM) for a, n in zip(new_kv, heads))
```

```python
import functools
import math

import numpy as np
import jax
import jax.numpy as jnp
from jax import lax
from jax.experimental import pallas as pl
from jax.experimental.pallas import tpu as pltpu

D_MODEL = 2048
DEPTH = 2
GRID_W = 64
N_BRANCH = 4
BRANCH_W = 512
HEAD_DIM = 64
WIN_HEADS = 8
WIN_KV_HEADS = 2
KV_W = WIN_KV_HEADS * HEAD_DIM
WIN_BLOCK = 128
FN_GROUPS = 4
FN_GROUP_W = BRANCH_W // FN_GROUPS
NA_HEADS = 8
NA_ROWS = 8
NA_COLS = 16
ROPE_BASE = 10000.0
EPS = 1e-6
ATTN_SCALE = HEAD_DIM ** -0.5
NEG_INF = -1e30
N_MAIN = 6400

F32 = jnp.float32
BF16 = jnp.bfloat16
LANES = 128

OFF_A = 0
OFF_BQ = OFF_A + 4 * BRANCH_W
OFF_BK = OFF_BQ + BRANCH_W
OFF_BG = OFF_BK + 2 * KV_W
OFF_FU = OFF_BG + BRANCH_W
assert OFF_FU + 6 * BRANCH_W == N_MAIN

TILE = 512
T_AX, T_AB, T_AC, T_AG, T_BQ, T_BG, T_BKV, T_FU, T_FG, T_DQ, T_DK, T_DV, T_DG = range(13)
STEP_TILES = 2
N_TILES = 14
P_WIDTH = N_TILES * TILE

V7X_VMEM_BYTES = 64 * 2 ** 20
PROJ_VMEM_BYTES = V7X_VMEM_BYTES - 2 ** 20

FFT_R = 64


def _silu(x):
    return x * jax.nn.sigmoid(x)


def _dot(a, b):
    return jnp.dot(a, b, preferred_element_type=F32)


def _dot_nt(a, b):
    return lax.dot_general(a, b, (((1,), (1,)), ((), ())), preferred_element_type=F32)


def _rms_mod(x, g, scale, shift):
    ms = jnp.mean(x * x, axis=-1, keepdims=True)
    return (x * lax.rsqrt(ms + EPS) * g) * (1.0 + scale) + shift


def _store_modulated(h_ref, x_ref, g_ref, sc_ref, sh_ref):
    h_ref[...] = _rms_mod(x_ref[...], g_ref[...], sc_ref[0], sh_ref[0]).astype(h_ref.dtype)


def _mod_kernel(cv_ref, w_ref, b_ref, o_ref):
    s = _silu(cv_ref[...]).astype(BF16)
    o_ref[0] = _dot(s, w_ref[0].astype(BF16)) + b_ref[0]


def _modulation(cv8, w_ada, b_ada):
    tn = 512
    n3 = 3 * D_MODEL
    return pl.pallas_call(
        _mod_kernel,
        grid=(DEPTH, n3 // tn),
        in_specs=[
            pl.BlockSpec((8, D_MODEL), lambda l, j: (0, 0)),
            pl.BlockSpec((1, D_MODEL, tn), lambda l, j: (l, 0, j)),
            pl.BlockSpec((1, 1, tn), lambda l, j: (l, 0, j)),
        ],
        out_specs=pl.BlockSpec((1, 8, tn), lambda l, j: (l, 0, j)),
        out_shape=jax.ShapeDtypeStruct((DEPTH, 8, n3), F32),
        name="modulation",
    )(cv8, w_ada, b_ada.reshape(DEPTH, 1, n3))


def _inproj_kernel(x_ref, sh_ref, sc_ref, g_ref, w_ref, *rest, extras, n_updated):
    p_ref, h_ref, extra_refs = rest[n_updated], rest[n_updated + 1], rest[n_updated + 2:]
    j = pl.program_id(1)

    @pl.when(j == 0)
    def _():
        _store_modulated(h_ref, x_ref, g_ref, sc_ref, sh_ref)

    acc = _dot(h_ref[...], w_ref[...])
    p_ref[...] = acc.astype(p_ref.dtype)
    for ref, (tile, lo) in zip(extra_refs, extras):
        @pl.when(j == tile // STEP_TILES)
        def _(ref=ref, lo=(tile % STEP_TILES) * TILE + lo):
            cols = acc[:, lo:lo + ref.shape[-1]].astype(ref.dtype)
            if ref.ndim == 2:
                ref[...] = cols
            else:
                ref[:, 0] = cols.reshape(ref.shape[0], ref.shape[2], ref.shape[3])


def _inproj(x2d, shift, scale, g, w, *, rows_per_cond, cond0, extras, extra_dtype=None, update=None, tm=1024):
    m = x2d.shape[0]
    assert w.shape[0] == D_MODEL and w.shape[1] >= P_WIDTH and m % tm == 0 and rows_per_cond % tm == 0
    tiles_per_cond = rows_per_cond // tm
    cond = lambda i, j: (cond0 + i // tiles_per_cond, 0, 0)
    if update is None:
        updated, aliases = [], {}
        extra_specs = [pl.BlockSpec((tm, width), lambda i, j: (i, 0)) for _, _, width in extras]
        extra_shapes = [jax.ShapeDtypeStruct((m, width), extra_dtype) for _, _, width in extras]
    else:
        updated, layer, seq = update
        assert tm % seq == 0 and all(a.shape[2:] == (seq, width) for a, (_, _, width) in zip(updated, extras))
        extra_specs = [pl.BlockSpec((tm // seq, 1, seq, width), lambda i, j: (i, layer, 0, 0)) for _, _, width in extras]
        extra_shapes = [jax.ShapeDtypeStruct(a.shape, a.dtype) for a in updated]
        aliases = {5 + k: 2 + k for k in range(len(updated))}
    return pl.pallas_call(
        functools.partial(_inproj_kernel, extras=tuple((t, lo) for t, lo, _ in extras), n_updated=len(updated)),
        grid=(m // tm, N_TILES // STEP_TILES),
        in_specs=[
            pl.BlockSpec((tm, D_MODEL), lambda i, j: (i, 0)),
            pl.BlockSpec((1, 1, D_MODEL), cond),
            pl.BlockSpec((1, 1, D_MODEL), cond),
            pl.BlockSpec((1, D_MODEL), lambda i, j: (0, 0)),
            pl.BlockSpec((D_MODEL, STEP_TILES * TILE), lambda i, j: (0, j)),
        ] + [pl.BlockSpec(memory_space=pl.ANY) for _ in updated],
        out_specs=[pl.BlockSpec((tm, STEP_TILES * TILE), lambda i, j: (i, j)),
                   pl.BlockSpec((tm, D_MODEL), lambda i, j: (i, 0))]
        + extra_specs,
        out_shape=[jax.ShapeDtypeStruct((m, P_WIDTH), BF16), jax.ShapeDtypeStruct((m, D_MODEL), BF16)] + extra_shapes,
        input_output_aliases=aliases,
        compiler_params=pltpu.CompilerParams(dimension_semantics=("parallel", "arbitrary"),
                                             vmem_limit_bytes=PROJ_VMEM_BYTES),
        name="inproj",
    )(x2d, shift, scale, g, w, *updated)


def _out_kernel(x_ref, h_ref, gt_ref, fg_ref, oa_ref, ow_ref, of_ref, on_ref,
                wg0_ref, wg1_ref, wg2_ref, wg3_ref, wb_ref, wo_ref, y_ref, merged_ref, *, final):
    j = pl.program_id(1)
    h = h_ref[...]
    merged = None
    for i, (o_ref, wg_ref) in enumerate(((oa_ref, wg0_ref), (ow_ref, wg1_ref), (of_ref, wg2_ref), (on_ref, wg3_ref))):
        term = jax.nn.sigmoid(_dot(h, wg_ref[...])) * _dot(o_ref[...], wb_ref[i])
        merged = term if merged is None else merged + term
    merged_ref[j] = merged.astype(BF16)

    @pl.when(j == pl.num_programs(1) - 1)
    def _():
        m_all = jnp.concatenate([merged_ref[k] for k in range(merged_ref.shape[0])], axis=1)
        xn = x_ref[...] + gt_ref[0] * _dot(m_all, wo_ref[...])
        if final:
            ms = jnp.mean(xn * xn, axis=-1, keepdims=True)
            xn = xn * lax.rsqrt(ms + EPS) * fg_ref[...]
        y_ref[...] = xn


def _out_stage(x2d, h2d, gate, final_g, o_arrays, o_cols, w_gate, w_branch, w_out,
               *, rows_per_cond, cond0, final, tm=512, tc=512):
    m = x2d.shape[0]
    assert m % tm == 0 and rows_per_cond % tm == 0
    tiles_per_cond = rows_per_cond // tm
    ncol = D_MODEL // tc
    cond = lambda i, j: (cond0 + i // tiles_per_cond, 0, 0)
    o_specs = [pl.BlockSpec((tm, BRANCH_W), functools.partial(lambda i, j, c: (i, c), c=c)) for c in o_cols]
    assert tc == TILE and w_gate.shape == (D_MODEL, P_WIDTH + N_BRANCH * D_MODEL)
    wg_specs = [pl.BlockSpec((D_MODEL, tc), functools.partial(lambda i, j, b: (0, N_TILES + b * ncol + j), b=b))
                for b in range(N_BRANCH)]
    return pl.pallas_call(
        functools.partial(_out_kernel, final=final),
        grid=(m // tm, ncol),
        in_specs=[
            pl.BlockSpec((tm, D_MODEL), lambda i, j: (i, 0)),
            pl.BlockSpec((tm, D_MODEL), lambda i, j: (i, 0)),
            pl.BlockSpec((1, 1, D_MODEL), cond),
            pl.BlockSpec((1, D_MODEL), lambda i, j: (0, 0)),
            *o_specs,
            *wg_specs,
            pl.BlockSpec((N_BRANCH, BRANCH_W, tc), lambda i, j: (0, 0, j)),
            pl.BlockSpec((D_MODEL, D_MODEL), lambda i, j: (0, 0), pipeline_mode=pl.Buffered(1)),
        ],
        out_specs=pl.BlockSpec((tm, D_MODEL), lambda i, j: (i, 0)),
        out_shape=jax.ShapeDtypeStruct((m, D_MODEL), F32),
        scratch_shapes=[pltpu.VMEM((ncol, tm, tc), BF16)],
        compiler_params=pltpu.CompilerParams(dimension_semantics=("parallel", "arbitrary"),
                                             vmem_limit_bytes=PROJ_VMEM_BYTES),
        name="out_stage",
    )(x2d, h2d, gate, final_g, *o_arrays, w_gate, w_gate, w_gate, w_gate, w_branch, w_out)


def _attend(scores, values, sinks=None):
    heads = range(len(scores))
    sinks = [None] * len(scores) if sinks is None else sinks
    m = []
    for h in heads:
        mh = scores[h][0].max(axis=-1, keepdims=True)
        for s in scores[h][1:]:
            mh = jnp.maximum(mh, s.max(axis=-1, keepdims=True))
        m.append(mh if sinks[h] is None else jnp.maximum(mh, sinks[h]))
    e = [[jnp.exp(s - m[h]) for s in scores[h]] for h in heads]
    den = []
    for h in heads:
        d = e[h][0].sum(axis=-1, keepdims=True)
        for x in e[h][1:]:
            d = d + x.sum(axis=-1, keepdims=True)
        den.append(d if sinks[h] is None else d + jnp.exp(sinks[h] - m[h]))
    out = []
    for h in heads:
        acc = _dot(e[h][0].astype(BF16), values[h][0])
        for x, v in zip(e[h][1:], values[h][1:]):
            acc = acc + _dot(x.astype(BF16), v)
        out.append(acc)
    return [a / d for a, d in zip(out, den)]


def _shift_rows(z, first_row, last_row):
    n = z.shape[0]
    row = lax.broadcasted_iota(jnp.int32, z.shape, 0)
    z_dn = jnp.where(row == 0, first_row, pltpu.roll(z, 1, axis=0))
    z_up = jnp.where(row == n - 1, last_row, pltpu.roll(z, n - 1, axis=0))
    return z_dn, z_up


def _tile(ref, t, dtype=None):
    v = ref[:, t * TILE:(t + 1) * TILE]
    return v if dtype is None else v.astype(dtype)


def _head(ref, t, h, base=0):
    lo = t * TILE + base + h * HEAD_DIM
    return ref[:, lo:lo + HEAD_DIM]


def _scaled_queries(q):
    assert math.frexp(ATTN_SCALE)[0] == 0.5
    return q * ATTN_SCALE


def _ctx_mixer_kernel(p_ref, cw_ref, sink_ref, ct_ref, st_ref, cb_ref, sb_ref, o_ref):
    z = _tile(p_ref, T_AC, F32) * _tile(p_ref, T_AX, F32)
    zero_row = jnp.zeros((1, BRANCH_W), F32)
    z_dn, z_up = _shift_rows(z, zero_row, zero_row)
    y = _tile(p_ref, T_AB, F32) * (z_dn * cw_ref[0:1, :] + z * cw_ref[1:2, :] + z_up * cw_ref[2:3, :])
    out = lambda b: slice(b * BRANCH_W, (b + 1) * BRANCH_W)
    o_ref[:, out(0)] = (y * _silu(_tile(p_ref, T_AG, F32))).astype(o_ref.dtype)

    gsz = WIN_HEADS // WIN_KV_HEADS
    head = lambda a, h: a[:, h * HEAD_DIM:(h + 1) * HEAD_DIM]
    qb = _scaled_queries(_tile(p_ref, T_BQ))
    qd = _scaled_queries(_tile(p_ref, T_DQ))
    scores = ([[_dot_nt(head(qb, h), _head(p_ref, T_BKV, h // gsz))] for h in range(WIN_HEADS)]
              + [[_dot_nt(head(qd, h), _head(p_ref, T_DK, h))] for h in range(NA_HEADS)])
    values = ([[_head(p_ref, T_BKV, h // gsz, base=KV_W)] for h in range(WIN_HEADS)]
              + [[_head(p_ref, T_DV, h)] for h in range(NA_HEADS)])
    sinks = [sink_ref[h:h + 1, 0:1] for h in range(WIN_HEADS)] + [None] * NA_HEADS
    heads = _attend(scores, values, sinks=sinks)
    o_w = jnp.concatenate(heads[:WIN_HEADS], axis=-1) * _silu(_tile(p_ref, T_BG, F32))
    o_ref[:, out(1)] = o_w.astype(o_ref.dtype)
    o_n = jnp.concatenate(heads[WIN_HEADS:], axis=-1) * _silu(_tile(p_ref, T_DG, F32))
    o_ref[:, out(3)] = o_n.astype(o_ref.dtype)

    u = _tile(p_ref, T_FU)
    uc = _dot(u, cb_ref[...]).astype(BF16)
    us = _dot(u, sb_ref[...]).astype(BF16)
    o_f = (_dot(ct_ref[...], uc) - _dot(st_ref[...], us)) * _silu(_tile(p_ref, T_FG, F32))
    o_ref[:, out(2)] = o_f.astype(o_ref.dtype)


def _ctx_mixers(p, conv_w, sink_b, ct, st, cb, sb, *, seq):
    m = p.shape[0]
    whole = lambda a: pl.BlockSpec(a.shape, lambda b: (0,) * a.ndim)
    return pl.pallas_call(
        _ctx_mixer_kernel,
        grid=(m // seq,),
        in_specs=[pl.BlockSpec((seq, P_WIDTH), lambda b: (b, 0)), whole(conv_w), whole(sink_b),
                  whole(ct), whole(st), whole(cb), whole(sb)],
        out_specs=pl.BlockSpec((seq, N_BRANCH * BRANCH_W), lambda b: (b, 0)),
        out_shape=jax.ShapeDtypeStruct((m, N_BRANCH * BRANCH_W), BF16),
        compiler_params=pltpu.CompilerParams(dimension_semantics=("parallel",)),
        name="ctx_mixers",
    )(p, conv_w, sink_b, ct, st, cb, sb)


HALO = 16


def _lat_conv_kernel(ax_ref, ab_ref, ac_ref, ag_ref, axp_ref, acp_ref, axn_ref, acn_ref, cw_ref, o_ref):
    i = pl.program_id(1)
    z = ac_ref[...].astype(F32) * ax_ref[...].astype(F32)
    zp = acp_ref[HALO - 1:HALO, :].astype(F32) * axp_ref[HALO - 1:HALO, :].astype(F32)
    zn = acn_ref[0:1, :].astype(F32) * axn_ref[0:1, :].astype(F32)
    zp = jnp.where(i == 0, 0.0, zp)
    zn = jnp.where(i == pl.num_programs(1) - 1, 0.0, zn)
    z_dn, z_up = _shift_rows(z, zp, zn)
    y = ab_ref[...].astype(F32) * (z_dn * cw_ref[0:1, :] + z * cw_ref[1:2, :] + z_up * cw_ref[2:3, :])
    o_ref[...] = (y * _silu(ag_ref[...].astype(F32))).astype(o_ref.dtype)


def _lat_conv(pa, conv_w, *, batch, seq, tr=512):
    nt = seq // tr
    hb = tr // HALO
    last_halo = batch * seq // HALO - 1
    main = lambda c: pl.BlockSpec((tr, BRANCH_W), functools.partial(lambda b, i, c: (b * nt + i, c), c=c))
    prev = lambda c: pl.BlockSpec(
        (HALO, BRANCH_W), functools.partial(lambda b, i, c: (jnp.maximum((b * nt + i) * hb - 1, 0), c), c=c))
    nxt = lambda c: pl.BlockSpec(
        (HALO, BRANCH_W), functools.partial(lambda b, i, c: (jnp.minimum((b * nt + i + 1) * hb, last_halo), c), c=c))
    return pl.pallas_call(
        _lat_conv_kernel,
        grid=(batch, nt),
        in_specs=[main(T_AX), main(T_AB), main(T_AC), main(T_AG), prev(T_AX), prev(T_AC), nxt(T_AX), nxt(T_AC),
                  pl.BlockSpec(conv_w.shape, lambda b, i: (0, 0))],
        out_specs=pl.BlockSpec((tr, BRANCH_W), lambda b, i: (b * nt + i, 0)),
        out_shape=jax.ShapeDtypeStruct((batch * seq, BRANCH_W), BF16),
        compiler_params=pltpu.CompilerParams(dimension_semantics=("parallel", "arbitrary")),
        name="lat_conv",
    )(pa, pa, pa, pa, pa, pa, pa, pa, conv_w)


def _rope(x, cos, sin_signed):
    lane = lax.broadcasted_iota(jnp.int32, x.shape, 1)
    quarter = HEAD_DIM // 4
    partner = jnp.where((lane % (2 * quarter)) < quarter,
                        pltpu.roll(x, LANES - quarter, axis=1), pltpu.roll(x, quarter, axis=1))
    return x * cos + partner * sin_signed


WIN_BLOCKS_PER_STEP = 8


def _lat_win_kernel(q_ref, g_ref, kp_ref, kc_ref, kn_ref, vp_ref, vc_ref, vn_ref,
                    cq_ref, sq_ref, cp_ref, sp_ref, cn_ref, sn_ref, ck_ref, cv_ref, sink_ref, o_ref):
    nq = WIN_BLOCKS_PER_STEP
    nb = pl.num_programs(1) * nq
    wb = WIN_BLOCK
    cq, sq = cq_ref[...], sq_ref[...]
    k_rot = jnp.concatenate([
        _rope(kp_ref[...].astype(F32), cp_ref[...], sp_ref[...]),
        _rope(kc_ref[...].astype(F32), cq, sq),
        _rope(kn_ref[...].astype(F32), cn_ref[...], sn_ref[...]),
    ], axis=0).astype(BF16)
    v_all = jnp.concatenate([vp_ref[...], vc_ref[...], vn_ref[...]], axis=0).astype(BF16)
    qi = lax.broadcasted_iota(jnp.int32, (wb, 3 * wb), 0)
    kj = lax.broadcasted_iota(jnp.int32, (wb, 3 * wb), 1)
    ck = ck_ref[0].astype(BF16)
    cv = cv_ref[0].astype(BF16)
    gsz = WIN_HEADS // WIN_KV_HEADS
    q_rot = [_scaled_queries(_rope(q_ref[:, pair * LANES:(pair + 1) * LANES].astype(F32), cq, sq)).astype(BF16)
             for pair in range(WIN_HEADS // 2)]
    scores, values, sinks = [], [], []
    for d in range(nq):
        n = pl.program_id(1) * nq + d
        mask = (((kj < wb) & (kj >= qi) & (n > 0)) | ((kj >= wb) & (kj < 2 * wb))
                | ((kj >= 2 * wb) & (kj - 2 * wb <= qi) & (n < nb - 1)))
        rows = slice(d * wb, (d + 1) * wb)
        win = slice(d * wb, (d + 3) * wb)
        for h in range(WIN_HEADS):
            kv = slice((h // gsz) * HEAD_DIM, (h // gsz + 1) * HEAD_DIM)
            q = q_rot[h // 2][rows, (h % 2) * HEAD_DIM:(h % 2 + 1) * HEAD_DIM]
            s_loc = jnp.where(mask, _dot_nt(q, k_rot[win, kv]), NEG_INF)
            scores.append([s_loc, _dot_nt(q, ck[:, kv])])
            values.append([v_all[win, kv], cv[:, kv]])
            sinks.append(sink_ref[h:h + 1, 0:1])
    heads = _attend(scores, values, sinks=sinks)
    o = jnp.concatenate([jnp.concatenate(heads[d * WIN_HEADS:(d + 1) * WIN_HEADS], axis=-1) for d in range(nq)], axis=0)
    o_ref[...] = (o * _silu(g_ref[...].astype(F32))).astype(o_ref.dtype)


def _lat_window(pb, cos_t, sin_t, ctx_k, ctx_v, sink_b, *, batch, seq):
    wb = WIN_BLOCK
    k_col = T_BKV * (TILE // KV_W)
    v_col = k_col + 1
    nq = WIN_BLOCKS_PER_STEP
    nb = seq // wb
    steps = nb // nq
    nbr = lambda s, d: jnp.clip(s * nq + (nq if d > 0 else -1), 0, nb - 1)
    own = lambda col, width: pl.BlockSpec(
        (nq * wb, width), functools.partial(lambda b, s, col: (b * steps + s, col), col=col))
    kv = lambda col, d: pl.BlockSpec(
        (wb, KV_W), functools.partial(lambda b, s, col, d: (b * nb + nbr(s, d), col), col=col, d=d))
    tab = lambda d: pl.BlockSpec((wb, LANES), functools.partial(lambda b, s, d: (nbr(s, d), 0), d=d))
    tab_own = pl.BlockSpec((nq * wb, LANES), lambda b, s: (s, 0))
    ctx = pl.BlockSpec((1,) + ctx_k.shape[1:], lambda b, s: (b, 0, 0))
    return pl.pallas_call(
        _lat_win_kernel,
        grid=(batch, steps),
        in_specs=[
            own(T_BQ, BRANCH_W), own(T_BG, BRANCH_W),
            kv(k_col, -1), own(k_col, KV_W), kv(k_col, 1), kv(v_col, -1), own(v_col, KV_W), kv(v_col, 1),
            tab_own, tab_own, tab(-1), tab(-1), tab(1), tab(1),
            ctx, ctx,
            pl.BlockSpec(sink_b.shape, lambda b, s: (0, 0)),
        ],
        out_specs=pl.BlockSpec((nq * wb, BRANCH_W), lambda b, s: (b * steps + s, 0)),
        out_shape=jax.ShapeDtypeStruct((batch * seq, BRANCH_W), BF16),
        compiler_params=pltpu.CompilerParams(dimension_semantics=("parallel", "arbitrary")),
        name="lat_window",
    )(pb, pb, pb, pb, pb, pb, pb, pb, cos_t, sin_t, cos_t, sin_t, cos_t, sin_t, ctx_k, ctx_v, sink_b)


def _fft1_kernel(u_ref, c_ref, s_ref, twr_ref, twi_ref, zr_ref, zi_ref, *, chunks):
    u = u_ref[0].astype(BF16)
    yr = _dot(c_ref[...], u)
    yi = -_dot(s_ref[...], u)
    for t in range(chunks):
        wr = twr_ref[0, :, t:t + 1]
        wi = twi_ref[0, :, t:t + 1]
        a = yr[:, t * BRANCH_W:(t + 1) * BRANCH_W]
        b = yi[:, t * BRANCH_W:(t + 1) * BRANCH_W]
        zr_ref[0, :, t * BRANCH_W:(t + 1) * BRANCH_W] = (a * wr - b * wi).astype(zr_ref.dtype)
        zi_ref[0, :, t * BRANCH_W:(t + 1) * BRANCH_W] = (a * wi + b * wr).astype(zi_ref.dtype)


def _fft2_kernel(zr_ref, zi_ref, g_ref, l_ref, cs_ref, o_ref, *, kblock):
    r = FFT_R
    zz = [jnp.concatenate([zr_ref[0, kk], zi_ref[0, kk]], axis=0) for kk in range(kblock)]
    xx = [_dot(l_ref[...], z) for z in zz]
    xcat = [jnp.concatenate([x[0:r], x[r:2 * r]], axis=1).astype(BF16) for x in xx]
    out = [_dot(x, cs_ref[...]) for x in xcat]
    for kk in range(kblock):
        sl = slice(kk * BRANCH_W, (kk + 1) * BRANCH_W)
        o_ref[0, :, sl] = (out[kk] * _silu(g_ref[0, :, sl].astype(F32))).astype(o_ref.dtype)


def _lat_fourier(fu, fg, tabs, *, batch, seq):
    r = FFT_R
    assert seq == r * r
    wide = r * BRANCH_W
    chunks = 8
    c64, s64, twr, twi, lmat, csmat = tabs
    u3 = fu.reshape(batch, r, wide)
    nct = r // chunks
    zr, zi = pl.pallas_call(
        functools.partial(_fft1_kernel, chunks=chunks),
        grid=(batch, nct),
        in_specs=[
            pl.BlockSpec((1, r, chunks * BRANCH_W), lambda b, t: (b, 0, t)),
            pl.BlockSpec((r, r), lambda b, t: (0, 0)),
            pl.BlockSpec((r, r), lambda b, t: (0, 0)),
            pl.BlockSpec((1, r, chunks), lambda b, t: (t, 0, 0)),
            pl.BlockSpec((1, r, chunks), lambda b, t: (t, 0, 0)),
        ],
        out_specs=[pl.BlockSpec((1, r, chunks * BRANCH_W), lambda b, t: (b, 0, t))] * 2,
        out_shape=[jax.ShapeDtypeStruct((batch, r, wide), BF16)] * 2,
        compiler_params=pltpu.CompilerParams(dimension_semantics=("parallel", "arbitrary")),
        name="lat_fft1",
    )(u3, c64, s64, twr, twi)
    kblock = 8
    z4 = lambda z: z.reshape(batch, r, r, BRANCH_W)
    out = pl.pallas_call(
        functools.partial(_fft2_kernel, kblock=kblock),
        grid=(batch, r // kblock),
        in_specs=[
            pl.BlockSpec((1, kblock, r, BRANCH_W), lambda b, k: (b, k, 0, 0)),
            pl.BlockSpec((1, kblock, r, BRANCH_W), lambda b, k: (b, k, 0, 0)),
            pl.BlockSpec((1, r, kblock * BRANCH_W), lambda b, k: (b, 0, k)),
            pl.BlockSpec((2 * r, 2 * r), lambda b, k: (0, 0)),
            pl.BlockSpec((2 * BRANCH_W, BRANCH_W), lambda b, k: (0, 0)),
        ],
        out_specs=pl.BlockSpec((1, r, kblock * BRANCH_W), lambda b, k: (b, 0, k)),
        out_shape=jax.ShapeDtypeStruct((batch, r, wide), BF16),
        compiler_params=pltpu.CompilerParams(dimension_semantics=("parallel", "arbitrary")),
        name="lat_fft2",
    )(z4(zr), z4(zi), fg.reshape(batch, r, wide), lmat, csmat)
    return out.reshape(batch * seq, BRANCH_W)


N_RPB_ROWS = 2 * NA_ROWS - 1
N_RPB_COLS = 2 * NA_COLS - 1


NA_ROWS_PER_STEP = 8


def _lat_na_kernel(q_ref, g_ref, k_ref, v_ref, bias_ref, ck_ref, cv_ref, o_ref, *, rows):
    ck = ck_ref[0].astype(BF16)
    cv = cv_ref[0].astype(BF16)
    sls = [slice(h * HEAD_DIM, (h + 1) * HEAD_DIM) for h in range(NA_HEADS)]
    scores, values = [], []
    for d in range(NA_ROWS_PER_STEP):
        r = pl.program_id(1) * NA_ROWS_PER_STEP + d
        r0 = jnp.clip(r - NA_ROWS // 2, 0, rows - NA_ROWS)
        start = pl.multiple_of(r0 * GRID_W, GRID_W)
        kw = k_ref[pl.ds(start, NA_ROWS * GRID_W), :]
        vw = v_ref[pl.ds(start, NA_ROWS * GRID_W), :]
        qa = _scaled_queries(q_ref[d * GRID_W:(d + 1) * GRID_W, :])
        a0 = r0 - r + NA_ROWS - 1
        for h, sl in enumerate(sls):
            bias = jnp.concatenate([bias_ref[h, a0 + 2 * i2] for i2 in range(NA_ROWS // 2)], axis=1)
            scores.append([_dot_nt(qa[:, sl], kw[:, sl]) + bias, _dot_nt(qa[:, sl], ck[:, sl])])
            values.append([vw[:, sl], cv[:, sl]])
    heads = _attend(scores, values)
    o = jnp.concatenate([jnp.concatenate(heads[d * NA_HEADS:(d + 1) * NA_HEADS], axis=-1)
                         for d in range(NA_ROWS_PER_STEP)], axis=0)
    o_ref[...] = (o * _silu(g_ref[...].astype(F32))).astype(o_ref.dtype)


def _lat_na(p, bias, ctx_k, ctx_v, *, batch, seq):
    rows = seq // GRID_W
    steps = rows // NA_ROWS_PER_STEP
    qrows = NA_ROWS_PER_STEP * GRID_W
    ctx = pl.BlockSpec((1,) + ctx_k.shape[1:], lambda b, r: (b, 0, 0))
    return pl.pallas_call(
        functools.partial(_lat_na_kernel, rows=rows),
        grid=(batch, steps),
        in_specs=[
            pl.BlockSpec((qrows, BRANCH_W), lambda b, r: (b * steps + r, T_DQ)),
            pl.BlockSpec((qrows, BRANCH_W), lambda b, r: (b * steps + r, T_DG)),
            pl.BlockSpec((seq, BRANCH_W), lambda b, r: (b, T_DK)),
            pl.BlockSpec((seq, BRANCH_W), lambda b, r: (b, T_DV)),
            pl.BlockSpec(bias.shape, lambda b, r: (0, 0, 0, 0)),
            ctx, ctx,
        ],
        out_specs=pl.BlockSpec((qrows, BRANCH_W), lambda b, r: (b * steps + r, 0)),
        out_shape=jax.ShapeDtypeStruct((batch * seq, BRANCH_W), BF16),
        compiler_params=pltpu.CompilerParams(dimension_semantics=("parallel", "arbitrary")),
        name="lat_na",
    )(p, p, p, p, bias, ctx_k, ctx_v)


def _toeplitz_kernel(r_ref, o_ref):
    x = r_ref[...]
    hi = x.astype(BF16)
    r1 = x - hi.astype(F32)
    mid = r1.astype(BF16)
    lo = (r1 - mid.astype(F32)).astype(BF16)
    shape = (x.shape[1], GRID_W * GRID_W)
    b = lax.broadcasted_iota(jnp.int32, shape, 0)
    col = lax.broadcasted_iota(jnp.int32, shape, 1)
    kc = jnp.bitwise_and(col, GRID_W - 1)
    qc = lax.shift_right_logical(col, GRID_W.bit_length() - 1)
    onehot = jnp.where(kc - qc + (NA_COLS - 1) == b, 1.0, 0.0).astype(BF16)
    o_ref[...] = _dot(hi, onehot) + _dot(mid, onehot) + _dot(lo, onehot)


def _na_bias_tables(na_rpb):
    depth = na_rpb.shape[0]
    n = depth * NA_HEADS * N_RPB_ROWS
    n_pad = -(-n // 8) * 8
    r2 = jnp.zeros((n_pad, LANES), F32).at[:n, :N_RPB_COLS].set(na_rpb.reshape(n, N_RPB_COLS))
    flat = pl.pallas_call(
        _toeplitz_kernel,
        out_shape=jax.ShapeDtypeStruct((n_pad, GRID_W * GRID_W), F32),
        name="na_bias_toeplitz",
    )(r2)
    t = flat[:n].reshape(depth, NA_HEADS, N_RPB_ROWS, GRID_W, GRID_W)
    qc = np.arange(GRID_W)[:, None]
    kc = np.arange(GRID_W)[None, :]
    win0 = np.clip(qc - NA_COLS // 2, 0, GRID_W - NA_COLS)
    vis = (kc >= win0) & (kc < win0 + NA_COLS)
    t = jnp.where(vis, t, NEG_INF)
    return jnp.concatenate([t[:, :, :-1], t[:, :, 1:]], axis=-1)


def _bf16_table(t):
    return jnp.asarray(t, F32).astype(BF16)


def _dft_cos_sin(n):
    idx = np.arange(n)
    ang = 2.0 * np.pi * ((idx[:, None] * idx[None, :]) % n) / n
    return np.cos(ang), np.sin(ang)


def _channel_dft_blocks(scale):
    c, s = _dft_cos_sin(FN_GROUP_W)
    eye = np.eye(FN_GROUPS)
    return np.kron(eye, c) * scale, np.kron(eye, s) * scale


def _ctx_fourier_tables(seq):
    ct, st = _dft_cos_sin(seq)
    cb, sb = _channel_dft_blocks(1.0 / np.sqrt(seq * FN_GROUP_W))
    return tuple(_bf16_table(t) for t in (ct, st, cb, sb))


def _lat_fourier_tables(seq, chunks=8):
    r = FFT_R
    c64, s64 = _dft_cos_sin(r)
    k1 = np.arange(r)[:, None]
    t2 = np.arange(r)[None, :]
    ang = 2.0 * np.pi * (k1 * t2) / (r * r)
    twr = np.cos(ang).reshape(r, r // chunks, chunks).transpose(1, 0, 2)
    twi = (-np.sin(ang)).reshape(r, r // chunks, chunks).transpose(1, 0, 2)
    lmat = np.block([[c64, s64], [-s64, c64]])
    cb, sb = _channel_dft_blocks(1.0 / np.sqrt(seq * FN_GROUP_W))
    csmat = np.concatenate([cb, sb], axis=0)
    return (_bf16_table(c64), _bf16_table(s64), jnp.asarray(twr, F32), jnp.asarray(twi, F32),
            _bf16_table(lmat), _bf16_table(csmat))


def _rope_tables(seq):
    half = HEAD_DIM // 2
    quarter = half // 2
    t = jnp.arange(seq)
    inv = ROPE_BASE ** (-jnp.arange(quarter, dtype=F32) / quarter)

    def cs(pos):
        ang = pos.astype(F32)[:, None] * inv[None, :]
        c, s = jnp.cos(ang), jnp.sin(ang)
        return jnp.concatenate([c, c], axis=-1), jnp.concatenate([-s, s], axis=-1)

    cr, sr = cs(t // GRID_W)
    cc, sc = cs(t % GRID_W)
    cos = jnp.concatenate([cr, cc], axis=-1)
    sin = jnp.concatenate([sr, sc], axis=-1)
    return jnp.tile(cos, (1, 2)), jnp.tile(sin, (1, 2))


def _tiled_in_weights(wl):
    kv_pad = jnp.zeros((D_MODEL, TILE - 2 * KV_W), F32)
    zero_tile = jnp.zeros((D_MODEL, TILE), F32)
    cols = [wl[:, OFF_A:OFF_BQ], wl[:, OFF_BQ:OFF_BK], wl[:, OFF_BG:OFF_FU], wl[:, OFF_BK:OFF_BG], kv_pad,
            wl[:, OFF_FU:N_MAIN], zero_tile, wl[:, N_MAIN:]]
    return jnp.concatenate(cols, axis=1).astype(BF16)


def kernel(x_prompt, x_sample, cache_win_k, cache_win_v, cache_na_k, cache_na_v, c, c_ctx, norm_g, w_ada, b_ada,
           w_in, conv_w, win_sink, na_rpb, w_branch, w_out, final_g):
    batch, seq, _ = x_prompt.shape
    dbatch, dseq, _ = x_sample.shape
    past = cache_win_k.shape[2]

    cv8 = jnp.zeros((8, D_MODEL), F32).at[0].set(c_ctx).at[1:1 + dbatch].set(c)
    mod = _modulation(cv8, w_ada, b_ada)

    ctx_tabs = _ctx_fourier_tables(seq)
    lat_tabs = _lat_fourier_tables(dseq)
    cos_t, sin_t = _rope_tables(dseq)
    na_bias = _na_bias_tables(na_rpb)
    final_g2 = final_g.reshape(1, D_MODEL)

    xp = x_prompt.reshape(batch * seq, D_MODEL)
    xs = x_sample.reshape(dbatch * dseq, D_MODEL)
    new_kv = [jnp.zeros((batch, DEPTH, seq, width), F32) for width in (KV_W, KV_W, BRANCH_W, BRANCH_W)]

    for l in range(DEPTH):
        shift = mod[l, :, 0:D_MODEL].reshape(8, 1, D_MODEL)
        scale = mod[l, :, D_MODEL:2 * D_MODEL].reshape(8, 1, D_MODEL)
        gate = mod[l, :, 2 * D_MODEL:].reshape(8, 1, D_MODEL)
        g = norm_g[l].reshape(1, D_MODEL)
        w_p = _tiled_in_weights(w_in[l])
        w_br = w_branch[l].astype(BF16)
        w_o = w_out[l].astype(BF16)
        cw = conv_w[l]
        sink_b = jnp.broadcast_to(win_sink[l][:, None], (WIN_HEADS, LANES))
        final = l == DEPTH - 1

        p, hp, *new_kv = _inproj(
            xp, shift, scale, g, w_p, rows_per_cond=batch * seq, cond0=0, update=(new_kv, l, seq),
            extras=((T_BKV, 0, KV_W), (T_BKV, KV_W, KV_W), (T_DK, 0, BRANCH_W), (T_DV, 0, BRANCH_W)))
        o_ctx = _ctx_mixers(p, cw, sink_b, *ctx_tabs, seq=seq)
        xp = _out_stage(xp, hp, gate, final_g2, [o_ctx] * 4, [0, 1, 2, 3], w_p, w_br, w_o,
                        rows_per_cond=batch * seq, cond0=0, final=final)

        q, hs, qfu, qfg = _inproj(xs, shift, scale, g, w_p, rows_per_cond=dseq, cond0=1, extra_dtype=BF16,
                              extras=((T_FU, 0, BRANCH_W), (T_FG, 0, BRANCH_W)))
        o_a = _lat_conv(q, cw, batch=dbatch, seq=dseq)
        o_w = _lat_window(q, cos_t, sin_t,
                          cache_win_k[:, l].reshape(dbatch, past, KV_W), cache_win_v[:, l].reshape(dbatch, past, KV_W),
                          sink_b, batch=dbatch, seq=dseq)
        o_f = _lat_fourier(qfu, qfg, lat_tabs, batch=dbatch, seq=dseq)
        o_n = _lat_na(q, na_bias[l],
                      cache_na_k[:, l].reshape(dbatch, past, BRANCH_W), cache_na_v[:, l].reshape(dbatch, past, BRANCH_W),
                      batch=dbatch, seq=dseq)
        xs = _out_stage(xs, hs, gate, final_g2, [o_a, o_w, o_f, o_n], [0, 0, 0, 0], w_p, w_br, w_o,
                        rows_per_cond=dseq, cond0=1, final=final)

    y_prompt = xp.reshape(batch, seq, D_MODEL)
    y_sample = xs.reshape(dbatch, dseq, D_MODEL)
    heads = (WIN_KV_HEADS, WIN_KV_HEADS, NA_HEADS, NA_HEADS)
    return (y_prompt, y_sample) + tuple(a.reshape(batch, DEPTH, seq, n, HEAD_DIM) for a, n in zip(new_kv, heads))
```

```python
import functools
import math

import numpy as np
import jax
import jax.numpy as jnp
from jax import lax
from jax.experimental import pallas as pl
from jax.experimental.pallas import tpu as pltpu

D_MODEL = 2048
DEPTH = 2
GRID_W = 64
N_BRANCH = 4
BRANCH_W = 512
HEAD_DIM = 64
WIN_HEADS = 8
WIN_KV_HEADS = 2
KV_W = WIN_KV_HEADS * HEAD_DIM
WIN_BLOCK = 128
FN_GROUPS = 4
FN_GROUP_W = BRANCH_W // FN_GROUPS
NA_HEADS = 8
NA_ROWS = 8
NA_COLS = 16
ROPE_BASE = 10000.0
EPS = 1e-6
ATTN_SCALE = HEAD_DIM ** -0.5
NEG_INF = -1e30
N_MAIN = 6400

F32 = jnp.float32
BF16 = jnp.bfloat16
LANES = 128

OFF_A = 0
OFF_BQ = OFF_A + 4 * BRANCH_W
OFF_BK = OFF_BQ + BRANCH_W
OFF_BG = OFF_BK + 2 * KV_W
OFF_FU = OFF_BG + BRANCH_W
assert OFF_FU + 6 * BRANCH_W == N_MAIN

TILE = 512
T_AX, T_AB, T_AC, T_AG, T_BQ, T_BG, T_BKV, T_FU, T_FG, T_DQ, T_DK, T_DV, T_DG = range(13)
STEP_TILES = 2
N_TILES = 14
P_WIDTH = N_TILES * TILE

V7X_VMEM_BYTES = 64 * 2 ** 20
PROJ_VMEM_BYTES = V7X_VMEM_BYTES - 2 ** 20

FFT_R = 64


def _silu(x):
    return x * jax.nn.sigmoid(x)


def _dot(a, b):
    return jnp.dot(a, b, preferred_element_type=F32)


def _dot_nt(a, b):
    return lax.dot_general(a, b, (((1,), (1,)), ((), ())), preferred_element_type=F32)


def _rms_mod(x, g, scale, shift):
    ms = jnp.mean(x * x, axis=-1, keepdims=True)
    return (x * lax.rsqrt(ms + EPS) * g) * (1.0 + scale) + shift


NORM_ROWS = 256


def _mod_kernel(cv_ref, w_ref, b_ref, o_ref):
    s = _silu(cv_ref[...]).astype(BF16)
    o_ref[0] = _dot(s, w_ref[0].astype(BF16)) + b_ref[0]


def _modulation(cv8, w_ada, b_ada):
    tn = 512
    n3 = 3 * D_MODEL
    return pl.pallas_call(
        _mod_kernel,
        grid=(DEPTH, n3 // tn),
        in_specs=[
            pl.BlockSpec((8, D_MODEL), lambda l, j: (0, 0)),
            pl.BlockSpec((1, D_MODEL, tn), lambda l, j: (l, 0, j)),
            pl.BlockSpec((1, 1, tn), lambda l, j: (l, 0, j)),
        ],
        out_specs=pl.BlockSpec((1, 8, tn), lambda l, j: (l, 0, j)),
        out_shape=jax.ShapeDtypeStruct((DEPTH, 8, n3), F32),
        name="modulation",
    )(cv8, w_ada, b_ada.reshape(DEPTH, 1, n3))


def _inproj_kernel(x_ref, sh_ref, sc_ref, g_ref, w_ref, *rest, extras, n_updated):
    p_ref, h_ref, extra_refs = rest[n_updated], rest[n_updated + 1], rest[n_updated + 2:]
    j = pl.program_id(1)
    assert all(tile >= STEP_TILES for tile, _ in extras)

    @pl.when(j == 0)
    def _():
        g, scale, shift = g_ref[...], sc_ref[0], sh_ref[0]
        for c in range(h_ref.shape[0] // NORM_ROWS):
            rows = slice(c * NORM_ROWS, (c + 1) * NORM_ROWS)
            h = _rms_mod(x_ref[rows, :], g, scale, shift).astype(h_ref.dtype)
            h_ref[rows, :] = h
            p_ref[rows, :] = _dot(h, w_ref[...]).astype(p_ref.dtype)

    @pl.when(j > 0)
    def _():
        acc = _dot(h_ref[...], w_ref[...])
        p_ref[...] = acc.astype(p_ref.dtype)
        for ref, (tile, lo) in zip(extra_refs, extras):
            @pl.when(j == tile // STEP_TILES)
            def _(ref=ref, lo=(tile % STEP_TILES) * TILE + lo):
                cols = acc[:, lo:lo + ref.shape[-1]].astype(ref.dtype)
                if ref.ndim == 2:
                    ref[...] = cols
                else:
                    ref[:, 0] = cols.reshape(ref.shape[0], ref.shape[2], ref.shape[3])


def _inproj(x2d, shift, scale, g, w, *, rows_per_cond, cond0, extras, extra_dtype=None, update=None, tm=1024):
    m = x2d.shape[0]
    assert w.shape[0] == D_MODEL and w.shape[1] >= P_WIDTH and m % tm == 0 and rows_per_cond % tm == 0
    tiles_per_cond = rows_per_cond // tm
    cond = lambda i, j: (cond0 + i // tiles_per_cond, 0, 0)
    if update is None:
        updated, aliases = [], {}
        extra_specs = [pl.BlockSpec((tm, width), lambda i, j: (i, 0)) for _, _, width in extras]
        extra_shapes = [jax.ShapeDtypeStruct((m, width), extra_dtype) for _, _, width in extras]
    else:
        updated, layer, seq = update
        assert tm % seq == 0 and all(a.shape[2:] == (seq, width) for a, (_, _, width) in zip(updated, extras))
        extra_specs = [pl.BlockSpec((tm // seq, 1, seq, width), lambda i, j: (i, layer, 0, 0)) for _, _, width in extras]
        extra_shapes = [jax.ShapeDtypeStruct(a.shape, a.dtype) for a in updated]
        aliases = {5 + k: 2 + k for k in range(len(updated))}
    return pl.pallas_call(
        functools.partial(_inproj_kernel, extras=tuple((t, lo) for t, lo, _ in extras), n_updated=len(updated)),
        grid=(m // tm, N_TILES // STEP_TILES),
        in_specs=[
            pl.BlockSpec((tm, D_MODEL), lambda i, j: (i, 0)),
            pl.BlockSpec((1, 1, D_MODEL), cond),
            pl.BlockSpec((1, 1, D_MODEL), cond),
            pl.BlockSpec((1, D_MODEL), lambda i, j: (0, 0)),
            pl.BlockSpec((D_MODEL, STEP_TILES * TILE), lambda i, j: (0, j)),
        ] + [pl.BlockSpec(memory_space=pl.ANY) for _ in updated],
        out_specs=[pl.BlockSpec((tm, STEP_TILES * TILE), lambda i, j: (i, j)),
                   pl.BlockSpec((tm, D_MODEL), lambda i, j: (i, 0))]
        + extra_specs,
        out_shape=[jax.ShapeDtypeStruct((m, P_WIDTH), BF16), jax.ShapeDtypeStruct((m, D_MODEL), BF16)] + extra_shapes,
        input_output_aliases=aliases,
        compiler_params=pltpu.CompilerParams(dimension_semantics=("parallel", "arbitrary"),
                                             vmem_limit_bytes=PROJ_VMEM_BYTES),
        name="inproj",
    )(x2d, shift, scale, g, w, *updated)


def _out_kernel(x_ref, h_ref, gt_ref, fg_ref, oa_ref, ow_ref, of_ref, on_ref,
                wg0_ref, wg1_ref, wg2_ref, wg3_ref, wb_ref, wo_ref, y_ref, merged_ref, *, final):
    j = pl.program_id(1)
    h = h_ref[...]
    merged = None
    for i, (o_ref, wg_ref) in enumerate(((oa_ref, wg0_ref), (ow_ref, wg1_ref), (of_ref, wg2_ref), (on_ref, wg3_ref))):
        term = jax.nn.sigmoid(_dot(h, wg_ref[...])) * _dot(o_ref[...], wb_ref[i])
        merged = term if merged is None else merged + term
    merged_ref[j] = merged.astype(BF16)

    @pl.when(j == pl.num_programs(1) - 1)
    def _():
        m_all = jnp.concatenate([merged_ref[k] for k in range(merged_ref.shape[0])], axis=1)
        xn = x_ref[...] + gt_ref[0] * _dot(m_all, wo_ref[...])
        if final:
            ms = jnp.mean(xn * xn, axis=-1, keepdims=True)
            xn = xn * lax.rsqrt(ms + EPS) * fg_ref[...]
        y_ref[...] = xn


def _out_stage(x2d, h2d, gate, final_g, o_arrays, o_cols, w_gate, w_branch, w_out,
               *, rows_per_cond, cond0, final, tm=512, tc=512):
    m = x2d.shape[0]
    assert m % tm == 0 and rows_per_cond % tm == 0
    tiles_per_cond = rows_per_cond // tm
    ncol = D_MODEL // tc
    cond = lambda i, j: (cond0 + i // tiles_per_cond, 0, 0)
    o_specs = [pl.BlockSpec((tm, BRANCH_W), functools.partial(lambda i, j, c: (i, c), c=c)) for c in o_cols]
    assert tc == TILE and w_gate.shape == (D_MODEL, P_WIDTH + N_BRANCH * D_MODEL)
    wg_specs = [pl.BlockSpec((D_MODEL, tc), functools.partial(lambda i, j, b: (0, N_TILES + b * ncol + j), b=b))
                for b in range(N_BRANCH)]
    return pl.pallas_call(
        functools.partial(_out_kernel, final=final),
        grid=(m // tm, ncol),
        in_specs=[
            pl.BlockSpec((tm, D_MODEL), lambda i, j: (i, 0)),
            pl.BlockSpec((tm, D_MODEL), lambda i, j: (i, 0)),
            pl.BlockSpec((1, 1, D_MODEL), cond),
            pl.BlockSpec((1, D_MODEL), lambda i, j: (0, 0)),
            *o_specs,
            *wg_specs,
            pl.BlockSpec((N_BRANCH, BRANCH_W, tc), lambda i, j: (0, 0, j)),
            pl.BlockSpec((D_MODEL, D_MODEL), lambda i, j: (0, 0), pipeline_mode=pl.Buffered(1)),
        ],
        out_specs=pl.BlockSpec((tm, D_MODEL), lambda i, j: (i, 0)),
        out_shape=jax.ShapeDtypeStruct((m, D_MODEL), F32),
        scratch_shapes=[pltpu.VMEM((ncol, tm, tc), BF16)],
        compiler_params=pltpu.CompilerParams(dimension_semantics=("parallel", "arbitrary"),
                                             vmem_limit_bytes=PROJ_VMEM_BYTES),
        name="out_stage",
    )(x2d, h2d, gate, final_g, *o_arrays, w_gate, w_gate, w_gate, w_gate, w_branch, w_out)


def _attend(scores, values, sinks=None):
    heads = range(len(scores))
    sinks = [None] * len(scores) if sinks is None else sinks
    m = []
    for h in heads:
        mh = scores[h][0].max(axis=-1, keepdims=True)
        for s in scores[h][1:]:
            mh = jnp.maximum(mh, s.max(axis=-1, keepdims=True))
        m.append(mh if sinks[h] is None else jnp.maximum(mh, sinks[h]))
    e = [[jnp.exp(s - m[h]) for s in scores[h]] for h in heads]
    den = []
    for h in heads:
        d = e[h][0].sum(axis=-1, keepdims=True)
        for x in e[h][1:]:
            d = d + x.sum(axis=-1, keepdims=True)
        den.append(d if sinks[h] is None else d + jnp.exp(sinks[h] - m[h]))
    out = []
    for h in heads:
        acc = _dot(e[h][0].astype(BF16), values[h][0])
        for x, v in zip(e[h][1:], values[h][1:]):
            acc = acc + _dot(x.astype(BF16), v)
        out.append(acc)
    return [a / d for a, d in zip(out, den)]


def _shift_rows(z, first_row, last_row):
    n = z.shape[0]
    row = lax.broadcasted_iota(jnp.int32, z.shape, 0)
    z_dn = jnp.where(row == 0, first_row, pltpu.roll(z, 1, axis=0))
    z_up = jnp.where(row == n - 1, last_row, pltpu.roll(z, n - 1, axis=0))
    return z_dn, z_up


def _tile(ref, t, dtype=None):
    v = ref[:, t * TILE:(t + 1) * TILE]
    return v if dtype is None else v.astype(dtype)


def _head(ref, t, h, base=0):
    lo = t * TILE + base + h * HEAD_DIM
    return ref[:, lo:lo + HEAD_DIM]


def _scaled_queries(q):
    assert math.frexp(ATTN_SCALE)[0] == 0.5
    return q * ATTN_SCALE


def _ctx_mixer_kernel(p_ref, cw_ref, sink_ref, ct_ref, st_ref, cb_ref, sb_ref, o_ref):
    z = _tile(p_ref, T_AC, F32) * _tile(p_ref, T_AX, F32)
    zero_row = jnp.zeros((1, BRANCH_W), F32)
    z_dn, z_up = _shift_rows(z, zero_row, zero_row)
    y = _tile(p_ref, T_AB, F32) * (z_dn * cw_ref[0:1, :] + z * cw_ref[1:2, :] + z_up * cw_ref[2:3, :])
    out = lambda b: slice(b * BRANCH_W, (b + 1) * BRANCH_W)
    o_ref[:, out(0)] = (y * _silu(_tile(p_ref, T_AG, F32))).astype(o_ref.dtype)

    gsz = WIN_HEADS // WIN_KV_HEADS
    head = lambda a, h: a[:, h * HEAD_DIM:(h + 1) * HEAD_DIM]
    qb = _scaled_queries(_tile(p_ref, T_BQ))
    qd = _scaled_queries(_tile(p_ref, T_DQ))
    scores = ([[_dot_nt(head(qb, h), _head(p_ref, T_BKV, h // gsz))] for h in range(WIN_HEADS)]
              + [[_dot_nt(head(qd, h), _head(p_ref, T_DK, h))] for h in range(NA_HEADS)])
    values = ([[_head(p_ref, T_BKV, h // gsz, base=KV_W)] for h in range(WIN_HEADS)]
              + [[_head(p_ref, T_DV, h)] for h in range(NA_HEADS)])
    sinks = [sink_ref[h:h + 1, 0:1] for h in range(WIN_HEADS)] + [None] * NA_HEADS
    heads = _attend(scores, values, sinks=sinks)
    o_w = jnp.concatenate(heads[:WIN_HEADS], axis=-1) * _silu(_tile(p_ref, T_BG, F32))
    o_ref[:, out(1)] = o_w.astype(o_ref.dtype)
    o_n = jnp.concatenate(heads[WIN_HEADS:], axis=-1) * _silu(_tile(p_ref, T_DG, F32))
    o_ref[:, out(3)] = o_n.astype(o_ref.dtype)

    u = _tile(p_ref, T_FU)
    uc = _dot(u, cb_ref[...]).astype(BF16)
    us = _dot(u, sb_ref[...]).astype(BF16)
    o_f = (_dot(ct_ref[...], uc) - _dot(st_ref[...], us)) * _silu(_tile(p_ref, T_FG, F32))
    o_ref[:, out(2)] = o_f.astype(o_ref.dtype)


def _ctx_mixers(p, conv_w, sink_b, ct, st, cb, sb, *, seq):
    m = p.shape[0]
    whole = lambda a: pl.BlockSpec(a.shape, lambda b: (0,) * a.ndim)
    return pl.pallas_call(
        _ctx_mixer_kernel,
        grid=(m // seq,),
        in_specs=[pl.BlockSpec((seq, P_WIDTH), lambda b: (b, 0)), whole(conv_w), whole(sink_b),
                  whole(ct), whole(st), whole(cb), whole(sb)],
        out_specs=pl.BlockSpec((seq, N_BRANCH * BRANCH_W), lambda b: (b, 0)),
        out_shape=jax.ShapeDtypeStruct((m, N_BRANCH * BRANCH_W), BF16),
        compiler_params=pltpu.CompilerParams(dimension_semantics=("parallel",)),
        name="ctx_mixers",
    )(p, conv_w, sink_b, ct, st, cb, sb)


HALO = 16


def _lat_conv_kernel(ax_ref, ab_ref, ac_ref, ag_ref, axp_ref, acp_ref, axn_ref, acn_ref, cw_ref, o_ref):
    i = pl.program_id(1)
    z = ac_ref[...].astype(F32) * ax_ref[...].astype(F32)
    zp = acp_ref[HALO - 1:HALO, :].astype(F32) * axp_ref[HALO - 1:HALO, :].astype(F32)
    zn = acn_ref[0:1, :].astype(F32) * axn_ref[0:1, :].astype(F32)
    zp = jnp.where(i == 0, 0.0, zp)
    zn = jnp.where(i == pl.num_programs(1) - 1, 0.0, zn)
    z_dn, z_up = _shift_rows(z, zp, zn)
    y = ab_ref[...].astype(F32) * (z_dn * cw_ref[0:1, :] + z * cw_ref[1:2, :] + z_up * cw_ref[2:3, :])
    o_ref[...] = (y * _silu(ag_ref[...].astype(F32))).astype(o_ref.dtype)


def _lat_conv(pa, conv_w, *, batch, seq, tr=512):
    nt = seq // tr
    hb = tr // HALO
    last_halo = batch * seq // HALO - 1
    main = lambda c: pl.BlockSpec((tr, BRANCH_W), functools.partial(lambda b, i, c: (b * nt + i, c), c=c))
    prev = lambda c: pl.BlockSpec(
        (HALO, BRANCH_W), functools.partial(lambda b, i, c: (jnp.maximum((b * nt + i) * hb - 1, 0), c), c=c))
    nxt = lambda c: pl.BlockSpec(
        (HALO, BRANCH_W), functools.partial(lambda b, i, c: (jnp.minimum((b * nt + i + 1) * hb, last_halo), c), c=c))
    return pl.pallas_call(
        _lat_conv_kernel,
        grid=(batch, nt),
        in_specs=[main(T_AX), main(T_AB), main(T_AC), main(T_AG), prev(T_AX), prev(T_AC), nxt(T_AX), nxt(T_AC),
                  pl.BlockSpec(conv_w.shape, lambda b, i: (0, 0))],
        out_specs=pl.BlockSpec((tr, BRANCH_W), lambda b, i: (b * nt + i, 0)),
        out_shape=jax.ShapeDtypeStruct((batch * seq, BRANCH_W), BF16),
        compiler_params=pltpu.CompilerParams(dimension_semantics=("parallel", "arbitrary")),
        name="lat_conv",
    )(pa, pa, pa, pa, pa, pa, pa, pa, conv_w)


def _rope(x, cos, sin_signed):
    lane = lax.broadcasted_iota(jnp.int32, x.shape, 1)
    quarter = HEAD_DIM // 4
    partner = jnp.where((lane % (2 * quarter)) < quarter,
                        pltpu.roll(x, LANES - quarter, axis=1), pltpu.roll(x, quarter, axis=1))
    return x * cos + partner * sin_signed


WIN_BLOCKS_PER_STEP = 8


def _lat_win_kernel(q_ref, g_ref, kp_ref, kc_ref, kn_ref, vp_ref, vc_ref, vn_ref,
                    cq_ref, sq_ref, cp_ref, sp_ref, cn_ref, sn_ref, ck_ref, cv_ref, sink_ref, o_ref):
    nq = WIN_BLOCKS_PER_STEP
    nb = pl.num_programs(1) * nq
    wb = WIN_BLOCK
    cq, sq = cq_ref[...], sq_ref[...]
    k_rot = jnp.concatenate([
        _rope(kp_ref[...].astype(F32), cp_ref[...], sp_ref[...]),
        _rope(kc_ref[...].astype(F32), cq, sq),
        _rope(kn_ref[...].astype(F32), cn_ref[...], sn_ref[...]),
    ], axis=0).astype(BF16)
    v_all = jnp.concatenate([vp_ref[...], vc_ref[...], vn_ref[...]], axis=0).astype(BF16)
    qi = lax.broadcasted_iota(jnp.int32, (wb, 3 * wb), 0)
    kj = lax.broadcasted_iota(jnp.int32, (wb, 3 * wb), 1)
    ck = ck_ref[0].astype(BF16)
    cv = cv_ref[0].astype(BF16)
    gsz = WIN_HEADS // WIN_KV_HEADS
    q_rot = [_scaled_queries(_rope(q_ref[:, pair * LANES:(pair + 1) * LANES].astype(F32), cq, sq)).astype(BF16)
             for pair in range(WIN_HEADS // 2)]
    scores, values, sinks = [], [], []
    for d in range(nq):
        n = pl.program_id(1) * nq + d
        mask = (((kj < wb) & (kj >= qi) & (n > 0)) | ((kj >= wb) & (kj < 2 * wb))
                | ((kj >= 2 * wb) & (kj - 2 * wb <= qi) & (n < nb - 1)))
        rows = slice(d * wb, (d + 1) * wb)
        win = slice(d * wb, (d + 3) * wb)
        for h in range(WIN_HEADS):
            kv = slice((h // gsz) * HEAD_DIM, (h // gsz + 1) * HEAD_DIM)
            q = q_rot[h // 2][rows, (h % 2) * HEAD_DIM:(h % 2 + 1) * HEAD_DIM]
            s_loc = jnp.where(mask, _dot_nt(q, k_rot[win, kv]), NEG_INF)
            scores.append([s_loc, _dot_nt(q, ck[:, kv])])
            values.append([v_all[win, kv], cv[:, kv]])
            sinks.append(sink_ref[h:h + 1, 0:1])
    heads = _attend(scores, values, sinks=sinks)
    o = jnp.concatenate([jnp.concatenate(heads[d * WIN_HEADS:(d + 1) * WIN_HEADS], axis=-1) for d in range(nq)], axis=0)
    o_ref[...] = (o * _silu(g_ref[...].astype(F32))).astype(o_ref.dtype)


def _lat_window(pb, cos_t, sin_t, ctx_k, ctx_v, sink_b, *, batch, seq):
    wb = WIN_BLOCK
    k_col = T_BKV * (TILE // KV_W)
    v_col = k_col + 1
    nq = WIN_BLOCKS_PER_STEP
    nb = seq // wb
    steps = nb // nq
    nbr = lambda s, d: jnp.clip(s * nq + (nq if d > 0 else -1), 0, nb - 1)
    own = lambda col, width: pl.BlockSpec(
        (nq * wb, width), functools.partial(lambda b, s, col: (b * steps + s, col), col=col))
    kv = lambda col, d: pl.BlockSpec(
        (wb, KV_W), functools.partial(lambda b, s, col, d: (b * nb + nbr(s, d), col), col=col, d=d))
    tab = lambda d: pl.BlockSpec((wb, LANES), functools.partial(lambda b, s, d: (nbr(s, d), 0), d=d))
    tab_own = pl.BlockSpec((nq * wb, LANES), lambda b, s: (s, 0))
    ctx = pl.BlockSpec((1,) + ctx_k.shape[1:], lambda b, s: (b, 0, 0))
    return pl.pallas_call(
        _lat_win_kernel,
        grid=(batch, steps),
        in_specs=[
            own(T_BQ, BRANCH_W), own(T_BG, BRANCH_W),
            kv(k_col, -1), own(k_col, KV_W), kv(k_col, 1), kv(v_col, -1), own(v_col, KV_W), kv(v_col, 1),
            tab_own, tab_own, tab(-1), tab(-1), tab(1), tab(1),
            ctx, ctx,
            pl.BlockSpec(sink_b.shape, lambda b, s: (0, 0)),
        ],
        out_specs=pl.BlockSpec((nq * wb, BRANCH_W), lambda b, s: (b * steps + s, 0)),
        out_shape=jax.ShapeDtypeStruct((batch * seq, BRANCH_W), BF16),
        compiler_params=pltpu.CompilerParams(dimension_semantics=("parallel", "arbitrary")),
        name="lat_window",
    )(pb, pb, pb, pb, pb, pb, pb, pb, cos_t, sin_t, cos_t, sin_t, cos_t, sin_t, ctx_k, ctx_v, sink_b)


def _fft1_kernel(u_ref, c_ref, s_ref, twr_ref, twi_ref, zr_ref, zi_ref, *, chunks):
    u = u_ref[0].astype(BF16)
    yr = _dot(c_ref[...], u)
    yi = -_dot(s_ref[...], u)
    for t in range(chunks):
        wr = twr_ref[0, :, t:t + 1]
        wi = twi_ref[0, :, t:t + 1]
        a = yr[:, t * BRANCH_W:(t + 1) * BRANCH_W]
        b = yi[:, t * BRANCH_W:(t + 1) * BRANCH_W]
        zr_ref[0, :, t * BRANCH_W:(t + 1) * BRANCH_W] = (a * wr - b * wi).astype(zr_ref.dtype)
        zi_ref[0, :, t * BRANCH_W:(t + 1) * BRANCH_W] = (a * wi + b * wr).astype(zi_ref.dtype)


def _fft2_kernel(zr_ref, zi_ref, g_ref, l_ref, cs_ref, o_ref, *, kblock):
    r = FFT_R
    zz = [jnp.concatenate([zr_ref[0, kk], zi_ref[0, kk]], axis=0) for kk in range(kblock)]
    xx = [_dot(l_ref[...], z) for z in zz]
    xcat = [jnp.concatenate([x[0:r], x[r:2 * r]], axis=1).astype(BF16) for x in xx]
    out = [_dot(x, cs_ref[...]) for x in xcat]
    for kk in range(kblock):
        sl = slice(kk * BRANCH_W, (kk + 1) * BRANCH_W)
        o_ref[0, :, sl] = (out[kk] * _silu(g_ref[0, :, sl].astype(F32))).astype(o_ref.dtype)


def _lat_fourier(fu, fg, tabs, *, batch, seq):
    r = FFT_R
    assert seq == r * r
    wide = r * BRANCH_W
    chunks = 8
    c64, s64, twr, twi, lmat, csmat = tabs
    u3 = fu.reshape(batch, r, wide)
    nct = r // chunks
    zr, zi = pl.pallas_call(
        functools.partial(_fft1_kernel, chunks=chunks),
        grid=(batch, nct),
        in_specs=[
            pl.BlockSpec((1, r, chunks * BRANCH_W), lambda b, t: (b, 0, t)),
            pl.BlockSpec((r, r), lambda b, t: (0, 0)),
            pl.BlockSpec((r, r), lambda b, t: (0, 0)),
            pl.BlockSpec((1, r, chunks), lambda b, t: (t, 0, 0)),
            pl.BlockSpec((1, r, chunks), lambda b, t: (t, 0, 0)),
        ],
        out_specs=[pl.BlockSpec((1, r, chunks * BRANCH_W), lambda b, t: (b, 0, t))] * 2,
        out_shape=[jax.ShapeDtypeStruct((batch, r, wide), BF16)] * 2,
        compiler_params=pltpu.CompilerParams(dimension_semantics=("parallel", "arbitrary")),
        name="lat_fft1",
    )(u3, c64, s64, twr, twi)
    kblock = 8
    z4 = lambda z: z.reshape(batch, r, r, BRANCH_W)
    out = pl.pallas_call(
        functools.partial(_fft2_kernel, kblock=kblock),
        grid=(batch, r // kblock),
        in_specs=[
            pl.BlockSpec((1, kblock, r, BRANCH_W), lambda b, k: (b, k, 0, 0)),
            pl.BlockSpec((1, kblock, r, BRANCH_W), lambda b, k: (b, k, 0, 0)),
            pl.BlockSpec((1, r, kblock * BRANCH_W), lambda b, k: (b, 0, k)),
            pl.BlockSpec((2 * r, 2 * r), lambda b, k: (0, 0)),
            pl.BlockSpec((2 * BRANCH_W, BRANCH_W), lambda b, k: (0, 0)),
        ],
        out_specs=pl.BlockSpec((1, r, kblock * BRANCH_W), lambda b, k: (b, 0, k)),
        out_shape=jax.ShapeDtypeStruct((batch, r, wide), BF16),
        compiler_params=pltpu.CompilerParams(dimension_semantics=("parallel", "arbitrary")),
        name="lat_fft2",
    )(z4(zr), z4(zi), fg.reshape(batch, r, wide), lmat, csmat)
    return out.reshape(batch * seq, BRANCH_W)


N_RPB_ROWS = 2 * NA_ROWS - 1
N_RPB_COLS = 2 * NA_COLS - 1


NA_ROWS_PER_STEP = 8


def _lat_na_kernel(q_ref, g_ref, k_ref, v_ref, bias_ref, ck_ref, cv_ref, o_ref, *, rows):
    ck = ck_ref[0].astype(BF16)
    cv = cv_ref[0].astype(BF16)
    sls = [slice(h * HEAD_DIM, (h + 1) * HEAD_DIM) for h in range(NA_HEADS)]
    scores, values = [], []
    for d in range(NA_ROWS_PER_STEP):
        r = pl.program_id(1) * NA_ROWS_PER_STEP + d
        r0 = jnp.clip(r - NA_ROWS // 2, 0, rows - NA_ROWS)
        start = pl.multiple_of(r0 * GRID_W, GRID_W)
        kw = k_ref[pl.ds(start, NA_ROWS * GRID_W), :]
        vw = v_ref[pl.ds(start, NA_ROWS * GRID_W), :]
        qa = _scaled_queries(q_ref[d * GRID_W:(d + 1) * GRID_W, :])
        a0 = r0 - r + NA_ROWS - 1
        for h, sl in enumerate(sls):
            bias = jnp.concatenate([bias_ref[h, a0 + 2 * i2] for i2 in range(NA_ROWS // 2)], axis=1)
            scores.append([_dot_nt(qa[:, sl], kw[:, sl]) + bias, _dot_nt(qa[:, sl], ck[:, sl])])
            values.append([vw[:, sl], cv[:, sl]])
    heads = _attend(scores, values)
    o = jnp.concatenate([jnp.concatenate(heads[d * NA_HEADS:(d + 1) * NA_HEADS], axis=-1)
                         for d in range(NA_ROWS_PER_STEP)], axis=0)
    o_ref[...] = (o * _silu(g_ref[...].astype(F32))).astype(o_ref.dtype)


def _lat_na(p, bias, ctx_k, ctx_v, *, batch, seq):
    rows = seq // GRID_W
    steps = rows // NA_ROWS_PER_STEP
    qrows = NA_ROWS_PER_STEP * GRID_W
    ctx = pl.BlockSpec((1,) + ctx_k.shape[1:], lambda b, r: (b, 0, 0))
    return pl.pallas_call(
        functools.partial(_lat_na_kernel, rows=rows),
        grid=(batch, steps),
        in_specs=[
            pl.BlockSpec((qrows, BRANCH_W), lambda b, r: (b * steps + r, T_DQ)),
            pl.BlockSpec((qrows, BRANCH_W), lambda b, r: (b * steps + r, T_DG)),
            pl.BlockSpec((seq, BRANCH_W), lambda b, r: (b, T_DK)),
            pl.BlockSpec((seq, BRANCH_W), lambda b, r: (b, T_DV)),
            pl.BlockSpec(bias.shape, lambda b, r: (0, 0, 0, 0)),
            ctx, ctx,
        ],
        out_specs=pl.BlockSpec((qrows, BRANCH_W), lambda b, r: (b * steps + r, 0)),
        out_shape=jax.ShapeDtypeStruct((batch * seq, BRANCH_W), BF16),
        compiler_params=pltpu.CompilerParams(dimension_semantics=("parallel", "arbitrary")),
        name="lat_na",
    )(p, p, p, p, bias, ctx_k, ctx_v)


def _toeplitz_kernel(r_ref, o_ref):
    x = r_ref[...]
    hi = x.astype(BF16)
    r1 = x - hi.astype(F32)
    mid = r1.astype(BF16)
    lo = (r1 - mid.astype(F32)).astype(BF16)
    shape = (x.shape[1], GRID_W * GRID_W)
    b = lax.broadcasted_iota(jnp.int32, shape, 0)
    col = lax.broadcasted_iota(jnp.int32, shape, 1)
    kc = jnp.bitwise_and(col, GRID_W - 1)
    qc = lax.shift_right_logical(col, GRID_W.bit_length() - 1)
    onehot = jnp.where(kc - qc + (NA_COLS - 1) == b, 1.0, 0.0).astype(BF16)
    o_ref[...] = _dot(hi, onehot) + _dot(mid, onehot) + _dot(lo, onehot)


def _na_bias_tables(na_rpb):
    depth = na_rpb.shape[0]
    n = depth * NA_HEADS * N_RPB_ROWS
    n_pad = -(-n // 8) * 8
    r2 = jnp.zeros((n_pad, LANES), F32).at[:n, :N_RPB_COLS].set(na_rpb.reshape(n, N_RPB_COLS))
    flat = pl.pallas_call(
        _toeplitz_kernel,
        out_shape=jax.ShapeDtypeStruct((n_pad, GRID_W * GRID_W), F32),
        name="na_bias_toeplitz",
    )(r2)
    t = flat[:n].reshape(depth, NA_HEADS, N_RPB_ROWS, GRID_W, GRID_W)
    qc = np.arange(GRID_W)[:, None]
    kc = np.arange(GRID_W)[None, :]
    win0 = np.clip(qc - NA_COLS // 2, 0, GRID_W - NA_COLS)
    vis = (kc >= win0) & (kc < win0 + NA_COLS)
    t = jnp.where(vis, t, NEG_INF)
    return jnp.concatenate([t[:, :, :-1], t[:, :, 1:]], axis=-1)


def _bf16_table(t):
    return jnp.asarray(t, F32).astype(BF16)


def _dft_cos_sin(n):
    idx = np.arange(n)
    ang = 2.0 * np.pi * ((idx[:, None] * idx[None, :]) % n) / n
    return np.cos(ang), np.sin(ang)


def _channel_dft_blocks(scale):
    c, s = _dft_cos_sin(FN_GROUP_W)
    eye = np.eye(FN_GROUPS)
    return np.kron(eye, c) * scale, np.kron(eye, s) * scale


def _ctx_fourier_tables(seq):
    ct, st = _dft_cos_sin(seq)
    cb, sb = _channel_dft_blocks(1.0 / np.sqrt(seq * FN_GROUP_W))
    return tuple(_bf16_table(t) for t in (ct, st, cb, sb))


def _lat_fourier_tables(seq, chunks=8):
    r = FFT_R
    c64, s64 = _dft_cos_sin(r)
    k1 = np.arange(r)[:, None]
    t2 = np.arange(r)[None, :]
    ang = 2.0 * np.pi * (k1 * t2) / (r * r)
    twr = np.cos(ang).reshape(r, r // chunks, chunks).transpose(1, 0, 2)
    twi = (-np.sin(ang)).reshape(r, r // chunks, chunks).transpose(1, 0, 2)
    lmat = np.block([[c64, s64], [-s64, c64]])
    cb, sb = _channel_dft_blocks(1.0 / np.sqrt(seq * FN_GROUP_W))
    csmat = np.concatenate([cb, sb], axis=0)
    return (_bf16_table(c64), _bf16_table(s64), jnp.asarray(twr, F32), jnp.asarray(twi, F32),
            _bf16_table(lmat), _bf16_table(csmat))


def _rope_tables(seq):
    half = HEAD_DIM // 2
    quarter = half // 2
    t = jnp.arange(seq)
    inv = ROPE_BASE ** (-jnp.arange(quarter, dtype=F32) / quarter)

    def cs(pos):
        ang = pos.astype(F32)[:, None] * inv[None, :]
        c, s = jnp.cos(ang), jnp.sin(ang)
        return jnp.concatenate([c, c], axis=-1), jnp.concatenate([-s, s], axis=-1)

    cr, sr = cs(t // GRID_W)
    cc, sc = cs(t % GRID_W)
    cos = jnp.concatenate([cr, cc], axis=-1)
    sin = jnp.concatenate([sr, sc], axis=-1)
    return jnp.tile(cos, (1, 2)), jnp.tile(sin, (1, 2))


def _tiled_in_weights(wl):
    kv_pad = jnp.zeros((D_MODEL, TILE - 2 * KV_W), F32)
    zero_tile = jnp.zeros((D_MODEL, TILE), F32)
    cols = [wl[:, OFF_A:OFF_BQ], wl[:, OFF_BQ:OFF_BK], wl[:, OFF_BG:OFF_FU], wl[:, OFF_BK:OFF_BG], kv_pad,
            wl[:, OFF_FU:N_MAIN], zero_tile, wl[:, N_MAIN:]]
    return jnp.concatenate(cols, axis=1).astype(BF16)


def kernel(x_prompt, x_sample, cache_win_k, cache_win_v, cache_na_k, cache_na_v, c, c_ctx, norm_g, w_ada, b_ada,
           w_in, conv_w, win_sink, na_rpb, w_branch, w_out, final_g):
    batch, seq, _ = x_prompt.shape
    dbatch, dseq, _ = x_sample.shape
    past = cache_win_k.shape[2]

    cv8 = jnp.zeros((8, D_MODEL), F32).at[0].set(c_ctx).at[1:1 + dbatch].set(c)
    mod = _modulation(cv8, w_ada, b_ada)

    ctx_tabs = _ctx_fourier_tables(seq)
    lat_tabs = _lat_fourier_tables(dseq)
    cos_t, sin_t = _rope_tables(dseq)
    na_bias = _na_bias_tables(na_rpb)
    final_g2 = final_g.reshape(1, D_MODEL)

    xp = x_prompt.reshape(batch * seq, D_MODEL)
    xs = x_sample.reshape(dbatch * dseq, D_MODEL)
    new_kv = [jnp.zeros((batch, DEPTH, seq, width), F32) for width in (KV_W, KV_W, BRANCH_W, BRANCH_W)]

    for l in range(DEPTH):
        shift = mod[l, :, 0:D_MODEL].reshape(8, 1, D_MODEL)
        scale = mod[l, :, D_MODEL:2 * D_MODEL].reshape(8, 1, D_MODEL)
        gate = mod[l, :, 2 * D_MODEL:].reshape(8, 1, D_MODEL)
        g = norm_g[l].reshape(1, D_MODEL)
        w_p = _tiled_in_weights(w_in[l])
        w_br = w_branch[l].astype(BF16)
        w_o = w_out[l].astype(BF16)
        cw = conv_w[l]
        sink_b = jnp.broadcast_to(win_sink[l][:, None], (WIN_HEADS, LANES))
        final = l == DEPTH - 1

        p, hp, *new_kv = _inproj(
            xp, shift, scale, g, w_p, rows_per_cond=batch * seq, cond0=0, update=(new_kv, l, seq),
            extras=((T_BKV, 0, KV_W), (T_BKV, KV_W, KV_W), (T_DK, 0, BRANCH_W), (T_DV, 0, BRANCH_W)))
        o_ctx = _ctx_mixers(p, cw, sink_b, *ctx_tabs, seq=seq)
        xp = _out_stage(xp, hp, gate, final_g2, [o_ctx] * 4, [0, 1, 2, 3], w_p, w_br, w_o,
                        rows_per_cond=batch * seq, cond0=0, final=final)

        q, hs, qfu, qfg = _inproj(xs, shift, scale, g, w_p, rows_per_cond=dseq, cond0=1, extra_dtype=BF16,
                              extras=((T_FU, 0, BRANCH_W), (T_FG, 0, BRANCH_W)))
        o_a = _lat_conv(q, cw, batch=dbatch, seq=dseq)
        o_w = _lat_window(q, cos_t, sin_t,
                          cache_win_k[:, l].reshape(dbatch, past, KV_W), cache_win_v[:, l].reshape(dbatch, past, KV_W),
                          sink_b, batch=dbatch, seq=dseq)
        o_f = _lat_fourier(qfu, qfg, lat_tabs, batch=dbatch, seq=dseq)
        o_n = _lat_na(q, na_bias[l],
                      cache_na_k[:, l].reshape(dbatch, past, BRANCH_W), cache_na_v[:, l].reshape(dbatch, past, BRANCH_W),
                      batch=dbatch, seq=dseq)
        xs = _out_stage(xs, hs, gate, final_g2, [o_a, o_w, o_f, o_n], [0, 0, 0, 0], w_p, w_br, w_o,
                        rows_per_cond=dseq, cond0=1, final=final)

    y_prompt = xp.reshape(batch, seq, D_MODEL)
    y_sample = xs.reshape(dbatch, dseq, D_MODEL)
    heads = (WIN_KV_HEADS, WIN_KV_HEADS, NA_HEADS, NA_HEADS)
    return (y_prompt, y_sample) + tuple(a.reshape(batch, DEPTH, seq, n, HEAD_DIM) for a, n in zip(new_kv, heads))
```

```python
import functools
import math

import numpy as np
import jax
import jax.numpy as jnp
from jax import lax
from jax.experimental import pallas as pl
from jax.experimental.pallas import tpu as pltpu

D_MODEL = 2048
DEPTH = 2
GRID_W = 64
N_BRANCH = 4
BRANCH_W = 512
HEAD_DIM = 64
WIN_HEADS = 8
WIN_KV_HEADS = 2
KV_W = WIN_KV_HEADS * HEAD_DIM
WIN_BLOCK = 128
FN_GROUPS = 4
FN_GROUP_W = BRANCH_W // FN_GROUPS
NA_HEADS = 8
NA_ROWS = 8
NA_COLS = 16
ROPE_BASE = 10000.0
EPS = 1e-6
ATTN_SCALE = HEAD_DIM ** -0.5
NEG_INF = -1e30
N_MAIN = 6400

F32 = jnp.float32
BF16 = jnp.bfloat16
LANES = 128

OFF_A = 0
OFF_BQ = OFF_A + 4 * BRANCH_W
OFF_BK = OFF_BQ + BRANCH_W
OFF_BG = OFF_BK + 2 * KV_W
OFF_FU = OFF_BG + BRANCH_W
assert OFF_FU + 6 * BRANCH_W == N_MAIN

TILE = 512
T_AX, T_AB, T_AC, T_AG, T_BQ, T_BG, T_BKV, T_FU, T_FG, T_DQ, T_DK, T_DV, T_DG = range(13)
STEP_TILES = 2
N_TILES = 14
P_WIDTH = N_TILES * TILE

V7X_VMEM_BYTES = 64 * 2 ** 20
PROJ_VMEM_BYTES = V7X_VMEM_BYTES - 2 ** 20

FFT_R = 64


def _silu(x):
    return x * jax.nn.sigmoid(x)


def _dot(a, b):
    return jnp.dot(a, b, preferred_element_type=F32)


def _dot_nt(a, b):
    return lax.dot_general(a, b, (((1,), (1,)), ((), ())), preferred_element_type=F32)


def _rms_mod(x, g, scale, shift):
    ms = jnp.mean(x * x, axis=-1, keepdims=True)
    return (x * lax.rsqrt(ms + EPS) * g) * (1.0 + scale) + shift


NORM_ROWS = 256


def _mod_kernel(cv_ref, w_ref, b_ref, o_ref):
    s = _silu(cv_ref[...]).astype(BF16)
    o_ref[0] = _dot(s, w_ref[0].astype(BF16)) + b_ref[0]


def _modulation(cv8, w_ada, b_ada):
    tn = 512
    n3 = 3 * D_MODEL
    return pl.pallas_call(
        _mod_kernel,
        grid=(DEPTH, n3 // tn),
        in_specs=[
            pl.BlockSpec((8, D_MODEL), lambda l, j: (0, 0)),
            pl.BlockSpec((1, D_MODEL, tn), lambda l, j: (l, 0, j)),
            pl.BlockSpec((1, 1, tn), lambda l, j: (l, 0, j)),
        ],
        out_specs=pl.BlockSpec((1, 8, tn), lambda l, j: (l, 0, j)),
        out_shape=jax.ShapeDtypeStruct((DEPTH, 8, n3), F32),
        name="modulation",
    )(cv8, w_ada, b_ada.reshape(DEPTH, 1, n3))


def _inproj_kernel(x_ref, sh_ref, sc_ref, g_ref, w_ref, *rest, extras, n_updated):
    p_ref, h_ref, extra_refs = rest[n_updated], rest[n_updated + 1], rest[n_updated + 2:]
    j = pl.program_id(1)
    assert all(tile >= STEP_TILES for tile, _ in extras)

    @pl.when(j == 0)
    def _():
        g, scale, shift = g_ref[...], sc_ref[0], sh_ref[0]
        for c in range(h_ref.shape[0] // NORM_ROWS):
            rows = slice(c * NORM_ROWS, (c + 1) * NORM_ROWS)
            h = _rms_mod(x_ref[rows, :], g, scale, shift).astype(h_ref.dtype)
            h_ref[rows, :] = h
            p_ref[rows, :] = _dot(h, w_ref[...]).astype(p_ref.dtype)

    @pl.when(j > 0)
    def _():
        acc = _dot(h_ref[...], w_ref[...])
        p_ref[...] = acc.astype(p_ref.dtype)
        for ref, (tile, lo) in zip(extra_refs, extras):
            @pl.when(j == tile // STEP_TILES)
            def _(ref=ref, lo=(tile % STEP_TILES) * TILE + lo):
                cols = acc[:, lo:lo + ref.shape[-1]].astype(ref.dtype)
                if ref.ndim == 2:
                    ref[...] = cols
                else:
                    ref[:, 0] = cols.reshape(ref.shape[0], ref.shape[2], ref.shape[3])


def _inproj(x2d, shift, scale, g, w, *, rows_per_cond, cond0, extras, extra_dtype=None, update=None, tm=1024):
    m = x2d.shape[0]
    assert w.shape[0] == D_MODEL and w.shape[1] >= P_WIDTH and m % tm == 0 and rows_per_cond % tm == 0
    tiles_per_cond = rows_per_cond // tm
    cond = lambda i, j: (cond0 + i // tiles_per_cond, 0, 0)
    if update is None:
        updated, aliases = [], {}
        extra_specs = [pl.BlockSpec((tm, width), lambda i, j: (i, 0)) for _, _, width in extras]
        extra_shapes = [jax.ShapeDtypeStruct((m, width), extra_dtype) for _, _, width in extras]
    else:
        updated, layer, seq = update
        assert tm % seq == 0 and all(a.shape[2:] == (seq, width) for a, (_, _, width) in zip(updated, extras))
        extra_specs = [pl.BlockSpec((tm // seq, 1, seq, width), lambda i, j: (i, layer, 0, 0)) for _, _, width in extras]
        extra_shapes = [jax.ShapeDtypeStruct(a.shape, a.dtype) for a in updated]
        aliases = {5 + k: 2 + k for k in range(len(updated))}
    return pl.pallas_call(
        functools.partial(_inproj_kernel, extras=tuple((t, lo) for t, lo, _ in extras), n_updated=len(updated)),
        grid=(m // tm, N_TILES // STEP_TILES),
        in_specs=[
            pl.BlockSpec((tm, D_MODEL), lambda i, j: (i, 0)),
            pl.BlockSpec((1, 1, D_MODEL), cond),
            pl.BlockSpec((1, 1, D_MODEL), cond),
            pl.BlockSpec((1, D_MODEL), lambda i, j: (0, 0)),
            pl.BlockSpec((D_MODEL, STEP_TILES * TILE), lambda i, j: (0, j)),
        ] + [pl.BlockSpec(memory_space=pl.ANY) for _ in updated],
        out_specs=[pl.BlockSpec((tm, STEP_TILES * TILE), lambda i, j: (i, j)),
                   pl.BlockSpec((tm, D_MODEL), lambda i, j: (i, 0))]
        + extra_specs,
        out_shape=[jax.ShapeDtypeStruct((m, P_WIDTH), BF16), jax.ShapeDtypeStruct((m, D_MODEL), BF16)] + extra_shapes,
        input_output_aliases=aliases,
        compiler_params=pltpu.CompilerParams(dimension_semantics=("parallel", "arbitrary"),
                                             vmem_limit_bytes=PROJ_VMEM_BYTES),
        name="inproj",
    )(x2d, shift, scale, g, w, *updated)


def _out_kernel(x_ref, h_ref, gt_ref, fg_ref, oa_ref, ow_ref, of_ref, on_ref,
                wg0_ref, wg1_ref, wg2_ref, wg3_ref, wb_ref, wo_ref, y_ref, merged_ref, *, final):
    j = pl.program_id(1)
    h = h_ref[...]
    merged = None
    for i, (o_ref, wg_ref) in enumerate(((oa_ref, wg0_ref), (ow_ref, wg1_ref), (of_ref, wg2_ref), (on_ref, wg3_ref))):
        term = jax.nn.sigmoid(_dot(h, wg_ref[...])) * _dot(o_ref[...], wb_ref[i])
        merged = term if merged is None else merged + term
    merged_ref[j] = merged.astype(BF16)

    @pl.when(j == pl.num_programs(1) - 1)
    def _():
        m_all = jnp.concatenate([merged_ref[k] for k in range(merged_ref.shape[0])], axis=1)
        xn = x_ref[...] + gt_ref[0] * _dot(m_all, wo_ref[...])
        if final:
            ms = jnp.mean(xn * xn, axis=-1, keepdims=True)
            xn = xn * lax.rsqrt(ms + EPS) * fg_ref[...]
        y_ref[...] = xn


def _out_stage(x2d, h2d, gate, final_g, o_arrays, o_cols, w_gate, w_branch, w_out,
               *, rows_per_cond, cond0, final, tm=512, tc=512):
    m = x2d.shape[0]
    assert m % tm == 0 and rows_per_cond % tm == 0
    tiles_per_cond = rows_per_cond // tm
    ncol = D_MODEL // tc
    cond = lambda i, j: (cond0 + i // tiles_per_cond, 0, 0)
    o_specs = [pl.BlockSpec((tm, BRANCH_W), functools.partial(lambda i, j, c: (i, c), c=c)) for c in o_cols]
    assert tc == TILE and w_gate.shape == (D_MODEL, P_WIDTH + N_BRANCH * D_MODEL)
    wg_specs = [pl.BlockSpec((D_MODEL, tc), functools.partial(lambda i, j, b: (0, N_TILES + b * ncol + j), b=b))
                for b in range(N_BRANCH)]
    return pl.pallas_call(
        functools.partial(_out_kernel, final=final),
        grid=(m // tm, ncol),
        in_specs=[
            pl.BlockSpec((tm, D_MODEL), lambda i, j: (i, 0)),
            pl.BlockSpec((tm, D_MODEL), lambda i, j: (i, 0)),
            pl.BlockSpec((1, 1, D_MODEL), cond),
            pl.BlockSpec((1, D_MODEL), lambda i, j: (0, 0)),
            *o_specs,
            *wg_specs,
            pl.BlockSpec((N_BRANCH, BRANCH_W, tc), lambda i, j: (0, 0, j)),
            pl.BlockSpec((D_MODEL, D_MODEL), lambda i, j: (0, 0), pipeline_mode=pl.Buffered(1)),
        ],
        out_specs=pl.BlockSpec((tm, D_MODEL), lambda i, j: (i, 0)),
        out_shape=jax.ShapeDtypeStruct((m, D_MODEL), F32),
        scratch_shapes=[pltpu.VMEM((ncol, tm, tc), BF16)],
        compiler_params=pltpu.CompilerParams(dimension_semantics=("parallel", "arbitrary"),
                                             vmem_limit_bytes=PROJ_VMEM_BYTES),
        name="out_stage",
    )(x2d, h2d, gate, final_g, *o_arrays, w_gate, w_gate, w_gate, w_gate, w_branch, w_out)


def _attend(scores, values, sinks=None):
    heads = range(len(scores))
    sinks = [None] * len(scores) if sinks is None else sinks
    m = []
    for h in heads:
        mh = scores[h][0].max(axis=-1, keepdims=True)
        for s in scores[h][1:]:
            mh = jnp.maximum(mh, s.max(axis=-1, keepdims=True))
        m.append(mh if sinks[h] is None else jnp.maximum(mh, sinks[h]))
    e = [[jnp.exp(s - m[h]) for s in scores[h]] for h in heads]
    den = []
    for h in heads:
        d = e[h][0].sum(axis=-1, keepdims=True)
        for x in e[h][1:]:
            d = d + x.sum(axis=-1, keepdims=True)
        den.append(d if sinks[h] is None else d + jnp.exp(sinks[h] - m[h]))
    out = []
    for h in heads:
        acc = _dot(e[h][0].astype(BF16), values[h][0])
        for x, v in zip(e[h][1:], values[h][1:]):
            acc = acc + _dot(x.astype(BF16), v)
        out.append(acc)
    return [a / d for a, d in zip(out, den)]


def _shift_rows(z, first_row, last_row):
    n = z.shape[0]
    row = lax.broadcasted_iota(jnp.int32, z.shape, 0)
    z_dn = jnp.where(row == 0, first_row, pltpu.roll(z, 1, axis=0))
    z_up = jnp.where(row == n - 1, last_row, pltpu.roll(z, n - 1, axis=0))
    return z_dn, z_up


def _tile(ref, t, dtype=None):
    v = ref[:, t * TILE:(t + 1) * TILE]
    return v if dtype is None else v.astype(dtype)


def _head(ref, t, h, base=0):
    lo = t * TILE + base + h * HEAD_DIM
    return ref[:, lo:lo + HEAD_DIM]


def _scaled_queries(q):
    assert math.frexp(ATTN_SCALE)[0] == 0.5
    return q * ATTN_SCALE


def _ctx_mixer_kernel(p_ref, cw_ref, sink_ref, ct_ref, st_ref, cb_ref, sb_ref, o_ref):
    z = _tile(p_ref, T_AC, F32) * _tile(p_ref, T_AX, F32)
    zero_row = jnp.zeros((1, BRANCH_W), F32)
    z_dn, z_up = _shift_rows(z, zero_row, zero_row)
    y = _tile(p_ref, T_AB, F32) * (z_dn * cw_ref[0:1, :] + z * cw_ref[1:2, :] + z_up * cw_ref[2:3, :])
    out = lambda b: slice(b * BRANCH_W, (b + 1) * BRANCH_W)
    o_ref[:, out(0)] = (y * _silu(_tile(p_ref, T_AG, F32))).astype(o_ref.dtype)

    gsz = WIN_HEADS // WIN_KV_HEADS
    head = lambda a, h: a[:, h * HEAD_DIM:(h + 1) * HEAD_DIM]
    qb = _scaled_queries(_tile(p_ref, T_BQ))
    qd = _scaled_queries(_tile(p_ref, T_DQ))
    scores = ([[_dot_nt(head(qb, h), _head(p_ref, T_BKV, h // gsz))] for h in range(WIN_HEADS)]
              + [[_dot_nt(head(qd, h), _head(p_ref, T_DK, h))] for h in range(NA_HEADS)])
    values = ([[_head(p_ref, T_BKV, h // gsz, base=KV_W)] for h in range(WIN_HEADS)]
              + [[_head(p_ref, T_DV, h)] for h in range(NA_HEADS)])
    sinks = [sink_ref[h:h + 1, 0:1] for h in range(WIN_HEADS)] + [None] * NA_HEADS
    heads = _attend(scores, values, sinks=sinks)
    o_w = jnp.concatenate(heads[:WIN_HEADS], axis=-1) * _silu(_tile(p_ref, T_BG, F32))
    o_ref[:, out(1)] = o_w.astype(o_ref.dtype)
    o_n = jnp.concatenate(heads[WIN_HEADS:], axis=-1) * _silu(_tile(p_ref, T_DG, F32))
    o_ref[:, out(3)] = o_n.astype(o_ref.dtype)

    u = _tile(p_ref, T_FU)
    uc = _dot(u, cb_ref[...]).astype(BF16)
    us = _dot(u, sb_ref[...]).astype(BF16)
    o_f = (_dot(ct_ref[...], uc) - _dot(st_ref[...], us)) * _silu(_tile(p_ref, T_FG, F32))
    o_ref[:, out(2)] = o_f.astype(o_ref.dtype)


def _ctx_mixers(p, conv_w, sink_b, ct, st, cb, sb, *, seq):
    m = p.shape[0]
    whole = lambda a: pl.BlockSpec(a.shape, lambda b: (0,) * a.ndim)
    return pl.pallas_call(
        _ctx_mixer_kernel,
        grid=(m // seq,),
        in_specs=[pl.BlockSpec((seq, P_WIDTH), lambda b: (b, 0)), whole(conv_w), whole(sink_b),
                  whole(ct), whole(st), whole(cb), whole(sb)],
        out_specs=pl.BlockSpec((seq, N_BRANCH * BRANCH_W), lambda b: (b, 0)),
        out_shape=jax.ShapeDtypeStruct((m, N_BRANCH * BRANCH_W), BF16),
        compiler_params=pltpu.CompilerParams(dimension_semantics=("parallel",)),
        name="ctx_mixers",
    )(p, conv_w, sink_b, ct, st, cb, sb)


HALO = 16


def _lat_conv_kernel(ax_ref, ab_ref, ac_ref, ag_ref, axp_ref, acp_ref, axn_ref, acn_ref, cw_ref, o_ref):
    i = pl.program_id(1)
    z = ac_ref[...].astype(F32) * ax_ref[...].astype(F32)
    zp = acp_ref[HALO - 1:HALO, :].astype(F32) * axp_ref[HALO - 1:HALO, :].astype(F32)
    zn = acn_ref[0:1, :].astype(F32) * axn_ref[0:1, :].astype(F32)
    zp = jnp.where(i == 0, 0.0, zp)
    zn = jnp.where(i == pl.num_programs(1) - 1, 0.0, zn)
    z_dn, z_up = _shift_rows(z, zp, zn)
    y = ab_ref[...].astype(F32) * (z_dn * cw_ref[0:1, :] + z * cw_ref[1:2, :] + z_up * cw_ref[2:3, :])
    o_ref[...] = (y * _silu(ag_ref[...].astype(F32))).astype(o_ref.dtype)


def _lat_conv(pa, conv_w, *, batch, seq, tr=512):
    nt = seq // tr
    hb = tr // HALO
    last_halo = batch * seq // HALO - 1
    main = lambda c: pl.BlockSpec((tr, BRANCH_W), functools.partial(lambda b, i, c: (b * nt + i, c), c=c))
    prev = lambda c: pl.BlockSpec(
        (HALO, BRANCH_W), functools.partial(lambda b, i, c: (jnp.maximum((b * nt + i) * hb - 1, 0), c), c=c))
    nxt = lambda c: pl.BlockSpec(
        (HALO, BRANCH_W), functools.partial(lambda b, i, c: (jnp.minimum((b * nt + i + 1) * hb, last_halo), c), c=c))
    return pl.pallas_call(
        _lat_conv_kernel,
        grid=(batch, nt),
        in_specs=[main(T_AX), main(T_AB), main(T_AC), main(T_AG), prev(T_AX), prev(T_AC), nxt(T_AX), nxt(T_AC),
                  pl.BlockSpec(conv_w.shape, lambda b, i: (0, 0))],
        out_specs=pl.BlockSpec((tr, BRANCH_W), lambda b, i: (b * nt + i, 0)),
        out_shape=jax.ShapeDtypeStruct((batch * seq, BRANCH_W), BF16),
        compiler_params=pltpu.CompilerParams(dimension_semantics=("parallel", "arbitrary")),
        name="lat_conv",
    )(pa, pa, pa, pa, pa, pa, pa, pa, conv_w)


def _rope(x, cos, sin_signed):
    lane = lax.broadcasted_iota(jnp.int32, x.shape, 1)
    quarter = HEAD_DIM // 4
    partner = jnp.where((lane % (2 * quarter)) < quarter,
                        pltpu.roll(x, LANES - quarter, axis=1), pltpu.roll(x, quarter, axis=1))
    return x * cos + partner * sin_signed


WIN_BLOCKS_PER_STEP = 8


def _lat_win_kernel(q_ref, g_ref, kp_ref, kc_ref, kn_ref, vp_ref, vc_ref, vn_ref,
                    cq_ref, sq_ref, cp_ref, sp_ref, cn_ref, sn_ref, ck_ref, cv_ref, sink_ref, o_ref):
    nq = WIN_BLOCKS_PER_STEP
    nb = pl.num_programs(1) * nq
    wb = WIN_BLOCK
    cq, sq = cq_ref[...], sq_ref[...]
    k_rot = jnp.concatenate([
        _rope(kp_ref[...].astype(F32), cp_ref[...], sp_ref[...]),
        _rope(kc_ref[...].astype(F32), cq, sq),
        _rope(kn_ref[...].astype(F32), cn_ref[...], sn_ref[...]),
    ], axis=0).astype(BF16)
    v_all = jnp.concatenate([vp_ref[...], vc_ref[...], vn_ref[...]], axis=0).astype(BF16)
    qi = lax.broadcasted_iota(jnp.int32, (wb, 3 * wb), 0)
    kj = lax.broadcasted_iota(jnp.int32, (wb, 3 * wb), 1)
    ck = ck_ref[0].astype(BF16)
    cv = cv_ref[0].astype(BF16)
    gsz = WIN_HEADS // WIN_KV_HEADS
    q_rot = [_scaled_queries(_rope(q_ref[:, pair * LANES:(pair + 1) * LANES].astype(F32), cq, sq)).astype(BF16)
             for pair in range(WIN_HEADS // 2)]
    scores, values, sinks = [], [], []
    for d in range(nq):
        n = pl.program_id(1) * nq + d
        mask = (((kj < wb) & (kj >= qi) & (n > 0)) | ((kj >= wb) & (kj < 2 * wb))
                | ((kj >= 2 * wb) & (kj - 2 * wb <= qi) & (n < nb - 1)))
        rows = slice(d * wb, (d + 1) * wb)
        win = slice(d * wb, (d + 3) * wb)
        for h in range(WIN_HEADS):
            kv = slice((h // gsz) * HEAD_DIM, (h // gsz + 1) * HEAD_DIM)
            q = q_rot[h // 2][rows, (h % 2) * HEAD_DIM:(h % 2 + 1) * HEAD_DIM]
            s_loc = jnp.where(mask, _dot_nt(q, k_rot[win, kv]), NEG_INF)
            scores.append([s_loc, _dot_nt(q, ck[:, kv])])
            values.append([v_all[win, kv], cv[:, kv]])
            sinks.append(sink_ref[h:h + 1, 0:1])
    heads = _attend(scores, values, sinks=sinks)
    o = jnp.concatenate([jnp.concatenate(heads[d * WIN_HEADS:(d + 1) * WIN_HEADS], axis=-1) for d in range(nq)], axis=0)
    o_ref[...] = (o * _silu(g_ref[...].astype(F32))).astype(o_ref.dtype)


def _lat_window(pb, cos_t, sin_t, ctx_k, ctx_v, sink_b, *, batch, seq):
    wb = WIN_BLOCK
    k_col = T_BKV * (TILE // KV_W)
    v_col = k_col + 1
    nq = WIN_BLOCKS_PER_STEP
    nb = seq // wb
    steps = nb // nq
    nbr = lambda s, d: jnp.clip(s * nq + (nq if d > 0 else -1), 0, nb - 1)
    own = lambda col, width: pl.BlockSpec(
        (nq * wb, width), functools.partial(lambda b, s, col: (b * steps + s, col), col=col))
    kv = lambda col, d: pl.BlockSpec(
        (wb, KV_W), functools.partial(lambda b, s, col, d: (b * nb + nbr(s, d), col), col=col, d=d))
    tab = lambda d: pl.BlockSpec((wb, LANES), functools.partial(lambda b, s, d: (nbr(s, d), 0), d=d))
    tab_own = pl.BlockSpec((nq * wb, LANES), lambda b, s: (s, 0))
    ctx = pl.BlockSpec((1,) + ctx_k.shape[1:], lambda b, s: (b, 0, 0))
    return pl.pallas_call(
        _lat_win_kernel,
        grid=(batch, steps),
        in_specs=[
            own(T_BQ, BRANCH_W), own(T_BG, BRANCH_W),
            kv(k_col, -1), own(k_col, KV_W), kv(k_col, 1), kv(v_col, -1), own(v_col, KV_W), kv(v_col, 1),
            tab_own, tab_own, tab(-1), tab(-1), tab(1), tab(1),
            ctx, ctx,
            pl.BlockSpec(sink_b.shape, lambda b, s: (0, 0)),
        ],
        out_specs=pl.BlockSpec((nq * wb, BRANCH_W), lambda b, s: (b * steps + s, 0)),
        out_shape=jax.ShapeDtypeStruct((batch * seq, BRANCH_W), BF16),
        compiler_params=pltpu.CompilerParams(dimension_semantics=("parallel", "arbitrary")),
        name="lat_window",
    )(pb, pb, pb, pb, pb, pb, pb, pb, cos_t, sin_t, cos_t, sin_t, cos_t, sin_t, ctx_k, ctx_v, sink_b)


def _fft1_kernel(u_ref, c_ref, s_ref, twr_ref, twi_ref, zr_ref, zi_ref, *, chunks):
    u = u_ref[0].astype(BF16)
    yr = _dot(c_ref[...], u)
    yi = -_dot(s_ref[...], u)
    for t in range(chunks):
        wr = twr_ref[0, :, t:t + 1]
        wi = twi_ref[0, :, t:t + 1]
        a = yr[:, t * BRANCH_W:(t + 1) * BRANCH_W]
        b = yi[:, t * BRANCH_W:(t + 1) * BRANCH_W]
        zr_ref[0, :, t * BRANCH_W:(t + 1) * BRANCH_W] = (a * wr - b * wi).astype(zr_ref.dtype)
        zi_ref[0, :, t * BRANCH_W:(t + 1) * BRANCH_W] = (a * wi + b * wr).astype(zi_ref.dtype)


def _fft2_kernel(zr_ref, zi_ref, g_ref, l_ref, cs_ref, o_ref, *, kblock):
    r = FFT_R
    zz = [jnp.concatenate([zr_ref[0, kk], zi_ref[0, kk]], axis=0) for kk in range(kblock)]
    xx = [_dot(l_ref[...], z) for z in zz]
    xcat = [jnp.concatenate([x[0:r], x[r:2 * r]], axis=1).astype(BF16) for x in xx]
    out = [_dot(x, cs_ref[...]) for x in xcat]
    for kk in range(kblock):
        sl = slice(kk * BRANCH_W, (kk + 1) * BRANCH_W)
        o_ref[0, :, sl] = (out[kk] * _silu(g_ref[0, :, sl].astype(F32))).astype(o_ref.dtype)


def _lat_fourier(fu, fg, tabs, *, batch, seq):
    r = FFT_R
    assert seq == r * r
    wide = r * BRANCH_W
    chunks = 8
    c64, s64, twr, twi, lmat, csmat = tabs
    u3 = fu.reshape(batch, r, wide)
    nct = r // chunks
    zr, zi = pl.pallas_call(
        functools.partial(_fft1_kernel, chunks=chunks),
        grid=(batch, nct),
        in_specs=[
            pl.BlockSpec((1, r, chunks * BRANCH_W), lambda b, t: (b, 0, t)),
            pl.BlockSpec((r, r), lambda b, t: (0, 0)),
            pl.BlockSpec((r, r), lambda b, t: (0, 0)),
            pl.BlockSpec((1, r, chunks), lambda b, t: (t, 0, 0)),
            pl.BlockSpec((1, r, chunks), lambda b, t: (t, 0, 0)),
        ],
        out_specs=[pl.BlockSpec((1, r, chunks * BRANCH_W), lambda b, t: (b, 0, t))] * 2,
        out_shape=[jax.ShapeDtypeStruct((batch, r, wide), BF16)] * 2,
        compiler_params=pltpu.CompilerParams(dimension_semantics=("parallel", "arbitrary")),
        name="lat_fft1",
    )(u3, c64, s64, twr, twi)
    kblock = 8
    z4 = lambda z: z.reshape(batch, r, r, BRANCH_W)
    out = pl.pallas_call(
        functools.partial(_fft2_kernel, kblock=kblock),
        grid=(batch, r // kblock),
        in_specs=[
            pl.BlockSpec((1, kblock, r, BRANCH_W), lambda b, k: (b, k, 0, 0)),
            pl.BlockSpec((1, kblock, r, BRANCH_W), lambda b, k: (b, k, 0, 0)),
            pl.BlockSpec((1, r, kblock * BRANCH_W), lambda b, k: (b, 0, k)),
            pl.BlockSpec((2 * r, 2 * r), lambda b, k: (0, 0)),
            pl.BlockSpec((2 * BRANCH_W, BRANCH_W), lambda b, k: (0, 0)),
        ],
        out_specs=pl.BlockSpec((1, r, kblock * BRANCH_W), lambda b, k: (b, 0, k)),
        out_shape=jax.ShapeDtypeStruct((batch, r, wide), BF16),
        compiler_params=pltpu.CompilerParams(dimension_semantics=("parallel", "arbitrary")),
        name="lat_fft2",
    )(z4(zr), z4(zi), fg.reshape(batch, r, wide), lmat, csmat)
    return out.reshape(batch * seq, BRANCH_W)


N_RPB_ROWS = 2 * NA_ROWS - 1
N_RPB_COLS = 2 * NA_COLS - 1


NA_ROWS_PER_STEP = 8


def _lat_na_kernel(q_ref, g_ref, k_ref, v_ref, bias_ref, ck_ref, cv_ref, o_ref, *, rows):
    ck = ck_ref[0].astype(BF16)
    cv = cv_ref[0].astype(BF16)
    sls = [slice(h * HEAD_DIM, (h + 1) * HEAD_DIM) for h in range(NA_HEADS)]
    scores, values = [], []
    for d in range(NA_ROWS_PER_STEP):
        r = pl.program_id(1) * NA_ROWS_PER_STEP + d
        r0 = jnp.clip(r - NA_ROWS // 2, 0, rows - NA_ROWS)
        start = pl.multiple_of(r0 * GRID_W, GRID_W)
        kw = k_ref[pl.ds(start, NA_ROWS * GRID_W), :]
        vw = v_ref[pl.ds(start, NA_ROWS * GRID_W), :]
        qa = _scaled_queries(q_ref[d * GRID_W:(d + 1) * GRID_W, :])
        a0 = r0 - r + NA_ROWS - 1
        for h, sl in enumerate(sls):
            bias = jnp.concatenate([bias_ref[h, a0 + 2 * i2] for i2 in range(NA_ROWS // 2)], axis=1)
            scores.append([_dot_nt(qa[:, sl], kw[:, sl]) + bias, _dot_nt(qa[:, sl], ck[:, sl])])
            values.append([vw[:, sl], cv[:, sl]])
    heads = _attend(scores, values)
    o = jnp.concatenate([jnp.concatenate(heads[d * NA_HEADS:(d + 1) * NA_HEADS], axis=-1)
                         for d in range(NA_ROWS_PER_STEP)], axis=0)
    o_ref[...] = (o * _silu(g_ref[...].astype(F32))).astype(o_ref.dtype)


def _lat_na(p, bias, ctx_k, ctx_v, *, batch, seq):
    rows = seq // GRID_W
    steps = rows // NA_ROWS_PER_STEP
    qrows = NA_ROWS_PER_STEP * GRID_W
    ctx = pl.BlockSpec((1,) + ctx_k.shape[1:], lambda b, r: (b, 0, 0))
    return pl.pallas_call(
        functools.partial(_lat_na_kernel, rows=rows),
        grid=(batch, steps),
        in_specs=[
            pl.BlockSpec((qrows, BRANCH_W), lambda b, r: (b * steps + r, T_DQ)),
            pl.BlockSpec((qrows, BRANCH_W), lambda b, r: (b * steps + r, T_DG)),
            pl.BlockSpec((seq, BRANCH_W), lambda b, r: (b, T_DK)),
            pl.BlockSpec((seq, BRANCH_W), lambda b, r: (b, T_DV)),
            pl.BlockSpec(bias.shape, lambda b, r: (0, 0, 0, 0)),
            ctx, ctx,
        ],
        out_specs=pl.BlockSpec((qrows, BRANCH_W), lambda b, r: (b * steps + r, 0)),
        out_shape=jax.ShapeDtypeStruct((batch * seq, BRANCH_W), BF16),
        compiler_params=pltpu.CompilerParams(dimension_semantics=("parallel", "arbitrary")),
        name="lat_na",
    )(p, p, p, p, bias, ctx_k, ctx_v)


def _toeplitz_kernel(r_ref, o_ref):
    x = r_ref[...]
    hi = x.astype(BF16)
    r1 = x - hi.astype(F32)
    mid = r1.astype(BF16)
    lo = (r1 - mid.astype(F32)).astype(BF16)
    shape = (x.shape[1], GRID_W * GRID_W)
    b = lax.broadcasted_iota(jnp.int32, shape, 0)
    col = lax.broadcasted_iota(jnp.int32, shape, 1)
    kc = jnp.bitwise_and(col, GRID_W - 1)
    qc = lax.shift_right_logical(col, GRID_W.bit_length() - 1)
    onehot = jnp.where(kc - qc + (NA_COLS - 1) == b, 1.0, 0.0).astype(BF16)
    o_ref[...] = _dot(hi, onehot) + _dot(mid, onehot) + _dot(lo, onehot)


def _na_bias_tables(na_rpb):
    depth = na_rpb.shape[0]
    n = depth * NA_HEADS * N_RPB_ROWS
    n_pad = -(-n // 8) * 8
    r2 = jnp.zeros((n_pad, LANES), F32).at[:n, :N_RPB_COLS].set(na_rpb.reshape(n, N_RPB_COLS))
    flat = pl.pallas_call(
        _toeplitz_kernel,
        out_shape=jax.ShapeDtypeStruct((n_pad, GRID_W * GRID_W), F32),
        name="na_bias_toeplitz",
    )(r2)
    t = flat[:n].reshape(depth, NA_HEADS, N_RPB_ROWS, GRID_W, GRID_W)
    qc = np.arange(GRID_W)[:, None]
    kc = np.arange(GRID_W)[None, :]
    win0 = np.clip(qc - NA_COLS // 2, 0, GRID_W - NA_COLS)
    vis = (kc >= win0) & (kc < win0 + NA_COLS)
    t = jnp.where(vis, t, NEG_INF)
    return jnp.concatenate([t[:, :, :-1], t[:, :, 1:]], axis=-1)


def _bf16_table(t):
    return jnp.asarray(t, F32).astype(BF16)


def _dft_cos_sin(n):
    idx = np.arange(n)
    ang = 2.0 * np.pi * ((idx[:, None] * idx[None, :]) % n) / n
    return np.cos(ang), np.sin(ang)


def _channel_dft_blocks(scale):
    c, s = _dft_cos_sin(FN_GROUP_W)
    eye = np.eye(FN_GROUPS)
    return np.kron(eye, c) * scale, np.kron(eye, s) * scale


def _ctx_fourier_tables(seq):
    ct, st = _dft_cos_sin(seq)
    cb, sb = _channel_dft_blocks(1.0 / np.sqrt(seq * FN_GROUP_W))
    return tuple(_bf16_table(t) for t in (ct, st, cb, sb))


def _lat_fourier_tables(seq, chunks=8):
    r = FFT_R
    c64, s64 = _dft_cos_sin(r)
    k1 = np.arange(r)[:, None]
    t2 = np.arange(r)[None, :]
    ang = 2.0 * np.pi * (k1 * t2) / (r * r)
    twr = np.cos(ang).reshape(r, r // chunks, chunks).transpose(1, 0, 2)
    twi = (-np.sin(ang)).reshape(r, r // chunks, chunks).transpose(1, 0, 2)
    lmat = np.block([[c64, s64], [-s64, c64]])
    cb, sb = _channel_dft_blocks(1.0 / np.sqrt(seq * FN_GROUP_W))
    csmat = np.concatenate([cb, sb], axis=0)
    return (_bf16_table(c64), _bf16_table(s64), jnp.asarray(twr, F32), jnp.asarray(twi, F32),
            _bf16_table(lmat), _bf16_table(csmat))


def _rope_tables(seq):
    half = HEAD_DIM // 2
    quarter = half // 2
    t = jnp.arange(seq)
    inv = ROPE_BASE ** (-jnp.arange(quarter, dtype=F32) / quarter)

    def cs(pos):
        ang = pos.astype(F32)[:, None] * inv[None, :]
        c, s = jnp.cos(ang), jnp.sin(ang)
        return jnp.concatenate([c, c], axis=-1), jnp.concatenate([-s, s], axis=-1)

    cr, sr = cs(t // GRID_W)
    cc, sc = cs(t % GRID_W)
    cos = jnp.concatenate([cr, cc], axis=-1)
    sin = jnp.concatenate([sr, sc], axis=-1)
    return jnp.tile(cos, (1, 2)), jnp.tile(sin, (1, 2))


HALF_TILE = TILE // 2


def _tile_sources():
    starts = ([OFF_A + i * TILE for i in range(4)] + [OFF_BQ, OFF_BG, OFF_BK]
              + [OFF_FU + i * TILE for i in range(6)] + [None] + [N_MAIN + i * TILE for i in range(N_BRANCH * D_MODEL // TILE)])
    src = []
    for t, start in enumerate(starts):
        halves = [None, None] if start is None else [start, start + HALF_TILE]
        if t == T_BKV:
            halves[1] = None
        for col in halves:
            assert col is None or col % HALF_TILE == 0
            src.append(-1 if col is None else col // HALF_TILE)
    assert len(starts) == N_TILES + N_BRANCH * D_MODEL // TILE
    return np.asarray(src, np.int32)


def _tile_weights_kernel(src_ref, lo_ref, hi_ref, o_ref):
    t = pl.program_id(0)
    zeros = jnp.zeros(lo_ref.shape[1:], BF16)
    o_ref[:, :HALF_TILE] = jnp.where(src_ref[2 * t] < 0, zeros, lo_ref[0].astype(BF16))
    o_ref[:, HALF_TILE:] = jnp.where(src_ref[2 * t + 1] < 0, zeros, hi_ref[0].astype(BF16))


def _tiled_in_weights(w_in, layer):
    src = _tile_sources()
    n_out_tiles = len(src) // 2
    half = lambda h: pl.BlockSpec(
        (1, D_MODEL, HALF_TILE), functools.partial(lambda t, s, h: (layer, 0, jnp.maximum(s[2 * t + h], 0)), h=h))
    return pl.pallas_call(
        _tile_weights_kernel,
        grid_spec=pltpu.PrefetchScalarGridSpec(
            num_scalar_prefetch=1,
            grid=(n_out_tiles,),
            in_specs=[half(0), half(1)],
            out_specs=pl.BlockSpec((D_MODEL, TILE), lambda t, s: (0, t)),
        ),
        out_shape=jax.ShapeDtypeStruct((D_MODEL, n_out_tiles * TILE), BF16),
        name="tile_weights",
    )(jnp.asarray(src), w_in, w_in)


def kernel(x_prompt, x_sample, cache_win_k, cache_win_v, cache_na_k, cache_na_v, c, c_ctx, norm_g, w_ada, b_ada,
           w_in, conv_w, win_sink, na_rpb, w_branch, w_out, final_g):
    batch, seq, _ = x_prompt.shape
    dbatch, dseq, _ = x_sample.shape
    past = cache_win_k.shape[2]

    cv8 = jnp.zeros((8, D_MODEL), F32).at[0].set(c_ctx).at[1:1 + dbatch].set(c)
    mod = _modulation(cv8, w_ada, b_ada)

    ctx_tabs = _ctx_fourier_tables(seq)
    lat_tabs = _lat_fourier_tables(dseq)
    cos_t, sin_t = _rope_tables(dseq)
    na_bias = _na_bias_tables(na_rpb)
    final_g2 = final_g.reshape(1, D_MODEL)

    xp = x_prompt.reshape(batch * seq, D_MODEL)
    xs = x_sample.reshape(dbatch * dseq, D_MODEL)
    new_kv = [jnp.zeros((batch, DEPTH, seq, width), F32) for width in (KV_W, KV_W, BRANCH_W, BRANCH_W)]

    for l in range(DEPTH):
        shift = mod[l, :, 0:D_MODEL].reshape(8, 1, D_MODEL)
        scale = mod[l, :, D_MODEL:2 * D_MODEL].reshape(8, 1, D_MODEL)
        gate = mod[l, :, 2 * D_MODEL:].reshape(8, 1, D_MODEL)
        g = norm_g[l].reshape(1, D_MODEL)
        w_p = _tiled_in_weights(w_in, l)
        w_br = w_branch[l].astype(BF16)
        w_o = w_out[l].astype(BF16)
        cw = conv_w[l]
        sink_b = jnp.broadcast_to(win_sink[l][:, None], (WIN_HEADS, LANES))
        final = l == DEPTH - 1

        p, hp, *new_kv = _inproj(
            xp, shift, scale, g, w_p, rows_per_cond=batch * seq, cond0=0, update=(new_kv, l, seq),
            extras=((T_BKV, 0, KV_W), (T_BKV, KV_W, KV_W), (T_DK, 0, BRANCH_W), (T_DV, 0, BRANCH_W)))
        o_ctx = _ctx_mixers(p, cw, sink_b, *ctx_tabs, seq=seq)
        xp = _out_stage(xp, hp, gate, final_g2, [o_ctx] * 4, [0, 1, 2, 3], w_p, w_br, w_o,
                        rows_per_cond=batch * seq, cond0=0, final=final)

        q, hs, qfu, qfg = _inproj(xs, shift, scale, g, w_p, rows_per_cond=dseq, cond0=1, extra_dtype=BF16,
                              extras=((T_FU, 0, BRANCH_W), (T_FG, 0, BRANCH_W)))
        o_a = _lat_conv(q, cw, batch=dbatch, seq=dseq)
        o_w = _lat_window(q, cos_t, sin_t,
                          cache_win_k[:, l].reshape(dbatch, past, KV_W), cache_win_v[:, l].reshape(dbatch, past, KV_W),
                          sink_b, batch=dbatch, seq=dseq)
        o_f = _lat_fourier(qfu, qfg, lat_tabs, batch=dbatch, seq=dseq)
        o_n = _lat_na(q, na_bias[l],
                      cache_na_k[:, l].reshape(dbatch, past, BRANCH_W), cache_na_v[:, l].reshape(dbatch, past, BRANCH_W),
                      batch=dbatch, seq=dseq)
        xs = _out_stage(xs, hs, gate, final_g2, [o_a, o_w, o_f, o_n], [0, 0, 0, 0], w_p, w_br, w_o,
                        rows_per_cond=dseq, cond0=1, final=final)

    y_prompt = xp.reshape(batch, seq, D_MODEL)
    y_sample = xs.reshape(dbatch, dseq, D_MODEL)
    heads = (WIN_KV_HEADS, WIN_KV_HEADS, NA_HEADS, NA_HEADS)
    return (y_prompt, y_sample) + tuple(a.reshape(batch, DEPTH, seq, n, HEAD_DIM) for a, n in zip(new_kv, heads))
```

```python
import functools
import math

import numpy as np
import jax
import jax.numpy as jnp
from jax import lax
from jax.experimental import pallas as pl
from jax.experimental.pallas import tpu as pltpu

D_MODEL = 2048
DEPTH = 2
GRID_W = 64
N_BRANCH = 4
BRANCH_W = 512
HEAD_DIM = 64
WIN_HEADS = 8
WIN_KV_HEADS = 2
KV_W = WIN_KV_HEADS * HEAD_DIM
WIN_BLOCK = 128
FN_GROUPS = 4
FN_GROUP_W = BRANCH_W // FN_GROUPS
NA_HEADS = 8
NA_ROWS = 8
NA_COLS = 16
ROPE_BASE = 10000.0
EPS = 1e-6
ATTN_SCALE = HEAD_DIM ** -0.5
NEG_INF = -1e30
N_MAIN = 6400

F32 = jnp.float32
BF16 = jnp.bfloat16
LANES = 128

OFF_A = 0
OFF_BQ = OFF_A + 4 * BRANCH_W
OFF_BK = OFF_BQ + BRANCH_W
OFF_BG = OFF_BK + 2 * KV_W
OFF_FU = OFF_BG + BRANCH_W
assert OFF_FU + 6 * BRANCH_W == N_MAIN

TILE = 512
T_AX, T_AB, T_AC, T_AG, T_BQ, T_BG, T_BKV, T_FU, T_FG, T_DQ, T_DK, T_DV, T_DG = range(13)
N_USED_TILES = 13
STEP_TILES = 2
N_TILES = 14
P_WIDTH = N_TILES * TILE

V7X_VMEM_BYTES = 64 * 2 ** 20
PROJ_VMEM_BYTES = V7X_VMEM_BYTES - 2 ** 20

FFT_R = 64


def _silu(x):
    return x * jax.nn.sigmoid(x)


def _dot(a, b):
    return jnp.dot(a, b, preferred_element_type=F32)


def _dot_nt(a, b):
    return lax.dot_general(a, b, (((1,), (1,)), ((), ())), preferred_element_type=F32)


def _rms_mod(x, g, scale, shift):
    ms = jnp.mean(x * x, axis=-1, keepdims=True)
    return (x * lax.rsqrt(ms + EPS) * g) * (1.0 + scale) + shift


NORM_ROWS = 256


def _mod_kernel(cv_ref, w_ref, b_ref, o_ref):
    s = _silu(cv_ref[...]).astype(BF16)
    o_ref[0] = _dot(s, w_ref[0].astype(BF16)) + b_ref[0]


def _modulation(cv8, w_ada, b_ada):
    tn = 512
    n3 = 3 * D_MODEL
    return pl.pallas_call(
        _mod_kernel,
        grid=(DEPTH, n3 // tn),
        in_specs=[
            pl.BlockSpec((8, D_MODEL), lambda l, j: (0, 0)),
            pl.BlockSpec((1, D_MODEL, tn), lambda l, j: (l, 0, j)),
            pl.BlockSpec((1, 1, tn), lambda l, j: (l, 0, j)),
        ],
        out_specs=pl.BlockSpec((1, 8, tn), lambda l, j: (l, 0, j)),
        out_shape=jax.ShapeDtypeStruct((DEPTH, 8, n3), F32),
        name="modulation",
    )(cv8, w_ada, b_ada.reshape(DEPTH, 1, n3))


def _inproj_kernel(x_ref, sh_ref, sc_ref, g_ref, w_ref, *rest, extras, n_updated):
    p_ref, h_ref, extra_refs = rest[n_updated], rest[n_updated + 1], rest[n_updated + 2:]
    j = pl.program_id(1)
    last = pl.num_programs(1) - 1
    assert N_USED_TILES == N_TILES - 1 and STEP_TILES == 2
    assert all(STEP_TILES <= tile < N_TILES - STEP_TILES for tile, _ in extras)

    @pl.when(j == last)
    def _():
        p_ref[:, :TILE] = _dot(h_ref[...], w_ref[:, :TILE]).astype(p_ref.dtype)
        p_ref[:, TILE:] = jnp.zeros((p_ref.shape[0], TILE), p_ref.dtype)

    @pl.when(j == 0)
    def _():
        g, scale, shift = g_ref[...], sc_ref[0], sh_ref[0]
        for c in range(h_ref.shape[0] // NORM_ROWS):
            rows = slice(c * NORM_ROWS, (c + 1) * NORM_ROWS)
            h = _rms_mod(x_ref[rows, :], g, scale, shift).astype(h_ref.dtype)
            h_ref[rows, :] = h
            p_ref[rows, :] = _dot(h, w_ref[...]).astype(p_ref.dtype)

    @pl.when((j > 0) & (j < last))
    def _():
        acc = _dot(h_ref[...], w_ref[...])
        p_ref[...] = acc.astype(p_ref.dtype)
        for ref, (tile, lo) in zip(extra_refs, extras):
            @pl.when(j == tile // STEP_TILES)
            def _(ref=ref, lo=(tile % STEP_TILES) * TILE + lo):
                cols = acc[:, lo:lo + ref.shape[-1]].astype(ref.dtype)
                if ref.ndim == 2:
                    ref[...] = cols
                else:
                    ref[:, 0] = cols.reshape(ref.shape[0], ref.shape[2], ref.shape[3])


def _inproj(x2d, shift, scale, g, w, *, rows_per_cond, cond0, extras, extra_dtype=None, update=None, tm=1024):
    m = x2d.shape[0]
    assert w.shape[0] == D_MODEL and w.shape[1] >= P_WIDTH and m % tm == 0 and rows_per_cond % tm == 0
    tiles_per_cond = rows_per_cond // tm
    cond = lambda i, j: (cond0 + i // tiles_per_cond, 0, 0)
    if update is None:
        updated, aliases = [], {}
        extra_specs = [pl.BlockSpec((tm, width), lambda i, j: (i, 0)) for _, _, width in extras]
        extra_shapes = [jax.ShapeDtypeStruct((m, width), extra_dtype) for _, _, width in extras]
    else:
        updated, layer, seq = update
        assert tm % seq == 0 and all(a.shape[2:] == (seq, width) for a, (_, _, width) in zip(updated, extras))
        extra_specs = [pl.BlockSpec((tm // seq, 1, seq, width), lambda i, j: (i, layer, 0, 0)) for _, _, width in extras]
        extra_shapes = [jax.ShapeDtypeStruct(a.shape, a.dtype) for a in updated]
        aliases = {5 + k: 2 + k for k in range(len(updated))}
    return pl.pallas_call(
        functools.partial(_inproj_kernel, extras=tuple((t, lo) for t, lo, _ in extras), n_updated=len(updated)),
        grid=(m // tm, N_TILES // STEP_TILES),
        in_specs=[
            pl.BlockSpec((tm, D_MODEL), lambda i, j: (i, 0)),
            pl.BlockSpec((1, 1, D_MODEL), cond),
            pl.BlockSpec((1, 1, D_MODEL), cond),
            pl.BlockSpec((1, D_MODEL), lambda i, j: (0, 0)),
            pl.BlockSpec((D_MODEL, STEP_TILES * TILE), lambda i, j: (0, j)),
        ] + [pl.BlockSpec(memory_space=pl.ANY) for _ in updated],
        out_specs=[pl.BlockSpec((tm, STEP_TILES * TILE), lambda i, j: (i, j)),
                   pl.BlockSpec((tm, D_MODEL), lambda i, j: (i, 0))]
        + extra_specs,
        out_shape=[jax.ShapeDtypeStruct((m, P_WIDTH), BF16), jax.ShapeDtypeStruct((m, D_MODEL), BF16)] + extra_shapes,
        input_output_aliases=aliases,
        compiler_params=pltpu.CompilerParams(dimension_semantics=("parallel", "arbitrary"),
                                             vmem_limit_bytes=PROJ_VMEM_BYTES),
        name="inproj",
    )(x2d, shift, scale, g, w, *updated)


def _out_kernel(x_ref, h_ref, gt_ref, fg_ref, oa_ref, ow_ref, of_ref, on_ref,
                wg0_ref, wg1_ref, wg2_ref, wg3_ref, wb_ref, wo_ref, y_ref, merged_ref, *, final):
    j = pl.program_id(1)
    h = h_ref[...]
    merged = None
    for i, (o_ref, wg_ref) in enumerate(((oa_ref, wg0_ref), (ow_ref, wg1_ref), (of_ref, wg2_ref), (on_ref, wg3_ref))):
        term = jax.nn.sigmoid(_dot(h, wg_ref[...])) * _dot(o_ref[...], wb_ref[i])
        merged = term if merged is None else merged + term
    merged_ref[j] = merged.astype(BF16)

    @pl.when(j == pl.num_programs(1) - 1)
    def _():
        m_all = jnp.concatenate([merged_ref[k] for k in range(merged_ref.shape[0])], axis=1)
        xn = x_ref[...] + gt_ref[0] * _dot(m_all, wo_ref[...])
        if final:
            ms = jnp.mean(xn * xn, axis=-1, keepdims=True)
            xn = xn * lax.rsqrt(ms + EPS) * fg_ref[...]
        y_ref[...] = xn


def _out_stage(x2d, h2d, gate, final_g, o_arrays, o_cols, w_gate, w_branch, w_out,
               *, rows_per_cond, cond0, final, tm=512, tc=512):
    m = x2d.shape[0]
    assert m % tm == 0 and rows_per_cond % tm == 0
    tiles_per_cond = rows_per_cond // tm
    ncol = D_MODEL // tc
    cond = lambda i, j: (cond0 + i // tiles_per_cond, 0, 0)
    o_specs = [pl.BlockSpec((tm, BRANCH_W), functools.partial(lambda i, j, c: (i, c), c=c)) for c in o_cols]
    assert tc == TILE and w_gate.shape == (D_MODEL, P_WIDTH + N_BRANCH * D_MODEL)
    wg_specs = [pl.BlockSpec((D_MODEL, tc), functools.partial(lambda i, j, b: (0, N_TILES + b * ncol + j), b=b))
                for b in range(N_BRANCH)]
    return pl.pallas_call(
        functools.partial(_out_kernel, final=final),
        grid=(m // tm, ncol),
        in_specs=[
            pl.BlockSpec((tm, D_MODEL), lambda i, j: (i, 0)),
            pl.BlockSpec((tm, D_MODEL), lambda i, j: (i, 0)),
            pl.BlockSpec((1, 1, D_MODEL), cond),
            pl.BlockSpec((1, D_MODEL), lambda i, j: (0, 0)),
            *o_specs,
            *wg_specs,
            pl.BlockSpec((N_BRANCH, BRANCH_W, tc), lambda i, j: (0, 0, j)),
            pl.BlockSpec((D_MODEL, D_MODEL), lambda i, j: (0, 0), pipeline_mode=pl.Buffered(1)),
        ],
        out_specs=pl.BlockSpec((tm, D_MODEL), lambda i, j: (i, 0)),
        out_shape=jax.ShapeDtypeStruct((m, D_MODEL), F32),
        scratch_shapes=[pltpu.VMEM((ncol, tm, tc), BF16)],
        compiler_params=pltpu.CompilerParams(dimension_semantics=("parallel", "arbitrary"),
                                             vmem_limit_bytes=PROJ_VMEM_BYTES),
        name="out_stage",
    )(x2d, h2d, gate, final_g, *o_arrays, w_gate, w_gate, w_gate, w_gate, w_branch, w_out)


def _attend(scores, values, sinks=None):
    heads = range(len(scores))
    sinks = [None] * len(scores) if sinks is None else sinks
    m = []
    for h in heads:
        mh = scores[h][0].max(axis=-1, keepdims=True)
        for s in scores[h][1:]:
            mh = jnp.maximum(mh, s.max(axis=-1, keepdims=True))
        m.append(mh if sinks[h] is None else jnp.maximum(mh, sinks[h]))
    e = [[jnp.exp(s - m[h]) for s in scores[h]] for h in heads]
    den = []
    for h in heads:
        d = e[h][0].sum(axis=-1, keepdims=True)
        for x in e[h][1:]:
            d = d + x.sum(axis=-1, keepdims=True)
        den.append(d if sinks[h] is None else d + jnp.exp(sinks[h] - m[h]))
    out = []
    for h in heads:
        acc = _dot(e[h][0].astype(BF16), values[h][0])
        for x, v in zip(e[h][1:], values[h][1:]):
            acc = acc + _dot(x.astype(BF16), v)
        out.append(acc)
    return [a / d for a, d in zip(out, den)]


def _shift_rows(z, first_row, last_row):
    n = z.shape[0]
    row = lax.broadcasted_iota(jnp.int32, z.shape, 0)
    z_dn = jnp.where(row == 0, first_row, pltpu.roll(z, 1, axis=0))
    z_up = jnp.where(row == n - 1, last_row, pltpu.roll(z, n - 1, axis=0))
    return z_dn, z_up


def _tile(ref, t, dtype=None):
    v = ref[:, t * TILE:(t + 1) * TILE]
    return v if dtype is None else v.astype(dtype)


def _head(ref, t, h, base=0):
    lo = t * TILE + base + h * HEAD_DIM
    return ref[:, lo:lo + HEAD_DIM]


def _scaled_queries(q):
    assert math.frexp(ATTN_SCALE)[0] == 0.5
    return q * ATTN_SCALE


def _ctx_mixer_kernel(p_ref, cw_ref, sink_ref, ct_ref, st_ref, cb_ref, sb_ref, o_ref):
    z = _tile(p_ref, T_AC, F32) * _tile(p_ref, T_AX, F32)
    zero_row = jnp.zeros((1, BRANCH_W), F32)
    z_dn, z_up = _shift_rows(z, zero_row, zero_row)
    y = _tile(p_ref, T_AB, F32) * (z_dn * cw_ref[0:1, :] + z * cw_ref[1:2, :] + z_up * cw_ref[2:3, :])
    out = lambda b: slice(b * BRANCH_W, (b + 1) * BRANCH_W)
    o_ref[:, out(0)] = (y * _silu(_tile(p_ref, T_AG, F32))).astype(o_ref.dtype)

    gsz = WIN_HEADS // WIN_KV_HEADS
    head = lambda a, h: a[:, h * HEAD_DIM:(h + 1) * HEAD_DIM]
    qb = _scaled_queries(_tile(p_ref, T_BQ))
    qd = _scaled_queries(_tile(p_ref, T_DQ))
    scores = ([[_dot_nt(head(qb, h), _head(p_ref, T_BKV, h // gsz))] for h in range(WIN_HEADS)]
              + [[_dot_nt(head(qd, h), _head(p_ref, T_DK, h))] for h in range(NA_HEADS)])
    values = ([[_head(p_ref, T_BKV, h // gsz, base=KV_W)] for h in range(WIN_HEADS)]
              + [[_head(p_ref, T_DV, h)] for h in range(NA_HEADS)])
    sinks = [sink_ref[h:h + 1, 0:1] for h in range(WIN_HEADS)] + [None] * NA_HEADS
    heads = _attend(scores, values, sinks=sinks)
    o_w = jnp.concatenate(heads[:WIN_HEADS], axis=-1) * _silu(_tile(p_ref, T_BG, F32))
    o_ref[:, out(1)] = o_w.astype(o_ref.dtype)
    o_n = jnp.concatenate(heads[WIN_HEADS:], axis=-1) * _silu(_tile(p_ref, T_DG, F32))
    o_ref[:, out(3)] = o_n.astype(o_ref.dtype)

    u = _tile(p_ref, T_FU)
    uc = _dot(u, cb_ref[...]).astype(BF16)
    us = _dot(u, sb_ref[...]).astype(BF16)
    o_f = (_dot(ct_ref[...], uc) - _dot(st_ref[...], us)) * _silu(_tile(p_ref, T_FG, F32))
    o_ref[:, out(2)] = o_f.astype(o_ref.dtype)


def _ctx_mixers(p, conv_w, sink_b, ct, st, cb, sb, *, seq):
    m = p.shape[0]
    whole = lambda a: pl.BlockSpec(a.shape, lambda b: (0,) * a.ndim)
    return pl.pallas_call(
        _ctx_mixer_kernel,
        grid=(m // seq,),
        in_specs=[pl.BlockSpec((seq, P_WIDTH), lambda b: (b, 0)), whole(conv_w), whole(sink_b),
                  whole(ct), whole(st), whole(cb), whole(sb)],
        out_specs=pl.BlockSpec((seq, N_BRANCH * BRANCH_W), lambda b: (b, 0)),
        out_shape=jax.ShapeDtypeStruct((m, N_BRANCH * BRANCH_W), BF16),
        compiler_params=pltpu.CompilerParams(dimension_semantics=("parallel",)),
        name="ctx_mixers",
    )(p, conv_w, sink_b, ct, st, cb, sb)


HALO = 16


def _lat_conv_kernel(ax_ref, ab_ref, ac_ref, ag_ref, axp_ref, acp_ref, axn_ref, acn_ref, cw_ref, o_ref):
    i = pl.program_id(1)
    z = ac_ref[...].astype(F32) * ax_ref[...].astype(F32)
    zp = acp_ref[HALO - 1:HALO, :].astype(F32) * axp_ref[HALO - 1:HALO, :].astype(F32)
    zn = acn_ref[0:1, :].astype(F32) * axn_ref[0:1, :].astype(F32)
    zp = jnp.where(i == 0, 0.0, zp)
    zn = jnp.where(i == pl.num_programs(1) - 1, 0.0, zn)
    z_dn, z_up = _shift_rows(z, zp, zn)
    y = ab_ref[...].astype(F32) * (z_dn * cw_ref[0:1, :] + z * cw_ref[1:2, :] + z_up * cw_ref[2:3, :])
    o_ref[...] = (y * _silu(ag_ref[...].astype(F32))).astype(o_ref.dtype)


def _lat_conv(pa, conv_w, *, batch, seq, tr=512):
    nt = seq // tr
    hb = tr // HALO
    last_halo = batch * seq // HALO - 1
    main = lambda c: pl.BlockSpec((tr, BRANCH_W), functools.partial(lambda b, i, c: (b * nt + i, c), c=c))
    prev = lambda c: pl.BlockSpec(
        (HALO, BRANCH_W), functools.partial(lambda b, i, c: (jnp.maximum((b * nt + i) * hb - 1, 0), c), c=c))
    nxt = lambda c: pl.BlockSpec(
        (HALO, BRANCH_W), functools.partial(lambda b, i, c: (jnp.minimum((b * nt + i + 1) * hb, last_halo), c), c=c))
    return pl.pallas_call(
        _lat_conv_kernel,
        grid=(batch, nt),
        in_specs=[main(T_AX), main(T_AB), main(T_AC), main(T_AG), prev(T_AX), prev(T_AC), nxt(T_AX), nxt(T_AC),
                  pl.BlockSpec(conv_w.shape, lambda b, i: (0, 0))],
        out_specs=pl.BlockSpec((tr, BRANCH_W), lambda b, i: (b * nt + i, 0)),
        out_shape=jax.ShapeDtypeStruct((batch * seq, BRANCH_W), BF16),
        compiler_params=pltpu.CompilerParams(dimension_semantics=("parallel", "arbitrary")),
        name="lat_conv",
    )(pa, pa, pa, pa, pa, pa, pa, pa, conv_w)


def _rope(x, cos, sin_signed):
    lane = lax.broadcasted_iota(jnp.int32, x.shape, 1)
    quarter = HEAD_DIM // 4
    partner = jnp.where((lane % (2 * quarter)) < quarter,
                        pltpu.roll(x, LANES - quarter, axis=1), pltpu.roll(x, quarter, axis=1))
    return x * cos + partner * sin_signed


WIN_BLOCKS_PER_STEP = 8


def _lat_win_kernel(q_ref, g_ref, kp_ref, kc_ref, kn_ref, vp_ref, vc_ref, vn_ref,
                    cq_ref, sq_ref, cp_ref, sp_ref, cn_ref, sn_ref, ck_ref, cv_ref, sink_ref, o_ref):
    nq = WIN_BLOCKS_PER_STEP
    nb = pl.num_programs(1) * nq
    wb = WIN_BLOCK
    cq, sq = cq_ref[...], sq_ref[...]
    k_rot = jnp.concatenate([
        _rope(kp_ref[...].astype(F32), cp_ref[...], sp_ref[...]),
        _rope(kc_ref[...].astype(F32), cq, sq),
        _rope(kn_ref[...].astype(F32), cn_ref[...], sn_ref[...]),
    ], axis=0).astype(BF16)
    v_all = jnp.concatenate([vp_ref[...], vc_ref[...], vn_ref[...]], axis=0).astype(BF16)
    qi = lax.broadcasted_iota(jnp.int32, (wb, 3 * wb), 0)
    kj = lax.broadcasted_iota(jnp.int32, (wb, 3 * wb), 1)
    ck = ck_ref[0].astype(BF16)
    cv = cv_ref[0].astype(BF16)
    gsz = WIN_HEADS // WIN_KV_HEADS
    q_rot = [_scaled_queries(_rope(q_ref[:, pair * LANES:(pair + 1) * LANES].astype(F32), cq, sq)).astype(BF16)
             for pair in range(WIN_HEADS // 2)]
    scores, values, sinks = [], [], []
    for d in range(nq):
        n = pl.program_id(1) * nq + d
        mask = (((kj < wb) & (kj >= qi) & (n > 0)) | ((kj >= wb) & (kj < 2 * wb))
                | ((kj >= 2 * wb) & (kj - 2 * wb <= qi) & (n < nb - 1)))
        rows = slice(d * wb, (d + 1) * wb)
        win = slice(d * wb, (d + 3) * wb)
        for h in range(WIN_HEADS):
            kv = slice((h // gsz) * HEAD_DIM, (h // gsz + 1) * HEAD_DIM)
            q = q_rot[h // 2][rows, (h % 2) * HEAD_DIM:(h % 2 + 1) * HEAD_DIM]
            s_loc = jnp.where(mask, _dot_nt(q, k_rot[win, kv]), NEG_INF)
            scores.append([s_loc, _dot_nt(q, ck[:, kv])])
            values.append([v_all[win, kv], cv[:, kv]])
            sinks.append(sink_ref[h:h + 1, 0:1])
    heads = _attend(scores, values, sinks=sinks)
    o = jnp.concatenate([jnp.concatenate(heads[d * WIN_HEADS:(d + 1) * WIN_HEADS], axis=-1) for d in range(nq)], axis=0)
    o_ref[...] = (o * _silu(g_ref[...].astype(F32))).astype(o_ref.dtype)


def _lat_window(pb, cos_t, sin_t, ctx_k, ctx_v, sink_b, *, batch, seq):
    wb = WIN_BLOCK
    k_col = T_BKV * (TILE // KV_W)
    v_col = k_col + 1
    nq = WIN_BLOCKS_PER_STEP
    nb = seq // wb
    steps = nb // nq
    nbr = lambda s, d: jnp.clip(s * nq + (nq if d > 0 else -1), 0, nb - 1)
    own = lambda col, width: pl.BlockSpec(
        (nq * wb, width), functools.partial(lambda b, s, col: (b * steps + s, col), col=col))
    kv = lambda col, d: pl.BlockSpec(
        (wb, KV_W), functools.partial(lambda b, s, col, d: (b * nb + nbr(s, d), col), col=col, d=d))
    tab = lambda d: pl.BlockSpec((wb, LANES), functools.partial(lambda b, s, d: (nbr(s, d), 0), d=d))
    tab_own = pl.BlockSpec((nq * wb, LANES), lambda b, s: (s, 0))
    ctx = pl.BlockSpec((1,) + ctx_k.shape[1:], lambda b, s: (b, 0, 0))
    return pl.pallas_call(
        _lat_win_kernel,
        grid=(batch, steps),
        in_specs=[
            own(T_BQ, BRANCH_W), own(T_BG, BRANCH_W),
            kv(k_col, -1), own(k_col, KV_W), kv(k_col, 1), kv(v_col, -1), own(v_col, KV_W), kv(v_col, 1),
            tab_own, tab_own, tab(-1), tab(-1), tab(1), tab(1),
            ctx, ctx,
            pl.BlockSpec(sink_b.shape, lambda b, s: (0, 0)),
        ],
        out_specs=pl.BlockSpec((nq * wb, BRANCH_W), lambda b, s: (b * steps + s, 0)),
        out_shape=jax.ShapeDtypeStruct((batch * seq, BRANCH_W), BF16),
        compiler_params=pltpu.CompilerParams(dimension_semantics=("parallel", "arbitrary")),
        name="lat_window",
    )(pb, pb, pb, pb, pb, pb, pb, pb, cos_t, sin_t, cos_t, sin_t, cos_t, sin_t, ctx_k, ctx_v, sink_b)


def _fft1_kernel(u_ref, c_ref, s_ref, twr_ref, twi_ref, zr_ref, zi_ref, *, chunks):
    u = u_ref[0].astype(BF16)
    yr = _dot(c_ref[...], u)
    yi = -_dot(s_ref[...], u)
    for t in range(chunks):
        wr = twr_ref[0, :, t:t + 1]
        wi = twi_ref[0, :, t:t + 1]
        a = yr[:, t * BRANCH_W:(t + 1) * BRANCH_W]
        b = yi[:, t * BRANCH_W:(t + 1) * BRANCH_W]
        zr_ref[0, :, t * BRANCH_W:(t + 1) * BRANCH_W] = (a * wr - b * wi).astype(zr_ref.dtype)
        zi_ref[0, :, t * BRANCH_W:(t + 1) * BRANCH_W] = (a * wi + b * wr).astype(zi_ref.dtype)


def _fft2_kernel(zr_ref, zi_ref, g_ref, l_ref, cs_ref, o_ref, *, kblock):
    r = FFT_R
    zz = [jnp.concatenate([zr_ref[0, kk], zi_ref[0, kk]], axis=0) for kk in range(kblock)]
    xx = [_dot(l_ref[...], z) for z in zz]
    xcat = [jnp.concatenate([x[0:r], x[r:2 * r]], axis=1).astype(BF16) for x in xx]
    out = [_dot(x, cs_ref[...]) for x in xcat]
    for kk in range(kblock):
        sl = slice(kk * BRANCH_W, (kk + 1) * BRANCH_W)
        o_ref[0, :, sl] = (out[kk] * _silu(g_ref[0, :, sl].astype(F32))).astype(o_ref.dtype)


def _lat_fourier(fu, fg, tabs, *, batch, seq):
    r = FFT_R
    assert seq == r * r
    wide = r * BRANCH_W
    chunks = 8
    c64, s64, twr, twi, lmat, csmat = tabs
    u3 = fu.reshape(batch, r, wide)
    nct = r // chunks
    zr, zi = pl.pallas_call(
        functools.partial(_fft1_kernel, chunks=chunks),
        grid=(batch, nct),
        in_specs=[
            pl.BlockSpec((1, r, chunks * BRANCH_W), lambda b, t: (b, 0, t)),
            pl.BlockSpec((r, r), lambda b, t: (0, 0)),
            pl.BlockSpec((r, r), lambda b, t: (0, 0)),
            pl.BlockSpec((1, r, chunks), lambda b, t: (t, 0, 0)),
            pl.BlockSpec((1, r, chunks), lambda b, t: (t, 0, 0)),
        ],
        out_specs=[pl.BlockSpec((1, r, chunks * BRANCH_W), lambda b, t: (b, 0, t))] * 2,
        out_shape=[jax.ShapeDtypeStruct((batch, r, wide), BF16)] * 2,
        compiler_params=pltpu.CompilerParams(dimension_semantics=("parallel", "arbitrary")),
        name="lat_fft1",
    )(u3, c64, s64, twr, twi)
    kblock = 8
    z4 = lambda z: z.reshape(batch, r, r, BRANCH_W)
    out = pl.pallas_call(
        functools.partial(_fft2_kernel, kblock=kblock),
        grid=(batch, r // kblock),
        in_specs=[
            pl.BlockSpec((1, kblock, r, BRANCH_W), lambda b, k: (b, k, 0, 0)),
            pl.BlockSpec((1, kblock, r, BRANCH_W), lambda b, k: (b, k, 0, 0)),
            pl.BlockSpec((1, r, kblock * BRANCH_W), lambda b, k: (b, 0, k)),
            pl.BlockSpec((2 * r, 2 * r), lambda b, k: (0, 0)),
            pl.BlockSpec((2 * BRANCH_W, BRANCH_W), lambda b, k: (0, 0)),
        ],
        out_specs=pl.BlockSpec((1, r, kblock * BRANCH_W), lambda b, k: (b, 0, k)),
        out_shape=jax.ShapeDtypeStruct((batch, r, wide), BF16),
        compiler_params=pltpu.CompilerParams(dimension_semantics=("parallel", "arbitrary")),
        name="lat_fft2",
    )(z4(zr), z4(zi), fg.reshape(batch, r, wide), lmat, csmat)
    return out.reshape(batch * seq, BRANCH_W)


N_RPB_ROWS = 2 * NA_ROWS - 1
N_RPB_COLS = 2 * NA_COLS - 1


NA_ROWS_PER_STEP = 8


def _lat_na_kernel(q_ref, g_ref, k_ref, v_ref, bias_ref, ck_ref, cv_ref, o_ref, *, rows):
    ck = ck_ref[0].astype(BF16)
    cv = cv_ref[0].astype(BF16)
    sls = [slice(h * HEAD_DIM, (h + 1) * HEAD_DIM) for h in range(NA_HEADS)]
    scores, values = [], []
    for d in range(NA_ROWS_PER_STEP):
        r = pl.program_id(1) * NA_ROWS_PER_STEP + d
        r0 = jnp.clip(r - NA_ROWS // 2, 0, rows - NA_ROWS)
        start = pl.multiple_of(r0 * GRID_W, GRID_W)
        kw = k_ref[pl.ds(start, NA_ROWS * GRID_W), :]
        vw = v_ref[pl.ds(start, NA_ROWS * GRID_W), :]
        qa = _scaled_queries(q_ref[d * GRID_W:(d + 1) * GRID_W, :])
        a0 = r0 - r + NA_ROWS - 1
        for h, sl in enumerate(sls):
            bias = jnp.concatenate([bias_ref[h, a0 + 2 * i2] for i2 in range(NA_ROWS // 2)], axis=1)
            scores.append([_dot_nt(qa[:, sl], kw[:, sl]) + bias, _dot_nt(qa[:, sl], ck[:, sl])])
            values.append([vw[:, sl], cv[:, sl]])
    heads = _attend(scores, values)
    o = jnp.concatenate([jnp.concatenate(heads[d * NA_HEADS:(d + 1) * NA_HEADS], axis=-1)
                         for d in range(NA_ROWS_PER_STEP)], axis=0)
    o_ref[...] = (o * _silu(g_ref[...].astype(F32))).astype(o_ref.dtype)


def _lat_na(p, bias, ctx_k, ctx_v, *, batch, seq):
    rows = seq // GRID_W
    steps = rows // NA_ROWS_PER_STEP
    qrows = NA_ROWS_PER_STEP * GRID_W
    ctx = pl.BlockSpec((1,) + ctx_k.shape[1:], lambda b, r: (b, 0, 0))
    return pl.pallas_call(
        functools.partial(_lat_na_kernel, rows=rows),
        grid=(batch, steps),
        in_specs=[
            pl.BlockSpec((qrows, BRANCH_W), lambda b, r: (b * steps + r, T_DQ)),
            pl.BlockSpec((qrows, BRANCH_W), lambda b, r: (b * steps + r, T_DG)),
            pl.BlockSpec((seq, BRANCH_W), lambda b, r: (b, T_DK)),
            pl.BlockSpec((seq, BRANCH_W), lambda b, r: (b, T_DV)),
            pl.BlockSpec(bias.shape, lambda b, r: (0, 0, 0, 0)),
            ctx, ctx,
        ],
        out_specs=pl.BlockSpec((qrows, BRANCH_W), lambda b, r: (b * steps + r, 0)),
        out_shape=jax.ShapeDtypeStruct((batch * seq, BRANCH_W), BF16),
        compiler_params=pltpu.CompilerParams(dimension_semantics=("parallel", "arbitrary")),
        name="lat_na",
    )(p, p, p, p, bias, ctx_k, ctx_v)


def _toeplitz_kernel(r_ref, o_ref):
    x = r_ref[...]
    hi = x.astype(BF16)
    r1 = x - hi.astype(F32)
    mid = r1.astype(BF16)
    lo = (r1 - mid.astype(F32)).astype(BF16)
    shape = (x.shape[1], GRID_W * GRID_W)
    b = lax.broadcasted_iota(jnp.int32, shape, 0)
    col = lax.broadcasted_iota(jnp.int32, shape, 1)
    kc = jnp.bitwise_and(col, GRID_W - 1)
    qc = lax.shift_right_logical(col, GRID_W.bit_length() - 1)
    onehot = jnp.where(kc - qc + (NA_COLS - 1) == b, 1.0, 0.0).astype(BF16)
    o_ref[...] = _dot(hi, onehot) + _dot(mid, onehot) + _dot(lo, onehot)


def _na_bias_tables(na_rpb):
    depth = na_rpb.shape[0]
    n = depth * NA_HEADS * N_RPB_ROWS
    n_pad = -(-n // 8) * 8
    r2 = jnp.zeros((n_pad, LANES), F32).at[:n, :N_RPB_COLS].set(na_rpb.reshape(n, N_RPB_COLS))
    flat = pl.pallas_call(
        _toeplitz_kernel,
        out_shape=jax.ShapeDtypeStruct((n_pad, GRID_W * GRID_W), F32),
        name="na_bias_toeplitz",
    )(r2)
    t = flat[:n].reshape(depth, NA_HEADS, N_RPB_ROWS, GRID_W, GRID_W)
    qc = np.arange(GRID_W)[:, None]
    kc = np.arange(GRID_W)[None, :]
    win0 = np.clip(qc - NA_COLS // 2, 0, GRID_W - NA_COLS)
    vis = (kc >= win0) & (kc < win0 + NA_COLS)
    t = jnp.where(vis, t, NEG_INF)
    return jnp.concatenate([t[:, :, :-1], t[:, :, 1:]], axis=-1)


def _bf16_table(t):
    return jnp.asarray(t, F32).astype(BF16)


def _dft_cos_sin(n):
    idx = np.arange(n)
    ang = 2.0 * np.pi * ((idx[:, None] * idx[None, :]) % n) / n
    return np.cos(ang), np.sin(ang)


def _channel_dft_blocks(scale):
    c, s = _dft_cos_sin(FN_GROUP_W)
    eye = np.eye(FN_GROUPS)
    return np.kron(eye, c) * scale, np.kron(eye, s) * scale


def _ctx_fourier_tables(seq):
    ct, st = _dft_cos_sin(seq)
    cb, sb = _channel_dft_blocks(1.0 / np.sqrt(seq * FN_GROUP_W))
    return tuple(_bf16_table(t) for t in (ct, st, cb, sb))


def _lat_fourier_tables(seq, chunks=8):
    r = FFT_R
    c64, s64 = _dft_cos_sin(r)
    k1 = np.arange(r)[:, None]
    t2 = np.arange(r)[None, :]
    ang = 2.0 * np.pi * (k1 * t2) / (r * r)
    twr = np.cos(ang).reshape(r, r // chunks, chunks).transpose(1, 0, 2)
    twi = (-np.sin(ang)).reshape(r, r // chunks, chunks).transpose(1, 0, 2)
    lmat = np.block([[c64, s64], [-s64, c64]])
    cb, sb = _channel_dft_blocks(1.0 / np.sqrt(seq * FN_GROUP_W))
    csmat = np.concatenate([cb, sb], axis=0)
    return (_bf16_table(c64), _bf16_table(s64), jnp.asarray(twr, F32), jnp.asarray(twi, F32),
            _bf16_table(lmat), _bf16_table(csmat))


def _rope_tables(seq):
    half = HEAD_DIM // 2
    quarter = half // 2
    t = jnp.arange(seq)
    inv = ROPE_BASE ** (-jnp.arange(quarter, dtype=F32) / quarter)

    def cs(pos):
        ang = pos.astype(F32)[:, None] * inv[None, :]
        c, s = jnp.cos(ang), jnp.sin(ang)
        return jnp.concatenate([c, c], axis=-1), jnp.concatenate([-s, s], axis=-1)

    cr, sr = cs(t // GRID_W)
    cc, sc = cs(t % GRID_W)
    cos = jnp.concatenate([cr, cc], axis=-1)
    sin = jnp.concatenate([sr, sc], axis=-1)
    return jnp.tile(cos, (1, 2)), jnp.tile(sin, (1, 2))


HALF_TILE = TILE // 2


def _tile_sources():
    starts = ([OFF_A + i * TILE for i in range(4)] + [OFF_BQ, OFF_BG, OFF_BK]
              + [OFF_FU + i * TILE for i in range(6)] + [None] + [N_MAIN + i * TILE for i in range(N_BRANCH * D_MODEL // TILE)])
    src = []
    for t, start in enumerate(starts):
        halves = [None, None] if start is None else [start, start + HALF_TILE]
        if t == T_BKV:
            halves[1] = None
        for col in halves:
            assert col is None or col % HALF_TILE == 0
            src.append(-1 if col is None else col // HALF_TILE)
    assert len(starts) == N_TILES + N_BRANCH * D_MODEL // TILE
    return np.asarray(src, np.int32)


def _tile_weights_kernel(src_ref, lo_ref, hi_ref, o_ref):
    t = pl.program_id(0)
    zeros = jnp.zeros(lo_ref.shape[1:], BF16)
    o_ref[:, :HALF_TILE] = jnp.where(src_ref[2 * t] < 0, zeros, lo_ref[0].astype(BF16))
    o_ref[:, HALF_TILE:] = jnp.where(src_ref[2 * t + 1] < 0, zeros, hi_ref[0].astype(BF16))


def _tiled_in_weights(w_in, layer):
    src = _tile_sources()
    n_out_tiles = len(src) // 2
    half = lambda h: pl.BlockSpec(
        (1, D_MODEL, HALF_TILE), functools.partial(lambda t, s, h: (layer, 0, jnp.maximum(s[2 * t + h], 0)), h=h))
    return pl.pallas_call(
        _tile_weights_kernel,
        grid_spec=pltpu.PrefetchScalarGridSpec(
            num_scalar_prefetch=1,
            grid=(n_out_tiles,),
            in_specs=[half(0), half(1)],
            out_specs=pl.BlockSpec((D_MODEL, TILE), lambda t, s: (0, t)),
        ),
        out_shape=jax.ShapeDtypeStruct((D_MODEL, n_out_tiles * TILE), BF16),
        name="tile_weights",
    )(jnp.asarray(src), w_in, w_in)


def kernel(x_prompt, x_sample, cache_win_k, cache_win_v, cache_na_k, cache_na_v, c, c_ctx, norm_g, w_ada, b_ada,
           w_in, conv_w, win_sink, na_rpb, w_branch, w_out, final_g):
    batch, seq, _ = x_prompt.shape
    dbatch, dseq, _ = x_sample.shape
    past = cache_win_k.shape[2]

    cv8 = jnp.zeros((8, D_MODEL), F32).at[0].set(c_ctx).at[1:1 + dbatch].set(c)
    mod = _modulation(cv8, w_ada, b_ada)

    ctx_tabs = _ctx_fourier_tables(seq)
    lat_tabs = _lat_fourier_tables(dseq)
    cos_t, sin_t = _rope_tables(dseq)
    na_bias = _na_bias_tables(na_rpb)
    final_g2 = final_g.reshape(1, D_MODEL)

    xp = x_prompt.reshape(batch * seq, D_MODEL)
    xs = x_sample.reshape(dbatch * dseq, D_MODEL)
    new_kv = [jnp.zeros((batch, DEPTH, seq, width), F32) for width in (KV_W, KV_W, BRANCH_W, BRANCH_W)]

    for l in range(DEPTH):
        shift = mod[l, :, 0:D_MODEL].reshape(8, 1, D_MODEL)
        scale = mod[l, :, D_MODEL:2 * D_MODEL].reshape(8, 1, D_MODEL)
        gate = mod[l, :, 2 * D_MODEL:].reshape(8, 1, D_MODEL)
        g = norm_g[l].reshape(1, D_MODEL)
        w_p = _tiled_in_weights(w_in, l)
        w_br = w_branch[l].astype(BF16)
        w_o = w_out[l].astype(BF16)
        cw = conv_w[l]
        sink_b = jnp.broadcast_to(win_sink[l][:, None], (WIN_HEADS, LANES))
        final = l == DEPTH - 1

        p, hp, *new_kv = _inproj(
            xp, shift, scale, g, w_p, rows_per_cond=batch * seq, cond0=0, update=(new_kv, l, seq),
            extras=((T_BKV, 0, KV_W), (T_BKV, KV_W, KV_W), (T_DK, 0, BRANCH_W), (T_DV, 0, BRANCH_W)))
        o_ctx = _ctx_mixers(p, cw, sink_b, *ctx_tabs, seq=seq)
        xp = _out_stage(xp, hp, gate, final_g2, [o_ctx] * 4, [0, 1, 2, 3], w_p, w_br, w_o,
                        rows_per_cond=batch * seq, cond0=0, final=final)

        q, hs, qfu, qfg = _inproj(xs, shift, scale, g, w_p, rows_per_cond=dseq, cond0=1, extra_dtype=BF16,
                              extras=((T_FU, 0, BRANCH_W), (T_FG, 0, BRANCH_W)))
        o_a = _lat_conv(q, cw, batch=dbatch, seq=dseq)
        o_w = _lat_window(q, cos_t, sin_t,
                          cache_win_k[:, l].reshape(dbatch, past, KV_W), cache_win_v[:, l].reshape(dbatch, past, KV_W),
                          sink_b, batch=dbatch, seq=dseq)
        o_f = _lat_fourier(qfu, qfg, lat_tabs, batch=dbatch, seq=dseq)
        o_n = _lat_na(q, na_bias[l],
                      cache_na_k[:, l].reshape(dbatch, past, BRANCH_W), cache_na_v[:, l].reshape(dbatch, past, BRANCH_W),
                      batch=dbatch, seq=dseq)
        xs = _out_stage(xs, hs, gate, final_g2, [o_a, o_w, o_f, o_n], [0, 0, 0, 0], w_p, w_br, w_o,
                        rows_per_cond=dseq, cond0=1, final=final)

    y_prompt = xp.reshape(batch, seq, D_MODEL)
    y_sample = xs.reshape(dbatch, dseq, D_MODEL)
    heads = (WIN_KV_HEADS, WIN_KV_HEADS, NA_HEADS, NA_HEADS)
    return (y_prompt, y_sample) + tuple(a.reshape(batch, DEPTH, seq, n, HEAD_DIM) for a, n in zip(new_kv, heads))
```

```python
import functools
import math

import numpy as np
import jax
import jax.numpy as jnp
from jax import lax
from jax.experimental import pallas as pl
from jax.experimental.pallas import tpu as pltpu

D_MODEL = 2048
DEPTH = 2
GRID_W = 64
N_BRANCH = 4
BRANCH_W = 512
HEAD_DIM = 64
WIN_HEADS = 8
WIN_KV_HEADS = 2
KV_W = WIN_KV_HEADS * HEAD_DIM
WIN_BLOCK = 128
FN_GROUPS = 4
FN_GROUP_W = BRANCH_W // FN_GROUPS
NA_HEADS = 8
NA_ROWS = 8
NA_COLS = 16
ROPE_BASE = 10000.0
EPS = 1e-6
ATTN_SCALE = HEAD_DIM ** -0.5
NEG_INF = -1e30
N_MAIN = 6400

F32 = jnp.float32
BF16 = jnp.bfloat16
LANES = 128

OFF_A = 0
OFF_BQ = OFF_A + 4 * BRANCH_W
OFF_BK = OFF_BQ + BRANCH_W
OFF_BG = OFF_BK + 2 * KV_W
OFF_FU = OFF_BG + BRANCH_W
assert OFF_FU + 6 * BRANCH_W == N_MAIN

TILE = 512
T_AX, T_AB, T_AC, T_AG, T_BQ, T_BG, T_BKV, T_ZERO, T_FU, T_FG, T_DQ, T_DK, T_DV, T_DG = range(14)
STEP_TILES = 2
N_TILES = 14
P_WIDTH = N_TILES * TILE

V7X_VMEM_BYTES = 64 * 2 ** 20
PROJ_VMEM_BYTES = V7X_VMEM_BYTES - 2 ** 20

FFT_R = 64


def _silu(x):
    return x * jax.nn.sigmoid(x)


def _dot(a, b):
    return jnp.dot(a, b, preferred_element_type=F32)


def _dot_nt(a, b):
    return lax.dot_general(a, b, (((1,), (1,)), ((), ())), preferred_element_type=F32)


def _rms_mod(x, g, scale, shift):
    ms = jnp.mean(x * x, axis=-1, keepdims=True)
    return (x * lax.rsqrt(ms + EPS) * g) * (1.0 + scale) + shift


NORM_ROWS = 256


def _mod_kernel(cv_ref, w_ref, b_ref, o_ref):
    s = _silu(cv_ref[...]).astype(BF16)
    o_ref[0] = _dot(s, w_ref[0].astype(BF16)) + b_ref[0]


def _modulation(cv8, w_ada, b_ada):
    tn = 512
    n3 = 3 * D_MODEL
    return pl.pallas_call(
        _mod_kernel,
        grid=(DEPTH, n3 // tn),
        in_specs=[
            pl.BlockSpec((8, D_MODEL), lambda l, j: (0, 0)),
            pl.BlockSpec((1, D_MODEL, tn), lambda l, j: (l, 0, j)),
            pl.BlockSpec((1, 1, tn), lambda l, j: (l, 0, j)),
        ],
        out_specs=pl.BlockSpec((1, 8, tn), lambda l, j: (l, 0, j)),
        out_shape=jax.ShapeDtypeStruct((DEPTH, 8, n3), F32),
        name="modulation",
    )(cv8, w_ada, b_ada.reshape(DEPTH, 1, n3))


def _inproj_kernel(x_ref, sh_ref, sc_ref, g_ref, w_ref, *rest, extras, n_updated):
    p_ref, h_ref, extra_refs = rest[n_updated], rest[n_updated + 1], rest[n_updated + 2:]
    j = pl.program_id(1)
    assert STEP_TILES == 2 and T_ZERO % 2 == 1 and 0 < T_ZERO // 2 < N_TILES // 2 - 1
    half_step = T_ZERO // STEP_TILES
    assert all(tile >= STEP_TILES for tile, _ in extras)

    def store_extras(acc, in_half_step):
        for ref, (tile, lo) in zip(extra_refs, extras):
            lo = (tile % STEP_TILES) * TILE + lo
            if (tile // STEP_TILES == half_step) != in_half_step:
                continue
            assert lo + ref.shape[-1] <= acc.shape[1]

            @pl.when(j == tile // STEP_TILES)
            def _(ref=ref, lo=lo):
                cols = acc[:, lo:lo + ref.shape[-1]].astype(ref.dtype)
                if ref.ndim == 2:
                    ref[...] = cols
                else:
                    ref[:, 0] = cols.reshape(ref.shape[0], ref.shape[2], ref.shape[3])

    @pl.when(j == 0)
    def _():
        g, scale, shift = g_ref[...], sc_ref[0], sh_ref[0]
        for c in range(h_ref.shape[0] // NORM_ROWS):
            rows = slice(c * NORM_ROWS, (c + 1) * NORM_ROWS)
            h = _rms_mod(x_ref[rows, :], g, scale, shift).astype(h_ref.dtype)
            h_ref[rows, :] = h
            p_ref[rows, :] = _dot(h, w_ref[...]).astype(p_ref.dtype)

    @pl.when(j == half_step)
    def _():
        acc = _dot(h_ref[...], w_ref[:, :TILE])
        p_ref[:, :TILE] = acc.astype(p_ref.dtype)
        p_ref[:, TILE:] = jnp.zeros((p_ref.shape[0], TILE), p_ref.dtype)
        store_extras(acc, True)

    @pl.when((j > 0) & (j != half_step))
    def _():
        acc = _dot(h_ref[...], w_ref[...])
        p_ref[...] = acc.astype(p_ref.dtype)
        store_extras(acc, False)


def _inproj(x2d, shift, scale, g, w, *, rows_per_cond, cond0, extras, extra_dtype=None, update=None, tm=1024):
    m = x2d.shape[0]
    assert w.shape[0] == D_MODEL and w.shape[1] >= P_WIDTH and m % tm == 0 and rows_per_cond % tm == 0
    tiles_per_cond = rows_per_cond // tm
    cond = lambda i, j: (cond0 + i // tiles_per_cond, 0, 0)
    if update is None:
        updated, aliases = [], {}
        extra_specs = [pl.BlockSpec((tm, width), lambda i, j: (i, 0)) for _, _, width in extras]
        extra_shapes = [jax.ShapeDtypeStruct((m, width), extra_dtype) for _, _, width in extras]
    else:
        updated, layer, seq = update
        assert tm % seq == 0 and all(a.shape[2:] == (seq, width) for a, (_, _, width) in zip(updated, extras))
        extra_specs = [pl.BlockSpec((tm // seq, 1, seq, width), lambda i, j: (i, layer, 0, 0)) for _, _, width in extras]
        extra_shapes = [jax.ShapeDtypeStruct(a.shape, a.dtype) for a in updated]
        aliases = {5 + k: 2 + k for k in range(len(updated))}
    return pl.pallas_call(
        functools.partial(_inproj_kernel, extras=tuple((t, lo) for t, lo, _ in extras), n_updated=len(updated)),
        grid=(m // tm, N_TILES // STEP_TILES),
        in_specs=[
            pl.BlockSpec((tm, D_MODEL), lambda i, j: (i, 0)),
            pl.BlockSpec((1, 1, D_MODEL), cond),
            pl.BlockSpec((1, 1, D_MODEL), cond),
            pl.BlockSpec((1, D_MODEL), lambda i, j: (0, 0)),
            pl.BlockSpec((D_MODEL, STEP_TILES * TILE), lambda i, j: (0, j)),
        ] + [pl.BlockSpec(memory_space=pl.ANY) for _ in updated],
        out_specs=[pl.BlockSpec((tm, STEP_TILES * TILE), lambda i, j: (i, j)),
                   pl.BlockSpec((tm, D_MODEL), lambda i, j: (i, 0))]
        + extra_specs,
        out_shape=[jax.ShapeDtypeStruct((m, P_WIDTH), BF16), jax.ShapeDtypeStruct((m, D_MODEL), BF16)] + extra_shapes,
        input_output_aliases=aliases,
        compiler_params=pltpu.CompilerParams(dimension_semantics=("parallel", "arbitrary"),
                                             vmem_limit_bytes=PROJ_VMEM_BYTES),
        name="inproj",
    )(x2d, shift, scale, g, w, *updated)


def _out_kernel(x_ref, h_ref, gt_ref, fg_ref, oa_ref, ow_ref, of_ref, on_ref,
                wg0_ref, wg1_ref, wg2_ref, wg3_ref, wb_ref, wo_ref, y_ref, merged_ref, *, final):
    j = pl.program_id(1)
    h = h_ref[...]
    merged = None
    for i, (o_ref, wg_ref) in enumerate(((oa_ref, wg0_ref), (ow_ref, wg1_ref), (of_ref, wg2_ref), (on_ref, wg3_ref))):
        term = jax.nn.sigmoid(_dot(h, wg_ref[...])) * _dot(o_ref[...], wb_ref[i])
        merged = term if merged is None else merged + term
    merged_ref[j] = merged.astype(BF16)

    @pl.when(j == pl.num_programs(1) - 1)
    def _():
        m_all = jnp.concatenate([merged_ref[k] for k in range(merged_ref.shape[0])], axis=1)
        xn = x_ref[...] + gt_ref[0] * _dot(m_all, wo_ref[...])
        if final:
            ms = jnp.mean(xn * xn, axis=-1, keepdims=True)
            xn = xn * lax.rsqrt(ms + EPS) * fg_ref[...]
        y_ref[...] = xn


def _out_stage(x2d, h2d, gate, final_g, o_arrays, o_cols, w_gate, w_branch, w_out,
               *, rows_per_cond, cond0, final, tm=512, tc=512):
    m = x2d.shape[0]
    assert m % tm == 0 and rows_per_cond % tm == 0
    tiles_per_cond = rows_per_cond // tm
    ncol = D_MODEL // tc
    cond = lambda i, j: (cond0 + i // tiles_per_cond, 0, 0)
    o_specs = [pl.BlockSpec((tm, BRANCH_W), functools.partial(lambda i, j, c: (i, c), c=c)) for c in o_cols]
    assert tc == TILE and w_gate.shape == (D_MODEL, P_WIDTH + N_BRANCH * D_MODEL)
    wg_specs = [pl.BlockSpec((D_MODEL, tc), functools.partial(lambda i, j, b: (0, N_TILES + b * ncol + j), b=b))
                for b in range(N_BRANCH)]
    return pl.pallas_call(
        functools.partial(_out_kernel, final=final),
        grid=(m // tm, ncol),
        in_specs=[
            pl.BlockSpec((tm, D_MODEL), lambda i, j: (i, 0)),
            pl.BlockSpec((tm, D_MODEL), lambda i, j: (i, 0)),
            pl.BlockSpec((1, 1, D_MODEL), cond),
            pl.BlockSpec((1, D_MODEL), lambda i, j: (0, 0)),
            *o_specs,
            *wg_specs,
            pl.BlockSpec((N_BRANCH, BRANCH_W, tc), lambda i, j: (0, 0, j)),
            pl.BlockSpec((D_MODEL, D_MODEL), lambda i, j: (0, 0), pipeline_mode=pl.Buffered(1)),
        ],
        out_specs=pl.BlockSpec((tm, D_MODEL), lambda i, j: (i, 0)),
        out_shape=jax.ShapeDtypeStruct((m, D_MODEL), F32),
        scratch_shapes=[pltpu.VMEM((ncol, tm, tc), BF16)],
        compiler_params=pltpu.CompilerParams(dimension_semantics=("parallel", "arbitrary"),
                                             vmem_limit_bytes=PROJ_VMEM_BYTES),
        name="out_stage",
    )(x2d, h2d, gate, final_g, *o_arrays, w_gate, w_gate, w_gate, w_gate, w_branch, w_out)


def _attend(scores, values, sinks=None):
    heads = range(len(scores))
    sinks = [None] * len(scores) if sinks is None else sinks
    m = []
    for h in heads:
        mh = scores[h][0].max(axis=-1, keepdims=True)
        for s in scores[h][1:]:
            mh = jnp.maximum(mh, s.max(axis=-1, keepdims=True))
        m.append(mh if sinks[h] is None else jnp.maximum(mh, sinks[h]))
    e = [[jnp.exp(s - m[h]) for s in scores[h]] for h in heads]
    den = []
    for h in heads:
        d = e[h][0].sum(axis=-1, keepdims=True)
        for x in e[h][1:]:
            d = d + x.sum(axis=-1, keepdims=True)
        den.append(d if sinks[h] is None else d + jnp.exp(sinks[h] - m[h]))
    out = []
    for h in heads:
        acc = _dot(e[h][0].astype(BF16), values[h][0])
        for x, v in zip(e[h][1:], values[h][1:]):
            acc = acc + _dot(x.astype(BF16), v)
        out.append(acc)
    return [a / d for a, d in zip(out, den)]


def _shift_rows(z, first_row, last_row):
    n = z.shape[0]
    row = lax.broadcasted_iota(jnp.int32, z.shape, 0)
    z_dn = jnp.where(row == 0, first_row, pltpu.roll(z, 1, axis=0))
    z_up = jnp.where(row == n - 1, last_row, pltpu.roll(z, n - 1, axis=0))
    return z_dn, z_up


def _tile(ref, t, dtype=None):
    v = ref[:, t * TILE:(t + 1) * TILE]
    return v if dtype is None else v.astype(dtype)


def _head(ref, t, h, base=0):
    lo = t * TILE + base + h * HEAD_DIM
    return ref[:, lo:lo + HEAD_DIM]


def _scaled_queries(q):
    assert math.frexp(ATTN_SCALE)[0] == 0.5
    return q * ATTN_SCALE


def _ctx_mixer_kernel(p_ref, cw_ref, sink_ref, ct_ref, st_ref, cb_ref, sb_ref, o_ref):
    z = _tile(p_ref, T_AC, F32) * _tile(p_ref, T_AX, F32)
    zero_row = jnp.zeros((1, BRANCH_W), F32)
    z_dn, z_up = _shift_rows(z, zero_row, zero_row)
    y = _tile(p_ref, T_AB, F32) * (z_dn * cw_ref[0:1, :] + z * cw_ref[1:2, :] + z_up * cw_ref[2:3, :])
    out = lambda b: slice(b * BRANCH_W, (b + 1) * BRANCH_W)
    o_ref[:, out(0)] = (y * _silu(_tile(p_ref, T_AG, F32))).astype(o_ref.dtype)

    gsz = WIN_HEADS // WIN_KV_HEADS
    head = lambda a, h: a[:, h * HEAD_DIM:(h + 1) * HEAD_DIM]
    qb = _scaled_queries(_tile(p_ref, T_BQ))
    qd = _scaled_queries(_tile(p_ref, T_DQ))
    scores = ([[_dot_nt(head(qb, h), _head(p_ref, T_BKV, h // gsz))] for h in range(WIN_HEADS)]
              + [[_dot_nt(head(qd, h), _head(p_ref, T_DK, h))] for h in range(NA_HEADS)])
    values = ([[_head(p_ref, T_BKV, h // gsz, base=KV_W)] for h in range(WIN_HEADS)]
              + [[_head(p_ref, T_DV, h)] for h in range(NA_HEADS)])
    sinks = [sink_ref[h:h + 1, 0:1] for h in range(WIN_HEADS)] + [None] * NA_HEADS
    heads = _attend(scores, values, sinks=sinks)
    o_w = jnp.concatenate(heads[:WIN_HEADS], axis=-1) * _silu(_tile(p_ref, T_BG, F32))
    o_ref[:, out(1)] = o_w.astype(o_ref.dtype)
    o_n = jnp.concatenate(heads[WIN_HEADS:], axis=-1) * _silu(_tile(p_ref, T_DG, F32))
    o_ref[:, out(3)] = o_n.astype(o_ref.dtype)

    u = _tile(p_ref, T_FU)
    uc = _dot(u, cb_ref[...]).astype(BF16)
    us = _dot(u, sb_ref[...]).astype(BF16)
    o_f = (_dot(ct_ref[...], uc) - _dot(st_ref[...], us)) * _silu(_tile(p_ref, T_FG, F32))
    o_ref[:, out(2)] = o_f.astype(o_ref.dtype)


def _ctx_mixers(p, conv_w, sink_b, ct, st, cb, sb, *, seq):
    m = p.shape[0]
    whole = lambda a: pl.BlockSpec(a.shape, lambda b: (0,) * a.ndim)
    return pl.pallas_call(
        _ctx_mixer_kernel,
        grid=(m // seq,),
        in_specs=[pl.BlockSpec((seq, P_WIDTH), lambda b: (b, 0)), whole(conv_w), whole(sink_b),
                  whole(ct), whole(st), whole(cb), whole(sb)],
        out_specs=pl.BlockSpec((seq, N_BRANCH * BRANCH_W), lambda b: (b, 0)),
        out_shape=jax.ShapeDtypeStruct((m, N_BRANCH * BRANCH_W), BF16),
        compiler_params=pltpu.CompilerParams(dimension_semantics=("parallel",)),
        name="ctx_mixers",
    )(p, conv_w, sink_b, ct, st, cb, sb)


HALO = 16


def _lat_conv_kernel(ax_ref, ab_ref, ac_ref, ag_ref, axp_ref, acp_ref, axn_ref, acn_ref, cw_ref, o_ref):
    i = pl.program_id(1)
    z = ac_ref[...].astype(F32) * ax_ref[...].astype(F32)
    zp = acp_ref[HALO - 1:HALO, :].astype(F32) * axp_ref[HALO - 1:HALO, :].astype(F32)
    zn = acn_ref[0:1, :].astype(F32) * axn_ref[0:1, :].astype(F32)
    zp = jnp.where(i == 0, 0.0, zp)
    zn = jnp.where(i == pl.num_programs(1) - 1, 0.0, zn)
    z_dn, z_up = _shift_rows(z, zp, zn)
    y = ab_ref[...].astype(F32) * (z_dn * cw_ref[0:1, :] + z * cw_ref[1:2, :] + z_up * cw_ref[2:3, :])
    o_ref[...] = (y * _silu(ag_ref[...].astype(F32))).astype(o_ref.dtype)


def _lat_conv(pa, conv_w, *, batch, seq, tr=512):
    nt = seq // tr
    hb = tr // HALO
    last_halo = batch * seq // HALO - 1
    main = lambda c: pl.BlockSpec((tr, BRANCH_W), functools.partial(lambda b, i, c: (b * nt + i, c), c=c))
    prev = lambda c: pl.BlockSpec(
        (HALO, BRANCH_W), functools.partial(lambda b, i, c: (jnp.maximum((b * nt + i) * hb - 1, 0), c), c=c))
    nxt = lambda c: pl.BlockSpec(
        (HALO, BRANCH_W), functools.partial(lambda b, i, c: (jnp.minimum((b * nt + i + 1) * hb, last_halo), c), c=c))
    return pl.pallas_call(
        _lat_conv_kernel,
        grid=(batch, nt),
        in_specs=[main(T_AX), main(T_AB), main(T_AC), main(T_AG), prev(T_AX), prev(T_AC), nxt(T_AX), nxt(T_AC),
                  pl.BlockSpec(conv_w.shape, lambda b, i: (0, 0))],
        out_specs=pl.BlockSpec((tr, BRANCH_W), lambda b, i: (b * nt + i, 0)),
        out_shape=jax.ShapeDtypeStruct((batch * seq, BRANCH_W), BF16),
        compiler_params=pltpu.CompilerParams(dimension_semantics=("parallel", "arbitrary")),
        name="lat_conv",
    )(pa, pa, pa, pa, pa, pa, pa, pa, conv_w)


def _rope(x, cos, sin_signed):
    lane = lax.broadcasted_iota(jnp.int32, x.shape, 1)
    quarter = HEAD_DIM // 4
    partner = jnp.where((lane % (2 * quarter)) < quarter,
                        pltpu.roll(x, LANES - quarter, axis=1), pltpu.roll(x, quarter, axis=1))
    return x * cos + partner * sin_signed


WIN_BLOCKS_PER_STEP = 8


def _lat_win_kernel(q_ref, g_ref, kp_ref, kc_ref, kn_ref, vp_ref, vc_ref, vn_ref,
                    cq_ref, sq_ref, cp_ref, sp_ref, cn_ref, sn_ref, ck_ref, cv_ref, sink_ref, o_ref):
    nq = WIN_BLOCKS_PER_STEP
    nb = pl.num_programs(1) * nq
    wb = WIN_BLOCK
    cq, sq = cq_ref[...], sq_ref[...]
    k_rot = jnp.concatenate([
        _rope(kp_ref[...].astype(F32), cp_ref[...], sp_ref[...]),
        _rope(kc_ref[...].astype(F32), cq, sq),
        _rope(kn_ref[...].astype(F32), cn_ref[...], sn_ref[...]),
    ], axis=0).astype(BF16)
    v_all = jnp.concatenate([vp_ref[...], vc_ref[...], vn_ref[...]], axis=0).astype(BF16)
    qi = lax.broadcasted_iota(jnp.int32, (wb, 3 * wb), 0)
    kj = lax.broadcasted_iota(jnp.int32, (wb, 3 * wb), 1)
    ck = ck_ref[0].astype(BF16)
    cv = cv_ref[0].astype(BF16)
    gsz = WIN_HEADS // WIN_KV_HEADS
    q_rot = [_scaled_queries(_rope(q_ref[:, pair * LANES:(pair + 1) * LANES].astype(F32), cq, sq)).astype(BF16)
             for pair in range(WIN_HEADS // 2)]
    scores, values, sinks = [], [], []
    for d in range(nq):
        n = pl.program_id(1) * nq + d
        mask = (((kj < wb) & (kj >= qi) & (n > 0)) | ((kj >= wb) & (kj < 2 * wb))
                | ((kj >= 2 * wb) & (kj - 2 * wb <= qi) & (n < nb - 1)))
        rows = slice(d * wb, (d + 1) * wb)
        win = slice(d * wb, (d + 3) * wb)
        for h in range(WIN_HEADS):
            kv = slice((h // gsz) * HEAD_DIM, (h // gsz + 1) * HEAD_DIM)
            q = q_rot[h // 2][rows, (h % 2) * HEAD_DIM:(h % 2 + 1) * HEAD_DIM]
            s_loc = jnp.where(mask, _dot_nt(q, k_rot[win, kv]), NEG_INF)
            scores.append([s_loc, _dot_nt(q, ck[:, kv])])
            values.append([v_all[win, kv], cv[:, kv]])
            sinks.append(sink_ref[h:h + 1, 0:1])
    heads = _attend(scores, values, sinks=sinks)
    o = jnp.concatenate([jnp.concatenate(heads[d * WIN_HEADS:(d + 1) * WIN_HEADS], axis=-1) for d in range(nq)], axis=0)
    o_ref[...] = (o * _silu(g_ref[...].astype(F32))).astype(o_ref.dtype)


def _lat_window(pb, cos_t, sin_t, ctx_k, ctx_v, sink_b, *, batch, seq):
    wb = WIN_BLOCK
    k_col = T_BKV * (TILE // KV_W)
    v_col = k_col + 1
    nq = WIN_BLOCKS_PER_STEP
    nb = seq // wb
    steps = nb // nq
    nbr = lambda s, d: jnp.clip(s * nq + (nq if d > 0 else -1), 0, nb - 1)
    own = lambda col, width: pl.BlockSpec(
        (nq * wb, width), functools.partial(lambda b, s, col: (b * steps + s, col), col=col))
    kv = lambda col, d: pl.BlockSpec(
        (wb, KV_W), functools.partial(lambda b, s, col, d: (b * nb + nbr(s, d), col), col=col, d=d))
    tab = lambda d: pl.BlockSpec((wb, LANES), functools.partial(lambda b, s, d: (nbr(s, d), 0), d=d))
    tab_own = pl.BlockSpec((nq * wb, LANES), lambda b, s: (s, 0))
    ctx = pl.BlockSpec((1,) + ctx_k.shape[1:], lambda b, s: (b, 0, 0))
    return pl.pallas_call(
        _lat_win_kernel,
        grid=(batch, steps),
        in_specs=[
            own(T_BQ, BRANCH_W), own(T_BG, BRANCH_W),
            kv(k_col, -1), own(k_col, KV_W), kv(k_col, 1), kv(v_col, -1), own(v_col, KV_W), kv(v_col, 1),
            tab_own, tab_own, tab(-1), tab(-1), tab(1), tab(1),
            ctx, ctx,
            pl.BlockSpec(sink_b.shape, lambda b, s: (0, 0)),
        ],
        out_specs=pl.BlockSpec((nq * wb, BRANCH_W), lambda b, s: (b * steps + s, 0)),
        out_shape=jax.ShapeDtypeStruct((batch * seq, BRANCH_W), BF16),
        compiler_params=pltpu.CompilerParams(dimension_semantics=("parallel", "arbitrary")),
        name="lat_window",
    )(pb, pb, pb, pb, pb, pb, pb, pb, cos_t, sin_t, cos_t, sin_t, cos_t, sin_t, ctx_k, ctx_v, sink_b)


def _fft1_kernel(u_ref, c_ref, s_ref, twr_ref, twi_ref, zr_ref, zi_ref, *, chunks):
    u = u_ref[0].astype(BF16)
    yr = _dot(c_ref[...], u)
    yi = -_dot(s_ref[...], u)
    for t in range(chunks):
        wr = twr_ref[0, :, t:t + 1]
        wi = twi_ref[0, :, t:t + 1]
        a = yr[:, t * BRANCH_W:(t + 1) * BRANCH_W]
        b = yi[:, t * BRANCH_W:(t + 1) * BRANCH_W]
        zr_ref[0, :, t * BRANCH_W:(t + 1) * BRANCH_W] = (a * wr - b * wi).astype(zr_ref.dtype)
        zi_ref[0, :, t * BRANCH_W:(t + 1) * BRANCH_W] = (a * wi + b * wr).astype(zi_ref.dtype)


def _fft2_kernel(zr_ref, zi_ref, g_ref, l_ref, cs_ref, o_ref, *, kblock):
    r = FFT_R
    zz = [jnp.concatenate([zr_ref[0, kk], zi_ref[0, kk]], axis=0) for kk in range(kblock)]
    xx = [_dot(l_ref[...], z) for z in zz]
    xcat = [jnp.concatenate([x[0:r], x[r:2 * r]], axis=1).astype(BF16) for x in xx]
    out = [_dot(x, cs_ref[...]) for x in xcat]
    for kk in range(kblock):
        sl = slice(kk * BRANCH_W, (kk + 1) * BRANCH_W)
        o_ref[0, :, sl] = (out[kk] * _silu(g_ref[0, :, sl].astype(F32))).astype(o_ref.dtype)


def _lat_fourier(fu, fg, tabs, *, batch, seq):
    r = FFT_R
    assert seq == r * r
    wide = r * BRANCH_W
    chunks = 8
    c64, s64, twr, twi, lmat, csmat = tabs
    u3 = fu.reshape(batch, r, wide)
    nct = r // chunks
    zr, zi = pl.pallas_call(
        functools.partial(_fft1_kernel, chunks=chunks),
        grid=(batch, nct),
        in_specs=[
            pl.BlockSpec((1, r, chunks * BRANCH_W), lambda b, t: (b, 0, t)),
            pl.BlockSpec((r, r), lambda b, t: (0, 0)),
            pl.BlockSpec((r, r), lambda b, t: (0, 0)),
            pl.BlockSpec((1, r, chunks), lambda b, t: (t, 0, 0)),
            pl.BlockSpec((1, r, chunks), lambda b, t: (t, 0, 0)),
        ],
        out_specs=[pl.BlockSpec((1, r, chunks * BRANCH_W), lambda b, t: (b, 0, t))] * 2,
        out_shape=[jax.ShapeDtypeStruct((batch, r, wide), BF16)] * 2,
        compiler_params=pltpu.CompilerParams(dimension_semantics=("parallel", "arbitrary")),
        name="lat_fft1",
    )(u3, c64, s64, twr, twi)
    kblock = 8
    z4 = lambda z: z.reshape(batch, r, r, BRANCH_W)
    out = pl.pallas_call(
        functools.partial(_fft2_kernel, kblock=kblock),
        grid=(batch, r // kblock),
        in_specs=[
            pl.BlockSpec((1, kblock, r, BRANCH_W), lambda b, k: (b, k, 0, 0)),
            pl.BlockSpec((1, kblock, r, BRANCH_W), lambda b, k: (b, k, 0, 0)),
            pl.BlockSpec((1, r, kblock * BRANCH_W), lambda b, k: (b, 0, k)),
            pl.BlockSpec((2 * r, 2 * r), lambda b, k: (0, 0)),
            pl.BlockSpec((2 * BRANCH_W, BRANCH_W), lambda b, k: (0, 0)),
        ],
        out_specs=pl.BlockSpec((1, r, kblock * BRANCH_W), lambda b, k: (b, 0, k)),
        out_shape=jax.ShapeDtypeStruct((batch, r, wide), BF16),
        compiler_params=pltpu.CompilerParams(dimension_semantics=("parallel", "arbitrary")),
        name="lat_fft2",
    )(z4(zr), z4(zi), fg.reshape(batch, r, wide), lmat, csmat)
    return out.reshape(batch * seq, BRANCH_W)


N_RPB_ROWS = 2 * NA_ROWS - 1
N_RPB_COLS = 2 * NA_COLS - 1


NA_ROWS_PER_STEP = 8


def _lat_na_kernel(q_ref, g_ref, k_ref, v_ref, bias_ref, ck_ref, cv_ref, o_ref, *, rows):
    ck = ck_ref[0].astype(BF16)
    cv = cv_ref[0].astype(BF16)
    sls = [slice(h * HEAD_DIM, (h + 1) * HEAD_DIM) for h in range(NA_HEADS)]
    scores, values = [], []
    for d in range(NA_ROWS_PER_STEP):
        r = pl.program_id(1) * NA_ROWS_PER_STEP + d
        r0 = jnp.clip(r - NA_ROWS // 2, 0, rows - NA_ROWS)
        start = pl.multiple_of(r0 * GRID_W, GRID_W)
        kw = k_ref[pl.ds(start, NA_ROWS * GRID_W), :]
        vw = v_ref[pl.ds(start, NA_ROWS * GRID_W), :]
        qa = _scaled_queries(q_ref[d * GRID_W:(d + 1) * GRID_W, :])
        a0 = r0 - r + NA_ROWS - 1
        for h, sl in enumerate(sls):
            bias = jnp.concatenate([bias_ref[h, a0 + 2 * i2] for i2 in range(NA_ROWS // 2)], axis=1)
            scores.append([_dot_nt(qa[:, sl], kw[:, sl]) + bias, _dot_nt(qa[:, sl], ck[:, sl])])
            values.append([vw[:, sl], cv[:, sl]])
    heads = _attend(scores, values)
    o = jnp.concatenate([jnp.concatenate(heads[d * NA_HEADS:(d + 1) * NA_HEADS], axis=-1)
                         for d in range(NA_ROWS_PER_STEP)], axis=0)
    o_ref[...] = (o * _silu(g_ref[...].astype(F32))).astype(o_ref.dtype)


def _lat_na(p, bias, ctx_k, ctx_v, *, batch, seq):
    rows = seq // GRID_W
    steps = rows // NA_ROWS_PER_STEP
    qrows = NA_ROWS_PER_STEP * GRID_W
    ctx = pl.BlockSpec((1,) + ctx_k.shape[1:], lambda b, r: (b, 0, 0))
    return pl.pallas_call(
        functools.partial(_lat_na_kernel, rows=rows),
        grid=(batch, steps),
        in_specs=[
            pl.BlockSpec((qrows, BRANCH_W), lambda b, r: (b * steps + r, T_DQ)),
            pl.BlockSpec((qrows, BRANCH_W), lambda b, r: (b * steps + r, T_DG)),
            pl.BlockSpec((seq, BRANCH_W), lambda b, r: (b, T_DK)),
            pl.BlockSpec((seq, BRANCH_W), lambda b, r: (b, T_DV)),
            pl.BlockSpec(bias.shape, lambda b, r: (0, 0, 0, 0)),
            ctx, ctx,
        ],
        out_specs=pl.BlockSpec((qrows, BRANCH_W), lambda b, r: (b * steps + r, 0)),
        out_shape=jax.ShapeDtypeStruct((batch * seq, BRANCH_W), BF16),
        compiler_params=pltpu.CompilerParams(dimension_semantics=("parallel", "arbitrary")),
        name="lat_na",
    )(p, p, p, p, bias, ctx_k, ctx_v)


def _toeplitz_kernel(r_ref, o_ref):
    x = r_ref[...]
    hi = x.astype(BF16)
    r1 = x - hi.astype(F32)
    mid = r1.astype(BF16)
    lo = (r1 - mid.astype(F32)).astype(BF16)
    shape = (x.shape[1], GRID_W * GRID_W)
    b = lax.broadcasted_iota(jnp.int32, shape, 0)
    col = lax.broadcasted_iota(jnp.int32, shape, 1)
    kc = jnp.bitwise_and(col, GRID_W - 1)
    qc = lax.shift_right_logical(col, GRID_W.bit_length() - 1)
    onehot = jnp.where(kc - qc + (NA_COLS - 1) == b, 1.0, 0.0).astype(BF16)
    o_ref[...] = _dot(hi, onehot) + _dot(mid, onehot) + _dot(lo, onehot)


def _na_bias_tables(na_rpb):
    depth = na_rpb.shape[0]
    n = depth * NA_HEADS * N_RPB_ROWS
    n_pad = -(-n // 8) * 8
    r2 = jnp.zeros((n_pad, LANES), F32).at[:n, :N_RPB_COLS].set(na_rpb.reshape(n, N_RPB_COLS))
    flat = pl.pallas_call(
        _toeplitz_kernel,
        out_shape=jax.ShapeDtypeStruct((n_pad, GRID_W * GRID_W), F32),
        name="na_bias_toeplitz",
    )(r2)
    t = flat[:n].reshape(depth, NA_HEADS, N_RPB_ROWS, GRID_W, GRID_W)
    qc = np.arange(GRID_W)[:, None]
    kc = np.arange(GRID_W)[None, :]
    win0 = np.clip(qc - NA_COLS // 2, 0, GRID_W - NA_COLS)
    vis = (kc >= win0) & (kc < win0 + NA_COLS)
    t = jnp.where(vis, t, NEG_INF)
    return jnp.concatenate([t[:, :, :-1], t[:, :, 1:]], axis=-1)


def _bf16_table(t):
    return jnp.asarray(t, F32).astype(BF16)


def _dft_cos_sin(n):
    idx = np.arange(n)
    ang = 2.0 * np.pi * ((idx[:, None] * idx[None, :]) % n) / n
    return np.cos(ang), np.sin(ang)


def _channel_dft_blocks(scale):
    c, s = _dft_cos_sin(FN_GROUP_W)
    eye = np.eye(FN_GROUPS)
    return np.kron(eye, c) * scale, np.kron(eye, s) * scale


def _ctx_fourier_tables(seq):
    ct, st = _dft_cos_sin(seq)
    cb, sb = _channel_dft_blocks(1.0 / np.sqrt(seq * FN_GROUP_W))
    return tuple(_bf16_table(t) for t in (ct, st, cb, sb))


def _lat_fourier_tables(seq, chunks=8):
    r = FFT_R
    c64, s64 = _dft_cos_sin(r)
    k1 = np.arange(r)[:, None]
    t2 = np.arange(r)[None, :]
    ang = 2.0 * np.pi * (k1 * t2) / (r * r)
    twr = np.cos(ang).reshape(r, r // chunks, chunks).transpose(1, 0, 2)
    twi = (-np.sin(ang)).reshape(r, r // chunks, chunks).transpose(1, 0, 2)
    lmat = np.block([[c64, s64], [-s64, c64]])
    cb, sb = _channel_dft_blocks(1.0 / np.sqrt(seq * FN_GROUP_W))
    csmat = np.concatenate([cb, sb], axis=0)
    return (_bf16_table(c64), _bf16_table(s64), jnp.asarray(twr, F32), jnp.asarray(twi, F32),
            _bf16_table(lmat), _bf16_table(csmat))


def _rope_tables(seq):
    half = HEAD_DIM // 2
    quarter = half // 2
    t = jnp.arange(seq)
    inv = ROPE_BASE ** (-jnp.arange(quarter, dtype=F32) / quarter)

    def cs(pos):
        ang = pos.astype(F32)[:, None] * inv[None, :]
        c, s = jnp.cos(ang), jnp.sin(ang)
        return jnp.concatenate([c, c], axis=-1), jnp.concatenate([-s, s], axis=-1)

    cr, sr = cs(t // GRID_W)
    cc, sc = cs(t % GRID_W)
    cos = jnp.concatenate([cr, cc], axis=-1)
    sin = jnp.concatenate([sr, sc], axis=-1)
    return jnp.tile(cos, (1, 2)), jnp.tile(sin, (1, 2))


HALF_TILE = TILE // 2


def _tile_sources():
    starts = ([OFF_A + i * TILE for i in range(4)] + [OFF_BQ, OFF_BG, OFF_BK, None]
              + [OFF_FU + i * TILE for i in range(6)] + [N_MAIN + i * TILE for i in range(N_BRANCH * D_MODEL // TILE)])
    assert starts[T_BKV] == OFF_BK and starts[T_ZERO] is None and starts[T_FU] == OFF_FU
    src = []
    for t, start in enumerate(starts):
        halves = [None, None] if start is None else [start, start + HALF_TILE]
        if t == T_BKV:
            halves[1] = None
        for col in halves:
            assert col is None or col % HALF_TILE == 0
            src.append(-1 if col is None else col // HALF_TILE)
    assert len(starts) == N_TILES + N_BRANCH * D_MODEL // TILE
    return np.asarray(src, np.int32)


def _tile_weights_kernel(src_ref, lo_ref, hi_ref, o_ref):
    t = pl.program_id(0)
    zeros = jnp.zeros(lo_ref.shape[1:], BF16)
    o_ref[:, :HALF_TILE] = jnp.where(src_ref[2 * t] < 0, zeros, lo_ref[0].astype(BF16))
    o_ref[:, HALF_TILE:] = jnp.where(src_ref[2 * t + 1] < 0, zeros, hi_ref[0].astype(BF16))


def _tiled_in_weights(w_in, layer):
    src = _tile_sources()
    n_out_tiles = len(src) // 2
    half = lambda h: pl.BlockSpec(
        (1, D_MODEL, HALF_TILE), functools.partial(lambda t, s, h: (layer, 0, jnp.maximum(s[2 * t + h], 0)), h=h))
    return pl.pallas_call(
        _tile_weights_kernel,
        grid_spec=pltpu.PrefetchScalarGridSpec(
            num_scalar_prefetch=1,
            grid=(n_out_tiles,),
            in_specs=[half(0), half(1)],
            out_specs=pl.BlockSpec((D_MODEL, TILE), lambda t, s: (0, t)),
        ),
        out_shape=jax.ShapeDtypeStruct((D_MODEL, n_out_tiles * TILE), BF16),
        name="tile_weights",
    )(jnp.asarray(src), w_in, w_in)


def kernel(x_prompt, x_sample, cache_win_k, cache_win_v, cache_na_k, cache_na_v, c, c_ctx, norm_g, w_ada, b_ada,
           w_in, conv_w, win_sink, na_rpb, w_branch, w_out, final_g):
    batch, seq, _ = x_prompt.shape
    dbatch, dseq, _ = x_sample.shape
    past = cache_win_k.shape[2]

    cv8 = jnp.zeros((8, D_MODEL), F32).at[0].set(c_ctx).at[1:1 + dbatch].set(c)
    mod = _modulation(cv8, w_ada, b_ada)

    ctx_tabs = _ctx_fourier_tables(seq)
    lat_tabs = _lat_fourier_tables(dseq)
    cos_t, sin_t = _rope_tables(dseq)
    na_bias = _na_bias_tables(na_rpb)
    final_g2 = final_g.reshape(1, D_MODEL)

    xp = x_prompt.reshape(batch * seq, D_MODEL)
    xs = x_sample.reshape(dbatch * dseq, D_MODEL)
    new_kv = [jnp.zeros((batch, DEPTH, seq, width), F32) for width in (KV_W, KV_W, BRANCH_W, BRANCH_W)]

    for l in range(DEPTH):
        shift = mod[l, :, 0:D_MODEL].reshape(8, 1, D_MODEL)
        scale = mod[l, :, D_MODEL:2 * D_MODEL].reshape(8, 1, D_MODEL)
        gate = mod[l, :, 2 * D_MODEL:].reshape(8, 1, D_MODEL)
        g = norm_g[l].reshape(1, D_MODEL)
        w_p = _tiled_in_weights(w_in, l)
        w_br = w_branch[l].astype(BF16)
        w_o = w_out[l].astype(BF16)
        cw = conv_w[l]
        sink_b = jnp.broadcast_to(win_sink[l][:, None], (WIN_HEADS, LANES))
        final = l == DEPTH - 1

        p, hp, *new_kv = _inproj(
            xp, shift, scale, g, w_p, rows_per_cond=batch * seq, cond0=0, update=(new_kv, l, seq),
            extras=((T_BKV, 0, KV_W), (T_BKV, KV_W, KV_W), (T_DK, 0, BRANCH_W), (T_DV, 0, BRANCH_W)))
        o_ctx = _ctx_mixers(p, cw, sink_b, *ctx_tabs, seq=seq)
        xp = _out_stage(xp, hp, gate, final_g2, [o_ctx] * 4, [0, 1, 2, 3], w_p, w_br, w_o,
                        rows_per_cond=batch * seq, cond0=0, final=final)

        q, hs, qfu, qfg = _inproj(xs, shift, scale, g, w_p, rows_per_cond=dseq, cond0=1, extra_dtype=BF16,
                              extras=((T_FU, 0, BRANCH_W), (T_FG, 0, BRANCH_W)))
        o_a = _lat_conv(q, cw, batch=dbatch, seq=dseq)
        o_w = _lat_window(q, cos_t, sin_t,
                          cache_win_k[:, l].reshape(dbatch, past, KV_W), cache_win_v[:, l].reshape(dbatch, past, KV_W),
                          sink_b, batch=dbatch, seq=dseq)
        o_f = _lat_fourier(qfu, qfg, lat_tabs, batch=dbatch, seq=dseq)
        o_n = _lat_na(q, na_bias[l],
                      cache_na_k[:, l].reshape(dbatch, past, BRANCH_W), cache_na_v[:, l].reshape(dbatch, past, BRANCH_W),
                      batch=dbatch, seq=dseq)
        xs = _out_stage(xs, hs, gate, final_g2, [o_a, o_w, o_f, o_n], [0, 0, 0, 0], w_p, w_br, w_o,
                        rows_per_cond=dseq, cond0=1, final=final)

    y_prompt = xp.reshape(batch, seq, D_MODEL)
    y_sample = xs.reshape(dbatch, dseq, D_MODEL)
    heads = (WIN_KV_HEADS, WIN_KV_HEADS, NA_HEADS, NA_HEADS)
    return (y_prompt, y_sample) + tuple(a.reshape(batch, DEPTH, seq, n, HEAD_DIM) for a, n in zip(new_kv, heads))
```

```python
import functools
import math

import numpy as np
import jax
import jax.numpy as jnp
from jax import lax
from jax.experimental import pallas as pl
from jax.experimental.pallas import tpu as pltpu

D_MODEL = 2048
DEPTH = 2
GRID_W = 64
N_BRANCH = 4
BRANCH_W = 512
HEAD_DIM = 64
WIN_HEADS = 8
WIN_KV_HEADS = 2
KV_W = WIN_KV_HEADS * HEAD_DIM
WIN_BLOCK = 128
FN_GROUPS = 4
FN_GROUP_W = BRANCH_W // FN_GROUPS
NA_HEADS = 8
NA_ROWS = 8
NA_COLS = 16
ROPE_BASE = 10000.0
EPS = 1e-6
ATTN_SCALE = HEAD_DIM ** -0.5
NEG_INF = -1e30
N_MAIN = 6400

F32 = jnp.float32
BF16 = jnp.bfloat16
LANES = 128

OFF_A = 0
OFF_BQ = OFF_A + 4 * BRANCH_W
OFF_BK = OFF_BQ + BRANCH_W
OFF_BG = OFF_BK + 2 * KV_W
OFF_FU = OFF_BG + BRANCH_W
assert OFF_FU + 6 * BRANCH_W == N_MAIN

TILE = 512
T_AX, T_AB, T_AC, T_AG, T_BQ, T_BG, T_BKV, T_ZERO, T_FU, T_FG, T_DQ, T_DK, T_DV, T_DG = range(14)
STEP_TILES = 2
N_TILES = 14
P_WIDTH = N_TILES * TILE

V7X_VMEM_BYTES = 64 * 2 ** 20
PROJ_VMEM_BYTES = V7X_VMEM_BYTES - 2 ** 20

INPROJ_ROWS = 1024
OUT_ROWS = 512
MOD_COLS = 512
CONV_ROWS = 512
FFT1_CHUNKS = 8
FFT2_KBLOCK = 8

FFT_R = 64


def _silu(x):
    return x * jax.nn.sigmoid(x)


def _dot(a, b):
    return jnp.dot(a, b, preferred_element_type=F32)


def _dot_nt(a, b):
    return lax.dot_general(a, b, (((1,), (1,)), ((), ())), preferred_element_type=F32)


def _rms_mod(x, g, scale, shift):
    ms = jnp.mean(x * x, axis=-1, keepdims=True)
    return (x * lax.rsqrt(ms + EPS) * g) * (1.0 + scale) + shift


NORM_ROWS = 256


def _mod_kernel(cv_ref, w_ref, b_ref, o_ref):
    s = _silu(cv_ref[...]).astype(BF16)
    o_ref[0] = _dot(s, w_ref[0].astype(BF16)) + b_ref[0]


def _modulation(cv8, w_ada, b_ada):
    tn = MOD_COLS
    n3 = 3 * D_MODEL
    return pl.pallas_call(
        _mod_kernel,
        grid=(DEPTH, n3 // tn),
        in_specs=[
            pl.BlockSpec((8, D_MODEL), lambda l, j: (0, 0)),
            pl.BlockSpec((1, D_MODEL, tn), lambda l, j: (l, 0, j)),
            pl.BlockSpec((1, 1, tn), lambda l, j: (l, 0, j)),
        ],
        out_specs=pl.BlockSpec((1, 8, tn), lambda l, j: (l, 0, j)),
        out_shape=jax.ShapeDtypeStruct((DEPTH, 8, n3), F32),
        name="modulation",
    )(cv8, w_ada, b_ada.reshape(DEPTH, 1, n3))


def _inproj_kernel(x_ref, sh_ref, sc_ref, g_ref, w_ref, *rest, extras, n_updated):
    p_ref, h_ref, extra_refs = rest[n_updated], rest[n_updated + 1], rest[n_updated + 2:]
    j = pl.program_id(1)
    assert STEP_TILES == 2 and T_ZERO % 2 == 1 and 0 < T_ZERO // 2 < N_TILES // 2 - 1
    half_step = T_ZERO // STEP_TILES
    assert all(tile >= STEP_TILES for tile, _ in extras)

    def store_extras(acc, in_half_step):
        for ref, (tile, lo) in zip(extra_refs, extras):
            lo = (tile % STEP_TILES) * TILE + lo
            if (tile // STEP_TILES == half_step) != in_half_step:
                continue
            assert lo + ref.shape[-1] <= acc.shape[1]

            @pl.when(j == tile // STEP_TILES)
            def _(ref=ref, lo=lo):
                cols = acc[:, lo:lo + ref.shape[-1]].astype(ref.dtype)
                if ref.ndim == 2:
                    ref[...] = cols
                else:
                    ref[:, 0] = cols.reshape(ref.shape[0], ref.shape[2], ref.shape[3])

    @pl.when(j == 0)
    def _():
        g, scale, shift = g_ref[...], sc_ref[0], sh_ref[0]
        for c in range(h_ref.shape[0] // NORM_ROWS):
            rows = slice(c * NORM_ROWS, (c + 1) * NORM_ROWS)
            h = _rms_mod(x_ref[rows, :], g, scale, shift).astype(h_ref.dtype)
            h_ref[rows, :] = h
            p_ref[rows, :] = _dot(h, w_ref[...]).astype(p_ref.dtype)

    @pl.when(j == half_step)
    def _():
        acc = _dot(h_ref[...], w_ref[:, :TILE])
        p_ref[:, :TILE] = acc.astype(p_ref.dtype)
        p_ref[:, TILE:] = jnp.zeros((p_ref.shape[0], TILE), p_ref.dtype)
        store_extras(acc, True)

    @pl.when((j > 0) & (j != half_step))
    def _():
        acc = _dot(h_ref[...], w_ref[...])
        p_ref[...] = acc.astype(p_ref.dtype)
        store_extras(acc, False)


def _inproj(x2d, shift, scale, g, w, *, rows_per_cond, cond0, extras, extra_dtype=None, update=None, tm=INPROJ_ROWS):
    m = x2d.shape[0]
    assert w.shape[0] == D_MODEL and w.shape[1] >= P_WIDTH and m % tm == 0 and rows_per_cond % tm == 0
    tiles_per_cond = rows_per_cond // tm
    cond = lambda i, j: (cond0 + i // tiles_per_cond, 0, 0)
    if update is None:
        updated, aliases = [], {}
        extra_specs = [pl.BlockSpec((tm, width), lambda i, j: (i, 0)) for _, _, width in extras]
        extra_shapes = [jax.ShapeDtypeStruct((m, width), extra_dtype) for _, _, width in extras]
    else:
        updated, layer, seq = update
        assert tm % seq == 0 and all(a.shape[2:] == (seq, width) for a, (_, _, width) in zip(updated, extras))
        extra_specs = [pl.BlockSpec((tm // seq, 1, seq, width), lambda i, j: (i, layer, 0, 0)) for _, _, width in extras]
        extra_shapes = [jax.ShapeDtypeStruct(a.shape, a.dtype) for a in updated]
        aliases = {5 + k: 2 + k for k in range(len(updated))}
    return pl.pallas_call(
        functools.partial(_inproj_kernel, extras=tuple((t, lo) for t, lo, _ in extras), n_updated=len(updated)),
        grid=(m // tm, N_TILES // STEP_TILES),
        in_specs=[
            pl.BlockSpec((tm, D_MODEL), lambda i, j: (i, 0)),
            pl.BlockSpec((1, 1, D_MODEL), cond),
            pl.BlockSpec((1, 1, D_MODEL), cond),
            pl.BlockSpec((1, D_MODEL), lambda i, j: (0, 0)),
            pl.BlockSpec((D_MODEL, STEP_TILES * TILE), lambda i, j: (0, j)),
        ] + [pl.BlockSpec(memory_space=pl.ANY) for _ in updated],
        out_specs=[pl.BlockSpec((tm, STEP_TILES * TILE), lambda i, j: (i, j)),
                   pl.BlockSpec((tm, D_MODEL), lambda i, j: (i, 0))]
        + extra_specs,
        out_shape=[jax.ShapeDtypeStruct((m, P_WIDTH), BF16), jax.ShapeDtypeStruct((m, D_MODEL), BF16)] + extra_shapes,
        input_output_aliases=aliases,
        compiler_params=pltpu.CompilerParams(dimension_semantics=("parallel", "arbitrary"),
                                             vmem_limit_bytes=PROJ_VMEM_BYTES),
        name="inproj",
    )(x2d, shift, scale, g, w, *updated)


def _out_kernel(x_ref, h_ref, gt_ref, fg_ref, oa_ref, ow_ref, of_ref, on_ref,
                wg0_ref, wg1_ref, wg2_ref, wg3_ref, wb_ref, wo_ref, y_ref, merged_ref, *, final):
    j = pl.program_id(1)
    h = h_ref[...]
    merged = None
    for i, (o_ref, wg_ref) in enumerate(((oa_ref, wg0_ref), (ow_ref, wg1_ref), (of_ref, wg2_ref), (on_ref, wg3_ref))):
        term = jax.nn.sigmoid(_dot(h, wg_ref[...])) * _dot(o_ref[...], wb_ref[i])
        merged = term if merged is None else merged + term
    merged_ref[j] = merged.astype(BF16)

    @pl.when(j == pl.num_programs(1) - 1)
    def _():
        m_all = jnp.concatenate([merged_ref[k] for k in range(merged_ref.shape[0])], axis=1)
        xn = x_ref[...] + gt_ref[0] * _dot(m_all, wo_ref[...])
        if final:
            ms = jnp.mean(xn * xn, axis=-1, keepdims=True)
            xn = xn * lax.rsqrt(ms + EPS) * fg_ref[...]
        y_ref[...] = xn


def _out_stage(x2d, h2d, gate, final_g, o_arrays, o_cols, w_gate, w_branch, w_out,
               *, rows_per_cond, cond0, final, tm=OUT_ROWS, tc=TILE):
    m = x2d.shape[0]
    assert m % tm == 0 and rows_per_cond % tm == 0
    tiles_per_cond = rows_per_cond // tm
    ncol = D_MODEL // tc
    cond = lambda i, j: (cond0 + i // tiles_per_cond, 0, 0)
    o_specs = [pl.BlockSpec((tm, BRANCH_W), functools.partial(lambda i, j, c: (i, c), c=c)) for c in o_cols]
    assert tc == TILE and w_gate.shape == (D_MODEL, P_WIDTH + N_BRANCH * D_MODEL)
    wg_specs = [pl.BlockSpec((D_MODEL, tc), functools.partial(lambda i, j, b: (0, N_TILES + b * ncol + j), b=b))
                for b in range(N_BRANCH)]
    return pl.pallas_call(
        functools.partial(_out_kernel, final=final),
        grid=(m // tm, ncol),
        in_specs=[
            pl.BlockSpec((tm, D_MODEL), lambda i, j: (i, 0)),
            pl.BlockSpec((tm, D_MODEL), lambda i, j: (i, 0)),
            pl.BlockSpec((1, 1, D_MODEL), cond),
            pl.BlockSpec((1, D_MODEL), lambda i, j: (0, 0)),
            *o_specs,
            *wg_specs,
            pl.BlockSpec((N_BRANCH, BRANCH_W, tc), lambda i, j: (0, 0, j)),
            pl.BlockSpec((D_MODEL, D_MODEL), lambda i, j: (0, 0), pipeline_mode=pl.Buffered(1)),
        ],
        out_specs=pl.BlockSpec((tm, D_MODEL), lambda i, j: (i, 0)),
        out_shape=jax.ShapeDtypeStruct((m, D_MODEL), F32),
        scratch_shapes=[pltpu.VMEM((ncol, tm, tc), BF16)],
        compiler_params=pltpu.CompilerParams(dimension_semantics=("parallel", "arbitrary"),
                                             vmem_limit_bytes=PROJ_VMEM_BYTES),
        name="out_stage",
    )(x2d, h2d, gate, final_g, *o_arrays, w_gate, w_gate, w_gate, w_gate, w_branch, w_out)


def _attend(scores, values, sinks=None):
    heads = range(len(scores))
    sinks = [None] * len(scores) if sinks is None else sinks
    m = []
    for h in heads:
        mh = scores[h][0].max(axis=-1, keepdims=True)
        for s in scores[h][1:]:
            mh = jnp.maximum(mh, s.max(axis=-1, keepdims=True))
        m.append(mh if sinks[h] is None else jnp.maximum(mh, sinks[h]))
    e = [[jnp.exp(s - m[h]) for s in scores[h]] for h in heads]
    den = []
    for h in heads:
        d = e[h][0].sum(axis=-1, keepdims=True)
        for x in e[h][1:]:
            d = d + x.sum(axis=-1, keepdims=True)
        den.append(d if sinks[h] is None else d + jnp.exp(sinks[h] - m[h]))
    out = []
    for h in heads:
        acc = _dot(e[h][0].astype(BF16), values[h][0])
        for x, v in zip(e[h][1:], values[h][1:]):
            acc = acc + _dot(x.astype(BF16), v)
        out.append(acc)
    return [a / d for a, d in zip(out, den)]


def _shift_rows(z, first_row, last_row):
    n = z.shape[0]
    row = lax.broadcasted_iota(jnp.int32, z.shape, 0)
    z_dn = jnp.where(row == 0, first_row, pltpu.roll(z, 1, axis=0))
    z_up = jnp.where(row == n - 1, last_row, pltpu.roll(z, n - 1, axis=0))
    return z_dn, z_up


def _tile(ref, t, dtype=None):
    v = ref[:, t * TILE:(t + 1) * TILE]
    return v if dtype is None else v.astype(dtype)


def _head(ref, t, h, base=0):
    lo = t * TILE + base + h * HEAD_DIM
    return ref[:, lo:lo + HEAD_DIM]


def _scaled_queries(q):
    assert math.frexp(ATTN_SCALE)[0] == 0.5
    return q * ATTN_SCALE


def _ctx_mixer_kernel(p_ref, cw_ref, sink_ref, ct_ref, st_ref, cb_ref, sb_ref, o_ref):
    z = _tile(p_ref, T_AC, F32) * _tile(p_ref, T_AX, F32)
    zero_row = jnp.zeros((1, BRANCH_W), F32)
    z_dn, z_up = _shift_rows(z, zero_row, zero_row)
    y = _tile(p_ref, T_AB, F32) * (z_dn * cw_ref[0:1, :] + z * cw_ref[1:2, :] + z_up * cw_ref[2:3, :])
    out = lambda b: slice(b * BRANCH_W, (b + 1) * BRANCH_W)
    o_ref[:, out(0)] = (y * _silu(_tile(p_ref, T_AG, F32))).astype(o_ref.dtype)

    gsz = WIN_HEADS // WIN_KV_HEADS
    head = lambda a, h: a[:, h * HEAD_DIM:(h + 1) * HEAD_DIM]
    qb = _scaled_queries(_tile(p_ref, T_BQ))
    qd = _scaled_queries(_tile(p_ref, T_DQ))
    scores = ([[_dot_nt(head(qb, h), _head(p_ref, T_BKV, h // gsz))] for h in range(WIN_HEADS)]
              + [[_dot_nt(head(qd, h), _head(p_ref, T_DK, h))] for h in range(NA_HEADS)])
    values = ([[_head(p_ref, T_BKV, h // gsz, base=KV_W)] for h in range(WIN_HEADS)]
              + [[_head(p_ref, T_DV, h)] for h in range(NA_HEADS)])
    sinks = [sink_ref[h:h + 1, 0:1] for h in range(WIN_HEADS)] + [None] * NA_HEADS
    heads = _attend(scores, values, sinks=sinks)
    o_w = jnp.concatenate(heads[:WIN_HEADS], axis=-1) * _silu(_tile(p_ref, T_BG, F32))
    o_ref[:, out(1)] = o_w.astype(o_ref.dtype)
    o_n = jnp.concatenate(heads[WIN_HEADS:], axis=-1) * _silu(_tile(p_ref, T_DG, F32))
    o_ref[:, out(3)] = o_n.astype(o_ref.dtype)

    u = _tile(p_ref, T_FU)
    uc = _dot(u, cb_ref[...]).astype(BF16)
    us = _dot(u, sb_ref[...]).astype(BF16)
    o_f = (_dot(ct_ref[...], uc) - _dot(st_ref[...], us)) * _silu(_tile(p_ref, T_FG, F32))
    o_ref[:, out(2)] = o_f.astype(o_ref.dtype)


def _ctx_mixers(p, conv_w, sink_b, ct, st, cb, sb, *, seq):
    m = p.shape[0]
    whole = lambda a: pl.BlockSpec(a.shape, lambda b: (0,) * a.ndim)
    return pl.pallas_call(
        _ctx_mixer_kernel,
        grid=(m // seq,),
        in_specs=[pl.BlockSpec((seq, P_WIDTH), lambda b: (b, 0)), whole(conv_w), whole(sink_b),
                  whole(ct), whole(st), whole(cb), whole(sb)],
        out_specs=pl.BlockSpec((seq, N_BRANCH * BRANCH_W), lambda b: (b, 0)),
        out_shape=jax.ShapeDtypeStruct((m, N_BRANCH * BRANCH_W), BF16),
        compiler_params=pltpu.CompilerParams(dimension_semantics=("parallel",)),
        name="ctx_mixers",
    )(p, conv_w, sink_b, ct, st, cb, sb)


HALO = 16


def _lat_conv_kernel(ax_ref, ab_ref, ac_ref, ag_ref, axp_ref, acp_ref, axn_ref, acn_ref, cw_ref, o_ref):
    i = pl.program_id(1)
    z = ac_ref[...].astype(F32) * ax_ref[...].astype(F32)
    zp = acp_ref[HALO - 1:HALO, :].astype(F32) * axp_ref[HALO - 1:HALO, :].astype(F32)
    zn = acn_ref[0:1, :].astype(F32) * axn_ref[0:1, :].astype(F32)
    zp = jnp.where(i == 0, 0.0, zp)
    zn = jnp.where(i == pl.num_programs(1) - 1, 0.0, zn)
    z_dn, z_up = _shift_rows(z, zp, zn)
    y = ab_ref[...].astype(F32) * (z_dn * cw_ref[0:1, :] + z * cw_ref[1:2, :] + z_up * cw_ref[2:3, :])
    o_ref[...] = (y * _silu(ag_ref[...].astype(F32))).astype(o_ref.dtype)


def _lat_conv(pa, conv_w, *, batch, seq, tr=CONV_ROWS):
    nt = seq // tr
    hb = tr // HALO
    last_halo = batch * seq // HALO - 1
    main = lambda c: pl.BlockSpec((tr, BRANCH_W), functools.partial(lambda b, i, c: (b * nt + i, c), c=c))
    prev = lambda c: pl.BlockSpec(
        (HALO, BRANCH_W), functools.partial(lambda b, i, c: (jnp.maximum((b * nt + i) * hb - 1, 0), c), c=c))
    nxt = lambda c: pl.BlockSpec(
        (HALO, BRANCH_W), functools.partial(lambda b, i, c: (jnp.minimum((b * nt + i + 1) * hb, last_halo), c), c=c))
    return pl.pallas_call(
        _lat_conv_kernel,
        grid=(batch, nt),
        in_specs=[main(T_AX), main(T_AB), main(T_AC), main(T_AG), prev(T_AX), prev(T_AC), nxt(T_AX), nxt(T_AC),
                  pl.BlockSpec(conv_w.shape, lambda b, i: (0, 0))],
        out_specs=pl.BlockSpec((tr, BRANCH_W), lambda b, i: (b * nt + i, 0)),
        out_shape=jax.ShapeDtypeStruct((batch * seq, BRANCH_W), BF16),
        compiler_params=pltpu.CompilerParams(dimension_semantics=("parallel", "arbitrary")),
        name="lat_conv",
    )(pa, pa, pa, pa, pa, pa, pa, pa, conv_w)


def _rope(x, cos, sin_signed):
    lane = lax.broadcasted_iota(jnp.int32, x.shape, 1)
    quarter = HEAD_DIM // 4
    partner = jnp.where((lane % (2 * quarter)) < quarter,
                        pltpu.roll(x, LANES - quarter, axis=1), pltpu.roll(x, quarter, axis=1))
    return x * cos + partner * sin_signed


WIN_BLOCKS_PER_STEP = 8


def _lat_win_kernel(q_ref, g_ref, kp_ref, kc_ref, kn_ref, vp_ref, vc_ref, vn_ref,
                    cq_ref, sq_ref, cp_ref, sp_ref, cn_ref, sn_ref, ck_ref, cv_ref, sink_ref, o_ref):
    nq = WIN_BLOCKS_PER_STEP
    nb = pl.num_programs(1) * nq
    wb = WIN_BLOCK
    cq, sq = cq_ref[...], sq_ref[...]
    k_rot = jnp.concatenate([
        _rope(kp_ref[...].astype(F32), cp_ref[...], sp_ref[...]),
        _rope(kc_ref[...].astype(F32), cq, sq),
        _rope(kn_ref[...].astype(F32), cn_ref[...], sn_ref[...]),
    ], axis=0).astype(BF16)
    v_all = jnp.concatenate([vp_ref[...], vc_ref[...], vn_ref[...]], axis=0).astype(BF16)
    qi = lax.broadcasted_iota(jnp.int32, (wb, 3 * wb), 0)
    kj = lax.broadcasted_iota(jnp.int32, (wb, 3 * wb), 1)
    ck = ck_ref[0].astype(BF16)
    cv = cv_ref[0].astype(BF16)
    gsz = WIN_HEADS // WIN_KV_HEADS
    q_rot = [_scaled_queries(_rope(q_ref[:, pair * LANES:(pair + 1) * LANES].astype(F32), cq, sq)).astype(BF16)
             for pair in range(WIN_HEADS // 2)]
    scores, values, sinks = [], [], []
    for d in range(nq):
        n = pl.program_id(1) * nq + d
        mask = (((kj < wb) & (kj >= qi) & (n > 0)) | ((kj >= wb) & (kj < 2 * wb))
                | ((kj >= 2 * wb) & (kj - 2 * wb <= qi) & (n < nb - 1)))
        rows = slice(d * wb, (d + 1) * wb)
        win = slice(d * wb, (d + 3) * wb)
        for h in range(WIN_HEADS):
            kv = slice((h // gsz) * HEAD_DIM, (h // gsz + 1) * HEAD_DIM)
            q = q_rot[h // 2][rows, (h % 2) * HEAD_DIM:(h % 2 + 1) * HEAD_DIM]
            s_loc = jnp.where(mask, _dot_nt(q, k_rot[win, kv]), NEG_INF)
            scores.append([s_loc, _dot_nt(q, ck[:, kv])])
            values.append([v_all[win, kv], cv[:, kv]])
            sinks.append(sink_ref[h:h + 1, 0:1])
    heads = _attend(scores, values, sinks=sinks)
    o = jnp.concatenate([jnp.concatenate(heads[d * WIN_HEADS:(d + 1) * WIN_HEADS], axis=-1) for d in range(nq)], axis=0)
    o_ref[...] = (o * _silu(g_ref[...].astype(F32))).astype(o_ref.dtype)


def _lat_window(pb, cos_t, sin_t, ctx_k, ctx_v, sink_b, *, batch, seq):
    wb = WIN_BLOCK
    k_col = T_BKV * (TILE // KV_W)
    v_col = k_col + 1
    nq = WIN_BLOCKS_PER_STEP
    nb = seq // wb
    steps = nb // nq
    nbr = lambda s, d: jnp.clip(s * nq + (nq if d > 0 else -1), 0, nb - 1)
    own = lambda col, width: pl.BlockSpec(
        (nq * wb, width), functools.partial(lambda b, s, col: (b * steps + s, col), col=col))
    kv = lambda col, d: pl.BlockSpec(
        (wb, KV_W), functools.partial(lambda b, s, col, d: (b * nb + nbr(s, d), col), col=col, d=d))
    tab = lambda d: pl.BlockSpec((wb, LANES), functools.partial(lambda b, s, d: (nbr(s, d), 0), d=d))
    tab_own = pl.BlockSpec((nq * wb, LANES), lambda b, s: (s, 0))
    ctx = pl.BlockSpec((1,) + ctx_k.shape[1:], lambda b, s: (b, 0, 0))
    return pl.pallas_call(
        _lat_win_kernel,
        grid=(batch, steps),
        in_specs=[
            own(T_BQ, BRANCH_W), own(T_BG, BRANCH_W),
            kv(k_col, -1), own(k_col, KV_W), kv(k_col, 1), kv(v_col, -1), own(v_col, KV_W), kv(v_col, 1),
            tab_own, tab_own, tab(-1), tab(-1), tab(1), tab(1),
            ctx, ctx,
            pl.BlockSpec(sink_b.shape, lambda b, s: (0, 0)),
        ],
        out_specs=pl.BlockSpec((nq * wb, BRANCH_W), lambda b, s: (b * steps + s, 0)),
        out_shape=jax.ShapeDtypeStruct((batch * seq, BRANCH_W), BF16),
        compiler_params=pltpu.CompilerParams(dimension_semantics=("parallel", "arbitrary")),
        name="lat_window",
    )(pb, pb, pb, pb, pb, pb, pb, pb, cos_t, sin_t, cos_t, sin_t, cos_t, sin_t, ctx_k, ctx_v, sink_b)


def _fft1_kernel(u_ref, c_ref, s_ref, twr_ref, twi_ref, zr_ref, zi_ref, *, chunks):
    u = u_ref[0].astype(BF16)
    yr = _dot(c_ref[...], u)
    yi = -_dot(s_ref[...], u)
    for t in range(chunks):
        wr = twr_ref[0, :, t:t + 1]
        wi = twi_ref[0, :, t:t + 1]
        a = yr[:, t * BRANCH_W:(t + 1) * BRANCH_W]
        b = yi[:, t * BRANCH_W:(t + 1) * BRANCH_W]
        zr_ref[0, :, t * BRANCH_W:(t + 1) * BRANCH_W] = (a * wr - b * wi).astype(zr_ref.dtype)
        zi_ref[0, :, t * BRANCH_W:(t + 1) * BRANCH_W] = (a * wi + b * wr).astype(zi_ref.dtype)


def _fft2_kernel(zr_ref, zi_ref, g_ref, l_ref, cs_ref, o_ref, *, kblock):
    r = FFT_R
    zz = [jnp.concatenate([zr_ref[0, kk], zi_ref[0, kk]], axis=0) for kk in range(kblock)]
    xx = [_dot(l_ref[...], z) for z in zz]
    xcat = [jnp.concatenate([x[0:r], x[r:2 * r]], axis=1).astype(BF16) for x in xx]
    out = [_dot(x, cs_ref[...]) for x in xcat]
    for kk in range(kblock):
        sl = slice(kk * BRANCH_W, (kk + 1) * BRANCH_W)
        o_ref[0, :, sl] = (out[kk] * _silu(g_ref[0, :, sl].astype(F32))).astype(o_ref.dtype)


def _lat_fourier(fu, fg, tabs, *, batch, seq):
    r = FFT_R
    assert seq == r * r
    wide = r * BRANCH_W
    chunks = FFT1_CHUNKS
    c64, s64, twr, twi, lmat, csmat = tabs
    u3 = fu.reshape(batch, r, wide)
    nct = r // chunks
    zr, zi = pl.pallas_call(
        functools.partial(_fft1_kernel, chunks=chunks),
        grid=(batch, nct),
        in_specs=[
            pl.BlockSpec((1, r, chunks * BRANCH_W), lambda b, t: (b, 0, t)),
            pl.BlockSpec((r, r), lambda b, t: (0, 0)),
            pl.BlockSpec((r, r), lambda b, t: (0, 0)),
            pl.BlockSpec((1, r, chunks), lambda b, t: (t, 0, 0)),
            pl.BlockSpec((1, r, chunks), lambda b, t: (t, 0, 0)),
        ],
        out_specs=[pl.BlockSpec((1, r, chunks * BRANCH_W), lambda b, t: (b, 0, t))] * 2,
        out_shape=[jax.ShapeDtypeStruct((batch, r, wide), BF16)] * 2,
        compiler_params=pltpu.CompilerParams(dimension_semantics=("parallel", "arbitrary")),
        name="lat_fft1",
    )(u3, c64, s64, twr, twi)
    kblock = FFT2_KBLOCK
    z4 = lambda z: z.reshape(batch, r, r, BRANCH_W)
    out = pl.pallas_call(
        functools.partial(_fft2_kernel, kblock=kblock),
        grid=(batch, r // kblock),
        in_specs=[
            pl.BlockSpec((1, kblock, r, BRANCH_W), lambda b, k: (b, k, 0, 0)),
            pl.BlockSpec((1, kblock, r, BRANCH_W), lambda b, k: (b, k, 0, 0)),
            pl.BlockSpec((1, r, kblock * BRANCH_W), lambda b, k: (b, 0, k)),
            pl.BlockSpec((2 * r, 2 * r), lambda b, k: (0, 0)),
            pl.BlockSpec((2 * BRANCH_W, BRANCH_W), lambda b, k: (0, 0)),
        ],
        out_specs=pl.BlockSpec((1, r, kblock * BRANCH_W), lambda b, k: (b, 0, k)),
        out_shape=jax.ShapeDtypeStruct((batch, r, wide), BF16),
        compiler_params=pltpu.CompilerParams(dimension_semantics=("parallel", "arbitrary")),
        name="lat_fft2",
    )(z4(zr), z4(zi), fg.reshape(batch, r, wide), lmat, csmat)
    return out.reshape(batch * seq, BRANCH_W)


N_RPB_ROWS = 2 * NA_ROWS - 1
N_RPB_COLS = 2 * NA_COLS - 1


NA_ROWS_PER_STEP = 8


def _lat_na_kernel(q_ref, g_ref, k_ref, v_ref, bias_ref, ck_ref, cv_ref, o_ref, *, rows):
    ck = ck_ref[0].astype(BF16)
    cv = cv_ref[0].astype(BF16)
    sls = [slice(h * HEAD_DIM, (h + 1) * HEAD_DIM) for h in range(NA_HEADS)]
    scores, values = [], []
    for d in range(NA_ROWS_PER_STEP):
        r = pl.program_id(1) * NA_ROWS_PER_STEP + d
        r0 = jnp.clip(r - NA_ROWS // 2, 0, rows - NA_ROWS)
        start = pl.multiple_of(r0 * GRID_W, GRID_W)
        kw = k_ref[pl.ds(start, NA_ROWS * GRID_W), :]
        vw = v_ref[pl.ds(start, NA_ROWS * GRID_W), :]
        qa = _scaled_queries(q_ref[d * GRID_W:(d + 1) * GRID_W, :])
        a0 = r0 - r + NA_ROWS - 1
        for h, sl in enumerate(sls):
            bias = jnp.concatenate([bias_ref[h, a0 + 2 * i2] for i2 in range(NA_ROWS // 2)], axis=1)
            scores.append([_dot_nt(qa[:, sl], kw[:, sl]) + bias, _dot_nt(qa[:, sl], ck[:, sl])])
            values.append([vw[:, sl], cv[:, sl]])
    heads = _attend(scores, values)
    o = jnp.concatenate([jnp.concatenate(heads[d * NA_HEADS:(d + 1) * NA_HEADS], axis=-1)
                         for d in range(NA_ROWS_PER_STEP)], axis=0)
    o_ref[...] = (o * _silu(g_ref[...].astype(F32))).astype(o_ref.dtype)


def _lat_na(p, bias, ctx_k, ctx_v, *, batch, seq):
    rows = seq // GRID_W
    steps = rows // NA_ROWS_PER_STEP
    qrows = NA_ROWS_PER_STEP * GRID_W
    ctx = pl.BlockSpec((1,) + ctx_k.shape[1:], lambda b, r: (b, 0, 0))
    return pl.pallas_call(
        functools.partial(_lat_na_kernel, rows=rows),
        grid=(batch, steps),
        in_specs=[
            pl.BlockSpec((qrows, BRANCH_W), lambda b, r: (b * steps + r, T_DQ)),
            pl.BlockSpec((qrows, BRANCH_W), lambda b, r: (b * steps + r, T_DG)),
            pl.BlockSpec((seq, BRANCH_W), lambda b, r: (b, T_DK)),
            pl.BlockSpec((seq, BRANCH_W), lambda b, r: (b, T_DV)),
            pl.BlockSpec(bias.shape, lambda b, r: (0, 0, 0, 0)),
            ctx, ctx,
        ],
        out_specs=pl.BlockSpec((qrows, BRANCH_W), lambda b, r: (b * steps + r, 0)),
        out_shape=jax.ShapeDtypeStruct((batch * seq, BRANCH_W), BF16),
        compiler_params=pltpu.CompilerParams(dimension_semantics=("parallel", "arbitrary")),
        name="lat_na",
    )(p, p, p, p, bias, ctx_k, ctx_v)


def _toeplitz_kernel(r_ref, o_ref):
    x = r_ref[...]
    hi = x.astype(BF16)
    r1 = x - hi.astype(F32)
    mid = r1.astype(BF16)
    lo = (r1 - mid.astype(F32)).astype(BF16)
    shape = (x.shape[1], GRID_W * GRID_W)
    b = lax.broadcasted_iota(jnp.int32, shape, 0)
    col = lax.broadcasted_iota(jnp.int32, shape, 1)
    kc = jnp.bitwise_and(col, GRID_W - 1)
    qc = lax.shift_right_logical(col, GRID_W.bit_length() - 1)
    onehot = jnp.where(kc - qc + (NA_COLS - 1) == b, 1.0, 0.0).astype(BF16)
    o_ref[...] = _dot(hi, onehot) + _dot(mid, onehot) + _dot(lo, onehot)


def _na_bias_tables(na_rpb):
    depth = na_rpb.shape[0]
    n = depth * NA_HEADS * N_RPB_ROWS
    n_pad = -(-n // 8) * 8
    r2 = jnp.zeros((n_pad, LANES), F32).at[:n, :N_RPB_COLS].set(na_rpb.reshape(n, N_RPB_COLS))
    flat = pl.pallas_call(
        _toeplitz_kernel,
        out_shape=jax.ShapeDtypeStruct((n_pad, GRID_W * GRID_W), F32),
        name="na_bias_toeplitz",
    )(r2)
    t = flat[:n].reshape(depth, NA_HEADS, N_RPB_ROWS, GRID_W, GRID_W)
    qc = np.arange(GRID_W)[:, None]
    kc = np.arange(GRID_W)[None, :]
    win0 = np.clip(qc - NA_COLS // 2, 0, GRID_W - NA_COLS)
    vis = (kc >= win0) & (kc < win0 + NA_COLS)
    t = jnp.where(vis, t, NEG_INF)
    return jnp.concatenate([t[:, :, :-1], t[:, :, 1:]], axis=-1)


def _bf16_table(t):
    return jnp.asarray(t, F32).astype(BF16)


def _dft_cos_sin(n):
    idx = np.arange(n)
    ang = 2.0 * np.pi * ((idx[:, None] * idx[None, :]) % n) / n
    return np.cos(ang), np.sin(ang)


def _channel_dft_blocks(scale):
    c, s = _dft_cos_sin(FN_GROUP_W)
    eye = np.eye(FN_GROUPS)
    return np.kron(eye, c) * scale, np.kron(eye, s) * scale


def _ctx_fourier_tables(seq):
    ct, st = _dft_cos_sin(seq)
    cb, sb = _channel_dft_blocks(1.0 / np.sqrt(seq * FN_GROUP_W))
    return tuple(_bf16_table(t) for t in (ct, st, cb, sb))


def _lat_fourier_tables(seq, chunks=FFT1_CHUNKS):
    r = FFT_R
    c64, s64 = _dft_cos_sin(r)
    k1 = np.arange(r)[:, None]
    t2 = np.arange(r)[None, :]
    ang = 2.0 * np.pi * (k1 * t2) / (r * r)
    twr = np.cos(ang).reshape(r, r // chunks, chunks).transpose(1, 0, 2)
    twi = (-np.sin(ang)).reshape(r, r // chunks, chunks).transpose(1, 0, 2)
    lmat = np.block([[c64, s64], [-s64, c64]])
    cb, sb = _channel_dft_blocks(1.0 / np.sqrt(seq * FN_GROUP_W))
    csmat = np.concatenate([cb, sb], axis=0)
    return (_bf16_table(c64), _bf16_table(s64), jnp.asarray(twr, F32), jnp.asarray(twi, F32),
            _bf16_table(lmat), _bf16_table(csmat))


def _rope_tables(seq):
    half = HEAD_DIM // 2
    quarter = half // 2
    t = jnp.arange(seq)
    inv = ROPE_BASE ** (-jnp.arange(quarter, dtype=F32) / quarter)

    def cs(pos):
        ang = pos.astype(F32)[:, None] * inv[None, :]
        c, s = jnp.cos(ang), jnp.sin(ang)
        return jnp.concatenate([c, c], axis=-1), jnp.concatenate([-s, s], axis=-1)

    cr, sr = cs(t // GRID_W)
    cc, sc = cs(t % GRID_W)
    cos = jnp.concatenate([cr, cc], axis=-1)
    sin = jnp.concatenate([sr, sc], axis=-1)
    return jnp.tile(cos, (1, 2)), jnp.tile(sin, (1, 2))


HALF_TILE = TILE // 2


def _tile_sources():
    starts = ([OFF_A + i * TILE for i in range(4)] + [OFF_BQ, OFF_BG, OFF_BK, None]
              + [OFF_FU + i * TILE for i in range(6)] + [N_MAIN + i * TILE for i in range(N_BRANCH * D_MODEL // TILE)])
    assert starts[T_BKV] == OFF_BK and starts[T_ZERO] is None and starts[T_FU] == OFF_FU
    src = []
    for t, start in enumerate(starts):
        halves = [None, None] if start is None else [start, start + HALF_TILE]
        if t == T_BKV:
            halves[1] = None
        for col in halves:
            assert col is None or col % HALF_TILE == 0
            src.append(-1 if col is None else col // HALF_TILE)
    assert len(starts) == N_TILES + N_BRANCH * D_MODEL // TILE
    return np.asarray(src, np.int32)


def _tile_weights_kernel(src_ref, lo_ref, hi_ref, o_ref):
    t = pl.program_id(0)
    zeros = jnp.zeros(lo_ref.shape[1:], BF16)
    o_ref[:, :HALF_TILE] = jnp.where(src_ref[2 * t] < 0, zeros, lo_ref[0].astype(BF16))
    o_ref[:, HALF_TILE:] = jnp.where(src_ref[2 * t + 1] < 0, zeros, hi_ref[0].astype(BF16))


def _tiled_in_weights(w_in, layer):
    src = _tile_sources()
    n_out_tiles = len(src) // 2
    half = lambda h: pl.BlockSpec(
        (1, D_MODEL, HALF_TILE), functools.partial(lambda t, s, h: (layer, 0, jnp.maximum(s[2 * t + h], 0)), h=h))
    return pl.pallas_call(
        _tile_weights_kernel,
        grid_spec=pltpu.PrefetchScalarGridSpec(
            num_scalar_prefetch=1,
            grid=(n_out_tiles,),
            in_specs=[half(0), half(1)],
            out_specs=pl.BlockSpec((D_MODEL, TILE), lambda t, s: (0, t)),
        ),
        out_shape=jax.ShapeDtypeStruct((D_MODEL, n_out_tiles * TILE), BF16),
        name="tile_weights",
    )(jnp.asarray(src), w_in, w_in)


def kernel(x_prompt, x_sample, cache_win_k, cache_win_v, cache_na_k, cache_na_v, c, c_ctx, norm_g, w_ada, b_ada,
           w_in, conv_w, win_sink, na_rpb, w_branch, w_out, final_g):
    batch, seq, _ = x_prompt.shape
    dbatch, dseq, _ = x_sample.shape
    past = cache_win_k.shape[2]

    cv8 = jnp.zeros((8, D_MODEL), F32).at[0].set(c_ctx).at[1:1 + dbatch].set(c)
    mod = _modulation(cv8, w_ada, b_ada)

    ctx_tabs = _ctx_fourier_tables(seq)
    lat_tabs = _lat_fourier_tables(dseq)
    cos_t, sin_t = _rope_tables(dseq)
    na_bias = _na_bias_tables(na_rpb)
    final_g2 = final_g.reshape(1, D_MODEL)

    xp = x_prompt.reshape(batch * seq, D_MODEL)
    xs = x_sample.reshape(dbatch * dseq, D_MODEL)
    new_kv = [jnp.zeros((batch, DEPTH, seq, width), F32) for width in (KV_W, KV_W, BRANCH_W, BRANCH_W)]

    for l in range(DEPTH):
        shift = mod[l, :, 0:D_MODEL].reshape(8, 1, D_MODEL)
        scale = mod[l, :, D_MODEL:2 * D_MODEL].reshape(8, 1, D_MODEL)
        gate = mod[l, :, 2 * D_MODEL:].reshape(8, 1, D_MODEL)
        g = norm_g[l].reshape(1, D_MODEL)
        w_p = _tiled_in_weights(w_in, l)
        w_br = w_branch[l].astype(BF16)
        w_o = w_out[l].astype(BF16)
        cw = conv_w[l]
        sink_b = jnp.broadcast_to(win_sink[l][:, None], (WIN_HEADS, LANES))
        final = l == DEPTH - 1

        p, hp, *new_kv = _inproj(
            xp, shift, scale, g, w_p, rows_per_cond=batch * seq, cond0=0, update=(new_kv, l, seq),
            extras=((T_BKV, 0, KV_W), (T_BKV, KV_W, KV_W), (T_DK, 0, BRANCH_W), (T_DV, 0, BRANCH_W)))
        o_ctx = _ctx_mixers(p, cw, sink_b, *ctx_tabs, seq=seq)
        xp = _out_stage(xp, hp, gate, final_g2, [o_ctx] * 4, [0, 1, 2, 3], w_p, w_br, w_o,
                        rows_per_cond=batch * seq, cond0=0, final=final)

        q, hs, qfu, qfg = _inproj(xs, shift, scale, g, w_p, rows_per_cond=dseq, cond0=1, extra_dtype=BF16,
                              extras=((T_FU, 0, BRANCH_W), (T_FG, 0, BRANCH_W)))
        o_a = _lat_conv(q, cw, batch=dbatch, seq=dseq)
        o_w = _lat_window(q, cos_t, sin_t,
                          cache_win_k[:, l].reshape(dbatch, past, KV_W), cache_win_v[:, l].reshape(dbatch, past, KV_W),
                          sink_b, batch=dbatch, seq=dseq)
        o_f = _lat_fourier(qfu, qfg, lat_tabs, batch=dbatch, seq=dseq)
        o_n = _lat_na(q, na_bias[l],
                      cache_na_k[:, l].reshape(dbatch, past, BRANCH_W), cache_na_v[:, l].reshape(dbatch, past, BRANCH_W),
                      batch=dbatch, seq=dseq)
        xs = _out_stage(xs, hs, gate, final_g2, [o_a, o_w, o_f, o_n], [0, 0, 0, 0], w_p, w_br, w_o,
                        rows_per_cond=dseq, cond0=1, final=final)

    y_prompt = xp.reshape(batch, seq, D_MODEL)
    y_sample = xs.reshape(dbatch, dseq, D_MODEL)
    heads = (WIN_KV_HEADS, WIN_KV_HEADS, NA_HEADS, NA_HEADS)
    return (y_prompt, y_sample) + tuple(a.reshape(batch, DEPTH, seq, n, HEAD_DIM) for a, n in zip(new_kv, heads))
```

```python
import functools
import math

import numpy as np
import jax
import jax.numpy as jnp
from jax import lax
from jax.experimental import pallas as pl
from jax.experimental.pallas import tpu as pltpu

D_MODEL = 2048
DEPTH = 2
GRID_W = 64
N_BRANCH = 4
BRANCH_W = 512
HEAD_DIM = 64
WIN_HEADS = 8
WIN_KV_HEADS = 2
KV_W = WIN_KV_HEADS * HEAD_DIM
WIN_BLOCK = 128
FN_GROUPS = 4
FN_GROUP_W = BRANCH_W // FN_GROUPS
NA_HEADS = 8
NA_ROWS = 8
NA_COLS = 16
ROPE_BASE = 10000.0
EPS = 1e-6
ATTN_SCALE = HEAD_DIM ** -0.5
NEG_INF = -1e30
N_MAIN = 6400

F32 = jnp.float32
BF16 = jnp.bfloat16
LANES = 128

OFF_A = 0
OFF_BQ = OFF_A + 4 * BRANCH_W
OFF_BK = OFF_BQ + BRANCH_W
OFF_BG = OFF_BK + 2 * KV_W
OFF_FU = OFF_BG + BRANCH_W
assert OFF_FU + 6 * BRANCH_W == N_MAIN

TILE = 512
T_AX, T_AB, T_AC, T_AG, T_BQ, T_BG, T_BKV, T_ZERO, T_FU, T_FG, T_DQ, T_DK, T_DV, T_DG = range(14)
STEP_TILES = 2
N_TILES = 14
P_WIDTH = N_TILES * TILE

V7X_VMEM_BYTES = 64 * 2 ** 20
PROJ_VMEM_BYTES = V7X_VMEM_BYTES - 2 ** 20

INPROJ_ROWS = 1024
OUT_ROWS = 512
MOD_COLS = 1024
CONV_ROWS = 1024
FFT1_CHUNKS = 16
FFT2_KBLOCK = 16

FFT_R = 64


def _silu(x):
    return x * jax.nn.sigmoid(x)


def _dot(a, b):
    return jnp.dot(a, b, preferred_element_type=F32)


def _dot_nt(a, b):
    return lax.dot_general(a, b, (((1,), (1,)), ((), ())), preferred_element_type=F32)


def _rms_mod(x, g, scale, shift):
    ms = jnp.mean(x * x, axis=-1, keepdims=True)
    return (x * lax.rsqrt(ms + EPS) * g) * (1.0 + scale) + shift


NORM_ROWS = 256


def _mod_kernel(cv_ref, w_ref, b_ref, o_ref):
    s = _silu(cv_ref[...]).astype(BF16)
    o_ref[0] = _dot(s, w_ref[0].astype(BF16)) + b_ref[0]


def _modulation(cv8, w_ada, b_ada):
    tn = MOD_COLS
    n3 = 3 * D_MODEL
    return pl.pallas_call(
        _mod_kernel,
        grid=(DEPTH, n3 // tn),
        in_specs=[
            pl.BlockSpec((8, D_MODEL), lambda l, j: (0, 0)),
            pl.BlockSpec((1, D_MODEL, tn), lambda l, j: (l, 0, j)),
            pl.BlockSpec((1, 1, tn), lambda l, j: (l, 0, j)),
        ],
        out_specs=pl.BlockSpec((1, 8, tn), lambda l, j: (l, 0, j)),
        out_shape=jax.ShapeDtypeStruct((DEPTH, 8, n3), F32),
        name="modulation",
    )(cv8, w_ada, b_ada.reshape(DEPTH, 1, n3))


def _inproj_kernel(x_ref, sh_ref, sc_ref, g_ref, w_ref, *rest, extras, n_updated):
    p_ref, h_ref, extra_refs = rest[n_updated], rest[n_updated + 1], rest[n_updated + 2:]
    j = pl.program_id(1)
    assert STEP_TILES == 2 and T_ZERO % 2 == 1 and 0 < T_ZERO // 2 < N_TILES // 2 - 1
    half_step = T_ZERO // STEP_TILES
    assert all(tile >= STEP_TILES for tile, _ in extras)

    def store_extras(acc, in_half_step):
        for ref, (tile, lo) in zip(extra_refs, extras):
            lo = (tile % STEP_TILES) * TILE + lo
            if (tile // STEP_TILES == half_step) != in_half_step:
                continue
            assert lo + ref.shape[-1] <= acc.shape[1]

            @pl.when(j == tile // STEP_TILES)
            def _(ref=ref, lo=lo):
                cols = acc[:, lo:lo + ref.shape[-1]].astype(ref.dtype)
                if ref.ndim == 2:
                    ref[...] = cols
                else:
                    ref[:, 0] = cols.reshape(ref.shape[0], ref.shape[2], ref.shape[3])

    @pl.when(j == 0)
    def _():
        g, scale, shift = g_ref[...], sc_ref[0], sh_ref[0]
        for c in range(h_ref.shape[0] // NORM_ROWS):
            rows = slice(c * NORM_ROWS, (c + 1) * NORM_ROWS)
            h = _rms_mod(x_ref[rows, :], g, scale, shift).astype(h_ref.dtype)
            h_ref[rows, :] = h
            p_ref[rows, :] = _dot(h, w_ref[...]).astype(p_ref.dtype)

    @pl.when(j == half_step)
    def _():
        acc = _dot(h_ref[...], w_ref[:, :TILE])
        p_ref[:, :TILE] = acc.astype(p_ref.dtype)
        p_ref[:, TILE:] = jnp.zeros((p_ref.shape[0], TILE), p_ref.dtype)
        store_extras(acc, True)

    @pl.when((j > 0) & (j != half_step))
    def _():
        acc = _dot(h_ref[...], w_ref[...])
        p_ref[...] = acc.astype(p_ref.dtype)
        store_extras(acc, False)


def _inproj(x2d, shift, scale, g, w, *, rows_per_cond, cond0, extras, extra_dtype=None, update=None, tm=INPROJ_ROWS):
    m = x2d.shape[0]
    assert w.shape[0] == D_MODEL and w.shape[1] >= P_WIDTH and m % tm == 0 and rows_per_cond % tm == 0
    tiles_per_cond = rows_per_cond // tm
    cond = lambda i, j: (cond0 + i // tiles_per_cond, 0, 0)
    if update is None:
        updated, aliases = [], {}
        extra_specs = [pl.BlockSpec((tm, width), lambda i, j: (i, 0)) for _, _, width in extras]
        extra_shapes = [jax.ShapeDtypeStruct((m, width), extra_dtype) for _, _, width in extras]
    else:
        updated, layer, seq = update
        assert tm % seq == 0 and all(a.shape[2:] == (seq, width) for a, (_, _, width) in zip(updated, extras))
        extra_specs = [pl.BlockSpec((tm // seq, 1, seq, width), lambda i, j: (i, layer, 0, 0)) for _, _, width in extras]
        extra_shapes = [jax.ShapeDtypeStruct(a.shape, a.dtype) for a in updated]
        aliases = {5 + k: 2 + k for k in range(len(updated))}
    return pl.pallas_call(
        functools.partial(_inproj_kernel, extras=tuple((t, lo) for t, lo, _ in extras), n_updated=len(updated)),
        grid=(m // tm, N_TILES // STEP_TILES),
        in_specs=[
            pl.BlockSpec((tm, D_MODEL), lambda i, j: (i, 0)),
            pl.BlockSpec((1, 1, D_MODEL), cond),
            pl.BlockSpec((1, 1, D_MODEL), cond),
            pl.BlockSpec((1, D_MODEL), lambda i, j: (0, 0)),
            pl.BlockSpec((D_MODEL, STEP_TILES * TILE), lambda i, j: (0, j)),
        ] + [pl.BlockSpec(memory_space=pl.ANY) for _ in updated],
        out_specs=[pl.BlockSpec((tm, STEP_TILES * TILE), lambda i, j: (i, j)),
                   pl.BlockSpec((tm, D_MODEL), lambda i, j: (i, 0))]
        + extra_specs,
        out_shape=[jax.ShapeDtypeStruct((m, P_WIDTH), BF16), jax.ShapeDtypeStruct((m, D_MODEL), BF16)] + extra_shapes,
        input_output_aliases=aliases,
        compiler_params=pltpu.CompilerParams(dimension_semantics=("parallel", "arbitrary"),
                                             vmem_limit_bytes=PROJ_VMEM_BYTES),
        name="inproj",
    )(x2d, shift, scale, g, w, *updated)


def _out_kernel(x_ref, h_ref, gt_ref, fg_ref, oa_ref, ow_ref, of_ref, on_ref,
                wg0_ref, wg1_ref, wg2_ref, wg3_ref, wb_ref, wo_ref, y_ref, merged_ref, *, final):
    j = pl.program_id(1)
    h = h_ref[...]
    merged = None
    for i, (o_ref, wg_ref) in enumerate(((oa_ref, wg0_ref), (ow_ref, wg1_ref), (of_ref, wg2_ref), (on_ref, wg3_ref))):
        term = jax.nn.sigmoid(_dot(h, wg_ref[...])) * _dot(o_ref[...], wb_ref[i])
        merged = term if merged is None else merged + term
    merged_ref[j] = merged.astype(BF16)

    @pl.when(j == pl.num_programs(1) - 1)
    def _():
        m_all = jnp.concatenate([merged_ref[k] for k in range(merged_ref.shape[0])], axis=1)
        xn = x_ref[...] + gt_ref[0] * _dot(m_all, wo_ref[...])
        if final:
            ms = jnp.mean(xn * xn, axis=-1, keepdims=True)
            xn = xn * lax.rsqrt(ms + EPS) * fg_ref[...]
        y_ref[...] = xn


def _out_stage(x2d, h2d, gate, final_g, o_arrays, o_cols, w_gate, w_branch, w_out,
               *, rows_per_cond, cond0, final, tm=OUT_ROWS, tc=TILE):
    m = x2d.shape[0]
    assert m % tm == 0 and rows_per_cond % tm == 0
    tiles_per_cond = rows_per_cond // tm
    ncol = D_MODEL // tc
    cond = lambda i, j: (cond0 + i // tiles_per_cond, 0, 0)
    o_specs = [pl.BlockSpec((tm, BRANCH_W), functools.partial(lambda i, j, c: (i, c), c=c)) for c in o_cols]
    assert tc == TILE and w_gate.shape == (D_MODEL, P_WIDTH + N_BRANCH * D_MODEL)
    wg_specs = [pl.BlockSpec((D_MODEL, tc), functools.partial(lambda i, j, b: (0, N_TILES + b * ncol + j), b=b))
                for b in range(N_BRANCH)]
    return pl.pallas_call(
        functools.partial(_out_kernel, final=final),
        grid=(m // tm, ncol),
        in_specs=[
            pl.BlockSpec((tm, D_MODEL), lambda i, j: (i, 0)),
            pl.BlockSpec((tm, D_MODEL), lambda i, j: (i, 0)),
            pl.BlockSpec((1, 1, D_MODEL), cond),
            pl.BlockSpec((1, D_MODEL), lambda i, j: (0, 0)),
            *o_specs,
            *wg_specs,
            pl.BlockSpec((N_BRANCH, BRANCH_W, tc), lambda i, j: (0, 0, j)),
            pl.BlockSpec((D_MODEL, D_MODEL), lambda i, j: (0, 0), pipeline_mode=pl.Buffered(1)),
        ],
        out_specs=pl.BlockSpec((tm, D_MODEL), lambda i, j: (i, 0)),
        out_shape=jax.ShapeDtypeStruct((m, D_MODEL), F32),
        scratch_shapes=[pltpu.VMEM((ncol, tm, tc), BF16)],
        compiler_params=pltpu.CompilerParams(dimension_semantics=("parallel", "arbitrary"),
                                             vmem_limit_bytes=PROJ_VMEM_BYTES),
        name="out_stage",
    )(x2d, h2d, gate, final_g, *o_arrays, w_gate, w_gate, w_gate, w_gate, w_branch, w_out)


def _attend(scores, values, sinks=None):
    heads = range(len(scores))
    sinks = [None] * len(scores) if sinks is None else sinks
    m = []
    for h in heads:
        mh = scores[h][0].max(axis=-1, keepdims=True)
        for s in scores[h][1:]:
            mh = jnp.maximum(mh, s.max(axis=-1, keepdims=True))
        m.append(mh if sinks[h] is None else jnp.maximum(mh, sinks[h]))
    e = [[jnp.exp(s - m[h]) for s in scores[h]] for h in heads]
    den = []
    for h in heads:
        d = e[h][0].sum(axis=-1, keepdims=True)
        for x in e[h][1:]:
            d = d + x.sum(axis=-1, keepdims=True)
        den.append(d if sinks[h] is None else d + jnp.exp(sinks[h] - m[h]))
    out = []
    for h in heads:
        acc = _dot(e[h][0].astype(BF16), values[h][0])
        for x, v in zip(e[h][1:], values[h][1:]):
            acc = acc + _dot(x.astype(BF16), v)
        out.append(acc)
    return [a / d for a, d in zip(out, den)]


def _shift_rows(z, first_row, last_row):
    n = z.shape[0]
    row = lax.broadcasted_iota(jnp.int32, z.shape, 0)
    z_dn = jnp.where(row == 0, first_row, pltpu.roll(z, 1, axis=0))
    z_up = jnp.where(row == n - 1, last_row, pltpu.roll(z, n - 1, axis=0))
    return z_dn, z_up


def _tile(ref, t, dtype=None):
    v = ref[:, t * TILE:(t + 1) * TILE]
    return v if dtype is None else v.astype(dtype)


def _head(ref, t, h, base=0):
    lo = t * TILE + base + h * HEAD_DIM
    return ref[:, lo:lo + HEAD_DIM]


def _scaled_queries(q):
    assert math.frexp(ATTN_SCALE)[0] == 0.5
    return q * ATTN_SCALE


def _ctx_mixer_kernel(p_ref, cw_ref, sink_ref, ct_ref, st_ref, cb_ref, sb_ref, o_ref):
    z = _tile(p_ref, T_AC, F32) * _tile(p_ref, T_AX, F32)
    zero_row = jnp.zeros((1, BRANCH_W), F32)
    z_dn, z_up = _shift_rows(z, zero_row, zero_row)
    y = _tile(p_ref, T_AB, F32) * (z_dn * cw_ref[0:1, :] + z * cw_ref[1:2, :] + z_up * cw_ref[2:3, :])
    out = lambda b: slice(b * BRANCH_W, (b + 1) * BRANCH_W)
    o_ref[:, out(0)] = (y * _silu(_tile(p_ref, T_AG, F32))).astype(o_ref.dtype)

    gsz = WIN_HEADS // WIN_KV_HEADS
    head = lambda a, h: a[:, h * HEAD_DIM:(h + 1) * HEAD_DIM]
    qb = _scaled_queries(_tile(p_ref, T_BQ))
    qd = _scaled_queries(_tile(p_ref, T_DQ))
    scores = ([[_dot_nt(head(qb, h), _head(p_ref, T_BKV, h // gsz))] for h in range(WIN_HEADS)]
              + [[_dot_nt(head(qd, h), _head(p_ref, T_DK, h))] for h in range(NA_HEADS)])
    values = ([[_head(p_ref, T_BKV, h // gsz, base=KV_W)] for h in range(WIN_HEADS)]
              + [[_head(p_ref, T_DV, h)] for h in range(NA_HEADS)])
    sinks = [sink_ref[h:h + 1, 0:1] for h in range(WIN_HEADS)] + [None] * NA_HEADS
    heads = _attend(scores, values, sinks=sinks)
    o_w = jnp.concatenate(heads[:WIN_HEADS], axis=-1) * _silu(_tile(p_ref, T_BG, F32))
    o_ref[:, out(1)] = o_w.astype(o_ref.dtype)
    o_n = jnp.concatenate(heads[WIN_HEADS:], axis=-1) * _silu(_tile(p_ref, T_DG, F32))
    o_ref[:, out(3)] = o_n.astype(o_ref.dtype)

    u = _tile(p_ref, T_FU)
    uc = _dot(u, cb_ref[...]).astype(BF16)
    us = _dot(u, sb_ref[...]).astype(BF16)
    o_f = (_dot(ct_ref[...], uc) - _dot(st_ref[...], us)) * _silu(_tile(p_ref, T_FG, F32))
    o_ref[:, out(2)] = o_f.astype(o_ref.dtype)


def _ctx_mixers(p, conv_w, sink_b, ct, st, cb, sb, *, seq):
    m = p.shape[0]
    whole = lambda a: pl.BlockSpec(a.shape, lambda b: (0,) * a.ndim)
    return pl.pallas_call(
        _ctx_mixer_kernel,
        grid=(m // seq,),
        in_specs=[pl.BlockSpec((seq, P_WIDTH), lambda b: (b, 0)), whole(conv_w), whole(sink_b),
                  whole(ct), whole(st), whole(cb), whole(sb)],
        out_specs=pl.BlockSpec((seq, N_BRANCH * BRANCH_W), lambda b: (b, 0)),
        out_shape=jax.ShapeDtypeStruct((m, N_BRANCH * BRANCH_W), BF16),
        compiler_params=pltpu.CompilerParams(dimension_semantics=("parallel",)),
        name="ctx_mixers",
    )(p, conv_w, sink_b, ct, st, cb, sb)


HALO = 16


def _lat_conv_kernel(ax_ref, ab_ref, ac_ref, ag_ref, axp_ref, acp_ref, axn_ref, acn_ref, cw_ref, o_ref):
    i = pl.program_id(1)
    z = ac_ref[...].astype(F32) * ax_ref[...].astype(F32)
    zp = acp_ref[HALO - 1:HALO, :].astype(F32) * axp_ref[HALO - 1:HALO, :].astype(F32)
    zn = acn_ref[0:1, :].astype(F32) * axn_ref[0:1, :].astype(F32)
    zp = jnp.where(i == 0, 0.0, zp)
    zn = jnp.where(i == pl.num_programs(1) - 1, 0.0, zn)
    z_dn, z_up = _shift_rows(z, zp, zn)
    y = ab_ref[...].astype(F32) * (z_dn * cw_ref[0:1, :] + z * cw_ref[1:2, :] + z_up * cw_ref[2:3, :])
    o_ref[...] = (y * _silu(ag_ref[...].astype(F32))).astype(o_ref.dtype)


def _lat_conv(pa, conv_w, *, batch, seq, tr=CONV_ROWS):
    nt = seq // tr
    hb = tr // HALO
    last_halo = batch * seq // HALO - 1
    main = lambda c: pl.BlockSpec((tr, BRANCH_W), functools.partial(lambda b, i, c: (b * nt + i, c), c=c))
    prev = lambda c: pl.BlockSpec(
        (HALO, BRANCH_W), functools.partial(lambda b, i, c: (jnp.maximum((b * nt + i) * hb - 1, 0), c), c=c))
    nxt = lambda c: pl.BlockSpec(
        (HALO, BRANCH_W), functools.partial(lambda b, i, c: (jnp.minimum((b * nt + i + 1) * hb, last_halo), c), c=c))
    return pl.pallas_call(
        _lat_conv_kernel,
        grid=(batch, nt),
        in_specs=[main(T_AX), main(T_AB), main(T_AC), main(T_AG), prev(T_AX), prev(T_AC), nxt(T_AX), nxt(T_AC),
                  pl.BlockSpec(conv_w.shape, lambda b, i: (0, 0))],
        out_specs=pl.BlockSpec((tr, BRANCH_W), lambda b, i: (b * nt + i, 0)),
        out_shape=jax.ShapeDtypeStruct((batch * seq, BRANCH_W), BF16),
        compiler_params=pltpu.CompilerParams(dimension_semantics=("parallel", "arbitrary")),
        name="lat_conv",
    )(pa, pa, pa, pa, pa, pa, pa, pa, conv_w)


def _rope(x, cos, sin_signed):
    lane = lax.broadcasted_iota(jnp.int32, x.shape, 1)
    quarter = HEAD_DIM // 4
    partner = jnp.where((lane % (2 * quarter)) < quarter,
                        pltpu.roll(x, LANES - quarter, axis=1), pltpu.roll(x, quarter, axis=1))
    return x * cos + partner * sin_signed


WIN_BLOCKS_PER_STEP = 8


def _lat_win_kernel(q_ref, g_ref, kp_ref, kc_ref, kn_ref, vp_ref, vc_ref, vn_ref,
                    cq_ref, sq_ref, cp_ref, sp_ref, cn_ref, sn_ref, ck_ref, cv_ref, sink_ref, o_ref):
    nq = WIN_BLOCKS_PER_STEP
    nb = pl.num_programs(1) * nq
    wb = WIN_BLOCK
    cq, sq = cq_ref[...], sq_ref[...]
    k_rot = jnp.concatenate([
        _rope(kp_ref[...].astype(F32), cp_ref[...], sp_ref[...]),
        _rope(kc_ref[...].astype(F32), cq, sq),
        _rope(kn_ref[...].astype(F32), cn_ref[...], sn_ref[...]),
    ], axis=0).astype(BF16)
    v_all = jnp.concatenate([vp_ref[...], vc_ref[...], vn_ref[...]], axis=0).astype(BF16)
    qi = lax.broadcasted_iota(jnp.int32, (wb, 3 * wb), 0)
    kj = lax.broadcasted_iota(jnp.int32, (wb, 3 * wb), 1)
    ck = ck_ref[0].astype(BF16)
    cv = cv_ref[0].astype(BF16)
    gsz = WIN_HEADS // WIN_KV_HEADS
    q_rot = [_scaled_queries(_rope(q_ref[:, pair * LANES:(pair + 1) * LANES].astype(F32), cq, sq)).astype(BF16)
             for pair in range(WIN_HEADS // 2)]
    scores, values, sinks = [], [], []
    for d in range(nq):
        n = pl.program_id(1) * nq + d
        mask = (((kj < wb) & (kj >= qi) & (n > 0)) | ((kj >= wb) & (kj < 2 * wb))
                | ((kj >= 2 * wb) & (kj - 2 * wb <= qi) & (n < nb - 1)))
        rows = slice(d * wb, (d + 1) * wb)
        win = slice(d * wb, (d + 3) * wb)
        for h in range(WIN_HEADS):
            kv = slice((h // gsz) * HEAD_DIM, (h // gsz + 1) * HEAD_DIM)
            q = q_rot[h // 2][rows, (h % 2) * HEAD_DIM:(h % 2 + 1) * HEAD_DIM]
            s_loc = jnp.where(mask, _dot_nt(q, k_rot[win, kv]), NEG_INF)
            scores.append([s_loc, _dot_nt(q, ck[:, kv])])
            values.append([v_all[win, kv], cv[:, kv]])
            sinks.append(sink_ref[h:h + 1, 0:1])
    heads = _attend(scores, values, sinks=sinks)
    o = jnp.concatenate([jnp.concatenate(heads[d * WIN_HEADS:(d + 1) * WIN_HEADS], axis=-1) for d in range(nq)], axis=0)
    o_ref[...] = (o * _silu(g_ref[...].astype(F32))).astype(o_ref.dtype)


def _lat_window(pb, cos_t, sin_t, ctx_k, ctx_v, sink_b, *, batch, seq):
    wb = WIN_BLOCK
    k_col = T_BKV * (TILE // KV_W)
    v_col = k_col + 1
    nq = WIN_BLOCKS_PER_STEP
    nb = seq // wb
    steps = nb // nq
    nbr = lambda s, d: jnp.clip(s * nq + (nq if d > 0 else -1), 0, nb - 1)
    own = lambda col, width: pl.BlockSpec(
        (nq * wb, width), functools.partial(lambda b, s, col: (b * steps + s, col), col=col))
    kv = lambda col, d: pl.BlockSpec(
        (wb, KV_W), functools.partial(lambda b, s, col, d: (b * nb + nbr(s, d), col), col=col, d=d))
    tab = lambda d: pl.BlockSpec((wb, LANES), functools.partial(lambda b, s, d: (nbr(s, d), 0), d=d))
    tab_own = pl.BlockSpec((nq * wb, LANES), lambda b, s: (s, 0))
    ctx = pl.BlockSpec((1,) + ctx_k.shape[1:], lambda b, s: (b, 0, 0))
    return pl.pallas_call(
        _lat_win_kernel,
        grid=(batch, steps),
        in_specs=[
            own(T_BQ, BRANCH_W), own(T_BG, BRANCH_W),
            kv(k_col, -1), own(k_col, KV_W), kv(k_col, 1), kv(v_col, -1), own(v_col, KV_W), kv(v_col, 1),
            tab_own, tab_own, tab(-1), tab(-1), tab(1), tab(1),
            ctx, ctx,
            pl.BlockSpec(sink_b.shape, lambda b, s: (0, 0)),
        ],
        out_specs=pl.BlockSpec((nq * wb, BRANCH_W), lambda b, s: (b * steps + s, 0)),
        out_shape=jax.ShapeDtypeStruct((batch * seq, BRANCH_W), BF16),
        compiler_params=pltpu.CompilerParams(dimension_semantics=("parallel", "arbitrary")),
        name="lat_window",
    )(pb, pb, pb, pb, pb, pb, pb, pb, cos_t, sin_t, cos_t, sin_t, cos_t, sin_t, ctx_k, ctx_v, sink_b)


def _fft1_kernel(u_ref, c_ref, s_ref, twr_ref, twi_ref, zr_ref, zi_ref, *, chunks):
    u = u_ref[0].astype(BF16)
    yr = _dot(c_ref[...], u)
    yi = -_dot(s_ref[...], u)
    for t in range(chunks):
        wr = twr_ref[0, :, t:t + 1]
        wi = twi_ref[0, :, t:t + 1]
        a = yr[:, t * BRANCH_W:(t + 1) * BRANCH_W]
        b = yi[:, t * BRANCH_W:(t + 1) * BRANCH_W]
        zr_ref[0, :, t * BRANCH_W:(t + 1) * BRANCH_W] = (a * wr - b * wi).astype(zr_ref.dtype)
        zi_ref[0, :, t * BRANCH_W:(t + 1) * BRANCH_W] = (a * wi + b * wr).astype(zi_ref.dtype)


def _fft2_kernel(zr_ref, zi_ref, g_ref, l_ref, cs_ref, o_ref, *, kblock):
    r = FFT_R
    zz = [jnp.concatenate([zr_ref[0, kk], zi_ref[0, kk]], axis=0) for kk in range(kblock)]
    xx = [_dot(l_ref[...], z) for z in zz]
    xcat = [jnp.concatenate([x[0:r], x[r:2 * r]], axis=1).astype(BF16) for x in xx]
    out = [_dot(x, cs_ref[...]) for x in xcat]
    for kk in range(kblock):
        sl = slice(kk * BRANCH_W, (kk + 1) * BRANCH_W)
        o_ref[0, :, sl] = (out[kk] * _silu(g_ref[0, :, sl].astype(F32))).astype(o_ref.dtype)


def _lat_fourier(fu, fg, tabs, *, batch, seq):
    r = FFT_R
    assert seq == r * r
    wide = r * BRANCH_W
    chunks = FFT1_CHUNKS
    c64, s64, twr, twi, lmat, csmat = tabs
    u3 = fu.reshape(batch, r, wide)
    nct = r // chunks
    zr, zi = pl.pallas_call(
        functools.partial(_fft1_kernel, chunks=chunks),
        grid=(batch, nct),
        in_specs=[
            pl.BlockSpec((1, r, chunks * BRANCH_W), lambda b, t: (b, 0, t)),
            pl.BlockSpec((r, r), lambda b, t: (0, 0)),
            pl.BlockSpec((r, r), lambda b, t: (0, 0)),
            pl.BlockSpec((1, r, chunks), lambda b, t: (t, 0, 0)),
            pl.BlockSpec((1, r, chunks), lambda b, t: (t, 0, 0)),
        ],
        out_specs=[pl.BlockSpec((1, r, chunks * BRANCH_W), lambda b, t: (b, 0, t))] * 2,
        out_shape=[jax.ShapeDtypeStruct((batch, r, wide), BF16)] * 2,
        compiler_params=pltpu.CompilerParams(dimension_semantics=("parallel", "arbitrary")),
        name="lat_fft1",
    )(u3, c64, s64, twr, twi)
    kblock = FFT2_KBLOCK
    z4 = lambda z: z.reshape(batch, r, r, BRANCH_W)
    out = pl.pallas_call(
        functools.partial(_fft2_kernel, kblock=kblock),
        grid=(batch, r // kblock),
        in_specs=[
            pl.BlockSpec((1, kblock, r, BRANCH_W), lambda b, k: (b, k, 0, 0)),
            pl.BlockSpec((1, kblock, r, BRANCH_W), lambda b, k: (b, k, 0, 0)),
            pl.BlockSpec((1, r, kblock * BRANCH_W), lambda b, k: (b, 0, k)),
            pl.BlockSpec((2 * r, 2 * r), lambda b, k: (0, 0)),
            pl.BlockSpec((2 * BRANCH_W, BRANCH_W), lambda b, k: (0, 0)),
        ],
        out_specs=pl.BlockSpec((1, r, kblock * BRANCH_W), lambda b, k: (b, 0, k)),
        out_shape=jax.ShapeDtypeStruct((batch, r, wide), BF16),
        compiler_params=pltpu.CompilerParams(dimension_semantics=("parallel", "arbitrary")),
        name="lat_fft2",
    )(z4(zr), z4(zi), fg.reshape(batch, r, wide), lmat, csmat)
    return out.reshape(batch * seq, BRANCH_W)


N_RPB_ROWS = 2 * NA_ROWS - 1
N_RPB_COLS = 2 * NA_COLS - 1


NA_ROWS_PER_STEP = 8


def _lat_na_kernel(q_ref, g_ref, k_ref, v_ref, bias_ref, ck_ref, cv_ref, o_ref, *, rows):
    ck = ck_ref[0].astype(BF16)
    cv = cv_ref[0].astype(BF16)
    sls = [slice(h * HEAD_DIM, (h + 1) * HEAD_DIM) for h in range(NA_HEADS)]
    scores, values = [], []
    for d in range(NA_ROWS_PER_STEP):
        r = pl.program_id(1) * NA_ROWS_PER_STEP + d
        r0 = jnp.clip(r - NA_ROWS // 2, 0, rows - NA_ROWS)
        start = pl.multiple_of(r0 * GRID_W, GRID_W)
        kw = k_ref[pl.ds(start, NA_ROWS * GRID_W), :]
        vw = v_ref[pl.ds(start, NA_ROWS * GRID_W), :]
        qa = _scaled_queries(q_ref[d * GRID_W:(d + 1) * GRID_W, :])
        a0 = r0 - r + NA_ROWS - 1
        for h, sl in enumerate(sls):
            bias = jnp.concatenate([bias_ref[h, a0 + 2 * i2] for i2 in range(NA_ROWS // 2)], axis=1)
            scores.append([_dot_nt(qa[:, sl], kw[:, sl]) + bias, _dot_nt(qa[:, sl], ck[:, sl])])
            values.append([vw[:, sl], cv[:, sl]])
    heads = _attend(scores, values)
    o = jnp.concatenate([jnp.concatenate(heads[d * NA_HEADS:(d + 1) * NA_HEADS], axis=-1)
                         for d in range(NA_ROWS_PER_STEP)], axis=0)
    o_ref[...] = (o * _silu(g_ref[...].astype(F32))).astype(o_ref.dtype)


def _lat_na(p, bias, ctx_k, ctx_v, *, batch, seq):
    rows = seq // GRID_W
    steps = rows // NA_ROWS_PER_STEP
    qrows = NA_ROWS_PER_STEP * GRID_W
    ctx = pl.BlockSpec((1,) + ctx_k.shape[1:], lambda b, r: (b, 0, 0))
    return pl.pallas_call(
        functools.partial(_lat_na_kernel, rows=rows),
        grid=(batch, steps),
        in_specs=[
            pl.BlockSpec((qrows, BRANCH_W), lambda b, r: (b * steps + r, T_DQ)),
            pl.BlockSpec((qrows, BRANCH_W), lambda b, r: (b * steps + r, T_DG)),
            pl.BlockSpec((seq, BRANCH_W), lambda b, r: (b, T_DK)),
            pl.BlockSpec((seq, BRANCH_W), lambda b, r: (b, T_DV)),
            pl.BlockSpec(bias.shape, lambda b, r: (0, 0, 0, 0)),
            ctx, ctx,
        ],
        out_specs=pl.BlockSpec((qrows, BRANCH_W), lambda b, r: (b * steps + r, 0)),
        out_shape=jax.ShapeDtypeStruct((batch * seq, BRANCH_W), BF16),
        compiler_params=pltpu.CompilerParams(dimension_semantics=("parallel", "arbitrary")),
        name="lat_na",
    )(p, p, p, p, bias, ctx_k, ctx_v)


def _toeplitz_kernel(r_ref, o_ref):
    x = r_ref[...]
    hi = x.astype(BF16)
    r1 = x - hi.astype(F32)
    mid = r1.astype(BF16)
    lo = (r1 - mid.astype(F32)).astype(BF16)
    shape = (x.shape[1], GRID_W * GRID_W)
    b = lax.broadcasted_iota(jnp.int32, shape, 0)
    col = lax.broadcasted_iota(jnp.int32, shape, 1)
    kc = jnp.bitwise_and(col, GRID_W - 1)
    qc = lax.shift_right_logical(col, GRID_W.bit_length() - 1)
    onehot = jnp.where(kc - qc + (NA_COLS - 1) == b, 1.0, 0.0).astype(BF16)
    o_ref[...] = _dot(hi, onehot) + _dot(mid, onehot) + _dot(lo, onehot)


def _na_bias_tables(na_rpb):
    depth = na_rpb.shape[0]
    n = depth * NA_HEADS * N_RPB_ROWS
    n_pad = -(-n // 8) * 8
    r2 = jnp.zeros((n_pad, LANES), F32).at[:n, :N_RPB_COLS].set(na_rpb.reshape(n, N_RPB_COLS))
    flat = pl.pallas_call(
        _toeplitz_kernel,
        out_shape=jax.ShapeDtypeStruct((n_pad, GRID_W * GRID_W), F32),
        name="na_bias_toeplitz",
    )(r2)
    t = flat[:n].reshape(depth, NA_HEADS, N_RPB_ROWS, GRID_W, GRID_W)
    qc = np.arange(GRID_W)[:, None]
    kc = np.arange(GRID_W)[None, :]
    win0 = np.clip(qc - NA_COLS // 2, 0, GRID_W - NA_COLS)
    vis = (kc >= win0) & (kc < win0 + NA_COLS)
    t = jnp.where(vis, t, NEG_INF)
    return jnp.concatenate([t[:, :, :-1], t[:, :, 1:]], axis=-1)


def _bf16_table(t):
    return jnp.asarray(t, F32).astype(BF16)


def _dft_cos_sin(n):
    idx = np.arange(n)
    ang = 2.0 * np.pi * ((idx[:, None] * idx[None, :]) % n) / n
    return np.cos(ang), np.sin(ang)


def _channel_dft_blocks(scale):
    c, s = _dft_cos_sin(FN_GROUP_W)
    eye = np.eye(FN_GROUPS)
    return np.kron(eye, c) * scale, np.kron(eye, s) * scale


def _ctx_fourier_tables(seq):
    ct, st = _dft_cos_sin(seq)
    cb, sb = _channel_dft_blocks(1.0 / np.sqrt(seq * FN_GROUP_W))
    return tuple(_bf16_table(t) for t in (ct, st, cb, sb))


def _lat_fourier_tables(seq, chunks=FFT1_CHUNKS):
    r = FFT_R
    c64, s64 = _dft_cos_sin(r)
    k1 = np.arange(r)[:, None]
    t2 = np.arange(r)[None, :]
    ang = 2.0 * np.pi * (k1 * t2) / (r * r)
    twr = np.cos(ang).reshape(r, r // chunks, chunks).transpose(1, 0, 2)
    twi = (-np.sin(ang)).reshape(r, r // chunks, chunks).transpose(1, 0, 2)
    lmat = np.block([[c64, s64], [-s64, c64]])
    cb, sb = _channel_dft_blocks(1.0 / np.sqrt(seq * FN_GROUP_W))
    csmat = np.concatenate([cb, sb], axis=0)
    return (_bf16_table(c64), _bf16_table(s64), jnp.asarray(twr, F32), jnp.asarray(twi, F32),
            _bf16_table(lmat), _bf16_table(csmat))


def _rope_tables(seq):
    half = HEAD_DIM // 2
    quarter = half // 2
    t = jnp.arange(seq)
    inv = ROPE_BASE ** (-jnp.arange(quarter, dtype=F32) / quarter)

    def cs(pos):
        ang = pos.astype(F32)[:, None] * inv[None, :]
        c, s = jnp.cos(ang), jnp.sin(ang)
        return jnp.concatenate([c, c], axis=-1), jnp.concatenate([-s, s], axis=-1)

    cr, sr = cs(t // GRID_W)
    cc, sc = cs(t % GRID_W)
    cos = jnp.concatenate([cr, cc], axis=-1)
    sin = jnp.concatenate([sr, sc], axis=-1)
    return jnp.tile(cos, (1, 2)), jnp.tile(sin, (1, 2))


HALF_TILE = TILE // 2


def _tile_sources():
    starts = ([OFF_A + i * TILE for i in range(4)] + [OFF_BQ, OFF_BG, OFF_BK, None]
              + [OFF_FU + i * TILE for i in range(6)] + [N_MAIN + i * TILE for i in range(N_BRANCH * D_MODEL // TILE)])
    assert starts[T_BKV] == OFF_BK and starts[T_ZERO] is None and starts[T_FU] == OFF_FU
    src = []
    for t, start in enumerate(starts):
        halves = [None, None] if start is None else [start, start + HALF_TILE]
        if t == T_BKV:
            halves[1] = None
        for col in halves:
            assert col is None or col % HALF_TILE == 0
            src.append(-1 if col is None else col // HALF_TILE)
    assert len(starts) == N_TILES + N_BRANCH * D_MODEL // TILE
    return np.asarray(src, np.int32)


def _tile_weights_kernel(src_ref, lo_ref, hi_ref, o_ref):
    t = pl.program_id(0)
    zeros = jnp.zeros(lo_ref.shape[1:], BF16)
    o_ref[:, :HALF_TILE] = jnp.where(src_ref[2 * t] < 0, zeros, lo_ref[0].astype(BF16))
    o_ref[:, HALF_TILE:] = jnp.where(src_ref[2 * t + 1] < 0, zeros, hi_ref[0].astype(BF16))


def _tiled_in_weights(w_in, layer):
    src = _tile_sources()
    n_out_tiles = len(src) // 2
    half = lambda h: pl.BlockSpec(
        (1, D_MODEL, HALF_TILE), functools.partial(lambda t, s, h: (layer, 0, jnp.maximum(s[2 * t + h], 0)), h=h))
    return pl.pallas_call(
        _tile_weights_kernel,
        grid_spec=pltpu.PrefetchScalarGridSpec(
            num_scalar_prefetch=1,
            grid=(n_out_tiles,),
            in_specs=[half(0), half(1)],
            out_specs=pl.BlockSpec((D_MODEL, TILE), lambda t, s: (0, t)),
        ),
        out_shape=jax.ShapeDtypeStruct((D_MODEL, n_out_tiles * TILE), BF16),
        name="tile_weights",
    )(jnp.asarray(src), w_in, w_in)


def kernel(x_prompt, x_sample, cache_win_k, cache_win_v, cache_na_k, cache_na_v, c, c_ctx, norm_g, w_ada, b_ada,
           w_in, conv_w, win_sink, na_rpb, w_branch, w_out, final_g):
    batch, seq, _ = x_prompt.shape
    dbatch, dseq, _ = x_sample.shape
    past = cache_win_k.shape[2]

    cv8 = jnp.zeros((8, D_MODEL), F32).at[0].set(c_ctx).at[1:1 + dbatch].set(c)
    mod = _modulation(cv8, w_ada, b_ada)

    ctx_tabs = _ctx_fourier_tables(seq)
    lat_tabs = _lat_fourier_tables(dseq)
    cos_t, sin_t = _rope_tables(dseq)
    na_bias = _na_bias_tables(na_rpb)
    final_g2 = final_g.reshape(1, D_MODEL)

    xp = x_prompt.reshape(batch * seq, D_MODEL)
    xs = x_sample.reshape(dbatch * dseq, D_MODEL)
    new_kv = [jnp.zeros((batch, DEPTH, seq, width), F32) for width in (KV_W, KV_W, BRANCH_W, BRANCH_W)]

    for l in range(DEPTH):
        shift = mod[l, :, 0:D_MODEL].reshape(8, 1, D_MODEL)
        scale = mod[l, :, D_MODEL:2 * D_MODEL].reshape(8, 1, D_MODEL)
        gate = mod[l, :, 2 * D_MODEL:].reshape(8, 1, D_MODEL)
        g = norm_g[l].reshape(1, D_MODEL)
        w_p = _tiled_in_weights(w_in, l)
        w_br = w_branch[l].astype(BF16)
        w_o = w_out[l].astype(BF16)
        cw = conv_w[l]
        sink_b = jnp.broadcast_to(win_sink[l][:, None], (WIN_HEADS, LANES))
        final = l == DEPTH - 1

        p, hp, *new_kv = _inproj(
            xp, shift, scale, g, w_p, rows_per_cond=batch * seq, cond0=0, update=(new_kv, l, seq),
            extras=((T_BKV, 0, KV_W), (T_BKV, KV_W, KV_W), (T_DK, 0, BRANCH_W), (T_DV, 0, BRANCH_W)))
        o_ctx = _ctx_mixers(p, cw, sink_b, *ctx_tabs, seq=seq)
        xp = _out_stage(xp, hp, gate, final_g2, [o_ctx] * 4, [0, 1, 2, 3], w_p, w_br, w_o,
                        rows_per_cond=batch * seq, cond0=0, final=final)

        q, hs, qfu, qfg = _inproj(xs, shift, scale, g, w_p, rows_per_cond=dseq, cond0=1, extra_dtype=BF16,
                              extras=((T_FU, 0, BRANCH_W), (T_FG, 0, BRANCH_W)))
        o_a = _lat_conv(q, cw, batch=dbatch, seq=dseq)
        o_w = _lat_window(q, cos_t, sin_t,
                          cache_win_k[:, l].reshape(dbatch, past, KV_W), cache_win_v[:, l].reshape(dbatch, past, KV_W),
                          sink_b, batch=dbatch, seq=dseq)
        o_f = _lat_fourier(qfu, qfg, lat_tabs, batch=dbatch, seq=dseq)
        o_n = _lat_na(q, na_bias[l],
                      cache_na_k[:, l].reshape(dbatch, past, BRANCH_W), cache_na_v[:, l].reshape(dbatch, past, BRANCH_W),
                      batch=dbatch, seq=dseq)
        xs = _out_stage(xs, hs, gate, final_g2, [o_a, o_w, o_f, o_n], [0, 0, 0, 0], w_p, w_br, w_o,
                        rows_per_cond=dseq, cond0=1, final=final)

    y_prompt = xp.reshape(batch, seq, D_MODEL)
    y_sample = xs.reshape(dbatch, dseq, D_MODEL)
    heads = (WIN_KV_HEADS, WIN_KV_HEADS, NA_HEADS, NA_HEADS)
    return (y_prompt, y_sample) + tuple(a.reshape(batch, DEPTH, seq, n, HEAD_DIM) for a, n in zip(new_kv, heads))
```
